```python
import jax, jax.numpy as jnp
from jax import lax
import numpy as np

D_MODEL = 2048
BATCH = 1
SEQ = 8192
DEPTH = 2

BRANCH_WIDTH = D_MODEL // 4
NSA_HEADS = 8
NSA_KV_HEADS = 2
NSA_HD = BRANCH_WIDTH // NSA_HEADS
CMP_LEN = 32
CMP_STRIDE = 16
SLC_LEN = 64
SLC_TOP = 16
WINDOW = 512
Q_BLOCK = 128
HG_HEADS = 4
HG_DK = BRANCH_WIDTH // HG_HEADS
HG_DV = BRANCH_WIDTH // HG_HEADS
GLA_HEADS = 4
GLA_DK = BRANCH_WIDTH // 8
GLA_DV = BRANCH_WIDTH // GLA_HEADS
GLA_RANK = 16
GLA_TAU = 16.0
CHUNK = 64
N_GROUPS = 4
EXPERTS_PER_GROUP = 8
N_EXPERTS = N_GROUPS * EXPERTS_PER_GROUP
TOP_K = 2
D_FF_EXPERT = D_MODEL // 4
EPS = 1e-6
NEG = -1e30
FORCE = 1e4
TINY = 1e-30

IN_SPLITS = (
    ("nsa_q", NSA_HEADS * NSA_HD),
    ("nsa_k_cmp", NSA_KV_HEADS * NSA_HD), ("nsa_v_cmp", NSA_KV_HEADS * NSA_HD),
    ("nsa_k_slc", NSA_KV_HEADS * NSA_HD), ("nsa_v_slc", NSA_KV_HEADS * NSA_HD),
    ("nsa_k_win", NSA_KV_HEADS * NSA_HD), ("nsa_v_win", NSA_KV_HEADS * NSA_HD),
    ("nsa_gate", 3 * NSA_HEADS),
    ("hg_q", HG_HEADS * HG_DK), ("hg_f", HG_HEADS * HG_DK),
    ("hg_i", HG_HEADS * HG_DV), ("hg_g", HG_HEADS * HG_DV),
    ("gla_q", GLA_HEADS * GLA_DK), ("gla_k", GLA_HEADS * GLA_DK),
    ("gla_v", GLA_HEADS * GLA_DV), ("gla_r", GLA_HEADS * GLA_DV),
    ("gla_a", GLA_RANK),
    ("merge_gate", 3 * D_MODEL),
)
IN_COLS = sum(size for _, size in IN_SPLITS)

kernel_name = "hybrid_nsa_hgrn2_gla_hmoe"


def rmsnorm(x, gain):
    xf = x.astype(jnp.float32)
    y = xf * lax.rsqrt(jnp.mean(xf * xf, axis=-1, keepdims=True) + EPS)
    return (y * gain.astype(jnp.float32)).astype(x.dtype)


def split_cols(h):
    parts, off = {}, 0
    for name, size in IN_SPLITS:
        parts[name] = h[..., off:off + size]
        off += size
    return parts


def to_heads(a, n):
    B, T, _ = a.shape
    return a.reshape(B, T, n, -1).transpose(0, 2, 1, 3)


def from_heads(a):
    B, H, T, d = a.shape
    return a.transpose(0, 2, 1, 3).reshape(B, T, H * d)


def alibi_slopes(n):
    return jnp.exp2(-8.0 * jnp.arange(1, n + 1, dtype=jnp.float32) / n)


def compress_blocks(a, pe, w1, w2):
    B, G, T, hd = a.shape
    n_cmp = (T - CMP_LEN) // CMP_STRIDE + 1
    idx = jnp.arange(n_cmp)[:, None] * CMP_STRIDE + jnp.arange(CMP_LEN)[None, :]
    blocks = (a[:, :, idx, :] + pe).reshape(B, G, n_cmp, CMP_LEN * hd)
    return jax.nn.gelu(blocks @ w1) @ w2


def nsa_mixer(p, cmp_pe, cmp_w1, cmp_w2, q_gain, k_gain, slopes):
    B, T, _ = p["nsa_q"].shape
    G, R, hd = NSA_KV_HEADS, NSA_HEADS // NSA_KV_HEADS, NSA_HD
    scale = hd ** -0.5
    m = slopes.reshape(G, R)[None, :, :, None, None]
    q = rmsnorm(to_heads(p["nsa_q"], NSA_HEADS), q_gain).reshape(B, G, R, T, hd)
    t_pos = jnp.arange(T)

    kc = rmsnorm(compress_blocks(to_heads(p["nsa_k_cmp"], G), cmp_pe[0], cmp_w1[0], cmp_w2[0]), k_gain[0])
    vc = compress_blocks(to_heads(p["nsa_v_cmp"], G), cmp_pe[1], cmp_w1[1], cmp_w2[1])
    n_cmp = kc.shape[2]
    c_end = jnp.arange(n_cmp) * CMP_STRIDE + CMP_LEN - 1
    dist_c = (t_pos[:, None] - c_end[None, :]).astype(jnp.float32)
    vis_c = dist_c >= 0
    s = jnp.einsum('bgrtd,bgnd->bgrtn', q, kc).astype(jnp.float32) * scale - m * dist_c
    p_cmp = jax.nn.softmax(jnp.where(vis_c, s, NEG), axis=-1) * vis_c
    o_cmp = jnp.einsum('bgrtn,bgnd->bgrtd', p_cmp.astype(vc.dtype), vc)

    imp = p_cmp.sum(axis=2)
    n_slc = T // SLC_LEN
    ratio, span = SLC_LEN // CMP_STRIDE, CMP_LEN // CMP_STRIDE
    pad_l = span - 1
    pad_r = ratio * n_slc + ratio - 1 - n_cmp
    imp = jnp.pad(imp, ((0, 0), (0, 0), (0, 0), (pad_l, pad_r)))
    score = jnp.zeros((B, G, T, n_slc), jnp.float32)
    for mm in range(ratio):
        for nn in range(span):
            st = pad_l + mm - nn
            score = score + imp[..., st:st + ratio * n_slc:ratio]
    blk = jnp.arange(n_slc)[None, :]
    cur = (t_pos // SLC_LEN)[:, None]
    forced = (blk == 0) | (blk == cur) | (blk == cur - 1)
    score = jnp.where(forced, FORCE, score)
    score = jnp.where(blk <= cur, score, -1.0)
    n_top = min(SLC_TOP, n_slc)
    _, sel = lax.top_k(score, n_top)

    ks = rmsnorm(to_heads(p["nsa_k_slc"], G), k_gain[1]).reshape(B, G, n_slc, SLC_LEN, hd)
    vs = to_heads(p["nsa_v_slc"], G).reshape(B, G, n_slc, SLC_LEN, hd)
    nq = T // Q_BLOCK
    q_blocks = q.reshape(B, G, R, nq, Q_BLOCK, hd).transpose(3, 0, 1, 2, 4, 5)
    sel_blocks = sel.reshape(B, G, nq, Q_BLOCK, n_top).transpose(2, 0, 1, 3, 4)
    starts = jnp.arange(nq) * Q_BLOCK
    bi = jnp.arange(B)[:, None, None, None]
    gi = jnp.arange(G)[None, :, None, None]
    offs = jnp.arange(SLC_LEN)

    def attend_selected(args):
        qb, ib, st = args
        kb = ks[bi, gi, ib].reshape(B, G, Q_BLOCK, n_top * SLC_LEN, hd)
        vb = vs[bi, gi, ib].reshape(B, G, Q_BLOCK, n_top * SLC_LEN, hd)
        qpos = st + jnp.arange(Q_BLOCK)
        kpos = (ib[..., None] * SLC_LEN + offs).reshape(B, G, Q_BLOCK, n_top * SLC_LEN)
        dist = (qpos[:, None] - kpos).astype(jnp.float32)[:, :, None]
        sb = jnp.einsum('bgrqd,bgqkd->bgrqk', qb, kb).astype(jnp.float32) * scale - m * dist
        pb = jax.nn.softmax(jnp.where(dist >= 0, sb, NEG), axis=-1)
        return jnp.einsum('bgrqk,bgqkd->bgrqd', pb.astype(vb.dtype), vb)

    o_slc = lax.map(attend_selected, (q_blocks, sel_blocks, starts))
    o_slc = o_slc.transpose(1, 2, 3, 0, 4, 5).reshape(B, G, R, T, hd)

    kw = rmsnorm(to_heads(p["nsa_k_win"], G), k_gain[2])
    vw = to_heads(p["nsa_v_win"], G)
    nb = WINDOW // Q_BLOCK

    def band(a):
        a = jnp.pad(a, ((0, 0), (0, 0), (WINDOW, 0), (0, 0))).reshape(B, G, nq + nb, Q_BLOCK, hd)
        return jnp.concatenate([a[:, :, i:i + nq] for i in range(nb + 1)], axis=3)

    kbw, vbw = band(kw), band(vw)
    qw = q.reshape(B, G, R, nq, Q_BLOCK, hd)
    qpos = jnp.arange(nq)[:, None] * Q_BLOCK + jnp.arange(Q_BLOCK)[None, :]
    kpos = (jnp.arange(nq)[:, None] - nb) * Q_BLOCK + jnp.arange((nb + 1) * Q_BLOCK)[None, :]
    dist = qpos[:, :, None] - kpos[:, None, :]
    mask_w = (dist >= 0) & (dist < WINDOW) & (kpos[:, None, :] >= 0)
    sw = jnp.einsum('bgrcqd,bgckd->bgrcqk', qw, kbw).astype(jnp.float32) * scale \
        - m[..., None] * dist.astype(jnp.float32)
    pw = jax.nn.softmax(jnp.where(mask_w, sw, NEG), axis=-1)
    o_win = jnp.einsum('bgrcqk,bgckd->bgrcqd', pw.astype(vbw.dtype), vbw).reshape(B, G, R, T, hd)

    gates = jax.nn.sigmoid(p["nsa_gate"].astype(jnp.float32)).reshape(B, T, 3, G, R)
    gates = gates.transpose(2, 0, 3, 4, 1)[..., None]
    o = gates[0] * o_cmp + gates[1] * o_slc + gates[2] * o_win
    return o.transpose(0, 3, 1, 2, 4).reshape(B, T, NSA_HEADS * hd)


def gated_linear_recurrence(q, k, v, log_f):
    B, H, T, dk = q.shape
    dv = v.shape[-1]
    n = T // CHUNK

    def chunks(a):
        return a.astype(jnp.float32).reshape(B, H, n, CHUNK, a.shape[-1]).transpose(2, 0, 1, 3, 4)

    causal = jnp.tril(jnp.ones((CHUNK, CHUNK), dtype=bool))[:, :, None]

    def step(S, inp):
        qc, kc, vc, gc = inp
        b = jnp.cumsum(gc, axis=2)
        diff = b[:, :, :, None, :] - b[:, :, None, :, :]
        decay = jnp.where(causal, jnp.exp(jnp.where(causal, diff, 0.0)), 0.0)
        scores = jnp.einsum('bhid,bhjd,bhijd->bhij', qc, kc, decay)
        o = scores @ vc + jnp.einsum('bhid,bhdv->bhiv', qc * jnp.exp(b), S)
        b_end = b[:, :, -1:, :]
        S = jnp.exp(b_end[:, :, 0, :, None]) * S + jnp.einsum('bhjd,bhjv->bhdv', kc * jnp.exp(b_end - b), vc)
        return S, o

    S0 = jnp.zeros((B, H, dk, dv), jnp.float32)
    _, o = lax.scan(step, S0, (chunks(q), chunks(k), chunks(v), chunks(log_f)))
    return o.transpose(1, 2, 0, 3, 4).reshape(B, H, T, dv).astype(v.dtype)


def hgrn2_mixer(p, lower_bound, norm_gain):
    q = to_heads(p["hg_q"], HG_HEADS)
    z = to_heads(p["hg_f"], HG_HEADS).astype(jnp.float32)
    i = to_heads(p["hg_i"], HG_HEADS)
    lb = lower_bound.astype(jnp.float32).reshape(HG_HEADS, 1, HG_DK)
    f = lb + (1.0 - lb) * jax.nn.sigmoid(z)
    log_f = jnp.log(jnp.maximum(f, TINY))
    k = (1.0 - lb) * jax.nn.sigmoid(-z)
    o = gated_linear_recurrence(q, k, i, log_f)
    o = rmsnorm(o, norm_gain) * jax.nn.sigmoid(to_heads(p["hg_g"], HG_HEADS))
    return from_heads(o)


def gla_mixer(p, w_a2, b_a, norm_gain):
    q = to_heads(p["gla_q"], GLA_HEADS) * (GLA_DK ** -0.5)
    k = to_heads(p["gla_k"], GLA_HEADS)
    v = to_heads(p["gla_v"], GLA_HEADS)
    a = (p["gla_a"] @ w_a2 + b_a).astype(jnp.float32)
    log_a = to_heads(jax.nn.log_sigmoid(a) / GLA_TAU, GLA_HEADS)
    o = gated_linear_recurrence(q, k, v, log_a)
    o = rmsnorm(o, norm_gain) * jax.nn.silu(to_heads(p["gla_r"], GLA_HEADS))
    return from_heads(o)


def hybrid_mixer(h, w_in, cmp_pe, cmp_w1, cmp_w2, q_gain, k_gain, lower_bound, hg_norm,
                 gla_w_a2, gla_b_a, gla_norm, w_branch, w_out, slopes):
    B, T, D = h.shape
    p = split_cols(h @ w_in)
    o_a = nsa_mixer(p, cmp_pe, cmp_w1, cmp_w2, q_gain, k_gain, slopes).astype(h.dtype)
    o_b = hgrn2_mixer(p, lower_bound, hg_norm).astype(h.dtype)
    o_c = gla_mixer(p, gla_w_a2, gla_b_a, gla_norm).astype(h.dtype)
    branches = jnp.stack([o_a, o_b, o_c])
    proj = jnp.einsum('nbtw,nwd->nbtd', branches, w_branch)
    gates = jax.nn.sigmoid(p["merge_gate"].reshape(B, T, 3, D)).transpose(2, 0, 1, 3)
    merged = jnp.sum(gates * proj, axis=0)
    return (merged @ w_out).astype(h.dtype)


def hierarchical_moe(h, w_grp, b_grp, w_exp, b_exp, w_gate_up, w_down):
    B, T, D = h.shape
    N = B * T
    xt = h.reshape(N, D)
    grp_prob = jax.nn.softmax((xt @ w_grp).astype(jnp.float32) + b_grp, axis=-1)
    grp_w, grp_idx = lax.top_k(grp_prob, 1)
    exp_logits = ((xt @ w_exp).astype(jnp.float32) + b_exp).reshape(N, N_GROUPS, EXPERTS_PER_GROUP)
    exp_logits = jnp.take_along_axis(exp_logits, grp_idx[:, :, None], axis=1)[:, 0]
    top_w, top_idx = lax.top_k(jax.nn.softmax(exp_logits, axis=-1), TOP_K)
    top_w = top_w / jnp.sum(top_w, axis=-1, keepdims=True)
    weights = grp_w * top_w
    expert_id = grp_idx * EXPERTS_PER_GROUP + top_idx
    combine = jnp.zeros((N, N_EXPERTS), jnp.float32).at[jnp.arange(N)[:, None], expert_id].add(weights)
    y = jnp.zeros((N, D), jnp.float32)
    for g in range(N_GROUPS):
        sl = slice(g * EXPERTS_PER_GROUP, (g + 1) * EXPERTS_PER_GROUP)
        gu = jnp.einsum('nd,edf->nef', xt, w_gate_up[sl])
        gate, up = jnp.split(gu, 2, axis=-1)
        act = jax.nn.silu(gate) * up * combine[:, sl, None].astype(gu.dtype)
        y = y + jnp.einsum('nef,efd->nd', act, w_down[sl])
    return y.reshape(B, T, D).astype(h.dtype)


def setup_inputs(seed: int = 0) -> dict:
    key = jax.random.key(seed)
    ks = jax.random.split(key, 24)
    f32 = jnp.float32
    L, D = DEPTH, D_MODEL

    def nrm(k, shape, scale):
        return jax.random.normal(k, shape, f32) * scale

    return {
        "x": nrm(ks[0], (BATCH, SEQ, D), 1.0),
        "norm_mix": 1.0 + nrm(ks[1], (L, D), 0.02),
        "w_in": nrm(ks[2], (L, D, IN_COLS), D ** -0.5),
        "cmp_pe": nrm(ks[3], (L, 2, CMP_LEN, NSA_HD), 0.1),
        "cmp_w1": nrm(ks[4], (L, 2, CMP_LEN * NSA_HD, NSA_HD), (CMP_LEN * NSA_HD) ** -0.5),
        "cmp_w2": nrm(ks[5], (L, 2, NSA_HD, NSA_HD), NSA_HD ** -0.5),
        "q_norm": 1.0 + nrm(ks[6], (L, NSA_HD), 0.02),
        "k_norm": 1.0 + nrm(ks[7], (L, 3, NSA_HD), 0.02),
        "hg_lb_logits": nrm(ks[8], (L, HG_HEADS * HG_DK), 0.1),
        "hg_norm": 1.0 + nrm(ks[9], (L, HG_DV), 0.02),
        "gla_w_a2": nrm(ks[10], (L, GLA_RANK, GLA_HEADS * GLA_DK), GLA_RANK ** -0.5),
        "gla_b_a": nrm(ks[11], (L, GLA_HEADS * GLA_DK), 0.1),
        "gla_norm": 1.0 + nrm(ks[12], (L, GLA_DV), 0.02),
        "w_branch": nrm(ks[13], (L, 3, BRANCH_WIDTH, D), BRANCH_WIDTH ** -0.5),
        "w_out": nrm(ks[14], (L, D, D), D ** -0.5),
        "norm_ffn": 1.0 + nrm(ks[15], (L, D), 0.02),
        "w_grp": nrm(ks[16], (L, D, N_GROUPS), D ** -0.5),
        "b_grp": nrm(ks[17], (L, N_GROUPS), 0.01),
        "w_exp": nrm(ks[18], (L, D, N_EXPERTS), D ** -0.5),
        "b_exp": nrm(ks[19], (L, N_EXPERTS), 0.01),
        "w_gate_up": nrm(ks[20], (L, N_EXPERTS, D, 2 * D_FF_EXPERT), D ** -0.5),
        "w_down": nrm(ks[21], (L, N_EXPERTS, D_FF_EXPERT, D), D_FF_EXPERT ** -0.5),
    }


def reference(x, norm_mix, w_in, cmp_pe, cmp_w1, cmp_w2, q_norm, k_norm, hg_lb_logits, hg_norm,
              gla_w_a2, gla_b_a, gla_norm, w_branch, w_out, norm_ffn, w_grp, b_grp, w_exp, b_exp,
              w_gate_up, w_down):
    slopes = alibi_slopes(NSA_HEADS)
    P = jax.nn.softmax(hg_lb_logits.astype(jnp.float32), axis=0)
    lower_bounds = jnp.cumsum(P, axis=0) - P[0]
    for l in range(DEPTH):
        h = rmsnorm(x, norm_mix[l])
        x = x + hybrid_mixer(h, w_in[l], cmp_pe[l], cmp_w1[l], cmp_w2[l], q_norm[l], k_norm[l],
                             lower_bounds[l], hg_norm[l], gla_w_a2[l], gla_b_a[l], gla_norm[l],
                             w_branch[l], w_out[l], slopes)
        h = rmsnorm(x, norm_ffn[l])
        x = x + hierarchical_moe(h, w_grp[l], b_grp[l], w_exp[l], b_exp[l], w_gate_up[l], w_down[l])
    return x
```

```python
import functools
import math

import numpy as np
import jax
import jax.numpy as jnp
from jax import lax
from jax.experimental import pallas as pl
from jax.experimental.pallas import tpu as pltpu

F32 = jnp.float32
BF16 = jnp.bfloat16
HIGHEST = lax.Precision.HIGHEST

D_MODEL = 2048
DEPTH = 2
BRANCH_WIDTH = D_MODEL // 4
NSA_HEADS = 8
NSA_KV_HEADS = 2
NSA_REP = NSA_HEADS // NSA_KV_HEADS
NSA_HD = 64
CMP_LEN = 32
CMP_STRIDE = 16
SLC_LEN = 64
SLC_TOP = 16
WINDOW = 512
HG_HEADS = 4
GLA_HEADS = 4
GLA_DK = 64
GLA_RANK = 16
GLA_TAU = 16.0
CHUNK = 64
SUB = 16
N_GROUPS = 4
EXPERTS_PER_GROUP = 8
N_EXPERTS = N_GROUPS * EXPERTS_PER_GROUP
D_FF_EXPERT = D_MODEL // 4
EPS = 1e-6
NEG = -1e30
FORCE = 1e4
TINY = 1e-30

LANES = 128
VMEM_LIMIT = 56 * 1024 * 1024

C_MERGE = 0
C_NSA_Q = 6144
C_NSA_KV = 6656
C_HG = 7424
C_GLA_QK = 9472
C_GLA_VR = 9984
C_TAIL = 11008
N_PROJ = 11264
TAIL_GATE0 = 0
TAIL_GLA_A0 = 24

_ORIG = {}
_off = 0
for _name, _size in (("nsa_q", 512), ("nsa_kv", 768), ("nsa_gate", 24), ("hg", 2048), ("gla_qk", 512),
                     ("gla_vr", 1024), ("gla_a", 16), ("merge_gate", 6144)):
    _ORIG[_name] = (_off, _size)
    _off += _size
IN_COLS = _off

NT = (((1,), (1,)), ((), ()))
TN = (((0,), (0,)), ((), ()))


def _cparams(sem):
    return pltpu.CompilerParams(dimension_semantics=sem, vmem_limit_bytes=VMEM_LIMIT)


def _sigmoid(x):
    return 1.0 / (1.0 + jnp.exp(-x))


def _norm_kernel(x_ref, g_ref, o_ref):
    x = x_ref[...]
    ms = jnp.mean(x * x, axis=-1, keepdims=True)
    o_ref[...] = (x * lax.rsqrt(ms + EPS) * g_ref[...]).astype(o_ref.dtype)


def rmsnorm_bf16(x, gain, tm=512):
    T, D = x.shape
    return pl.pallas_call(
        _norm_kernel,
        grid=(T // tm,),
        in_specs=[pl.BlockSpec((tm, D), lambda i: (i, 0)), pl.BlockSpec((1, D), lambda i: (0, 0))],
        out_specs=pl.BlockSpec((tm, D), lambda i: (i, 0)),
        out_shape=jax.ShapeDtypeStruct((T, D), BF16),
        compiler_params=_cparams(("parallel",)),
        name="rmsnorm_bf16",
    )(x, gain.reshape(1, D))


def _mm_kernel(a_ref, b_ref, o_ref):
    o_ref[...] = jnp.dot(a_ref[...], b_ref[...], preferred_element_type=F32)


def _mm_res_kernel(a_ref, b_ref, r_ref, o_ref):
    o_ref[...] = r_ref[...] + jnp.dot(a_ref[...], b_ref[...], preferred_element_type=F32)


def matmul_bf16(a, b, res=None, tm=1024, tn=512, name="matmul_bf16"):
    T, K = a.shape
    N = b.shape[1]
    in_specs = [pl.BlockSpec((tm, K), lambda i, j: (i, 0)), pl.BlockSpec((K, tn), lambda i, j: (0, j))]
    args = [a, b]
    kern = _mm_kernel
    if res is not None:
        in_specs.append(pl.BlockSpec((tm, tn), lambda i, j: (i, j)))
        args.append(res)
        kern = _mm_res_kernel
    return pl.pallas_call(
        kern,
        grid=(T // tm, N // tn),
        in_specs=in_specs,
        out_specs=pl.BlockSpec((tm, tn), lambda i, j: (i, j)),
        out_shape=jax.ShapeDtypeStruct((T, N), F32),
        compiler_params=_cparams(("parallel", "arbitrary")),
        name=name,
    )(*args)


def _half_rmsnorm(x, gain2, lo):
    x2 = x * x
    s_lo = jnp.sum(jnp.where(lo, x2, 0.0), axis=1, keepdims=True)
    s_hi = jnp.sum(jnp.where(lo, 0.0, x2), axis=1, keepdims=True)
    ms = jnp.where(lo, s_lo, s_hi) * (1.0 / NSA_HD)
    return x * lax.rsqrt(ms + EPS) * gain2


def _nsa_prep_kernel(q_ref, kvc_ref, kvs_ref, kvw_ref, tail_ref, qg_ref, kg_ref,
                     qn_ref, kaug_ref, vs_ref, kw_ref, vw_ref, gate_ref, *, tm):
    i = pl.program_id(0)
    lane = lax.broadcasted_iota(jnp.int32, (tm, LANES), 1)
    lo = lane < NSA_HD
    qg = qg_ref[...]
    for c in range(4):
        x = q_ref[:, c * LANES:(c + 1) * LANES]
        qn_ref[:, c * LANES:(c + 1) * LANES] = (_half_rmsnorm(x, qg, lo) * (NSA_HD ** -0.5)).astype(BF16)

    def dup(kn):
        rolled = pltpu.roll(kn, NSA_HD, axis=1)
        return jnp.where(lo, kn, rolled), jnp.where(lo, rolled, kn)

    def lohi(v):
        rolled = pltpu.roll(v, NSA_HD, axis=1)
        zero = jnp.zeros_like(v)
        g0 = (jnp.where(lo, v, zero), jnp.where(lo, zero, rolled))
        g1 = (jnp.where(lo, rolled, zero), jnp.where(lo, zero, v))
        return g0, g1

    row = i * tm + lax.broadcasted_iota(jnp.int32, (tm, LANES), 0)
    onehot = jnp.where(row // SLC_LEN == lane, 1.0, 0.0).astype(BF16)
    ks = dup(_half_rmsnorm(kvs_ref[:, 0:LANES], kg_ref[1:2, :], lo))
    vs = lohi(kvs_ref[:, LANES:2 * LANES])
    kw = dup(_half_rmsnorm(kvw_ref[:, 0:LANES], kg_ref[2:3, :], lo))
    vw = lohi(kvw_ref[:, LANES:2 * LANES])
    for g in range(NSA_KV_HEADS):
        kaug_ref[g, :, 0:LANES] = onehot
        kaug_ref[g, :, LANES:2 * LANES] = ks[g].astype(BF16)
        vs_ref[g, :, 0:LANES] = vs[g][0].astype(BF16)
        vs_ref[g, :, LANES:2 * LANES] = vs[g][1].astype(BF16)
        kw_ref[g] = kw[g].astype(BF16)
        vw_ref[g, :, 0:LANES] = vw[g][0].astype(BF16)
        vw_ref[g, :, LANES:2 * LANES] = vw[g][1].astype(BF16)
    gate_ref[...] = _sigmoid(tail_ref[...])


def nsa_prep(proj, q_gain, k_gain, tm=512):
    T = proj.shape[0]
    qg2 = jnp.tile(q_gain.reshape(1, NSA_HD), (1, 2))
    kg2 = jnp.tile(k_gain.reshape(3, NSA_HD), (1, 2))
    G = NSA_KV_HEADS
    out_shape = (
        jax.ShapeDtypeStruct((T, 512), BF16),
        jax.ShapeDtypeStruct((G, T, 256), BF16),
        jax.ShapeDtypeStruct((G, T, 256), BF16),
        jax.ShapeDtypeStruct((G, T, 128), BF16),
        jax.ShapeDtypeStruct((G, T, 256), BF16),
        jax.ShapeDtypeStruct((T, 128), F32),
    )
    return pl.pallas_call(
        functools.partial(_nsa_prep_kernel, tm=tm),
        grid=(T // tm,),
        in_specs=[
            pl.BlockSpec((tm, 512), lambda i: (i, C_NSA_Q // 512)),
            pl.BlockSpec((tm, 256), lambda i: (i, C_NSA_KV // 256)),
            pl.BlockSpec((tm, 256), lambda i: (i, C_NSA_KV // 256 + 1)),
            pl.BlockSpec((tm, 256), lambda i: (i, C_NSA_KV // 256 + 2)),
            pl.BlockSpec((tm, 128), lambda i: (i, C_TAIL // 128)),
            pl.BlockSpec((1, 128), lambda i: (0, 0)),
            pl.BlockSpec((3, 128), lambda i: (0, 0)),
        ],
        out_specs=(
            pl.BlockSpec((tm, 512), lambda i: (i, 0)),
            pl.BlockSpec((G, tm, 256), lambda i: (0, i, 0)),
            pl.BlockSpec((G, tm, 256), lambda i: (0, i, 0)),
            pl.BlockSpec((G, tm, 128), lambda i: (0, i, 0)),
            pl.BlockSpec((G, tm, 256), lambda i: (0, i, 0)),
            pl.BlockSpec((tm, 128), lambda i: (i, 0)),
        ),
        out_shape=out_shape,
        compiler_params=_cparams(("parallel",)),
        name="nsa_prep",
    )(proj, proj, proj, proj, proj, qg2, kg2)


def _gelu_tanh(x):
    c = math.sqrt(2.0 / math.pi)
    return 0.5 * x * (1.0 + jnp.tanh(c * (x + 0.044715 * (x * x * x))))


def _compress_kernel(a_ref, pe_ref, w1_ref, w2_ref, kg_ref, o_ref, *, nb):
    kind = pl.program_id(0)
    a_lo = a_ref[0, 0, 0:nb, :]
    a_hi = a_ref[0, 0, 1:nb + 1, :]
    blocks = jnp.concatenate([a_lo, a_hi], axis=1) + pe_ref[0]
    h1 = jnp.dot(blocks.astype(BF16), w1_ref[0].astype(BF16), preferred_element_type=F32)
    y = jnp.dot(_gelu_tanh(h1).astype(BF16), w2_ref[0].astype(BF16), preferred_element_type=F32)
    ms = jnp.mean(y * y, axis=-1, keepdims=True)
    yn = y * lax.rsqrt(ms + EPS) * kg_ref[...]
    o_ref[0, 0] = jnp.where(kind == 0, yn, y)


def nsa_compress(kv_cmp, cmp_pe, cmp_w1, cmp_w2, k_gain0):
    T = kv_cmp.shape[0]
    nb = T // CMP_STRIDE
    G = NSA_KV_HEADS
    a = kv_cmp.reshape(T, 2, G, NSA_HD).transpose(1, 2, 0, 3).reshape(2, G, nb, CMP_STRIDE * NSA_HD)
    a = jnp.pad(a, ((0, 0), (0, 0), (0, 8), (0, 0)))
    pe = cmp_pe.reshape(2, 1, CMP_LEN * NSA_HD)
    return pl.pallas_call(
        functools.partial(_compress_kernel, nb=nb),
        grid=(2, G),
        in_specs=[
            pl.BlockSpec((1, 1, nb + 8, CMP_STRIDE * NSA_HD), lambda k, g: (k, g, 0, 0)),
            pl.BlockSpec((1, 1, CMP_LEN * NSA_HD), lambda k, g: (k, 0, 0)),
            pl.BlockSpec((1, CMP_LEN * NSA_HD, NSA_HD), lambda k, g: (k, 0, 0)),
            pl.BlockSpec((1, NSA_HD, NSA_HD), lambda k, g: (k, 0, 0)),
            pl.BlockSpec((1, NSA_HD), lambda k, g: (0, 0)),
        ],
        out_specs=pl.BlockSpec((1, 1, nb, NSA_HD), lambda k, g: (k, g, 0, 0)),
        out_shape=jax.ShapeDtypeStruct((2, G, nb, NSA_HD), F32),
        compiler_params=_cparams(("arbitrary", "arbitrary")),
        name="nsa_compress",
    )(a, pe, cmp_w1, cmp_w2, k_gain0.reshape(1, NSA_HD))


def _slope(g, r):
    return jnp.where(g == 0, 2.0 ** -(r + 1), 2.0 ** -(NSA_REP + r + 1)).astype(F32)


def _gate_pair(gates, lane, lo, col_even):
    ge = jnp.sum(jnp.where(lane == col_even, gates, 0.0), axis=1, keepdims=True)
    go = jnp.sum(jnp.where(lane == col_even + 1, gates, 0.0), axis=1, keepdims=True)
    return jnp.where(lo, ge, go)


def _cmp_sel_kernel(q_ref, kc_ref, vc_ref, gate_ref, mt_ref, o_ref, sel_ref, *, tq, nb, ns_pad):
    g = pl.program_id(0)
    qi = pl.program_id(1)
    t0 = qi * tq
    lane = lax.broadcasted_iota(jnp.int32, (tq, LANES), 1)
    lo = lane < NSA_HD
    n_idx = lax.broadcasted_iota(jnp.int32, (nb, tq), 0)
    t_idx = t0 + lax.broadcasted_iota(jnp.int32, (nb, tq), 1)
    dist = (t_idx - (n_idx * CMP_STRIDE + (CMP_LEN - 1))).astype(F32)
    vis = dist >= 0.0
    kc = kc_ref[0]
    imp = jnp.zeros((nb, tq), F32)
    gates = gate_ref[...]
    for a in range(2):
        acc = jnp.zeros((tq, LANES), F32)
        for half in range(2):
            r = 2 * a + half
            qh = q_ref[:, a * LANES:(a + 1) * LANES]
            qh = jnp.where(lo if half == 0 else jnp.logical_not(lo), qh, jnp.zeros_like(qh))
            s = lax.dot_general(kc, qh, NT, preferred_element_type=F32)
            s = jnp.where(vis, s - _slope(g, r) * dist, NEG)
            mx = jnp.max(s, axis=0, keepdims=True)
            e = jnp.where(vis, jnp.exp(s - mx), 0.0)
            den = jnp.sum(e, axis=0, keepdims=True)
            p = e * jnp.where(den > 0.0, 1.0 / den, 0.0)
            imp = imp + p
            v = vc_ref[0, :, half * LANES:(half + 1) * LANES]
            acc = acc + lax.dot_general(p.astype(BF16), v, TN, preferred_element_type=F32)
        gp = _gate_pair(gates, lane, lo, g * NSA_REP + 2 * a)
        o_ref[:, a * LANES:(a + 1) * LANES] = acc * gp

    score = jnp.dot(mt_ref[...], imp, precision=HIGHEST, preferred_element_type=F32)
    blk = lax.broadcasted_iota(jnp.int32, (ns_pad, tq), 0)
    cur = (t0 + lax.broadcasted_iota(jnp.int32, (ns_pad, tq), 1)) // SLC_LEN
    forced = (blk == 0) | (blk == cur) | (blk == cur - 1)
    score = jnp.where(forced, FORCE, score)
    score = jnp.where(blk <= cur, score, -1.0)
    sel = jnp.zeros((ns_pad, tq), F32)
    for _ in range(SLC_TOP):
        mx = jnp.max(score, axis=0, keepdims=True)
        idx = jnp.min(jnp.where(score == mx, blk, ns_pad), axis=0, keepdims=True)
        hit = blk == idx
        sel = jnp.where(hit, 1.0, sel)
        score = jnp.where(hit, -jnp.inf, score)
    bias_t = jnp.where((sel > 0.0) & (blk <= cur), 0.0, NEG)
    sel_ref[0] = jnp.transpose(bias_t).astype(BF16)


def _score_matrix(nb, ns_pad):
    ratio, span = SLC_LEN // CMP_STRIDE, CMP_LEN // CMP_STRIDE
    n_cmp = nb - 1
    m = np.zeros((ns_pad, nb), np.float32)
    for s in range(nb // ratio):
        for mm in range(ratio):
            for nn in range(span):
                c = ratio * s + mm - nn
                if 0 <= c < n_cmp:
                    m[s, c] += 1.0
    return jnp.asarray(m)


def nsa_cmp_select(qn, kc2, vc_lohi, gates, tq=256):
    T = qn.shape[0]
    nb = T // CMP_STRIDE
    ns_pad = LANES
    G = NSA_KV_HEADS
    return pl.pallas_call(
        functools.partial(_cmp_sel_kernel, tq=tq, nb=nb, ns_pad=ns_pad),
        grid=(G, T // tq),
        in_specs=[
            pl.BlockSpec((tq, 256), lambda g, i: (i, g)),
            pl.BlockSpec((1, nb, 128), lambda g, i: (g, 0, 0)),
            pl.BlockSpec((1, nb, 256), lambda g, i: (g, 0, 0)),
            pl.BlockSpec((tq, 128), lambda g, i: (i, 0)),
            pl.BlockSpec((ns_pad, nb), lambda g, i: (0, 0)),
        ],
        out_specs=(
            pl.BlockSpec((tq, 256), lambda g, i: (i, g)),
            pl.BlockSpec((1, tq, ns_pad), lambda g, i: (g, i, 0)),
        ),
        out_shape=(jax.ShapeDtypeStruct((T, 512), F32), jax.ShapeDtypeStruct((G, T, ns_pad), BF16)),
        compiler_params=_cparams(("parallel", "parallel")),
        name="nsa_cmp_select",
    )(qn, kc2, vc_lohi, gates, _score_matrix(nb, ns_pad))


def _sel_attn_kernel(qi_tab, kj_tab, first_tab, last_tab, q_ref, sb_ref, k_ref, v_ref, gate_ref, o_ref,
                     qaug, m_scr, l_scr, acc_scr, *, tq, tk):
    g = pl.program_id(0)
    s = pl.program_id(1)
    qi = qi_tab[s]
    kj = kj_tab[s]
    lane = lax.broadcasted_iota(jnp.int32, (tq, LANES), 1)
    lo = lane < NSA_HD

    @pl.when(first_tab[s] == 1)
    def _():
        sb = sb_ref[0]
        for r in range(NSA_REP):
            a, half = r // 2, r % 2
            qh = q_ref[:, a * LANES:(a + 1) * LANES]
            qh = jnp.where(lo if half == 0 else jnp.logical_not(lo), qh, jnp.zeros_like(qh))
            qaug[r * tq:(r + 1) * tq, 0:LANES] = sb
            qaug[r * tq:(r + 1) * tq, LANES:2 * LANES] = qh
        m_scr[...] = jnp.full(m_scr.shape, NEG, F32)
        l_scr[...] = jnp.zeros(l_scr.shape, F32)
        acc_scr[...] = jnp.zeros(acc_scr.shape, F32)

    sc = lax.dot_general(qaug[...], k_ref[0], NT, preferred_element_type=F32)
    qpos = qi * tq + lax.broadcasted_iota(jnp.int32, (tq, tk), 0)
    kpos = kj * tk + lax.broadcasted_iota(jnp.int32, (tq, tk), 1)
    dist = (qpos - kpos).astype(F32)
    causal = dist >= 0.0
    for a in range(2):
        pv = jnp.zeros((tq, LANES), F32)
        alphas = []
        for half in range(2):
            r = 2 * a + half
            sr = sc[r * tq:(r + 1) * tq, :]
            sr = jnp.where(causal, sr - _slope(g, r) * dist, NEG)
            m_prev = m_scr[r * tq:(r + 1) * tq, :]
            m_new = jnp.maximum(m_prev, jnp.max(sr, axis=1, keepdims=True))
            alpha = jnp.exp(m_prev - m_new)
            p = jnp.exp(sr - m_new)
            l_scr[r * tq:(r + 1) * tq, :] = alpha * l_scr[r * tq:(r + 1) * tq, :] + jnp.sum(p, axis=1, keepdims=True)
            m_scr[r * tq:(r + 1) * tq, :] = m_new
            v = v_ref[0, :, half * LANES:(half + 1) * LANES]
            pv = pv + jnp.dot(p.astype(BF16), v, preferred_element_type=F32)
            alphas.append(alpha)
        alpha_pair = jnp.where(lo, alphas[0], alphas[1])
        acc_scr[a * tq:(a + 1) * tq, :] = alpha_pair * acc_scr[a * tq:(a + 1) * tq, :] + pv

    @pl.when(last_tab[s] == 1)
    def _():
        gates = gate_ref[...]
        for a in range(2):
            l_pair = jnp.where(lo, l_scr[(2 * a) * tq:(2 * a + 1) * tq, :], l_scr[(2 * a + 1) * tq:(2 * a + 2) * tq, :])
            gp = _gate_pair(gates, lane, lo, NSA_HEADS + g * NSA_REP + 2 * a)
            o_ref[:, a * LANES:(a + 1) * LANES] = acc_scr[a * tq:(a + 1) * tq, :] / l_pair * gp


def nsa_selected_attention(qn, selbias, kaug, vs_lohi, gates, tq=256, tk=512):
    T = qn.shape[0]
    G = NSA_KV_HEADS
    qi_l, kj_l, first_l, last_l = [], [], [], []
    for qi in range(T // tq):
        nk = (qi * tq + tq - 1) // tk + 1
        for kj in range(nk):
            qi_l.append(qi)
            kj_l.append(kj)
            first_l.append(1 if kj == 0 else 0)
            last_l.append(1 if kj == nk - 1 else 0)
    tabs = [jnp.asarray(np.asarray(t, np.int32)) for t in (qi_l, kj_l, first_l, last_l)]
    nsteps = len(qi_l)
    grid_spec = pltpu.PrefetchScalarGridSpec(
        num_scalar_prefetch=4,
        grid=(G, nsteps),
        in_specs=[
            pl.BlockSpec((tq, 256), lambda g, s, qt, kt, ft, lt: (qt[s], g)),
            pl.BlockSpec((1, tq, 128), lambda g, s, qt, kt, ft, lt: (g, qt[s], 0)),
            pl.BlockSpec((1, tk, 256), lambda g, s, qt, kt, ft, lt: (g, kt[s], 0)),
            pl.BlockSpec((1, tk, 256), lambda g, s, qt, kt, ft, lt: (g, kt[s], 0)),
            pl.BlockSpec((tq, 128), lambda g, s, qt, kt, ft, lt: (qt[s], 0)),
        ],
        out_specs=pl.BlockSpec((tq, 256), lambda g, s, qt, kt, ft, lt: (qt[s], g)),
        scratch_shapes=[
            pltpu.VMEM((NSA_REP * tq, 256), BF16),
            pltpu.VMEM((NSA_REP * tq, 1), F32),
            pltpu.VMEM((NSA_REP * tq, 1), F32),
            pltpu.VMEM((2 * tq, LANES), F32),
        ],
    )
    return pl.pallas_call(
        functools.partial(_sel_attn_kernel, tq=tq, tk=tk),
        grid_spec=grid_spec,
        out_shape=jax.ShapeDtypeStruct((T, 512), F32),
        compiler_params=_cparams(("parallel", "arbitrary")),
        name="nsa_selected_attention",
    )(*tabs, qn, selbias, kaug, vs_lohi, gates)


def _win_attn_kernel(q_ref, k0_ref, k1_ref, k2_ref, v0_ref, v1_ref, v2_ref, gate_ref, o_ref, *, tq):
    g = pl.program_id(0)
    qi = pl.program_id(1)
    lane = lax.broadcasted_iota(jnp.int32, (tq, LANES), 1)
    lo = lane < NSA_HD
    qpos = qi * tq + lax.broadcasted_iota(jnp.int32, (tq, tq), 0)
    col = lax.broadcasted_iota(jnp.int32, (tq, tq), 1)
    k_refs = (k0_ref, k1_ref, k2_ref)
    v_refs = (v0_ref, v1_ref, v2_ref)
    dists, masks = [], []
    for d in range(3):
        kpos = (qi - 2 + d) * tq + col
        dd = qpos - kpos
        dists.append(dd.astype(F32))
        masks.append((dd >= 0) & (dd < WINDOW) & (kpos >= 0))
    gates = gate_ref[...]
    for a in range(2):
        pv = jnp.zeros((tq, LANES), F32)
        ls = []
        for half in range(2):
            r = 2 * a + half
            qh = q_ref[:, a * LANES:(a + 1) * LANES]
            qh = jnp.where(lo if half == 0 else jnp.logical_not(lo), qh, jnp.zeros_like(qh))
            ss = []
            for d in range(3):
                sd = lax.dot_general(qh, k_refs[d][0], NT, preferred_element_type=F32)
                ss.append(jnp.where(masks[d], sd - _slope(g, r) * dists[d], NEG))
            mx = jnp.maximum(jnp.maximum(jnp.max(ss[0], axis=1, keepdims=True), jnp.max(ss[1], axis=1, keepdims=True)),
                             jnp.max(ss[2], axis=1, keepdims=True))
            l = jnp.zeros((tq, 1), F32)
            for d in range(3):
                p = jnp.exp(ss[d] - mx)
                l = l + jnp.sum(p, axis=1, keepdims=True)
                v = v_refs[d][0, :, half * LANES:(half + 1) * LANES]
                pv = pv + jnp.dot(p.astype(BF16), v, preferred_element_type=F32)
            ls.append(l)
        l_pair = jnp.where(lo, ls[0], ls[1])
        gp = _gate_pair(gates, lane, lo, 2 * NSA_HEADS + g * NSA_REP + 2 * a)
        o_ref[:, a * LANES:(a + 1) * LANES] = pv / l_pair * gp


def nsa_window_attention(qn, kw2, vw_lohi, gates, tq=256):
    T = qn.shape[0]
    G = NSA_KV_HEADS
    assert WINDOW == 2 * tq

    def kspec(d, width):
        return pl.BlockSpec((1, tq, width), lambda g, i, d=d: (g, jnp.maximum(i - 2 + d, 0), 0))

    return pl.pallas_call(
        functools.partial(_win_attn_kernel, tq=tq),
        grid=(G, T // tq),
        in_specs=[pl.BlockSpec((tq, 256), lambda g, i: (i, g)),
                  kspec(0, 128), kspec(1, 128), kspec(2, 128),
                  kspec(0, 256), kspec(1, 256), kspec(2, 256),
                  pl.BlockSpec((tq, 128), lambda g, i: (i, 0))],
        out_specs=pl.BlockSpec((tq, 256), lambda g, i: (i, g)),
        out_shape=jax.ShapeDtypeStruct((T, 512), F32),
        compiler_params=_cparams(("parallel", "parallel")),
        name="nsa_window_attention",
    )(qn, kw2, kw2, kw2, vw_lohi, vw_lohi, vw_lohi, gates)


def nsa_mixer(proj, cmp_pe, cmp_w1, cmp_w2, q_gain, k_gain, parts=False):
    qn, kaug, vs_lohi, kw2, vw_lohi, gates = nsa_prep(proj, q_gain, k_gain)
    kvc = nsa_compress(proj[:, C_NSA_KV:C_NSA_KV + 256], cmp_pe, cmp_w1, cmp_w2, k_gain[0])
    kc, vc = kvc[0], kvc[1]
    kc2 = jnp.concatenate([kc, kc], axis=-1).astype(BF16)
    zero = jnp.zeros_like(vc)
    vc_lohi = jnp.concatenate([vc, zero, zero, vc], axis=-1).astype(BF16)
    o_cmp, selbias = nsa_cmp_select(qn, kc2, vc_lohi, gates)
    o_slc = nsa_selected_attention(qn, selbias, kaug, vs_lohi, gates)
    o_win = nsa_window_attention(qn, kw2, vw_lohi, gates)
    if parts:
        return o_cmp + o_slc + o_win, (o_cmp, o_slc, o_win)
    return o_cmp, o_slc, o_win


def _diag_selector():
    m = np.zeros((SUB * LANES, LANES), np.float32)
    for j in range(SUB):
        for rep in range(CHUNK // SUB):
            m[j * LANES:(j + 1) * LANES, rep * SUB + j] = 1.0
    return jnp.asarray(m, dtype=BF16)


def _recur_tile(q_scr, k_scr, v_scr, g_scr, o_scr, st_scr, sel_ref, tb):
    row = lax.broadcasted_iota(jnp.int32, (CHUNK, LANES), 0)
    lane = lax.broadcasted_iota(jnp.int32, (CHUNK, LANES), 1)
    sub_row = row % SUB
    blockdiag = (row // SUB) == (lane // SUB)
    r64 = lax.broadcasted_iota(jnp.int32, (CHUNK, CHUNK), 0)
    c64 = lax.broadcasted_iota(jnp.int32, (CHUNK, CHUNK), 1)
    ltri = jnp.where(r64 >= c64, 1.0, 0.0).astype(F32)
    c_sub = lax.broadcasted_iota(jnp.int32, (SUB, CHUNK), 1)
    nsub = CHUNK // SUB

    def chunk(c, carry):
        r0 = pl.multiple_of(c * CHUNK, CHUNK)
        qc = q_scr[pl.ds(r0, CHUNK), :]
        kc = k_scr[pl.ds(r0, CHUNK), :]
        vc = v_scr[pl.ds(r0, CHUNK), :]
        gc = g_scr[pl.ds(r0, CHUNK), :]
        b = jnp.dot(ltri, gc, precision=HIGHEST, preferred_element_type=F32)
        bend = b[CHUNK - 1:CHUNK, :]
        st = st_scr[...]
        o = lax.dot_general((qc * jnp.exp(b)).astype(BF16), st.astype(BF16), NT, preferred_element_type=F32)
        k4 = kc.reshape(nsub, SUB, LANES)
        b4 = b.reshape(nsub, SUB, LANES)
        pieces = []
        for j in range(SUB):
            k_rep = jnp.broadcast_to(k4[:, j:j + 1, :], (nsub, SUB, LANES)).reshape(CHUNK, LANES)
            b_rep = jnp.broadcast_to(b4[:, j:j + 1, :], (nsub, SUB, LANES)).reshape(CHUNK, LANES)
            tri = sub_row >= j
            e = jnp.where(tri, qc * k_rep * jnp.exp(jnp.where(tri, b - b_rep, 0.0)), 0.0)
            pieces.append(e.astype(BF16))
        a_diag = jnp.dot(jnp.concatenate(pieces, axis=1), sel_ref[...], preferred_element_type=F32)
        a_diag = jnp.where(blockdiag, a_diag, 0.0)[:, 0:CHUNK]
        rows = [jnp.zeros((SUB, CHUNK), F32)]
        for i_sub in range(1, nsub):
            ref_b = b[i_sub * SUB - 1:i_sub * SUB, :]
            qt = qc[i_sub * SUB:(i_sub + 1) * SUB, :] * jnp.exp(b[i_sub * SUB:(i_sub + 1) * SUB, :] - ref_b)
            kt = kc * jnp.exp(jnp.minimum(ref_b - b, 0.0))
            a_i = lax.dot_general(qt.astype(BF16), kt.astype(BF16), NT, preferred_element_type=F32)
            rows.append(jnp.where(c_sub < i_sub * SUB, a_i, 0.0))
        a = jnp.concatenate(rows, axis=0) + a_diag
        o = o + jnp.dot(a.astype(BF16), vc.astype(BF16), preferred_element_type=F32)
        o_scr[pl.ds(r0, CHUNK), :] = o
        kend = kc * jnp.exp(bend - b)
        st_scr[...] = st * jnp.exp(bend) + lax.dot_general(vc.astype(BF16), kend.astype(BF16), TN,
                                                           preferred_element_type=F32)
        return carry

    lax.fori_loop(0, tb // CHUNK, chunk, 0)


def _finish_recur(o_scr, gain_ref, gate, o_ref):
    o = o_scr[...]
    ms = jnp.mean(o * o, axis=-1, keepdims=True)
    o_ref[...] = (o * lax.rsqrt(ms + EPS) * gain_ref[...] * gate).astype(o_ref.dtype)


def _hgrn2_kernel(q_ref, f_ref, i_ref, gg_ref, lb_ref, gain_ref, sel_ref, o_ref,
                  q_scr, k_scr, v_scr, g_scr, o_scr, st_scr, *, tb):
    @pl.when(pl.program_id(1) == 0)
    def _():
        st_scr[...] = jnp.zeros(st_scr.shape, F32)

    lb = lb_ref[...]
    z = f_ref[...]
    f = lb + (1.0 - lb) * _sigmoid(z)
    q_scr[...] = q_ref[...]
    k_scr[...] = (1.0 - lb) * _sigmoid(-z)
    v_scr[...] = i_ref[...]
    g_scr[...] = jnp.log(jnp.maximum(f, TINY))
    _recur_tile(q_scr, k_scr, v_scr, g_scr, o_scr, st_scr, sel_ref, tb)
    _finish_recur(o_scr, gain_ref, _sigmoid(gg_ref[...]), o_ref)


def _recur_scratch(tb):
    return [pltpu.VMEM((tb, LANES), F32) for _ in range(5)] + [pltpu.VMEM((LANES, LANES), F32)]


def hgrn2_mixer(proj, lower_bound, norm_gain, tb=512):
    T = proj.shape[0]
    cb = C_HG // LANES

    def col(k):
        return pl.BlockSpec((tb, LANES), lambda h, i, k=k: (i, cb + 4 * k + h))

    return pl.pallas_call(
        functools.partial(_hgrn2_kernel, tb=tb),
        grid=(HG_HEADS, T // tb),
        in_specs=[col(0), col(1), col(2), col(3),
                  pl.BlockSpec((1, LANES), lambda h, i: (0, h)),
                  pl.BlockSpec((1, LANES), lambda h, i: (0, 0)),
                  pl.BlockSpec((SUB * LANES, LANES), lambda h, i: (0, 0))],
        out_specs=pl.BlockSpec((tb, LANES), lambda h, i: (i, h)),
        out_shape=jax.ShapeDtypeStruct((T, BRANCH_WIDTH), BF16),
        scratch_shapes=_recur_scratch(tb),
        compiler_params=_cparams(("parallel", "arbitrary")),
        name="hgrn2_mixer",
    )(proj, proj, proj, proj, lower_bound.reshape(1, -1), norm_gain.reshape(1, -1), _diag_selector())


def _gla_kernel(q_ref, k_ref, v_ref, r_ref, tail_ref, wa_ref, ba_ref, gain_ref, sel_ref, o_ref,
                q_scr, k_scr, v_scr, g_scr, o_scr, st_scr, *, tb):
    h = pl.program_id(0)

    @pl.when(pl.program_id(1) == 0)
    def _():
        st_scr[...] = jnp.zeros(st_scr.shape, F32)

    lane = lax.broadcasted_iota(jnp.int32, (tb, LANES), 1)
    mine = (lane < GLA_DK) == (h % 2 == 0)
    a = jnp.dot(tail_ref[...], wa_ref[...], precision=HIGHEST, preferred_element_type=F32) + ba_ref[...]
    log_sig = -(jnp.maximum(-a, 0.0) + jnp.log1p(jnp.exp(-jnp.abs(a))))
    q_scr[...] = jnp.where(mine, q_ref[...] * (GLA_DK ** -0.5), 0.0)
    k_scr[...] = jnp.where(mine, k_ref[...], 0.0)
    v_scr[...] = v_ref[...]
    g_scr[...] = jnp.where(mine, log_sig * (1.0 / GLA_TAU), 0.0)
    _recur_tile(q_scr, k_scr, v_scr, g_scr, o_scr, st_scr, sel_ref, tb)
    r = r_ref[...]
    _finish_recur(o_scr, gain_ref, r * _sigmoid(r), o_ref)


def gla_mixer(proj, w_a2, b_a, norm_gain, tb=512):
    T = proj.shape[0]
    cqk = C_GLA_QK // LANES
    cvr = C_GLA_VR // LANES
    wa = jnp.zeros((LANES, GLA_HEADS * GLA_DK), F32).at[TAIL_GLA_A0:TAIL_GLA_A0 + GLA_RANK].set(w_a2)
    return pl.pallas_call(
        functools.partial(_gla_kernel, tb=tb),
        grid=(GLA_HEADS, T // tb),
        in_specs=[pl.BlockSpec((tb, LANES), lambda h, i: (i, cqk + h // 2)),
                  pl.BlockSpec((tb, LANES), lambda h, i: (i, cqk + 2 + h // 2)),
                  pl.BlockSpec((tb, LANES), lambda h, i: (i, cvr + h)),
                  pl.BlockSpec((tb, LANES), lambda h, i: (i, cvr + 4 + h)),
                  pl.BlockSpec((tb, LANES), lambda h, i: (i, C_TAIL // LANES)),
                  pl.BlockSpec((LANES, LANES), lambda h, i: (0, h // 2)),
                  pl.BlockSpec((1, LANES), lambda h, i: (0, h // 2)),
                  pl.BlockSpec((1, LANES), lambda h, i: (0, 0)),
                  pl.BlockSpec((SUB * LANES, LANES), lambda h, i: (0, 0))],
        out_specs=pl.BlockSpec((tb, LANES), lambda h, i: (i, h)),
        out_shape=jax.ShapeDtypeStruct((T, BRANCH_WIDTH), BF16),
        scratch_shapes=_recur_scratch(tb),
        compiler_params=_cparams(("parallel", "arbitrary")),
        name="gla_mixer",
    )(proj, proj, proj, proj, proj, wa, b_a.reshape(1, -1), norm_gain.reshape(1, -1), _diag_selector())


def _merge_kernel(oc_ref, os_ref, ow_ref, ob_ref, og_ref, wb_ref, g0_ref, g1_ref, g2_ref, o_ref):
    o_a = (oc_ref[...] + os_ref[...] + ow_ref[...]).astype(BF16)
    acc = _sigmoid(g0_ref[...]) * jnp.dot(o_a, wb_ref[0], preferred_element_type=F32)
    acc = acc + _sigmoid(g1_ref[...]) * jnp.dot(ob_ref[...], wb_ref[1], preferred_element_type=F32)
    acc = acc + _sigmoid(g2_ref[...]) * jnp.dot(og_ref[...], wb_ref[2], preferred_element_type=F32)
    o_ref[...] = acc.astype(o_ref.dtype)


def merge_branches(proj, o_cmp, o_slc, o_win, o_b, o_c, w_branch_bf16, tm=512, tn=512):
    T = proj.shape[0]
    W = BRANCH_WIDTH
    nj = D_MODEL // tn
    ospec = pl.BlockSpec((tm, W), lambda i, j: (i, 0))

    def gspec(n):
        return pl.BlockSpec((tm, tn), lambda i, j, n=n: (i, C_MERGE // tn + n * nj + j))

    return pl.pallas_call(
        _merge_kernel,
        grid=(T // tm, nj),
        in_specs=[ospec, ospec, ospec, ospec, ospec,
                  pl.BlockSpec((3, W, tn), lambda i, j: (0, 0, j)),
                  gspec(0), gspec(1), gspec(2)],
        out_specs=pl.BlockSpec((tm, tn), lambda i, j: (i, j)),
        out_shape=jax.ShapeDtypeStruct((T, D_MODEL), BF16),
        compiler_params=_cparams(("parallel", "arbitrary")),
        name="merge_branches",
    )(o_cmp, o_slc, o_win, o_b, o_c, w_branch_bf16, proj, proj, proj)


MOE_TILE = 256


def _route_kernel(x_ref, g_ref, wr_ref, br_ref, h_ref, route_ref, cnt_ref, carry, *, tm):
    i = pl.program_id(0)

    @pl.when(i == 0)
    def _():
        carry[...] = jnp.zeros(carry.shape, F32)

    x = x_ref[...]
    ms = jnp.mean(x * x, axis=-1, keepdims=True)
    h = x * lax.rsqrt(ms + EPS) * g_ref[...]
    h_ref[...] = h
    logits = jnp.dot(h, wr_ref[...], precision=HIGHEST, preferred_element_type=F32) + br_ref[...]
    lane = lax.broadcasted_iota(jnp.int32, (tm, LANES), 1)

    def masked_softmax(mask):
        l = jnp.where(mask, logits, NEG)
        e = jnp.where(mask, jnp.exp(l - jnp.max(l, axis=1, keepdims=True)), 0.0)
        return e / jnp.sum(e, axis=1, keepdims=True)

    def top1(prob, mask):
        p = jnp.max(jnp.where(mask, prob, -1.0), axis=1, keepdims=True)
        idx = jnp.min(jnp.where(mask & (prob == p), lane, LANES), axis=1, keepdims=True)
        return p, idx

    gmask = lane < N_GROUPS
    gw, gidx = top1(masked_softmax(gmask), gmask)
    emask = (lane >= N_GROUPS) & (lane < N_GROUPS + N_EXPERTS) & ((lane - N_GROUPS) // EXPERTS_PER_GROUP == gidx)
    eprob = masked_softmax(emask)
    p1, i1 = top1(eprob, emask)
    rest = emask & (lane != i1)
    p2, i2 = top1(eprob, rest)
    psum = p1 + p2
    w1 = gw * (p1 / psum)
    w2 = gw * (p2 / psum)
    e1 = i1 - N_GROUPS
    e2 = i2 - N_GROUPS

    onehot = jnp.where((lane == e1) | (lane == e2), 1.0, 0.0)
    r = lax.broadcasted_iota(jnp.int32, (tm, tm), 0)
    c = lax.broadcasted_iota(jnp.int32, (tm, tm), 1)
    strict = jnp.where(r > c, 1.0, 0.0).astype(BF16)
    before = jnp.dot(strict, onehot.astype(BF16), preferred_element_type=F32) + carry[0:1, :]
    rank1 = jnp.sum(jnp.where(lane == e1, before, 0.0), axis=1, keepdims=True)
    rank2 = jnp.sum(jnp.where(lane == e2, before, 0.0), axis=1, keepdims=True)
    total = carry[0:1, :] + jnp.sum(onehot, axis=0, keepdims=True)
    carry[...] = jnp.broadcast_to(total, carry.shape)
    cnt_ref[...] = jnp.broadcast_to(total, cnt_ref.shape)

    out = jnp.where(lane == 0, w1, 0.0)
    out = jnp.where(lane == 1, w2, out)
    out = jnp.where(lane == 2, e1.astype(F32), out)
    out = jnp.where(lane == 3, e2.astype(F32), out)
    out = jnp.where(lane == 4, rank1, out)
    out = jnp.where(lane == 5, rank2, out)
    route_ref[...] = out


def moe_route(x, gain, w_grp, b_grp, w_exp, b_exp, tm=512):
    T, D = x.shape
    wr = jnp.zeros((D, LANES), F32).at[:, 0:N_GROUPS].set(w_grp).at[:, N_GROUPS:N_GROUPS + N_EXPERTS].set(w_exp)
    br = jnp.zeros((1, LANES), F32).at[0, 0:N_GROUPS].set(b_grp).at[0, N_GROUPS:N_GROUPS + N_EXPERTS].set(b_exp)
    return pl.pallas_call(
        functools.partial(_route_kernel, tm=tm),
        grid=(T // tm,),
        in_specs=[pl.BlockSpec((tm, D), lambda i: (i, 0)),
                  pl.BlockSpec((1, D), lambda i: (0, 0)),
                  pl.BlockSpec((D, LANES), lambda i: (0, 0)),
                  pl.BlockSpec((1, LANES), lambda i: (0, 0))],
        out_specs=(pl.BlockSpec((tm, D), lambda i: (i, 0)),
                   pl.BlockSpec((tm, LANES), lambda i: (i, 0)),
                   pl.BlockSpec((8, LANES), lambda i: (0, 0))),
        out_shape=(jax.ShapeDtypeStruct((T, D), F32),
                   jax.ShapeDtypeStruct((T, LANES), F32),
                   jax.ShapeDtypeStruct((8, LANES), F32)),
        scratch_shapes=[pltpu.VMEM((8, LANES), F32)],
        compiler_params=_cparams(("arbitrary",)),
        name="moe_route",
    )(x, gain.reshape(1, D), wr, br)


def _row_copy(src_hbm, row, dst, slot, sem):
    return pltpu.make_async_copy(src_hbm.at[pl.ds(row, 1), :], dst.at[pl.ds(slot, 1), :], sem)


def _expert_kernel(tile_expert, row_token, n_used, h_hbm, wgu_ref, wd_ref, rw_ref, o_ref,
                   xbuf, wgu_bf, wd_bf, sem):
    i = pl.program_id(0)
    used = i < n_used[0]

    @pl.when(used)
    def _():
        base = i * MOE_TILE

        def issue(r, c):
            _row_copy(h_hbm, row_token[base + r], xbuf, r, sem).start()
            return c

        lax.fori_loop(0, MOE_TILE, issue, 0)

        prev = tile_expert[jnp.maximum(i - 1, 0)]

        @pl.when((i == 0) | (tile_expert[i] != prev))
        def _():
            wgu_bf[...] = wgu_ref[0].astype(BF16)
            wd_bf[...] = wd_ref[0].astype(BF16)

        def drain(r, c):
            _row_copy(h_hbm, 0, xbuf, r, sem).wait()
            return c

        lax.fori_loop(0, MOE_TILE, drain, 0)

        x = xbuf[...].astype(BF16)
        gu = jnp.dot(x, wgu_bf[...], preferred_element_type=F32)
        gate = gu[:, 0:D_FF_EXPERT]
        up = gu[:, D_FF_EXPERT:2 * D_FF_EXPERT]
        act = gate * _sigmoid(gate) * up * rw_ref[...]
        o_ref[...] = jnp.dot(act.astype(BF16), wd_bf[...], preferred_element_type=F32)

    @pl.when(jnp.logical_not(used))
    def _():
        o_ref[...] = jnp.zeros(o_ref.shape, F32)


def moe_experts(h, tile_expert, row_token, n_used, row_w, w_gate_up, w_down):
    T, D = h.shape
    R = row_token.shape[0]
    grid_spec = pltpu.PrefetchScalarGridSpec(
        num_scalar_prefetch=3,
        grid=(R // MOE_TILE,),
        in_specs=[
            pl.BlockSpec(memory_space=pl.ANY),
            pl.BlockSpec((1, D, 2 * D_FF_EXPERT), lambda i, te, rt, nu: (te[i], 0, 0)),
            pl.BlockSpec((1, D_FF_EXPERT, D), lambda i, te, rt, nu: (te[i], 0, 0)),
            pl.BlockSpec((MOE_TILE, 1), lambda i, te, rt, nu: (i, 0)),
        ],
        out_specs=pl.BlockSpec((MOE_TILE, D), lambda i, te, rt, nu: (i, 0)),
        scratch_shapes=[
            pltpu.VMEM((MOE_TILE, D), F32),
            pltpu.VMEM((D, 2 * D_FF_EXPERT), BF16),
            pltpu.VMEM((D_FF_EXPERT, D), BF16),
            pltpu.SemaphoreType.DMA(()),
        ],
    )
    return pl.pallas_call(
        _expert_kernel,
        grid_spec=grid_spec,
        out_shape=jax.ShapeDtypeStruct((R, D), F32),
        compiler_params=_cparams(("arbitrary",)),
        name="moe_experts",
    )(tile_expert, row_token, n_used, h, w_gate_up, w_down, row_w.reshape(R, 1))


def _combine_kernel(dest1, dest2, x_ref, y_hbm, o_ref, buf1, buf2, sem, *, tm):
    base = pl.program_id(0) * tm

    def issue(r, c):
        _row_copy(y_hbm, dest1[base + r], buf1, r, sem).start()
        _row_copy(y_hbm, dest2[base + r], buf2, r, sem).start()
        return c

    lax.fori_loop(0, tm, issue, 0)

    def drain(r, c):
        _row_copy(y_hbm, 0, buf1, r, sem).wait()
        _row_copy(y_hbm, 0, buf2, r, sem).wait()
        return c

    lax.fori_loop(0, tm, drain, 0)
    o_ref[...] = x_ref[...] + (buf1[...] + buf2[...])


def moe_combine(x, ys, dest1, dest2, tm=256):
    T, D = x.shape
    grid_spec = pltpu.PrefetchScalarGridSpec(
        num_scalar_prefetch=2,
        grid=(T // tm,),
        in_specs=[pl.BlockSpec((tm, D), lambda i, d1, d2: (i, 0)), pl.BlockSpec(memory_space=pl.ANY)],
        out_specs=pl.BlockSpec((tm, D), lambda i, d1, d2: (i, 0)),
        scratch_shapes=[pltpu.VMEM((tm, D), F32), pltpu.VMEM((tm, D), F32), pltpu.SemaphoreType.DMA(())],
    )
    return pl.pallas_call(
        functools.partial(_combine_kernel, tm=tm),
        grid_spec=grid_spec,
        out_shape=jax.ShapeDtypeStruct((T, D), F32),
        compiler_params=_cparams(("arbitrary",)),
        name="moe_combine",
    )(dest1, dest2, x, ys)


def hierarchical_moe(x, gain, w_grp, b_grp, w_exp, b_exp, w_gate_up, w_down):
    T, D = x.shape
    h, route, cnt = moe_route(x, gain, w_grp, b_grp, w_exp, b_exp)
    weights = route[:, 0:2]
    expert = route[:, 2:4].astype(jnp.int32)
    rank = route[:, 4:6].astype(jnp.int32)
    counts = cnt[0, 0:N_EXPERTS].astype(jnp.int32)
    padded = ((counts + MOE_TILE - 1) // MOE_TILE) * MOE_TILE
    ends = jnp.cumsum(padded)
    offs = ends - padded
    dest = offs[expert] + rank
    n_rows = 2 * T + N_EXPERTS * MOE_TILE
    n_tiles = n_rows // MOE_TILE
    flat = dest.reshape(-1)
    row_token = jnp.zeros((n_rows,), jnp.int32).at[flat].set(jnp.repeat(jnp.arange(T, dtype=jnp.int32), 2))
    row_w = jnp.zeros((n_rows,), F32).at[flat].set(weights.reshape(-1))
    n_used = (ends[-1] // MOE_TILE).astype(jnp.int32)
    tile_start = jnp.arange(n_tiles, dtype=jnp.int32) * MOE_TILE
    tile_expert = jnp.searchsorted(ends, tile_start, side="right").astype(jnp.int32)
    last_expert = tile_expert[jnp.maximum(n_used - 1, 0)]
    tile_expert = jnp.where(tile_start < ends[-1], tile_expert, last_expert)
    ys = moe_experts(h, tile_expert, row_token, n_used.reshape(1), row_w, w_gate_up, w_down)
    return moe_combine(x, ys, dest[:, 0], dest[:, 1])


def _reorder_w_in(w):
    def seg(name):
        o, s = _ORIG[name]
        return w[:, o:o + s]

    pad = jnp.zeros((w.shape[0], N_PROJ - IN_COLS), w.dtype)
    return jnp.concatenate([seg("merge_gate"), seg("nsa_q"), seg("nsa_kv"), seg("hg"), seg("gla_qk"),
                            seg("gla_vr"), seg("nsa_gate"), seg("gla_a"), pad], axis=1)


def kernel(x, norm_mix, w_in, cmp_pe, cmp_w1, cmp_w2, q_norm, k_norm, hg_lb_logits, hg_norm, gla_w_a2, gla_b_a,
           gla_norm, w_branch, w_out, norm_ffn, w_grp, b_grp, w_exp, b_exp, w_gate_up, w_down):
    B, T, D = x.shape
    assert B == 1 and D == D_MODEL
    xt = x[0]
    p_lb = jax.nn.softmax(hg_lb_logits.astype(F32), axis=0)
    lower_bounds = jnp.cumsum(p_lb, axis=0) - p_lb[0]
    for l in range(DEPTH):
        wp = _reorder_w_in(w_in[l]).astype(BF16)
        h = rmsnorm_bf16(xt, norm_mix[l])
        proj = matmul_bf16(h, wp, tm=2048, tn=512, name="in_proj")
        o_cmp, o_slc, o_win = nsa_mixer(proj, cmp_pe[l], cmp_w1[l], cmp_w2[l], q_norm[l], k_norm[l])
        o_b = hgrn2_mixer(proj, lower_bounds[l], hg_norm[l])
        o_c = gla_mixer(proj, gla_w_a2[l], gla_b_a[l], gla_norm[l])
        merged = merge_branches(proj, o_cmp, o_slc, o_win, o_b, o_c, w_branch[l].astype(BF16))
        xt = matmul_bf16(merged, w_out[l].astype(BF16), res=xt, tm=1024, tn=512, name="out_proj")
        xt = hierarchical_moe(xt, norm_ffn[l], w_grp[l], b_grp[l], w_exp[l], b_exp[l], w_gate_up[l], w_down[l])
    return xt[None]
```

```python
import functools
import math

import numpy as np
import jax
import jax.numpy as jnp
from jax import lax
from jax.experimental import pallas as pl
from jax.experimental.pallas import tpu as pltpu

F32 = jnp.float32
BF16 = jnp.bfloat16
HIGHEST = lax.Precision.HIGHEST

D_MODEL = 2048
DEPTH = 2
BRANCH_WIDTH = D_MODEL // 4
NSA_HEADS = 8
NSA_KV_HEADS = 2
NSA_REP = NSA_HEADS // NSA_KV_HEADS
NSA_HD = 64
CMP_LEN = 32
CMP_STRIDE = 16
SLC_LEN = 64
SLC_TOP = 16
WINDOW = 512
HG_HEADS = 4
GLA_HEADS = 4
GLA_DK = 64
GLA_RANK = 16
GLA_TAU = 16.0
CHUNK = 64
SUB = 16
N_GROUPS = 4
EXPERTS_PER_GROUP = 8
N_EXPERTS = N_GROUPS * EXPERTS_PER_GROUP
D_FF_EXPERT = D_MODEL // 4
EPS = 1e-6
NEG = -1e30
FORCE = 1e4
TINY = 1e-30

LANES = 128
SUBLANES = 8
VMEM_LIMIT = 56 * 1024 * 1024

W_NSA = (0, 1280)
W_REC = (1304, 3584)
W_MERGE = (4904, 6144)
W_GATE = (1280, 24)
W_GLA_A = (4888, 16)
C_NSA_KV = 512
C_GLA_QK = 2048
C_GLA_VR = 2560
TAIL_GLA_A0 = W_GATE[1]

NT = (((1,), (1,)), ((), ()))
TN = (((0,), (0,)), ((), ()))


def _cparams(sem):
    return pltpu.CompilerParams(dimension_semantics=sem, vmem_limit_bytes=VMEM_LIMIT)


def _sigmoid(x):
    return 1.0 / (1.0 + jnp.exp(-x))


def _norm_kernel(x_ref, g_ref, o_ref):
    x = x_ref[...]
    ms = jnp.mean(x * x, axis=-1, keepdims=True)
    o_ref[...] = (x * lax.rsqrt(ms + EPS) * g_ref[...]).astype(o_ref.dtype)


def rmsnorm_bf16(x, gain, tm=512):
    T, D = x.shape
    return pl.pallas_call(
        _norm_kernel,
        grid=(T // tm,),
        in_specs=[pl.BlockSpec((tm, D), lambda i: (i, 0)), pl.BlockSpec((1, D), lambda i: (0, 0))],
        out_specs=pl.BlockSpec((tm, D), lambda i: (i, 0)),
        out_shape=jax.ShapeDtypeStruct((T, D), BF16),
        compiler_params=_cparams(("parallel",)),
        name="rmsnorm_bf16",
    )(x, gain.reshape(1, D))


def _mm_kernel(a_ref, b_ref, o_ref):
    o_ref[...] = jnp.dot(a_ref[...], b_ref[...], preferred_element_type=F32)


def _mm_res_kernel(a_ref, b_ref, r_ref, o_ref):
    o_ref[...] = r_ref[...] + jnp.dot(a_ref[...], b_ref[...], preferred_element_type=F32)


def matmul_bf16(a, b, res=None, tm=1024, tn=512, name="matmul_bf16"):
    T, K = a.shape
    N = b.shape[1]
    in_specs = [pl.BlockSpec((tm, K), lambda i, j: (i, 0)), pl.BlockSpec((K, tn), lambda i, j: (0, j))]
    args = [a, b]
    kern = _mm_kernel
    if res is not None:
        in_specs.append(pl.BlockSpec((tm, tn), lambda i, j: (i, j)))
        args.append(res)
        kern = _mm_res_kernel
    return pl.pallas_call(
        kern,
        grid=(T // tm, N // tn),
        in_specs=in_specs,
        out_specs=pl.BlockSpec((tm, tn), lambda i, j: (i, j)),
        out_shape=jax.ShapeDtypeStruct((T, N), F32),
        compiler_params=_cparams(("parallel", "arbitrary")),
        name=name,
    )(*args)


def _half_rmsnorm(x, gain2, lo):
    x2 = x * x
    s_lo = jnp.sum(jnp.where(lo, x2, 0.0), axis=1, keepdims=True)
    s_hi = jnp.sum(jnp.where(lo, 0.0, x2), axis=1, keepdims=True)
    ms = jnp.where(lo, s_lo, s_hi) * (1.0 / NSA_HD)
    return x * lax.rsqrt(ms + EPS) * gain2


def _nsa_prep_kernel(q_ref, kvc_ref, kvs_ref, kvw_ref, tail_ref, qg_ref, kg_ref,
                     qn_ref, kaug_ref, vs_ref, kw_ref, vw_ref, gate_ref, *, tm):
    i = pl.program_id(0)
    lane = lax.broadcasted_iota(jnp.int32, (tm, LANES), 1)
    lo = lane < NSA_HD
    qg = qg_ref[...]
    for c in range(4):
        x = q_ref[:, c * LANES:(c + 1) * LANES]
        qn_ref[:, c * LANES:(c + 1) * LANES] = (_half_rmsnorm(x, qg, lo) * (NSA_HD ** -0.5)).astype(BF16)

    def dup(kn):
        rolled = pltpu.roll(kn, NSA_HD, axis=1)
        return jnp.where(lo, kn, rolled), jnp.where(lo, rolled, kn)

    def lohi(v):
        rolled = pltpu.roll(v, NSA_HD, axis=1)
        zero = jnp.zeros_like(v)
        g0 = (jnp.where(lo, v, zero), jnp.where(lo, zero, rolled))
        g1 = (jnp.where(lo, rolled, zero), jnp.where(lo, zero, v))
        return g0, g1

    row = i * tm + lax.broadcasted_iota(jnp.int32, (tm, LANES), 0)
    onehot = jnp.where(row // SLC_LEN == lane, 1.0, 0.0).astype(BF16)
    ks = dup(_half_rmsnorm(kvs_ref[:, 0:LANES], kg_ref[1:2, :], lo))
    vs = lohi(kvs_ref[:, LANES:2 * LANES])
    kw = dup(_half_rmsnorm(kvw_ref[:, 0:LANES], kg_ref[2:3, :], lo))
    vw = lohi(kvw_ref[:, LANES:2 * LANES])
    for g in range(NSA_KV_HEADS):
        kaug_ref[g, :, 0:LANES] = onehot
        kaug_ref[g, :, LANES:2 * LANES] = ks[g].astype(BF16)
        vs_ref[g, :, 0:LANES] = vs[g][0].astype(BF16)
        vs_ref[g, :, LANES:2 * LANES] = vs[g][1].astype(BF16)
        kw_ref[g] = kw[g].astype(BF16)
        vw_ref[g, :, 0:LANES] = vw[g][0].astype(BF16)
        vw_ref[g, :, LANES:2 * LANES] = vw[g][1].astype(BF16)
    gate_ref[...] = _sigmoid(tail_ref[...])


def nsa_prep(proj_n, proj_t, q_gain, k_gain, tm=512):
    T = proj_n.shape[0]
    qg2 = jnp.tile(q_gain.reshape(1, NSA_HD), (1, 2))
    kg2 = jnp.tile(k_gain.reshape(3, NSA_HD), (1, 2))
    G = NSA_KV_HEADS
    out_shape = (
        jax.ShapeDtypeStruct((T, 512), BF16),
        jax.ShapeDtypeStruct((G, T, 256), BF16),
        jax.ShapeDtypeStruct((G, T, 256), BF16),
        jax.ShapeDtypeStruct((G, T, 128), BF16),
        jax.ShapeDtypeStruct((G, T, 256), BF16),
        jax.ShapeDtypeStruct((T, 128), F32),
    )
    return pl.pallas_call(
        functools.partial(_nsa_prep_kernel, tm=tm),
        grid=(T // tm,),
        in_specs=[
            pl.BlockSpec((tm, 512), lambda i: (i, 0)),
            pl.BlockSpec((tm, 256), lambda i: (i, C_NSA_KV // 256)),
            pl.BlockSpec((tm, 256), lambda i: (i, C_NSA_KV // 256 + 1)),
            pl.BlockSpec((tm, 256), lambda i: (i, C_NSA_KV // 256 + 2)),
            pl.BlockSpec((tm, 128), lambda i: (i, 0)),
            pl.BlockSpec((1, 128), lambda i: (0, 0)),
            pl.BlockSpec((3, 128), lambda i: (0, 0)),
        ],
        out_specs=(
            pl.BlockSpec((tm, 512), lambda i: (i, 0)),
            pl.BlockSpec((G, tm, 256), lambda i: (0, i, 0)),
            pl.BlockSpec((G, tm, 256), lambda i: (0, i, 0)),
            pl.BlockSpec((G, tm, 128), lambda i: (0, i, 0)),
            pl.BlockSpec((G, tm, 256), lambda i: (0, i, 0)),
            pl.BlockSpec((tm, 128), lambda i: (i, 0)),
        ),
        out_shape=out_shape,
        compiler_params=_cparams(("parallel",)),
        name="nsa_prep",
    )(proj_n, proj_n, proj_n, proj_n, proj_t, qg2, kg2)


def _gelu_tanh(x):
    c = math.sqrt(2.0 / math.pi)
    return 0.5 * x * (1.0 + jnp.tanh(c * (x + 0.044715 * (x * x * x))))


def _compress_kernel(a_ref, pe_ref, w1_ref, w2_ref, kg_ref, o_ref, *, nb):
    kind = pl.program_id(0)
    a_lo = a_ref[0, 0, 0:nb, :]
    a_hi = a_ref[0, 0, 1:nb + 1, :]
    blocks = jnp.concatenate([a_lo, a_hi], axis=1) + pe_ref[0]
    h1 = jnp.dot(blocks.astype(BF16), w1_ref[0].astype(BF16), preferred_element_type=F32)
    y = jnp.dot(_gelu_tanh(h1).astype(BF16), w2_ref[0].astype(BF16), preferred_element_type=F32)
    ms = jnp.mean(y * y, axis=-1, keepdims=True)
    yn = y * lax.rsqrt(ms + EPS) * kg_ref[...]
    o_ref[0, 0] = jnp.where(kind == 0, yn, y)


def nsa_compress(kv_cmp, cmp_pe, cmp_w1, cmp_w2, k_gain0):
    T = kv_cmp.shape[0]
    nb = T // CMP_STRIDE
    G = NSA_KV_HEADS
    a = kv_cmp.reshape(T, 2, G, NSA_HD).transpose(1, 2, 0, 3).reshape(2, G, nb, CMP_STRIDE * NSA_HD)
    a = jnp.pad(a, ((0, 0), (0, 0), (0, 8), (0, 0)))
    pe = cmp_pe.reshape(2, 1, CMP_LEN * NSA_HD)
    return pl.pallas_call(
        functools.partial(_compress_kernel, nb=nb),
        grid=(2, G),
        in_specs=[
            pl.BlockSpec((1, 1, nb + 8, CMP_STRIDE * NSA_HD), lambda k, g: (k, g, 0, 0)),
            pl.BlockSpec((1, 1, CMP_LEN * NSA_HD), lambda k, g: (k, 0, 0)),
            pl.BlockSpec((1, CMP_LEN * NSA_HD, NSA_HD), lambda k, g: (k, 0, 0)),
            pl.BlockSpec((1, NSA_HD, NSA_HD), lambda k, g: (k, 0, 0)),
            pl.BlockSpec((1, NSA_HD), lambda k, g: (0, 0)),
        ],
        out_specs=pl.BlockSpec((1, 1, nb, NSA_HD), lambda k, g: (k, g, 0, 0)),
        out_shape=jax.ShapeDtypeStruct((2, G, nb, NSA_HD), F32),
        compiler_params=_cparams(("arbitrary", "arbitrary")),
        name="nsa_compress",
    )(a, pe, cmp_w1, cmp_w2, k_gain0.reshape(1, NSA_HD))


def _slope(g, r):
    return jnp.where(g == 0, 2.0 ** -(r + 1), 2.0 ** -(NSA_REP + r + 1)).astype(F32)


def _gate_pair(gates, lane, lo, col_even):
    ge = jnp.sum(jnp.where(lane == col_even, gates, 0.0), axis=1, keepdims=True)
    go = jnp.sum(jnp.where(lane == col_even + 1, gates, 0.0), axis=1, keepdims=True)
    return jnp.where(lo, ge, go)


def _cmp_sel_kernel(q_ref, kc_ref, vc_ref, gate_ref, mt_ref, o_ref, sel_ref, *, tq, nb, ns_pad):
    g = pl.program_id(0)
    qi = pl.program_id(1)
    t0 = qi * tq
    lane = lax.broadcasted_iota(jnp.int32, (tq, LANES), 1)
    lo = lane < NSA_HD
    n_idx = lax.broadcasted_iota(jnp.int32, (nb, tq), 0)
    t_idx = t0 + lax.broadcasted_iota(jnp.int32, (nb, tq), 1)
    dist = (t_idx - (n_idx * CMP_STRIDE + (CMP_LEN - 1))).astype(F32)
    vis = dist >= 0.0
    kc = kc_ref[0]
    imp = jnp.zeros((nb, tq), F32)
    gates = gate_ref[...]
    for a in range(2):
        acc = jnp.zeros((tq, LANES), F32)
        for half in range(2):
            r = 2 * a + half
            qh = q_ref[:, a * LANES:(a + 1) * LANES]
            qh = jnp.where(lo if half == 0 else jnp.logical_not(lo), qh, jnp.zeros_like(qh))
            s = lax.dot_general(kc, qh, NT, preferred_element_type=F32)
            s = jnp.where(vis, s - _slope(g, r) * dist, NEG)
            mx = jnp.max(s, axis=0, keepdims=True)
            e = jnp.where(vis, jnp.exp(s - mx), 0.0)
            den = jnp.sum(e, axis=0, keepdims=True)
            p = e * jnp.where(den > 0.0, 1.0 / den, 0.0)
            imp = imp + p
            v = vc_ref[0, :, half * LANES:(half + 1) * LANES]
            acc = acc + lax.dot_general(p.astype(BF16), v, TN, preferred_element_type=F32)
        gp = _gate_pair(gates, lane, lo, g * NSA_REP + 2 * a)
        o_ref[:, a * LANES:(a + 1) * LANES] = acc * gp

    score = jnp.dot(mt_ref[...], imp, precision=HIGHEST, preferred_element_type=F32)
    blk = lax.broadcasted_iota(jnp.int32, (ns_pad, tq), 0)
    cur = (t0 + lax.broadcasted_iota(jnp.int32, (ns_pad, tq), 1)) // SLC_LEN
    forced = (blk == 0) | (blk == cur) | (blk == cur - 1)
    score = jnp.where(forced, FORCE, score)
    score = jnp.where(blk <= cur, score, -1.0)
    sel = jnp.zeros((ns_pad, tq), F32)
    for _ in range(SLC_TOP):
        mx = jnp.max(score, axis=0, keepdims=True)
        idx = jnp.min(jnp.where(score == mx, blk, ns_pad), axis=0, keepdims=True)
        hit = blk == idx
        sel = jnp.where(hit, 1.0, sel)
        score = jnp.where(hit, -jnp.inf, score)
    bias_t = jnp.where((sel > 0.0) & (blk <= cur), 0.0, NEG)
    sel_ref[0] = jnp.transpose(bias_t).astype(BF16)


def _score_matrix(nb, ns_pad):
    ratio, span = SLC_LEN // CMP_STRIDE, CMP_LEN // CMP_STRIDE
    n_cmp = nb - 1
    m = np.zeros((ns_pad, nb), np.float32)
    for s in range(nb // ratio):
        for mm in range(ratio):
            for nn in range(span):
                c = ratio * s + mm - nn
                if 0 <= c < n_cmp:
                    m[s, c] += 1.0
    return jnp.asarray(m)


def nsa_cmp_select(qn, kc2, vc_lohi, gates, tq=256):
    T = qn.shape[0]
    nb = T // CMP_STRIDE
    ns_pad = LANES
    G = NSA_KV_HEADS
    return pl.pallas_call(
        functools.partial(_cmp_sel_kernel, tq=tq, nb=nb, ns_pad=ns_pad),
        grid=(G, T // tq),
        in_specs=[
            pl.BlockSpec((tq, 256), lambda g, i: (i, g)),
            pl.BlockSpec((1, nb, 128), lambda g, i: (g, 0, 0)),
            pl.BlockSpec((1, nb, 256), lambda g, i: (g, 0, 0)),
            pl.BlockSpec((tq, 128), lambda g, i: (i, 0)),
            pl.BlockSpec((ns_pad, nb), lambda g, i: (0, 0)),
        ],
        out_specs=(
            pl.BlockSpec((tq, 256), lambda g, i: (i, g)),
            pl.BlockSpec((1, tq, ns_pad), lambda g, i: (g, i, 0)),
        ),
        out_shape=(jax.ShapeDtypeStruct((T, 512), F32), jax.ShapeDtypeStruct((G, T, ns_pad), BF16)),
        compiler_params=_cparams(("parallel", "parallel")),
        name="nsa_cmp_select",
    )(qn, kc2, vc_lohi, gates, _score_matrix(nb, ns_pad))


def _sel_attn_kernel(qi_tab, kj_tab, first_tab, last_tab, q_ref, sb_ref, k_ref, v_ref, gate_ref, o_ref,
                     qaug, m_scr, l_scr, acc_scr, *, tq, tk):
    g = pl.program_id(0)
    s = pl.program_id(1)
    qi = qi_tab[s]
    kj = kj_tab[s]
    lane = lax.broadcasted_iota(jnp.int32, (tq, LANES), 1)
    lo = lane < NSA_HD

    @pl.when(first_tab[s] == 1)
    def _():
        sb = sb_ref[0]
        for r in range(NSA_REP):
            a, half = r // 2, r % 2
            qh = q_ref[:, a * LANES:(a + 1) * LANES]
            qh = jnp.where(lo if half == 0 else jnp.logical_not(lo), qh, jnp.zeros_like(qh))
            qaug[r * tq:(r + 1) * tq, 0:LANES] = sb
            qaug[r * tq:(r + 1) * tq, LANES:2 * LANES] = qh
        m_scr[...] = jnp.full(m_scr.shape, NEG, F32)
        l_scr[...] = jnp.zeros(l_scr.shape, F32)
        acc_scr[...] = jnp.zeros(acc_scr.shape, F32)

    sc = lax.dot_general(qaug[...], k_ref[0], NT, preferred_element_type=F32)
    qpos = qi * tq + lax.broadcasted_iota(jnp.int32, (tq, tk), 0)
    kpos = kj * tk + lax.broadcasted_iota(jnp.int32, (tq, tk), 1)
    dist = (qpos - kpos).astype(F32)
    causal = dist >= 0.0
    for a in range(2):
        pv = jnp.zeros((tq, LANES), F32)
        alphas = []
        for half in range(2):
            r = 2 * a + half
            sr = sc[r * tq:(r + 1) * tq, :]
            sr = jnp.where(causal, sr - _slope(g, r) * dist, NEG)
            m_prev = m_scr[r * tq:(r + 1) * tq, :]
            m_new = jnp.maximum(m_prev, jnp.max(sr, axis=1, keepdims=True))
            alpha = jnp.exp(m_prev - m_new)
            p = jnp.exp(sr - m_new)
            l_scr[r * tq:(r + 1) * tq, :] = alpha * l_scr[r * tq:(r + 1) * tq, :] + jnp.sum(p, axis=1, keepdims=True)
            m_scr[r * tq:(r + 1) * tq, :] = m_new
            v = v_ref[0, :, half * LANES:(half + 1) * LANES]
            pv = pv + jnp.dot(p.astype(BF16), v, preferred_element_type=F32)
            alphas.append(alpha)
        alpha_pair = jnp.where(lo, alphas[0], alphas[1])
        acc_scr[a * tq:(a + 1) * tq, :] = alpha_pair * acc_scr[a * tq:(a + 1) * tq, :] + pv

    @pl.when(last_tab[s] == 1)
    def _():
        gates = gate_ref[...]
        for a in range(2):
            l_pair = jnp.where(lo, l_scr[(2 * a) * tq:(2 * a + 1) * tq, :], l_scr[(2 * a + 1) * tq:(2 * a + 2) * tq, :])
            gp = _gate_pair(gates, lane, lo, NSA_HEADS + g * NSA_REP + 2 * a)
            o_ref[:, a * LANES:(a + 1) * LANES] = acc_scr[a * tq:(a + 1) * tq, :] / l_pair * gp


def nsa_selected_attention(qn, selbias, kaug, vs_lohi, gates, tq=256, tk=512):
    T = qn.shape[0]
    G = NSA_KV_HEADS
    qi_l, kj_l, first_l, last_l = [], [], [], []
    for qi in range(T // tq):
        nk = (qi * tq + tq - 1) // tk + 1
        for kj in range(nk):
            qi_l.append(qi)
            kj_l.append(kj)
            first_l.append(1 if kj == 0 else 0)
            last_l.append(1 if kj == nk - 1 else 0)
    tabs = [jnp.asarray(np.asarray(t, np.int32)) for t in (qi_l, kj_l, first_l, last_l)]
    nsteps = len(qi_l)
    grid_spec = pltpu.PrefetchScalarGridSpec(
        num_scalar_prefetch=4,
        grid=(G, nsteps),
        in_specs=[
            pl.BlockSpec((tq, 256), lambda g, s, qt, kt, ft, lt: (qt[s], g)),
            pl.BlockSpec((1, tq, 128), lambda g, s, qt, kt, ft, lt: (g, qt[s], 0)),
            pl.BlockSpec((1, tk, 256), lambda g, s, qt, kt, ft, lt: (g, kt[s], 0)),
            pl.BlockSpec((1, tk, 256), lambda g, s, qt, kt, ft, lt: (g, kt[s], 0)),
            pl.BlockSpec((tq, 128), lambda g, s, qt, kt, ft, lt: (qt[s], 0)),
        ],
        out_specs=pl.BlockSpec((tq, 256), lambda g, s, qt, kt, ft, lt: (qt[s], g)),
        scratch_shapes=[
            pltpu.VMEM((NSA_REP * tq, 256), BF16),
            pltpu.VMEM((NSA_REP * tq, 1), F32),
            pltpu.VMEM((NSA_REP * tq, 1), F32),
            pltpu.VMEM((2 * tq, LANES), F32),
        ],
    )
    return pl.pallas_call(
        functools.partial(_sel_attn_kernel, tq=tq, tk=tk),
        grid_spec=grid_spec,
        out_shape=jax.ShapeDtypeStruct((T, 512), F32),
        compiler_params=_cparams(("parallel", "arbitrary")),
        name="nsa_selected_attention",
    )(*tabs, qn, selbias, kaug, vs_lohi, gates)


def _win_attn_kernel(q_ref, k0_ref, k1_ref, k2_ref, v0_ref, v1_ref, v2_ref, gate_ref, o_ref, *, tq):
    g = pl.program_id(0)
    qi = pl.program_id(1)
    lane = lax.broadcasted_iota(jnp.int32, (tq, LANES), 1)
    lo = lane < NSA_HD
    qpos = qi * tq + lax.broadcasted_iota(jnp.int32, (tq, tq), 0)
    col = lax.broadcasted_iota(jnp.int32, (tq, tq), 1)
    k_refs = (k0_ref, k1_ref, k2_ref)
    v_refs = (v0_ref, v1_ref, v2_ref)
    dists, masks = [], []
    for d in range(3):
        kpos = (qi - 2 + d) * tq + col
        dd = qpos - kpos
        dists.append(dd.astype(F32))
        masks.append((dd >= 0) & (dd < WINDOW) & (kpos >= 0))
    gates = gate_ref[...]
    for a in range(2):
        pv = jnp.zeros((tq, LANES), F32)
        ls = []
        for half in range(2):
            r = 2 * a + half
            qh = q_ref[:, a * LANES:(a + 1) * LANES]
            qh = jnp.where(lo if half == 0 else jnp.logical_not(lo), qh, jnp.zeros_like(qh))
            ss = []
            for d in range(3):
                sd = lax.dot_general(qh, k_refs[d][0], NT, preferred_element_type=F32)
                ss.append(jnp.where(masks[d], sd - _slope(g, r) * dists[d], NEG))
            mx = jnp.maximum(jnp.maximum(jnp.max(ss[0], axis=1, keepdims=True), jnp.max(ss[1], axis=1, keepdims=True)),
                             jnp.max(ss[2], axis=1, keepdims=True))
            l = jnp.zeros((tq, 1), F32)
            for d in range(3):
                p = jnp.exp(ss[d] - mx)
                l = l + jnp.sum(p, axis=1, keepdims=True)
                v = v_refs[d][0, :, half * LANES:(half + 1) * LANES]
                pv = pv + jnp.dot(p.astype(BF16), v, preferred_element_type=F32)
            ls.append(l)
        l_pair = jnp.where(lo, ls[0], ls[1])
        gp = _gate_pair(gates, lane, lo, 2 * NSA_HEADS + g * NSA_REP + 2 * a)
        o_ref[:, a * LANES:(a + 1) * LANES] = pv / l_pair * gp


def nsa_window_attention(qn, kw2, vw_lohi, gates, tq=256):
    T = qn.shape[0]
    G = NSA_KV_HEADS
    assert WINDOW == 2 * tq

    def kspec(d, width):
        return pl.BlockSpec((1, tq, width), lambda g, i, d=d: (g, jnp.maximum(i - 2 + d, 0), 0))

    return pl.pallas_call(
        functools.partial(_win_attn_kernel, tq=tq),
        grid=(G, T // tq),
        in_specs=[pl.BlockSpec((tq, 256), lambda g, i: (i, g)),
                  kspec(0, 128), kspec(1, 128), kspec(2, 128),
                  kspec(0, 256), kspec(1, 256), kspec(2, 256),
                  pl.BlockSpec((tq, 128), lambda g, i: (i, 0))],
        out_specs=pl.BlockSpec((tq, 256), lambda g, i: (i, g)),
        out_shape=jax.ShapeDtypeStruct((T, 512), F32),
        compiler_params=_cparams(("parallel", "parallel")),
        name="nsa_window_attention",
    )(qn, kw2, kw2, kw2, vw_lohi, vw_lohi, vw_lohi, gates)


def nsa_mixer(proj_n, proj_t, cmp_pe, cmp_w1, cmp_w2, q_gain, k_gain, parts=False):
    qn, kaug, vs_lohi, kw2, vw_lohi, gates = nsa_prep(proj_n, proj_t, q_gain, k_gain)
    kvc = nsa_compress(proj_n[:, C_NSA_KV:C_NSA_KV + 256], cmp_pe, cmp_w1, cmp_w2, k_gain[0])
    kc, vc = kvc[0], kvc[1]
    kc2 = jnp.concatenate([kc, kc], axis=-1).astype(BF16)
    zero = jnp.zeros_like(vc)
    vc_lohi = jnp.concatenate([vc, zero, zero, vc], axis=-1).astype(BF16)
    o_cmp, selbias = nsa_cmp_select(qn, kc2, vc_lohi, gates)
    o_slc = nsa_selected_attention(qn, selbias, kaug, vs_lohi, gates)
    o_win = nsa_window_attention(qn, kw2, vw_lohi, gates)
    if parts:
        return o_cmp + o_slc + o_win, (o_cmp, o_slc, o_win)
    return o_cmp, o_slc, o_win


def _diag_selector():
    m = np.zeros((SUB * LANES, LANES), np.float32)
    for j in range(SUB):
        for rep in range(CHUNK // SUB):
            m[j * LANES:(j + 1) * LANES, rep * SUB + j] = 1.0
    return jnp.asarray(m, dtype=BF16)


def _recur_tile(q_scr, k_scr, v_scr, g_scr, o_scr, st_scr, sel_ref, tb):
    row = lax.broadcasted_iota(jnp.int32, (CHUNK, LANES), 0)
    lane = lax.broadcasted_iota(jnp.int32, (CHUNK, LANES), 1)
    sub_row = row % SUB
    blockdiag = (row // SUB) == (lane // SUB)
    r64 = lax.broadcasted_iota(jnp.int32, (CHUNK, CHUNK), 0)
    c64 = lax.broadcasted_iota(jnp.int32, (CHUNK, CHUNK), 1)
    ltri = jnp.where(r64 >= c64, 1.0, 0.0).astype(F32)
    c_sub = lax.broadcasted_iota(jnp.int32, (SUB, CHUNK), 1)
    nsub = CHUNK // SUB

    def chunk(c, carry):
        r0 = pl.multiple_of(c * CHUNK, CHUNK)
        qc = q_scr[pl.ds(r0, CHUNK), :]
        kc = k_scr[pl.ds(r0, CHUNK), :]
        vc = v_scr[pl.ds(r0, CHUNK), :]
        gc = g_scr[pl.ds(r0, CHUNK), :]
        b = jnp.dot(ltri, gc, precision=HIGHEST, preferred_element_type=F32)
        bend = b[CHUNK - 1:CHUNK, :]
        st = st_scr[...]
        o = lax.dot_general((qc * jnp.exp(b)).astype(BF16), st.astype(BF16), NT, preferred_element_type=F32)
        k4 = kc.reshape(nsub, SUB, LANES)
        b4 = b.reshape(nsub, SUB, LANES)
        pieces = []
        for j in range(SUB):
            k_rep = jnp.broadcast_to(k4[:, j:j + 1, :], (nsub, SUB, LANES)).reshape(CHUNK, LANES)
            b_rep = jnp.broadcast_to(b4[:, j:j + 1, :], (nsub, SUB, LANES)).reshape(CHUNK, LANES)
            tri = sub_row >= j
            e = jnp.where(tri, qc * k_rep * jnp.exp(jnp.where(tri, b - b_rep, 0.0)), 0.0)
            pieces.append(e.astype(BF16))
        a_diag = jnp.dot(jnp.concatenate(pieces, axis=1), sel_ref[...], preferred_element_type=F32)
        a_diag = jnp.where(blockdiag, a_diag, 0.0)[:, 0:CHUNK]
        rows = [jnp.zeros((SUB, CHUNK), F32)]
        for i_sub in range(1, nsub):
            ref_b = b[i_sub * SUB - 1:i_sub * SUB, :]
            qt = qc[i_sub * SUB:(i_sub + 1) * SUB, :] * jnp.exp(b[i_sub * SUB:(i_sub + 1) * SUB, :] - ref_b)
            kt = kc * jnp.exp(jnp.minimum(ref_b - b, 0.0))
            a_i = lax.dot_general(qt.astype(BF16), kt.astype(BF16), NT, preferred_element_type=F32)
            rows.append(jnp.where(c_sub < i_sub * SUB, a_i, 0.0))
        a = jnp.concatenate(rows, axis=0) + a_diag
        o = o + jnp.dot(a.astype(BF16), vc.astype(BF16), preferred_element_type=F32)
        o_scr[pl.ds(r0, CHUNK), :] = o
        kend = kc * jnp.exp(bend - b)
        st_scr[...] = st * jnp.exp(bend) + lax.dot_general(vc.astype(BF16), kend.astype(BF16), TN,
                                                           preferred_element_type=F32)
        return carry

    lax.fori_loop(0, tb // CHUNK, chunk, 0)


def _finish_recur(o_scr, gain_ref, gate, o_ref):
    o = o_scr[...]
    ms = jnp.mean(o * o, axis=-1, keepdims=True)
    o_ref[...] = (o * lax.rsqrt(ms + EPS) * gain_ref[...] * gate).astype(o_ref.dtype)


def _hgrn2_kernel(q_ref, f_ref, i_ref, gg_ref, lb_ref, gain_ref, sel_ref, o_ref,
                  q_scr, k_scr, v_scr, g_scr, o_scr, st_scr, *, tb):
    @pl.when(pl.program_id(1) == 0)
    def _():
        st_scr[...] = jnp.zeros(st_scr.shape, F32)

    lb = lb_ref[...]
    z = f_ref[...]
    f = lb + (1.0 - lb) * _sigmoid(z)
    q_scr[...] = q_ref[...]
    k_scr[...] = (1.0 - lb) * _sigmoid(-z)
    v_scr[...] = i_ref[...]
    g_scr[...] = jnp.log(jnp.maximum(f, TINY))
    _recur_tile(q_scr, k_scr, v_scr, g_scr, o_scr, st_scr, sel_ref, tb)
    _finish_recur(o_scr, gain_ref, _sigmoid(gg_ref[...]), o_ref)


def _recur_scratch(tb):
    return [pltpu.VMEM((tb, LANES), F32) for _ in range(5)] + [pltpu.VMEM((LANES, LANES), F32)]


def hgrn2_mixer(proj, lower_bound, norm_gain, tb=512):
    T = proj.shape[0]
    def col(k):
        return pl.BlockSpec((tb, LANES), lambda h, i, k=k: (i, 4 * k + h))

    return pl.pallas_call(
        functools.partial(_hgrn2_kernel, tb=tb),
        grid=(HG_HEADS, T // tb),
        in_specs=[col(0), col(1), col(2), col(3),
                  pl.BlockSpec((1, LANES), lambda h, i: (0, h)),
                  pl.BlockSpec((1, LANES), lambda h, i: (0, 0)),
                  pl.BlockSpec((SUB * LANES, LANES), lambda h, i: (0, 0))],
        out_specs=pl.BlockSpec((tb, LANES), lambda h, i: (i, h)),
        out_shape=jax.ShapeDtypeStruct((T, BRANCH_WIDTH), BF16),
        scratch_shapes=_recur_scratch(tb),
        compiler_params=_cparams(("parallel", "arbitrary")),
        name="hgrn2_mixer",
    )(proj, proj, proj, proj, lower_bound.reshape(1, -1), norm_gain.reshape(1, -1), _diag_selector())


def _gla_kernel(q_ref, k_ref, v_ref, r_ref, tail_ref, wa_ref, ba_ref, gain_ref, sel_ref, o_ref,
                q_scr, k_scr, v_scr, g_scr, o_scr, st_scr, *, tb):
    h = pl.program_id(0)

    @pl.when(pl.program_id(1) == 0)
    def _():
        st_scr[...] = jnp.zeros(st_scr.shape, F32)

    lane = lax.broadcasted_iota(jnp.int32, (tb, LANES), 1)
    mine = (lane < GLA_DK) == (h % 2 == 0)
    a = jnp.dot(tail_ref[...], wa_ref[...], precision=HIGHEST, preferred_element_type=F32) + ba_ref[...]
    log_sig = -(jnp.maximum(-a, 0.0) + jnp.log1p(jnp.exp(-jnp.abs(a))))
    q_scr[...] = jnp.where(mine, q_ref[...] * (GLA_DK ** -0.5), 0.0)
    k_scr[...] = jnp.where(mine, k_ref[...], 0.0)
    v_scr[...] = v_ref[...]
    g_scr[...] = jnp.where(mine, log_sig * (1.0 / GLA_TAU), 0.0)
    _recur_tile(q_scr, k_scr, v_scr, g_scr, o_scr, st_scr, sel_ref, tb)
    r = r_ref[...]
    _finish_recur(o_scr, gain_ref, r * _sigmoid(r), o_ref)


def gla_mixer(proj, proj_t, w_a2, b_a, norm_gain, tb=512):
    T = proj.shape[0]
    cqk = C_GLA_QK // LANES
    cvr = C_GLA_VR // LANES
    wa = jnp.zeros((LANES, GLA_HEADS * GLA_DK), F32).at[TAIL_GLA_A0:TAIL_GLA_A0 + GLA_RANK].set(w_a2)
    return pl.pallas_call(
        functools.partial(_gla_kernel, tb=tb),
        grid=(GLA_HEADS, T // tb),
        in_specs=[pl.BlockSpec((tb, LANES), lambda h, i: (i, cqk + h // 2)),
                  pl.BlockSpec((tb, LANES), lambda h, i: (i, cqk + 2 + h // 2)),
                  pl.BlockSpec((tb, LANES), lambda h, i: (i, cvr + h)),
                  pl.BlockSpec((tb, LANES), lambda h, i: (i, cvr + 4 + h)),
                  pl.BlockSpec((tb, LANES), lambda h, i: (i, 0)),
                  pl.BlockSpec((LANES, LANES), lambda h, i: (0, h // 2)),
                  pl.BlockSpec((1, LANES), lambda h, i: (0, h // 2)),
                  pl.BlockSpec((1, LANES), lambda h, i: (0, 0)),
                  pl.BlockSpec((SUB * LANES, LANES), lambda h, i: (0, 0))],
        out_specs=pl.BlockSpec((tb, LANES), lambda h, i: (i, h)),
        out_shape=jax.ShapeDtypeStruct((T, BRANCH_WIDTH), BF16),
        scratch_shapes=_recur_scratch(tb),
        compiler_params=_cparams(("parallel", "arbitrary")),
        name="gla_mixer",
    )(proj, proj, proj, proj, proj_t, wa, b_a.reshape(1, -1), norm_gain.reshape(1, -1), _diag_selector())


def _merge_kernel(oc_ref, os_ref, ow_ref, ob_ref, og_ref, wb_ref, g0_ref, g1_ref, g2_ref, o_ref):
    o_a = (oc_ref[...] + os_ref[...] + ow_ref[...]).astype(BF16)
    acc = _sigmoid(g0_ref[...]) * jnp.dot(o_a, wb_ref[0], preferred_element_type=F32)
    acc = acc + _sigmoid(g1_ref[...]) * jnp.dot(ob_ref[...], wb_ref[1], preferred_element_type=F32)
    acc = acc + _sigmoid(g2_ref[...]) * jnp.dot(og_ref[...], wb_ref[2], preferred_element_type=F32)
    o_ref[...] = acc.astype(o_ref.dtype)


def merge_branches(proj, o_cmp, o_slc, o_win, o_b, o_c, w_branch_bf16, tm=512, tn=512):
    T = proj.shape[0]
    W = BRANCH_WIDTH
    nj = D_MODEL // tn
    ospec = pl.BlockSpec((tm, W), lambda i, j: (i, 0))

    def gspec(n):
        return pl.BlockSpec((tm, tn), lambda i, j, n=n: (i, n * nj + j))

    return pl.pallas_call(
        _merge_kernel,
        grid=(T // tm, nj),
        in_specs=[ospec, ospec, ospec, ospec, ospec,
                  pl.BlockSpec((3, W, tn), lambda i, j: (0, 0, j)),
                  gspec(0), gspec(1), gspec(2)],
        out_specs=pl.BlockSpec((tm, tn), lambda i, j: (i, j)),
        out_shape=jax.ShapeDtypeStruct((T, D_MODEL), BF16),
        compiler_params=_cparams(("parallel", "arbitrary")),
        name="merge_branches",
    )(o_cmp, o_slc, o_win, o_b, o_c, w_branch_bf16, proj, proj, proj)


MOE_TILE = 256


def _route_kernel(x_ref, g_ref, wr_ref, br_ref, h_ref, route_ref, cnt_ref, carry, *, tm):
    i = pl.program_id(0)

    @pl.when(i == 0)
    def _():
        carry[...] = jnp.zeros(carry.shape, F32)

    x = x_ref[...]
    ms = jnp.mean(x * x, axis=-1, keepdims=True)
    h = x * lax.rsqrt(ms + EPS) * g_ref[...]
    h_ref[...] = h
    logits = jnp.dot(h, wr_ref[...], precision=HIGHEST, preferred_element_type=F32) + br_ref[...]
    lane = lax.broadcasted_iota(jnp.int32, (tm, LANES), 1)

    def masked_softmax(mask):
        l = jnp.where(mask, logits, NEG)
        e = jnp.where(mask, jnp.exp(l - jnp.max(l, axis=1, keepdims=True)), 0.0)
        return e / jnp.sum(e, axis=1, keepdims=True)

    def top1(prob, mask):
        p = jnp.max(jnp.where(mask, prob, -1.0), axis=1, keepdims=True)
        idx = jnp.min(jnp.where(mask & (prob == p), lane, LANES), axis=1, keepdims=True)
        return p, idx

    gmask = lane < N_GROUPS
    gw, gidx = top1(masked_softmax(gmask), gmask)
    emask = (lane >= N_GROUPS) & (lane < N_GROUPS + N_EXPERTS) & ((lane - N_GROUPS) // EXPERTS_PER_GROUP == gidx)
    eprob = masked_softmax(emask)
    p1, i1 = top1(eprob, emask)
    rest = emask & (lane != i1)
    p2, i2 = top1(eprob, rest)
    psum = p1 + p2
    w1 = gw * (p1 / psum)
    w2 = gw * (p2 / psum)
    e1 = i1 - N_GROUPS
    e2 = i2 - N_GROUPS

    onehot = jnp.where((lane == e1) | (lane == e2), 1.0, 0.0)
    r = lax.broadcasted_iota(jnp.int32, (tm, tm), 0)
    c = lax.broadcasted_iota(jnp.int32, (tm, tm), 1)
    strict = jnp.where(r > c, 1.0, 0.0).astype(BF16)
    before = jnp.dot(strict, onehot.astype(BF16), preferred_element_type=F32) + carry[0:1, :]
    rank1 = jnp.sum(jnp.where(lane == e1, before, 0.0), axis=1, keepdims=True)
    rank2 = jnp.sum(jnp.where(lane == e2, before, 0.0), axis=1, keepdims=True)
    total = carry[0:1, :] + jnp.sum(onehot, axis=0, keepdims=True)
    carry[...] = jnp.broadcast_to(total, carry.shape)
    cnt_ref[...] = jnp.broadcast_to(total, cnt_ref.shape)

    out = jnp.where(lane == 0, w1, 0.0)
    out = jnp.where(lane == 1, w2, out)
    out = jnp.where(lane == 2, e1.astype(F32), out)
    out = jnp.where(lane == 3, e2.astype(F32), out)
    out = jnp.where(lane == 4, rank1, out)
    out = jnp.where(lane == 5, rank2, out)
    route_ref[...] = out


def moe_route(x, gain, w_grp, b_grp, w_exp, b_exp, tm=512):
    T, D = x.shape
    wr = jnp.zeros((D, LANES), F32).at[:, 0:N_GROUPS].set(w_grp).at[:, N_GROUPS:N_GROUPS + N_EXPERTS].set(w_exp)
    br = jnp.zeros((1, LANES), F32).at[0, 0:N_GROUPS].set(b_grp).at[0, N_GROUPS:N_GROUPS + N_EXPERTS].set(b_exp)
    return pl.pallas_call(
        functools.partial(_route_kernel, tm=tm),
        grid=(T // tm,),
        in_specs=[pl.BlockSpec((tm, D), lambda i: (i, 0)),
                  pl.BlockSpec((1, D), lambda i: (0, 0)),
                  pl.BlockSpec((D, LANES), lambda i: (0, 0)),
                  pl.BlockSpec((1, LANES), lambda i: (0, 0))],
        out_specs=(pl.BlockSpec((tm, D), lambda i: (i, 0)),
                   pl.BlockSpec((tm, LANES), lambda i: (i, 0)),
                   pl.BlockSpec((8, LANES), lambda i: (0, 0))),
        out_shape=(jax.ShapeDtypeStruct((T, D), F32),
                   jax.ShapeDtypeStruct((T, LANES), F32),
                   jax.ShapeDtypeStruct((8, LANES), F32)),
        scratch_shapes=[pltpu.VMEM((8, LANES), F32)],
        compiler_params=_cparams(("arbitrary",)),
        name="moe_route",
    )(x, gain.reshape(1, D), wr, br)


def _row_dma(src, src_row, dst, dst_row, sem):
    return pltpu.make_async_copy(src.at[pl.ds(src_row, 1), :], dst.at[pl.ds(dst_row, 1), :], sem)


def _rows_wait(src, dst, n, sem):
    pltpu.make_async_copy(src.at[pl.ds(0, n), :], dst.at[pl.ds(0, n), :], sem).wait()


PAD_BITS = MOE_TILE.bit_length() - 1


def _dispatch_kernel(pad_start, pad_len, tail_start, dest_ref, h_hbm, xs_hbm, zbuf, sem, zsem, *, tm):
    i = pl.program_id(0)
    n = pl.num_programs(0)

    def pad_copies(e, wait):
        start = pad_start[e]
        length = pad_len[e]
        head = jnp.minimum((-start) & (SUBLANES - 1), length)
        body = length - head

        def go(cp):
            if wait:
                cp.wait()
            else:
                cp.start()

        for k in range(SUBLANES - 1):
            @pl.when(k < head)
            def _():
                go(_row_dma(zbuf, 0, xs_hbm, start + k, zsem))

        for b in range(SUBLANES.bit_length() - 1, PAD_BITS):
            size = 1 << b
            off = pl.multiple_of(start + head + ((body >> (b + 1)) << (b + 1)), SUBLANES)

            @pl.when((body & size) != 0)
            def _():
                go(pltpu.make_async_copy(zbuf.at[pl.ds(0, size), :], xs_hbm.at[pl.ds(off, size), :], zsem))

    zrows = zbuf.shape[0]
    n_tail = xs_hbm.shape[0] // zrows

    def tail_copies(wait):
        def body(k, c):
            cp = pltpu.make_async_copy(zbuf, xs_hbm.at[pl.ds(pl.multiple_of(k * zrows, zrows), zrows), :], zsem)
            if wait:
                cp.wait()
            else:
                cp.start()
            return c

        lax.fori_loop(tail_start[0], n_tail, body, 0)

    @pl.when(i == 0)
    def _():
        zbuf[...] = jnp.zeros(zbuf.shape, F32)

        def start_e(e, c):
            pad_copies(e, False)
            return c

        lax.fori_loop(0, N_EXPERTS, start_e, 0)
        tail_copies(False)

    def issue(r, c):
        t = i * tm + r
        _row_dma(h_hbm, t, xs_hbm, dest_ref[0, 0, 2 * r], sem).start()
        _row_dma(h_hbm, t, xs_hbm, dest_ref[0, 0, 2 * r + 1], sem).start()
        return c

    lax.fori_loop(0, tm, issue, 0, unroll=8)

    @pl.when(i > 0)
    def _():
        _rows_wait(h_hbm, xs_hbm, 2 * tm, sem)

    @pl.when(i == n - 1)
    def _():
        _rows_wait(h_hbm, xs_hbm, 2 * tm, sem)

        def wait_e(e, c):
            pad_copies(e, True)
            return c

        lax.fori_loop(0, N_EXPERTS, wait_e, 0)
        tail_copies(True)


def moe_dispatch(h, dest, pad_start, pad_len, rows_used, n_rows, tm=256):
    T, D = h.shape
    zrows = MOE_TILE // 2
    tail_start = (rows_used // zrows).astype(jnp.int32).reshape(1)
    grid_spec = pltpu.PrefetchScalarGridSpec(
        num_scalar_prefetch=3,
        grid=(T // tm,),
        in_specs=[pl.BlockSpec((1, 1, 2 * tm), lambda i, ps, pln, ts: (i, 0, 0), memory_space=pltpu.SMEM),
                  pl.BlockSpec(memory_space=pl.ANY)],
        out_specs=pl.BlockSpec(memory_space=pl.ANY),
        scratch_shapes=[pltpu.VMEM((zrows, D), F32), pltpu.SemaphoreType.DMA(()),
                        pltpu.SemaphoreType.DMA(())],
    )
    return pl.pallas_call(
        functools.partial(_dispatch_kernel, tm=tm),
        grid_spec=grid_spec,
        out_shape=jax.ShapeDtypeStruct((n_rows, D), F32),
        compiler_params=_cparams(("arbitrary",)),
        name="moe_dispatch",
    )(pad_start, pad_len, tail_start, dest.reshape(T // tm, 1, 2 * tm), h)


def _expert_kernel(tile_expert, n_used, xs_ref, wgu_ref, wd_ref, o_ref, wgu_bf, wd_bf):
    i = pl.program_id(0)
    used = i < n_used[0]

    @pl.when(used)
    def _():
        prev = tile_expert[jnp.maximum(i - 1, 0)]

        @pl.when((i == 0) | (tile_expert[i] != prev))
        def _():
            wgu_bf[...] = wgu_ref[0, 0].astype(BF16)
            wd_bf[...] = wd_ref[0, 0].astype(BF16)

        x = xs_ref[...].astype(BF16)
        gu = jnp.dot(x, wgu_bf[...], preferred_element_type=F32)
        gate = gu[:, 0:D_FF_EXPERT]
        up = gu[:, D_FF_EXPERT:2 * D_FF_EXPERT]
        act = gate * _sigmoid(gate) * up
        o_ref[...] = jnp.dot(act.astype(BF16), wd_bf[...], preferred_element_type=F32)

    @pl.when(jnp.logical_not(used))
    def _():
        o_ref[...] = jnp.zeros(o_ref.shape, F32)


def moe_experts(xs, tile_expert, n_used, w_gate_up, w_down, layer):
    R, D = xs.shape
    grid_spec = pltpu.PrefetchScalarGridSpec(
        num_scalar_prefetch=2,
        grid=(R // MOE_TILE,),
        in_specs=[
            pl.BlockSpec((MOE_TILE, D), lambda i, te, nu: (jnp.minimum(i, nu[0] - 1), 0)),
            pl.BlockSpec((1, 1, D, 2 * D_FF_EXPERT), lambda i, te, nu: (layer, te[i], 0, 0)),
            pl.BlockSpec((1, 1, D_FF_EXPERT, D), lambda i, te, nu: (layer, te[i], 0, 0)),
        ],
        out_specs=pl.BlockSpec((MOE_TILE, D), lambda i, te, nu: (i, 0)),
        scratch_shapes=[
            pltpu.VMEM((D, 2 * D_FF_EXPERT), BF16),
            pltpu.VMEM((D_FF_EXPERT, D), BF16),
        ],
    )
    return pl.pallas_call(
        _expert_kernel,
        grid_spec=grid_spec,
        out_shape=jax.ShapeDtypeStruct((R, D), F32),
        compiler_params=_cparams(("arbitrary",)),
        name="moe_experts",
    )(tile_expert, n_used, xs, w_gate_up, w_down)


def _combine_kernel(dcur_ref, dnext_ref, x_ref, route_ref, y_hbm, o_ref, buf, sem, *, tm):
    i = pl.program_id(0)
    n = pl.num_programs(0)
    slot = i % 2

    def fetch(dref, s):
        def body(r, c):
            _row_dma(y_hbm, dref[0, 0, 2 * r], buf.at[s, 0], r, sem.at[s]).start()
            _row_dma(y_hbm, dref[0, 0, 2 * r + 1], buf.at[s, 1], r, sem.at[s]).start()
            return c

        lax.fori_loop(0, tm, body, 0, unroll=8)

    @pl.when(i == 0)
    def _():
        fetch(dcur_ref, 0)

    @pl.when(i + 1 < n)
    def _():
        fetch(dnext_ref, 1 - slot)

    _rows_wait(y_hbm, buf.at[slot, 0], tm, sem.at[slot])
    _rows_wait(y_hbm, buf.at[slot, 1], tm, sem.at[slot])
    route = route_ref[...]
    lane = lax.broadcasted_iota(jnp.int32, route.shape, 1)
    w1 = jnp.sum(jnp.where(lane == 0, route, 0.0), axis=1, keepdims=True)
    w2 = jnp.sum(jnp.where(lane == 1, route, 0.0), axis=1, keepdims=True)
    o_ref[...] = x_ref[...] + (w1 * buf[slot, 0] + w2 * buf[slot, 1])


def moe_combine(x, ys, route, dest, tm=256):
    T, D = x.shape
    n = T // tm
    dest_tiles = dest.reshape(n, 1, 2 * tm)
    return pl.pallas_call(
        functools.partial(_combine_kernel, tm=tm),
        grid=(n,),
        in_specs=[pl.BlockSpec((1, 1, 2 * tm), lambda i: (i, 0, 0), memory_space=pltpu.SMEM),
                  pl.BlockSpec((1, 1, 2 * tm), lambda i: (jnp.minimum(i + 1, n - 1), 0, 0), memory_space=pltpu.SMEM),
                  pl.BlockSpec((tm, D), lambda i: (i, 0)),
                  pl.BlockSpec((tm, LANES), lambda i: (i, 0)),
                  pl.BlockSpec(memory_space=pl.ANY)],
        out_specs=pl.BlockSpec((tm, D), lambda i: (i, 0)),
        out_shape=jax.ShapeDtypeStruct((T, D), F32),
        scratch_shapes=[pltpu.VMEM((2, 2, tm, D), F32), pltpu.SemaphoreType.DMA((2,))],
        compiler_params=_cparams(("arbitrary",)),
        name="moe_combine",
    )(dest_tiles, dest_tiles, x, route, ys)


def hierarchical_moe(x, gain, w_grp, b_grp, w_exp, b_exp, w_gate_up, w_down, layer):
    T, D = x.shape
    h, route, cnt = moe_route(x, gain, w_grp, b_grp, w_exp, b_exp)
    expert = route[:, 2:4].astype(jnp.int32)
    rank = route[:, 4:6].astype(jnp.int32)
    counts = cnt[0, 0:N_EXPERTS].astype(jnp.int32)
    padded = ((counts + MOE_TILE - 1) // MOE_TILE) * MOE_TILE
    ends = jnp.cumsum(padded)
    offs = ends - padded
    dest = offs[expert] + rank
    n_rows = 2 * T + N_EXPERTS * MOE_TILE
    n_tiles = n_rows // MOE_TILE
    n_used = (ends[-1] // MOE_TILE).astype(jnp.int32)
    tile_start = jnp.arange(n_tiles, dtype=jnp.int32) * MOE_TILE
    tile_expert = jnp.sum((ends[None, :] <= tile_start[:, None]).astype(jnp.int32), axis=1)
    last_expert = tile_expert[jnp.maximum(n_used - 1, 0)]
    tile_expert = jnp.where(tile_start < ends[-1], tile_expert, last_expert)
    xs = moe_dispatch(h, dest, offs + counts, padded - counts, ends[-1], n_rows)
    ys = moe_experts(xs, tile_expert, n_used.reshape(1), w_gate_up, w_down, layer)
    return moe_combine(x, ys, route, dest)


def _shift_cast_kernel(a_ref, b_ref, c_ref, o_ref, *, shift):
    blocks = [a_ref[0], b_ref[0], c_ref[0]]
    if shift:
        lane = lax.broadcasted_iota(jnp.int32, blocks[0].shape, 1)
        keep = lane < LANES - shift
        rolled = [pltpu.roll(b, LANES - shift, axis=1) for b in blocks]
        blocks = [jnp.where(keep, rolled[0], rolled[1]), jnp.where(keep, rolled[1], rolled[2])]
    o_ref[:, 0:LANES] = blocks[0].astype(o_ref.dtype)
    o_ref[:, LANES:2 * LANES] = blocks[1].astype(o_ref.dtype)


def shift_cast_columns(w, layer, col0, n_cols):
    _, D, n_in = w.shape
    b0, shift = divmod(col0, LANES)
    last = (n_in - 1) // LANES

    def spec(k):
        return pl.BlockSpec((1, D, LANES), lambda j, k=k: (layer, 0, jnp.minimum(b0 + 2 * j + k, last)))

    return pl.pallas_call(
        functools.partial(_shift_cast_kernel, shift=shift),
        grid=(n_cols // (2 * LANES),),
        in_specs=[spec(0), spec(1), spec(2)],
        out_specs=pl.BlockSpec((D, 2 * LANES), lambda j: (0, j)),
        out_shape=jax.ShapeDtypeStruct((D, n_cols), BF16),
        compiler_params=_cparams(("parallel",)),
        name="shift_cast_columns",
    )(w, w, w)


def kernel(x, norm_mix, w_in, cmp_pe, cmp_w1, cmp_w2, q_norm, k_norm, hg_lb_logits, hg_norm, gla_w_a2, gla_b_a,
           gla_norm, w_branch, w_out, norm_ffn, w_grp, b_grp, w_exp, b_exp, w_gate_up, w_down):
    B, T, D = x.shape
    assert B == 1 and D == D_MODEL
    xt = x[0]
    p_lb = jax.nn.softmax(hg_lb_logits.astype(F32), axis=0)
    lower_bounds = jnp.cumsum(p_lb, axis=0) - p_lb[0]
    for l in range(DEPTH):
        w_n = shift_cast_columns(w_in, l, *W_NSA)
        w_r = shift_cast_columns(w_in, l, *W_REC)
        w_m = shift_cast_columns(w_in, l, *W_MERGE)
        w_t = jnp.zeros((D, LANES), BF16)
        w_t = w_t.at[:, 0:W_GATE[1]].set(w_in[l, :, W_GATE[0]:W_GATE[0] + W_GATE[1]].astype(BF16))
        w_t = w_t.at[:, TAIL_GLA_A0:TAIL_GLA_A0 + W_GLA_A[1]].set(
            w_in[l, :, W_GLA_A[0]:W_GLA_A[0] + W_GLA_A[1]].astype(BF16))
        h = rmsnorm_bf16(xt, norm_mix[l])
        proj_n = matmul_bf16(h, w_n, tm=2048, tn=256, name="in_proj_nsa")
        proj_r = matmul_bf16(h, w_r, tm=2048, tn=512, name="in_proj_rec")
        proj_m = matmul_bf16(h, w_m, tm=2048, tn=512, name="in_proj_merge")
        proj_t = matmul_bf16(h, w_t, tm=2048, tn=LANES, name="in_proj_gates")
        o_cmp, o_slc, o_win = nsa_mixer(proj_n, proj_t, cmp_pe[l], cmp_w1[l], cmp_w2[l], q_norm[l], k_norm[l])
        o_b = hgrn2_mixer(proj_r, lower_bounds[l], hg_norm[l])
        o_c = gla_mixer(proj_r, proj_t, gla_w_a2[l], gla_b_a[l], gla_norm[l])
        merged = merge_branches(proj_m, o_cmp, o_slc, o_win, o_b, o_c, w_branch[l].astype(BF16))
        xt = matmul_bf16(merged, w_out[l].astype(BF16), res=xt, tm=1024, tn=512, name="out_proj")
        xt = hierarchical_moe(xt, norm_ffn[l], w_grp[l], b_grp[l], w_exp[l], b_exp[l], w_gate_up, w_down, l)
    return xt[None]
```

```python
import functools
import math

import numpy as np
import jax
import jax.numpy as jnp
from jax import lax
from jax.experimental import pallas as pl
from jax.experimental.pallas import tpu as pltpu

F32 = jnp.float32
BF16 = jnp.bfloat16
HIGHEST = lax.Precision.HIGHEST

D_MODEL = 2048
DEPTH = 2
BRANCH_WIDTH = D_MODEL // 4
NSA_HEADS = 8
NSA_KV_HEADS = 2
NSA_REP = NSA_HEADS // NSA_KV_HEADS
NSA_HD = 64
CMP_LEN = 32
CMP_STRIDE = 16
SLC_LEN = 64
SLC_TOP = 16
WINDOW = 512
HG_HEADS = 4
GLA_HEADS = 4
GLA_DK = 64
GLA_RANK = 16
GLA_TAU = 16.0
CHUNK = 64
SUB = 16
N_GROUPS = 4
EXPERTS_PER_GROUP = 8
N_EXPERTS = N_GROUPS * EXPERTS_PER_GROUP
D_FF_EXPERT = D_MODEL // 4
EPS = 1e-6
NEG = -1e30
FORCE = 1e4
TINY = 1e-30

LANES = 128
SUBLANES = 8
VMEM_LIMIT = 56 * 1024 * 1024

W_NSA = (0, 1280)
W_REC = (1304, 3584)
W_MERGE = (4904, 6144)
W_GATE = (1280, 24)
W_GLA_A = (4888, 16)
C_NSA_KV = 512
C_GLA_QK = 2048
C_GLA_VR = 2560
TAIL_GLA_A0 = W_GATE[1]

NT = (((1,), (1,)), ((), ()))
TN = (((0,), (0,)), ((), ()))


def _cparams(sem):
    return pltpu.CompilerParams(dimension_semantics=sem, vmem_limit_bytes=VMEM_LIMIT)


def _sigmoid(x):
    return 1.0 / (1.0 + jnp.exp(-x))


def _norm_kernel(x_ref, g_ref, o_ref):
    x = x_ref[...]
    ms = jnp.mean(x * x, axis=-1, keepdims=True)
    o_ref[...] = (x * lax.rsqrt(ms + EPS) * g_ref[...]).astype(o_ref.dtype)


def rmsnorm_bf16(x, gain, tm=512):
    T, D = x.shape
    return pl.pallas_call(
        _norm_kernel,
        grid=(T // tm,),
        in_specs=[pl.BlockSpec((tm, D), lambda i: (i, 0)), pl.BlockSpec((1, D), lambda i: (0, 0))],
        out_specs=pl.BlockSpec((tm, D), lambda i: (i, 0)),
        out_shape=jax.ShapeDtypeStruct((T, D), BF16),
        compiler_params=_cparams(("parallel",)),
        name="rmsnorm_bf16",
    )(x, gain.reshape(1, D))


def _mm_kernel(a_ref, b_ref, o_ref):
    o_ref[...] = jnp.dot(a_ref[...], b_ref[...], preferred_element_type=F32)


def _mm_res_kernel(a_ref, b_ref, r_ref, o_ref):
    o_ref[...] = r_ref[...] + jnp.dot(a_ref[...], b_ref[...], preferred_element_type=F32)


def matmul_bf16(a, b, res=None, tm=1024, tn=512, name="matmul_bf16"):
    T, K = a.shape
    N = b.shape[1]
    in_specs = [pl.BlockSpec((tm, K), lambda i, j: (i, 0)), pl.BlockSpec((K, tn), lambda i, j: (0, j))]
    args = [a, b]
    kern = _mm_kernel
    if res is not None:
        in_specs.append(pl.BlockSpec((tm, tn), lambda i, j: (i, j)))
        args.append(res)
        kern = _mm_res_kernel
    return pl.pallas_call(
        kern,
        grid=(T // tm, N // tn),
        in_specs=in_specs,
        out_specs=pl.BlockSpec((tm, tn), lambda i, j: (i, j)),
        out_shape=jax.ShapeDtypeStruct((T, N), F32),
        compiler_params=_cparams(("parallel", "arbitrary")),
        name=name,
    )(*args)


def _half_rmsnorm(x, gain2, lo):
    x2 = x * x
    s_lo = jnp.sum(jnp.where(lo, x2, 0.0), axis=1, keepdims=True)
    s_hi = jnp.sum(jnp.where(lo, 0.0, x2), axis=1, keepdims=True)
    ms = jnp.where(lo, s_lo, s_hi) * (1.0 / NSA_HD)
    return x * lax.rsqrt(ms + EPS) * gain2


def _nsa_prep_kernel(q_ref, kvc_ref, kvs_ref, kvw_ref, tail_ref, qg_ref, kg_ref,
                     qn_ref, kaug_ref, vs_ref, kw_ref, vw_ref, gate_ref, *, tm):
    i = pl.program_id(0)
    lane = lax.broadcasted_iota(jnp.int32, (tm, LANES), 1)
    lo = lane < NSA_HD
    qg = qg_ref[...]
    for c in range(4):
        x = q_ref[:, c * LANES:(c + 1) * LANES]
        qn_ref[:, c * LANES:(c + 1) * LANES] = (_half_rmsnorm(x, qg, lo) * (NSA_HD ** -0.5)).astype(BF16)

    def dup(kn):
        rolled = pltpu.roll(kn, NSA_HD, axis=1)
        return jnp.where(lo, kn, rolled), jnp.where(lo, rolled, kn)

    def lohi(v):
        rolled = pltpu.roll(v, NSA_HD, axis=1)
        zero = jnp.zeros_like(v)
        g0 = (jnp.where(lo, v, zero), jnp.where(lo, zero, rolled))
        g1 = (jnp.where(lo, rolled, zero), jnp.where(lo, zero, v))
        return g0, g1

    row = i * tm + lax.broadcasted_iota(jnp.int32, (tm, LANES), 0)
    onehot = jnp.where(row // SLC_LEN == lane, 1.0, 0.0).astype(BF16)
    k_extra = jnp.where(lane == NSA_HD, (row % SLC_LEN).astype(F32), 0.0)
    ks = dup(_half_rmsnorm(kvs_ref[:, 0:LANES], kg_ref[1:2, :], lo))
    vs = lohi(kvs_ref[:, LANES:2 * LANES])
    kw = dup(_half_rmsnorm(kvw_ref[:, 0:LANES], kg_ref[2:3, :], lo))
    vw = lohi(kvw_ref[:, LANES:2 * LANES])
    for g in range(NSA_KV_HEADS):
        kaug_ref[g, :, 0:LANES] = onehot
        kaug_ref[g, :, LANES:2 * LANES] = jnp.where(lo, ks[g], k_extra).astype(BF16)
        vs_ref[g, :, 0:LANES] = vs[g][0].astype(BF16)
        vs_ref[g, :, LANES:2 * LANES] = vs[g][1].astype(BF16)
        kw_ref[g] = kw[g].astype(BF16)
        vw_ref[g, :, 0:LANES] = vw[g][0].astype(BF16)
        vw_ref[g, :, LANES:2 * LANES] = vw[g][1].astype(BF16)
    gate_ref[...] = _sigmoid(tail_ref[...])


def nsa_prep(proj_n, proj_t, q_gain, k_gain, tm=512):
    T = proj_n.shape[0]
    qg2 = jnp.tile(q_gain.reshape(1, NSA_HD), (1, 2))
    kg2 = jnp.tile(k_gain.reshape(3, NSA_HD), (1, 2))
    G = NSA_KV_HEADS
    out_shape = (
        jax.ShapeDtypeStruct((T, 512), BF16),
        jax.ShapeDtypeStruct((G, T, 256), BF16),
        jax.ShapeDtypeStruct((G, T, 256), BF16),
        jax.ShapeDtypeStruct((G, T, 128), BF16),
        jax.ShapeDtypeStruct((G, T, 256), BF16),
        jax.ShapeDtypeStruct((T, 128), F32),
    )
    return pl.pallas_call(
        functools.partial(_nsa_prep_kernel, tm=tm),
        grid=(T // tm,),
        in_specs=[
            pl.BlockSpec((tm, 512), lambda i: (i, 0)),
            pl.BlockSpec((tm, 256), lambda i: (i, C_NSA_KV // 256)),
            pl.BlockSpec((tm, 256), lambda i: (i, C_NSA_KV // 256 + 1)),
            pl.BlockSpec((tm, 256), lambda i: (i, C_NSA_KV // 256 + 2)),
            pl.BlockSpec((tm, 128), lambda i: (i, 0)),
            pl.BlockSpec((1, 128), lambda i: (0, 0)),
            pl.BlockSpec((3, 128), lambda i: (0, 0)),
        ],
        out_specs=(
            pl.BlockSpec((tm, 512), lambda i: (i, 0)),
            pl.BlockSpec((G, tm, 256), lambda i: (0, i, 0)),
            pl.BlockSpec((G, tm, 256), lambda i: (0, i, 0)),
            pl.BlockSpec((G, tm, 128), lambda i: (0, i, 0)),
            pl.BlockSpec((G, tm, 256), lambda i: (0, i, 0)),
            pl.BlockSpec((tm, 128), lambda i: (i, 0)),
        ),
        out_shape=out_shape,
        compiler_params=_cparams(("parallel",)),
        name="nsa_prep",
    )(proj_n, proj_n, proj_n, proj_n, proj_t, qg2, kg2)


def _gelu_tanh(x):
    c = math.sqrt(2.0 / math.pi)
    return 0.5 * x * (1.0 + jnp.tanh(c * (x + 0.044715 * (x * x * x))))


def _compress_kernel(a_ref, pe_ref, w1_ref, w2_ref, kg_ref, o_ref, *, nb):
    kind = pl.program_id(0)
    a_lo = a_ref[0, 0, 0:nb, :]
    a_hi = a_ref[0, 0, 1:nb + 1, :]
    blocks = jnp.concatenate([a_lo, a_hi], axis=1) + pe_ref[0]
    h1 = jnp.dot(blocks.astype(BF16), w1_ref[0].astype(BF16), preferred_element_type=F32)
    y = jnp.dot(_gelu_tanh(h1).astype(BF16), w2_ref[0].astype(BF16), preferred_element_type=F32)
    ms = jnp.mean(y * y, axis=-1, keepdims=True)
    yn = y * lax.rsqrt(ms + EPS) * kg_ref[...]
    o_ref[0, 0] = jnp.where(kind == 0, yn, y)


def nsa_compress(kv_cmp, cmp_pe, cmp_w1, cmp_w2, k_gain0):
    T = kv_cmp.shape[0]
    nb = T // CMP_STRIDE
    G = NSA_KV_HEADS
    a = kv_cmp.reshape(T, 2, G, NSA_HD).transpose(1, 2, 0, 3).reshape(2, G, nb, CMP_STRIDE * NSA_HD)
    a = jnp.pad(a, ((0, 0), (0, 0), (0, 8), (0, 0)))
    pe = cmp_pe.reshape(2, 1, CMP_LEN * NSA_HD)
    return pl.pallas_call(
        functools.partial(_compress_kernel, nb=nb),
        grid=(2, G),
        in_specs=[
            pl.BlockSpec((1, 1, nb + 8, CMP_STRIDE * NSA_HD), lambda k, g: (k, g, 0, 0)),
            pl.BlockSpec((1, 1, CMP_LEN * NSA_HD), lambda k, g: (k, 0, 0)),
            pl.BlockSpec((1, CMP_LEN * NSA_HD, NSA_HD), lambda k, g: (k, 0, 0)),
            pl.BlockSpec((1, NSA_HD, NSA_HD), lambda k, g: (k, 0, 0)),
            pl.BlockSpec((1, NSA_HD), lambda k, g: (0, 0)),
        ],
        out_specs=pl.BlockSpec((1, 1, nb, NSA_HD), lambda k, g: (k, g, 0, 0)),
        out_shape=jax.ShapeDtypeStruct((2, G, nb, NSA_HD), F32),
        compiler_params=_cparams(("arbitrary", "arbitrary")),
        name="nsa_compress",
    )(a, pe, cmp_w1, cmp_w2, k_gain0.reshape(1, NSA_HD))


def _slope(g, r):
    return jnp.where(g == 0, 2.0 ** -(r + 1), 2.0 ** -(NSA_REP + r + 1)).astype(F32)


def _gate_pair(gates, lane, lo, col_even):
    ge = jnp.sum(jnp.where(lane == col_even, gates, 0.0), axis=1, keepdims=True)
    go = jnp.sum(jnp.where(lane == col_even + 1, gates, 0.0), axis=1, keepdims=True)
    return jnp.where(lo, ge, go)


def _cmp_sel_kernel(q_ref, kc_ref, vc_ref, gate_ref, mt_ref, o_ref, sel_ref, *, tq, nb, ns_pad):
    g = pl.program_id(0)
    qi = pl.program_id(1)
    t0 = qi * tq
    lane = lax.broadcasted_iota(jnp.int32, (tq, LANES), 1)
    lo = lane < NSA_HD
    n_idx = lax.broadcasted_iota(jnp.int32, (nb, tq), 0)
    t_idx = t0 + lax.broadcasted_iota(jnp.int32, (nb, tq), 1)
    dist = (t_idx - (n_idx * CMP_STRIDE + (CMP_LEN - 1))).astype(F32)
    vis = dist >= 0.0
    kc = kc_ref[0]
    imp = jnp.zeros((nb, tq), F32)
    gates = gate_ref[...]
    for a in range(2):
        acc = jnp.zeros((tq, LANES), F32)
        for half in range(2):
            r = 2 * a + half
            qh = q_ref[:, a * LANES:(a + 1) * LANES]
            qh = jnp.where(lo if half == 0 else jnp.logical_not(lo), qh, jnp.zeros_like(qh))
            s = lax.dot_general(kc, qh, NT, preferred_element_type=F32)
            s = jnp.where(vis, s - _slope(g, r) * dist, NEG)
            mx = jnp.max(s, axis=0, keepdims=True)
            e = jnp.where(vis, jnp.exp(s - mx), 0.0)
            den = jnp.sum(e, axis=0, keepdims=True)
            p = e * jnp.where(den > 0.0, 1.0 / den, 0.0)
            imp = imp + p
            v = vc_ref[0, :, half * LANES:(half + 1) * LANES]
            acc = acc + lax.dot_general(p.astype(BF16), v, TN, preferred_element_type=F32)
        gp = _gate_pair(gates, lane, lo, g * NSA_REP + 2 * a)
        o_ref[:, a * LANES:(a + 1) * LANES] = acc * gp

    score = jnp.dot(mt_ref[...], imp, precision=HIGHEST, preferred_element_type=F32)
    blk = lax.broadcasted_iota(jnp.int32, (ns_pad, tq), 0)
    cur = (t0 + lax.broadcasted_iota(jnp.int32, (ns_pad, tq), 1)) // SLC_LEN
    forced = (blk == 0) | (blk == cur) | (blk == cur - 1)
    score = jnp.where(forced, FORCE, score)
    score = jnp.where(blk <= cur, score, -1.0)
    sel = jnp.zeros((ns_pad, tq), F32)
    for _ in range(SLC_TOP):
        mx = jnp.max(score, axis=0, keepdims=True)
        idx = jnp.min(jnp.where(score == mx, blk, ns_pad), axis=0, keepdims=True)
        hit = blk == idx
        sel = jnp.where(hit, 1.0, sel)
        score = jnp.where(hit, -jnp.inf, score)
    bias_t = jnp.where((sel > 0.0) & (blk <= cur), 0.0, NEG)
    sel_ref[0] = jnp.transpose(bias_t).astype(BF16)


def _score_matrix(nb, ns_pad):
    ratio, span = SLC_LEN // CMP_STRIDE, CMP_LEN // CMP_STRIDE
    n_cmp = nb - 1
    m = np.zeros((ns_pad, nb), np.float32)
    for s in range(nb // ratio):
        for mm in range(ratio):
            for nn in range(span):
                c = ratio * s + mm - nn
                if 0 <= c < n_cmp:
                    m[s, c] += 1.0
    return jnp.asarray(m)


def nsa_cmp_select(qn, kc2, vc_lohi, gates, tq=256):
    T = qn.shape[0]
    nb = T // CMP_STRIDE
    ns_pad = LANES
    G = NSA_KV_HEADS
    return pl.pallas_call(
        functools.partial(_cmp_sel_kernel, tq=tq, nb=nb, ns_pad=ns_pad),
        grid=(G, T // tq),
        in_specs=[
            pl.BlockSpec((tq, 256), lambda g, i: (i, g)),
            pl.BlockSpec((1, nb, 128), lambda g, i: (g, 0, 0)),
            pl.BlockSpec((1, nb, 256), lambda g, i: (g, 0, 0)),
            pl.BlockSpec((tq, 128), lambda g, i: (i, 0)),
            pl.BlockSpec((ns_pad, nb), lambda g, i: (0, 0)),
        ],
        out_specs=(
            pl.BlockSpec((tq, 256), lambda g, i: (i, g)),
            pl.BlockSpec((1, tq, ns_pad), lambda g, i: (g, i, 0)),
        ),
        out_shape=(jax.ShapeDtypeStruct((T, 512), F32), jax.ShapeDtypeStruct((G, T, ns_pad), BF16)),
        compiler_params=_cparams(("parallel", "parallel")),
        name="nsa_cmp_select",
    )(qn, kc2, vc_lohi, gates, _score_matrix(nb, ns_pad))


def _sel_attn_kernel(qi_tab, kj_tab, first_tab, last_tab, live_tab, q_ref, sb_ref, k_ref, v_ref, gate_ref, o_ref,
                     qaug, m_scr, l_scr, acc_scr, *, tq, tk):
    g = pl.program_id(0)
    s = g * pl.num_programs(1) + pl.program_id(1)
    qi = qi_tab[s]
    kj = kj_tab[s]
    live = live_tab[s] == 1
    lane = lax.broadcasted_iota(jnp.int32, (tq, LANES), 1)
    lo = lane < NSA_HD

    @pl.when(first_tab[s] == 1)
    def _():
        sb = sb_ref[0].astype(F32)
        blk_rel = (lane - qi * (tq // SLC_LEN)).astype(F32)
        for r in range(NSA_REP):
            a, half = r // 2, r % 2
            slope = _slope(g, r)
            qh = q_ref[:, a * LANES:(a + 1) * LANES].astype(F32)
            if half:
                qh = pltpu.roll(qh, NSA_HD, axis=1)
            qh = jnp.where(lo, qh, jnp.where(lane == NSA_HD, slope, 0.0))
            qaug[r * tq:(r + 1) * tq, 0:LANES] = (sb + (slope * SLC_LEN) * blk_rel).astype(BF16)
            qaug[r * tq:(r + 1) * tq, LANES:2 * LANES] = qh.astype(BF16)
        m_scr[...] = jnp.full(m_scr.shape, NEG, F32)
        l_scr[...] = jnp.zeros(l_scr.shape, F32)
        acc_scr[...] = jnp.zeros(acc_scr.shape, F32)

    def step(masked):
        if masked:
            qpos = qi * tq + lax.broadcasted_iota(jnp.int32, (tq, tk), 0)
            kpos = kj * tk + lax.broadcasted_iota(jnp.int32, (tq, tk), 1)
            causal = kpos <= qpos
        k = k_ref[0]
        for a in range(2):
            pv = jnp.zeros((tq, LANES), F32)
            alphas = []
            for half in range(2):
                r = 2 * a + half
                sr = lax.dot_general(qaug[r * tq:(r + 1) * tq, :], k, NT, preferred_element_type=F32)
                if masked:
                    sr = jnp.where(causal, sr, NEG)
                m_prev = m_scr[r * tq:(r + 1) * tq, :]
                m_new = jnp.maximum(m_prev, jnp.max(sr, axis=1, keepdims=True))
                alpha = jnp.exp(m_prev - m_new)
                p = jnp.exp(sr - m_new)
                l_scr[r * tq:(r + 1) * tq, :] = (alpha * l_scr[r * tq:(r + 1) * tq, :]
                                                 + jnp.sum(p, axis=1, keepdims=True))
                m_scr[r * tq:(r + 1) * tq, :] = m_new
                v = v_ref[0, :, half * LANES:(half + 1) * LANES]
                pv = pv + jnp.dot(p.astype(BF16), v, preferred_element_type=F32)
                alphas.append(alpha)
            alpha_pair = jnp.where(lo, alphas[0], alphas[1])
            acc_scr[a * tq:(a + 1) * tq, :] = alpha_pair * acc_scr[a * tq:(a + 1) * tq, :] + pv

    on_diagonal = kj * tk + (tk - 1) > qi * tq

    @pl.when(live & on_diagonal)
    def _():
        step(True)

    @pl.when(live & jnp.logical_not(on_diagonal))
    def _():
        step(False)

    @pl.when(last_tab[s] == 1)
    def _():
        gates = gate_ref[...]
        for a in range(2):
            l_pair = jnp.where(lo, l_scr[(2 * a) * tq:(2 * a + 1) * tq, :], l_scr[(2 * a + 1) * tq:(2 * a + 2) * tq, :])
            gp = _gate_pair(gates, lane, lo, NSA_HEADS + g * NSA_REP + 2 * a)
            o_ref[:, a * LANES:(a + 1) * LANES] = acc_scr[a * tq:(a + 1) * tq, :] / l_pair * gp


def nsa_selected_attention(qn, selbias, kaug, vs_lohi, gates, tq=256, tk=512):
    T = qn.shape[0]
    G = NSA_KV_HEADS
    qi_l, kj_l, first_l, last_l = [], [], [], []
    for qi in range(T // tq):
        nk = (qi * tq + tq - 1) // tk + 1
        for kj in range(nk):
            qi_l.append(qi)
            kj_l.append(kj)
            first_l.append(1 if kj == 0 else 0)
            last_l.append(1 if kj == nk - 1 else 0)
    nsteps = len(qi_l)
    qi_c, kj_c, first_c, last_c = [np.asarray(t, np.int32) for t in (qi_l, kj_l, first_l, last_l)]
    nq, nkb = T // tq, tk // SLC_LEN
    chosen = (selbias.astype(F32) > 0.5 * NEG).reshape(G, nq, tq, LANES // nkb, nkb)
    tile_any = jnp.any(chosen, axis=(2, 4))
    active = tile_any[:, qi_c, kj_c] | (first_c == 1)[None, :] | (last_c == 1)[None, :]
    csum = jnp.cumsum(active.astype(jnp.int32), axis=1)
    n_active = csum[:, -1]
    k_idx = jnp.arange(nsteps, dtype=jnp.int32)
    src = jnp.sum((csum[:, None, :] <= k_idx[None, :, None]).astype(jnp.int32), axis=2)
    last_src = jnp.sum((csum < n_active[:, None]).astype(jnp.int32), axis=1)
    live = k_idx[None, :] < n_active[:, None]
    src = jnp.where(live, src, last_src[:, None])

    def table(values, mask_dead):
        t = jnp.asarray(values)[src]
        if mask_dead:
            t = jnp.where(live, t, 0)
        return t.reshape(-1).astype(jnp.int32)

    tabs = [table(qi_c, False), table(kj_c, False), table(first_c, True), table(last_c, True),
            live.reshape(-1).astype(jnp.int32)]

    def at(tab, g, s):
        return tab[g * nsteps + s]

    grid_spec = pltpu.PrefetchScalarGridSpec(
        num_scalar_prefetch=5,
        grid=(G, nsteps),
        in_specs=[
            pl.BlockSpec((tq, 256), lambda g, s, qt, kt, ft, lt, at_: (at(qt, g, s), g)),
            pl.BlockSpec((1, tq, 128), lambda g, s, qt, kt, ft, lt, at_: (g, at(qt, g, s), 0)),
            pl.BlockSpec((1, tk, 256), lambda g, s, qt, kt, ft, lt, at_: (g, at(kt, g, s), 0)),
            pl.BlockSpec((1, tk, 256), lambda g, s, qt, kt, ft, lt, at_: (g, at(kt, g, s), 0)),
            pl.BlockSpec((tq, 128), lambda g, s, qt, kt, ft, lt, at_: (at(qt, g, s), 0)),
        ],
        out_specs=pl.BlockSpec((tq, 256), lambda g, s, qt, kt, ft, lt, at_: (at(qt, g, s), g)),
        scratch_shapes=[
            pltpu.VMEM((NSA_REP * tq, 256), BF16),
            pltpu.VMEM((NSA_REP * tq, 1), F32),
            pltpu.VMEM((NSA_REP * tq, 1), F32),
            pltpu.VMEM((2 * tq, LANES), F32),
        ],
    )
    return pl.pallas_call(
        functools.partial(_sel_attn_kernel, tq=tq, tk=tk),
        grid_spec=grid_spec,
        out_shape=jax.ShapeDtypeStruct((T, 512), F32),
        compiler_params=_cparams(("parallel", "arbitrary")),
        name="nsa_selected_attention",
    )(*tabs, qn, selbias, kaug, vs_lohi, gates)


def _win_attn_kernel(q_ref, k0_ref, k1_ref, k2_ref, v0_ref, v1_ref, v2_ref, gate_ref, o_ref, *, tq):
    g = pl.program_id(0)
    qi = pl.program_id(1)
    lane = lax.broadcasted_iota(jnp.int32, (tq, LANES), 1)
    lo = lane < NSA_HD
    qpos = qi * tq + lax.broadcasted_iota(jnp.int32, (tq, tq), 0)
    col = lax.broadcasted_iota(jnp.int32, (tq, tq), 1)
    k_refs = (k0_ref, k1_ref, k2_ref)
    v_refs = (v0_ref, v1_ref, v2_ref)
    dists, masks = [], []
    for d in range(3):
        kpos = (qi - 2 + d) * tq + col
        dd = qpos - kpos
        dists.append(dd.astype(F32))
        masks.append((dd >= 0) & (dd < WINDOW) & (kpos >= 0))
    gates = gate_ref[...]
    for a in range(2):
        pv = jnp.zeros((tq, LANES), F32)
        ls = []
        for half in range(2):
            r = 2 * a + half
            qh = q_ref[:, a * LANES:(a + 1) * LANES]
            qh = jnp.where(lo if half == 0 else jnp.logical_not(lo), qh, jnp.zeros_like(qh))
            ss = []
            for d in range(3):
                sd = lax.dot_general(qh, k_refs[d][0], NT, preferred_element_type=F32)
                ss.append(jnp.where(masks[d], sd - _slope(g, r) * dists[d], NEG))
            mx = jnp.maximum(jnp.maximum(jnp.max(ss[0], axis=1, keepdims=True), jnp.max(ss[1], axis=1, keepdims=True)),
                             jnp.max(ss[2], axis=1, keepdims=True))
            l = jnp.zeros((tq, 1), F32)
            for d in range(3):
                p = jnp.exp(ss[d] - mx)
                l = l + jnp.sum(p, axis=1, keepdims=True)
                v = v_refs[d][0, :, half * LANES:(half + 1) * LANES]
                pv = pv + jnp.dot(p.astype(BF16), v, preferred_element_type=F32)
            ls.append(l)
        l_pair = jnp.where(lo, ls[0], ls[1])
        gp = _gate_pair(gates, lane, lo, 2 * NSA_HEADS + g * NSA_REP + 2 * a)
        o_ref[:, a * LANES:(a + 1) * LANES] = pv / l_pair * gp


def nsa_window_attention(qn, kw2, vw_lohi, gates, tq=256):
    T = qn.shape[0]
    G = NSA_KV_HEADS
    assert WINDOW == 2 * tq

    def kspec(d, width):
        return pl.BlockSpec((1, tq, width), lambda g, i, d=d: (g, jnp.maximum(i - 2 + d, 0), 0))

    return pl.pallas_call(
        functools.partial(_win_attn_kernel, tq=tq),
        grid=(G, T // tq),
        in_specs=[pl.BlockSpec((tq, 256), lambda g, i: (i, g)),
                  kspec(0, 128), kspec(1, 128), kspec(2, 128),
                  kspec(0, 256), kspec(1, 256), kspec(2, 256),
                  pl.BlockSpec((tq, 128), lambda g, i: (i, 0))],
        out_specs=pl.BlockSpec((tq, 256), lambda g, i: (i, g)),
        out_shape=jax.ShapeDtypeStruct((T, 512), F32),
        compiler_params=_cparams(("parallel", "parallel")),
        name="nsa_window_attention",
    )(qn, kw2, kw2, kw2, vw_lohi, vw_lohi, vw_lohi, gates)


def nsa_mixer(proj_n, proj_t, cmp_pe, cmp_w1, cmp_w2, q_gain, k_gain, parts=False):
    qn, kaug, vs_lohi, kw2, vw_lohi, gates = nsa_prep(proj_n, proj_t, q_gain, k_gain)
    kvc = nsa_compress(proj_n[:, C_NSA_KV:C_NSA_KV + 256], cmp_pe, cmp_w1, cmp_w2, k_gain[0])
    kc, vc = kvc[0], kvc[1]
    kc2 = jnp.concatenate([kc, kc], axis=-1).astype(BF16)
    zero = jnp.zeros_like(vc)
    vc_lohi = jnp.concatenate([vc, zero, zero, vc], axis=-1).astype(BF16)
    o_cmp, selbias = nsa_cmp_select(qn, kc2, vc_lohi, gates)
    o_slc = nsa_selected_attention(qn, selbias, kaug, vs_lohi, gates)
    o_win = nsa_window_attention(qn, kw2, vw_lohi, gates)
    if parts:
        return o_cmp + o_slc + o_win, (o_cmp, o_slc, o_win)
    return o_cmp, o_slc, o_win


def _diag_selector():
    m = np.zeros((SUB * LANES, LANES), np.float32)
    for j in range(SUB):
        for rep in range(CHUNK // SUB):
            m[j * LANES:(j + 1) * LANES, rep * SUB + j] = 1.0
    return jnp.asarray(m, dtype=BF16)


LOG2E = 1.0 / math.log(2.0)
REC_HEADS = 2


def _recur_tile(q_scr, k_scr, v_scr, g_scr, o_scr, st_scr, sel_ref, tb):
    row = lax.broadcasted_iota(jnp.int32, (CHUNK, LANES), 0)
    lane = lax.broadcasted_iota(jnp.int32, (CHUNK, LANES), 1)
    sub_row = row % SUB
    blockdiag = (row // SUB) == (lane // SUB)
    r64 = lax.broadcasted_iota(jnp.int32, (CHUNK, CHUNK), 0)
    c64 = lax.broadcasted_iota(jnp.int32, (CHUNK, CHUNK), 1)
    ltri = jnp.where(r64 >= c64, 1.0, 0.0).astype(F32)
    c_sub = lax.broadcasted_iota(jnp.int32, (SUB, CHUNK), 1)
    nsub = CHUNK // SUB

    def head_chunk(h, r0):
        qc = q_scr[h, pl.ds(r0, CHUNK), :]
        kc = k_scr[h, pl.ds(r0, CHUNK), :]
        vc = v_scr[h, pl.ds(r0, CHUNK), :]
        gc = g_scr[h, pl.ds(r0, CHUNK), :]
        b = jnp.dot(ltri, gc, precision=HIGHEST, preferred_element_type=F32)
        bend = b[CHUNK - 1:CHUNK, :]
        st = st_scr[h]
        o = lax.dot_general((qc * jnp.exp2(b)).astype(BF16), st.astype(BF16), NT, preferred_element_type=F32)
        k4 = kc.reshape(nsub, SUB, LANES)
        b4 = b.reshape(nsub, SUB, LANES)
        pieces = []
        for j in range(SUB):
            k_rep = jnp.broadcast_to(k4[:, j:j + 1, :], (nsub, SUB, LANES)).reshape(CHUNK, LANES)
            b_rep = jnp.broadcast_to(b4[:, j:j + 1, :], (nsub, SUB, LANES)).reshape(CHUNK, LANES)
            e = qc * k_rep * jnp.exp2(jnp.where(sub_row >= j, b - b_rep, NEG))
            pieces.append(e.astype(BF16))
        a_diag = jnp.dot(jnp.concatenate(pieces, axis=1), sel_ref[...], preferred_element_type=F32)
        a_diag = jnp.where(blockdiag, a_diag, 0.0)[:, 0:CHUNK]
        rows = [jnp.zeros((SUB, CHUNK), F32)]
        for i_sub in range(1, nsub):
            ref_b = b[i_sub * SUB - 1:i_sub * SUB, :]
            qt = qc[i_sub * SUB:(i_sub + 1) * SUB, :] * jnp.exp2(b[i_sub * SUB:(i_sub + 1) * SUB, :] - ref_b)
            kt = kc * jnp.exp2(jnp.minimum(ref_b - b, 0.0))
            a_i = lax.dot_general(qt.astype(BF16), kt.astype(BF16), NT, preferred_element_type=F32)
            rows.append(jnp.where(c_sub < i_sub * SUB, a_i, 0.0))
        a = jnp.concatenate(rows, axis=0) + a_diag
        o = o + jnp.dot(a.astype(BF16), vc.astype(BF16), preferred_element_type=F32)
        o_scr[h, pl.ds(r0, CHUNK), :] = o
        kend = kc * jnp.exp2(bend - b)
        st_scr[h] = st * jnp.exp2(bend) + lax.dot_general(vc.astype(BF16), kend.astype(BF16), TN,
                                                          preferred_element_type=F32)

    def chunk(c, carry):
        r0 = pl.multiple_of(c * CHUNK, CHUNK)
        for h in range(REC_HEADS):
            head_chunk(h, r0)
        return carry

    lax.fori_loop(0, tb // CHUNK, chunk, 0)


def _finish_recur(o_scr, gain_ref, gate, o_ref):
    for h in range(REC_HEADS):
        o = o_scr[h]
        ms = jnp.mean(o * o, axis=-1, keepdims=True)
        y = o * lax.rsqrt(ms + EPS) * gain_ref[...] * gate[:, h * LANES:(h + 1) * LANES]
        o_ref[:, h * LANES:(h + 1) * LANES] = y.astype(o_ref.dtype)


def _hgrn2_kernel(q_ref, f_ref, i_ref, gg_ref, lb_ref, gain_ref, sel_ref, o_ref,
                  q_scr, k_scr, v_scr, g_scr, o_scr, st_scr, *, tb):
    @pl.when(pl.program_id(1) == 0)
    def _():
        st_scr[...] = jnp.zeros(st_scr.shape, F32)

    for h in range(REC_HEADS):
        cols = slice(h * LANES, (h + 1) * LANES)
        lb = lb_ref[:, cols]
        z = f_ref[:, cols]
        f = lb + (1.0 - lb) * _sigmoid(z)
        q_scr[h] = q_ref[:, cols]
        k_scr[h] = (1.0 - lb) * _sigmoid(-z)
        v_scr[h] = i_ref[:, cols]
        g_scr[h] = jnp.log(jnp.maximum(f, TINY)) * LOG2E
    _recur_tile(q_scr, k_scr, v_scr, g_scr, o_scr, st_scr, sel_ref, tb)
    _finish_recur(o_scr, gain_ref, _sigmoid(gg_ref[...]), o_ref)


def _recur_scratch(tb):
    return ([pltpu.VMEM((REC_HEADS, tb, LANES), F32) for _ in range(5)]
            + [pltpu.VMEM((REC_HEADS, LANES, LANES), F32)])


def hgrn2_mixer(proj, lower_bound, norm_gain, tb=512):
    T = proj.shape[0]
    wide = REC_HEADS * LANES
    per = HG_HEADS // REC_HEADS

    def col(k):
        return pl.BlockSpec((tb, wide), lambda hp, i, k=k: (i, per * k + hp))

    return pl.pallas_call(
        functools.partial(_hgrn2_kernel, tb=tb),
        grid=(per, T // tb),
        in_specs=[col(0), col(1), col(2), col(3),
                  pl.BlockSpec((1, wide), lambda hp, i: (0, hp)),
                  pl.BlockSpec((1, LANES), lambda hp, i: (0, 0)),
                  pl.BlockSpec((SUB * LANES, LANES), lambda hp, i: (0, 0))],
        out_specs=pl.BlockSpec((tb, wide), lambda hp, i: (i, hp)),
        out_shape=jax.ShapeDtypeStruct((T, BRANCH_WIDTH), BF16),
        scratch_shapes=_recur_scratch(tb),
        compiler_params=_cparams(("parallel", "arbitrary")),
        name="hgrn2_mixer",
    )(proj, proj, proj, proj, lower_bound.reshape(1, -1), norm_gain.reshape(1, -1), _diag_selector())


def _gla_kernel(q_ref, k_ref, v_ref, r_ref, tail_ref, wa_ref, ba_ref, gain_ref, sel_ref, o_ref,
                q_scr, k_scr, v_scr, g_scr, o_scr, st_scr, *, tb):
    @pl.when(pl.program_id(1) == 0)
    def _():
        st_scr[...] = jnp.zeros(st_scr.shape, F32)

    lane = lax.broadcasted_iota(jnp.int32, (tb, LANES), 1)
    a = jnp.dot(tail_ref[...], wa_ref[...], precision=HIGHEST, preferred_element_type=F32) + ba_ref[...]
    log_sig = -(jnp.maximum(-a, 0.0) + jnp.log1p(jnp.exp(-jnp.abs(a))))
    q = q_ref[...] * (GLA_DK ** -0.5)
    k = k_ref[...]
    g2 = log_sig * (LOG2E / GLA_TAU)
    for h in range(REC_HEADS):
        mine = (lane < GLA_DK) if h == 0 else (lane >= GLA_DK)
        q_scr[h] = jnp.where(mine, q, 0.0)
        k_scr[h] = jnp.where(mine, k, 0.0)
        v_scr[h] = v_ref[:, h * LANES:(h + 1) * LANES]
        g_scr[h] = jnp.where(mine, g2, 0.0)
    _recur_tile(q_scr, k_scr, v_scr, g_scr, o_scr, st_scr, sel_ref, tb)
    r = r_ref[...]
    _finish_recur(o_scr, gain_ref, r * _sigmoid(r), o_ref)


def gla_mixer(proj, proj_t, w_a2, b_a, norm_gain, tb=512):
    T = proj.shape[0]
    assert REC_HEADS == 2 and LANES == 2 * GLA_DK
    wide = REC_HEADS * LANES
    cqk = C_GLA_QK // LANES
    cvr = C_GLA_VR // wide
    per = GLA_HEADS // REC_HEADS
    wa = jnp.zeros((LANES, GLA_HEADS * GLA_DK), F32).at[TAIL_GLA_A0:TAIL_GLA_A0 + GLA_RANK].set(w_a2)
    return pl.pallas_call(
        functools.partial(_gla_kernel, tb=tb),
        grid=(per, T // tb),
        in_specs=[pl.BlockSpec((tb, LANES), lambda hp, i: (i, cqk + hp)),
                  pl.BlockSpec((tb, LANES), lambda hp, i: (i, cqk + per + hp)),
                  pl.BlockSpec((tb, wide), lambda hp, i: (i, cvr + hp)),
                  pl.BlockSpec((tb, wide), lambda hp, i: (i, cvr + per + hp)),
                  pl.BlockSpec((tb, LANES), lambda hp, i: (i, 0)),
                  pl.BlockSpec((LANES, LANES), lambda hp, i: (0, hp)),
                  pl.BlockSpec((1, LANES), lambda hp, i: (0, hp)),
                  pl.BlockSpec((1, LANES), lambda hp, i: (0, 0)),
                  pl.BlockSpec((SUB * LANES, LANES), lambda hp, i: (0, 0))],
        out_specs=pl.BlockSpec((tb, wide), lambda hp, i: (i, hp)),
        out_shape=jax.ShapeDtypeStruct((T, BRANCH_WIDTH), BF16),
        scratch_shapes=_recur_scratch(tb),
        compiler_params=_cparams(("parallel", "arbitrary")),
        name="gla_mixer",
    )(proj, proj, proj, proj, proj_t, wa, b_a.reshape(1, -1), norm_gain.reshape(1, -1), _diag_selector())


def _merge_kernel(oc_ref, os_ref, ow_ref, ob_ref, og_ref, wb_ref, g0_ref, g1_ref, g2_ref, o_ref):
    o_a = (oc_ref[...] + os_ref[...] + ow_ref[...]).astype(BF16)
    acc = _sigmoid(g0_ref[...]) * jnp.dot(o_a, wb_ref[0], preferred_element_type=F32)
    acc = acc + _sigmoid(g1_ref[...]) * jnp.dot(ob_ref[...], wb_ref[1], preferred_element_type=F32)
    acc = acc + _sigmoid(g2_ref[...]) * jnp.dot(og_ref[...], wb_ref[2], preferred_element_type=F32)
    o_ref[...] = acc.astype(o_ref.dtype)


def merge_branches(proj, o_cmp, o_slc, o_win, o_b, o_c, w_branch_bf16, tm=512, tn=512):
    T = proj.shape[0]
    W = BRANCH_WIDTH
    nj = D_MODEL // tn
    ospec = pl.BlockSpec((tm, W), lambda i, j: (i, 0))

    def gspec(n):
        return pl.BlockSpec((tm, tn), lambda i, j, n=n: (i, n * nj + j))

    return pl.pallas_call(
        _merge_kernel,
        grid=(T // tm, nj),
        in_specs=[ospec, ospec, ospec, ospec, ospec,
                  pl.BlockSpec((3, W, tn), lambda i, j: (0, 0, j)),
                  gspec(0), gspec(1), gspec(2)],
        out_specs=pl.BlockSpec((tm, tn), lambda i, j: (i, j)),
        out_shape=jax.ShapeDtypeStruct((T, D_MODEL), BF16),
        compiler_params=_cparams(("parallel", "arbitrary")),
        name="merge_branches",
    )(o_cmp, o_slc, o_win, o_b, o_c, w_branch_bf16, proj, proj, proj)


MOE_TILE = 256


def _route_kernel(x_ref, g_ref, wr_ref, br_ref, h_ref, route_ref, cnt_ref, carry, *, tm):
    i = pl.program_id(0)

    @pl.when(i == 0)
    def _():
        carry[...] = jnp.zeros(carry.shape, F32)

    x = x_ref[...]
    ms = jnp.mean(x * x, axis=-1, keepdims=True)
    h = x * lax.rsqrt(ms + EPS) * g_ref[...]
    h_ref[...] = h
    logits = jnp.dot(h, wr_ref[...], precision=HIGHEST, preferred_element_type=F32) + br_ref[...]
    lane = lax.broadcasted_iota(jnp.int32, (tm, LANES), 1)

    def masked_softmax(mask):
        l = jnp.where(mask, logits, NEG)
        e = jnp.where(mask, jnp.exp(l - jnp.max(l, axis=1, keepdims=True)), 0.0)
        return e / jnp.sum(e, axis=1, keepdims=True)

    def top1(prob, mask):
        p = jnp.max(jnp.where(mask, prob, -1.0), axis=1, keepdims=True)
        idx = jnp.min(jnp.where(mask & (prob == p), lane, LANES), axis=1, keepdims=True)
        return p, idx

    gmask = lane < N_GROUPS
    gw, gidx = top1(masked_softmax(gmask), gmask)
    emask = (lane >= N_GROUPS) & (lane < N_GROUPS + N_EXPERTS) & ((lane - N_GROUPS) // EXPERTS_PER_GROUP == gidx)
    eprob = masked_softmax(emask)
    p1, i1 = top1(eprob, emask)
    rest = emask & (lane != i1)
    p2, i2 = top1(eprob, rest)
    psum = p1 + p2
    w1 = gw * (p1 / psum)
    w2 = gw * (p2 / psum)
    e1 = i1 - N_GROUPS
    e2 = i2 - N_GROUPS

    onehot = jnp.where((lane == e1) | (lane == e2), 1.0, 0.0)
    r = lax.broadcasted_iota(jnp.int32, (tm, tm), 0)
    c = lax.broadcasted_iota(jnp.int32, (tm, tm), 1)
    strict = jnp.where(r > c, 1.0, 0.0).astype(BF16)
    before = jnp.dot(strict, onehot.astype(BF16), preferred_element_type=F32) + carry[0:1, :]
    rank1 = jnp.sum(jnp.where(lane == e1, before, 0.0), axis=1, keepdims=True)
    rank2 = jnp.sum(jnp.where(lane == e2, before, 0.0), axis=1, keepdims=True)
    total = carry[0:1, :] + jnp.sum(onehot, axis=0, keepdims=True)
    carry[...] = jnp.broadcast_to(total, carry.shape)
    cnt_ref[...] = jnp.broadcast_to(total, cnt_ref.shape)

    out = jnp.where(lane == 0, w1, 0.0)
    out = jnp.where(lane == 1, w2, out)
    out = jnp.where(lane == 2, e1.astype(F32), out)
    out = jnp.where(lane == 3, e2.astype(F32), out)
    out = jnp.where(lane == 4, rank1, out)
    out = jnp.where(lane == 5, rank2, out)
    route_ref[...] = out


def moe_route(x, gain, w_grp, b_grp, w_exp, b_exp, tm=512):
    T, D = x.shape
    wr = jnp.zeros((D, LANES), F32).at[:, 0:N_GROUPS].set(w_grp).at[:, N_GROUPS:N_GROUPS + N_EXPERTS].set(w_exp)
    br = jnp.zeros((1, LANES), F32).at[0, 0:N_GROUPS].set(b_grp).at[0, N_GROUPS:N_GROUPS + N_EXPERTS].set(b_exp)
    return pl.pallas_call(
        functools.partial(_route_kernel, tm=tm),
        grid=(T // tm,),
        in_specs=[pl.BlockSpec((tm, D), lambda i: (i, 0)),
                  pl.BlockSpec((1, D), lambda i: (0, 0)),
                  pl.BlockSpec((D, LANES), lambda i: (0, 0)),
                  pl.BlockSpec((1, LANES), lambda i: (0, 0))],
        out_specs=(pl.BlockSpec((tm, D), lambda i: (i, 0)),
                   pl.BlockSpec((tm, LANES), lambda i: (i, 0)),
                   pl.BlockSpec((8, LANES), lambda i: (0, 0))),
        out_shape=(jax.ShapeDtypeStruct((T, D), F32),
                   jax.ShapeDtypeStruct((T, LANES), F32),
                   jax.ShapeDtypeStruct((8, LANES), F32)),
        scratch_shapes=[pltpu.VMEM((8, LANES), F32)],
        compiler_params=_cparams(("arbitrary",)),
        name="moe_route",
    )(x, gain.reshape(1, D), wr, br)


def _row_dma(src, src_row, dst, dst_row, sem):
    return pltpu.make_async_copy(src.at[pl.ds(src_row, 1), :], dst.at[pl.ds(dst_row, 1), :], sem)


def _rows_wait(src, dst, n, sem):
    pltpu.make_async_copy(src.at[pl.ds(0, n), :], dst.at[pl.ds(0, n), :], sem).wait()


def _expert_kernel(tile_expert, n_used, rcur_ref, rnext_ref, h_hbm, wgu_ref, wd_ref, o_ref,
                   xbuf, wgu_bf, wd_bf, sem):
    i = pl.program_id(0)
    used = i < n_used[0]
    slot = i % 2

    def fetch(rref, s):
        def body(r, c):
            _row_dma(h_hbm, rref[0, 0, r], xbuf.at[s], r, sem.at[s]).start()
            return c

        lax.fori_loop(0, MOE_TILE, body, 0, unroll=8)

    @pl.when(i == 0)
    def _():
        fetch(rcur_ref, 0)

    @pl.when(i + 1 < n_used[0])
    def _():
        fetch(rnext_ref, 1 - slot)

    @pl.when(used)
    def _():
        prev = tile_expert[jnp.maximum(i - 1, 0)]

        @pl.when((i == 0) | (tile_expert[i] != prev))
        def _():
            wgu_bf[...] = wgu_ref[0, 0].astype(BF16)
            wd_bf[...] = wd_ref[0, 0].astype(BF16)

        _rows_wait(h_hbm, xbuf.at[slot], MOE_TILE, sem.at[slot])
        x = xbuf[slot].astype(BF16)
        gu = jnp.dot(x, wgu_bf[...], preferred_element_type=F32)
        gate = gu[:, 0:D_FF_EXPERT]
        up = gu[:, D_FF_EXPERT:2 * D_FF_EXPERT]
        act = gate * _sigmoid(gate) * up
        o_ref[...] = jnp.dot(act.astype(BF16), wd_bf[...], preferred_element_type=F32)

    @pl.when(jnp.logical_not(used))
    def _():
        o_ref[...] = jnp.zeros(o_ref.shape, F32)


def moe_experts(h, row_token, tile_expert, n_used, w_gate_up, w_down, layer):
    T, D = h.shape
    n_tiles = row_token.shape[0] // MOE_TILE
    rows = row_token.reshape(n_tiles, 1, MOE_TILE)
    grid_spec = pltpu.PrefetchScalarGridSpec(
        num_scalar_prefetch=2,
        grid=(n_tiles,),
        in_specs=[
            pl.BlockSpec((1, 1, MOE_TILE), lambda i, te, nu: (i, 0, 0), memory_space=pltpu.SMEM),
            pl.BlockSpec((1, 1, MOE_TILE), lambda i, te, nu: (jnp.minimum(i + 1, n_tiles - 1), 0, 0),
                         memory_space=pltpu.SMEM),
            pl.BlockSpec(memory_space=pl.ANY),
            pl.BlockSpec((1, 1, D, 2 * D_FF_EXPERT), lambda i, te, nu: (layer, te[i], 0, 0)),
            pl.BlockSpec((1, 1, D_FF_EXPERT, D), lambda i, te, nu: (layer, te[i], 0, 0)),
        ],
        out_specs=pl.BlockSpec((MOE_TILE, D), lambda i, te, nu: (i, 0)),
        scratch_shapes=[
            pltpu.VMEM((2, MOE_TILE, D), F32),
            pltpu.VMEM((D, 2 * D_FF_EXPERT), BF16),
            pltpu.VMEM((D_FF_EXPERT, D), BF16),
            pltpu.SemaphoreType.DMA((2,)),
        ],
    )
    return pl.pallas_call(
        _expert_kernel,
        grid_spec=grid_spec,
        out_shape=jax.ShapeDtypeStruct((n_tiles * MOE_TILE, D), F32),
        compiler_params=_cparams(("arbitrary",)),
        name="moe_experts",
    )(tile_expert, n_used, rows, rows, h, w_gate_up, w_down)


def _combine_kernel(dcur_ref, dnext_ref, x_ref, route_ref, y_hbm, o_ref, buf, sem, *, tm):
    i = pl.program_id(0)
    n = pl.num_programs(0)
    slot = i % 2

    def fetch(dref, s):
        def body(r, c):
            _row_dma(y_hbm, dref[0, 0, 2 * r], buf.at[s, 0], r, sem.at[s]).start()
            _row_dma(y_hbm, dref[0, 0, 2 * r + 1], buf.at[s, 1], r, sem.at[s]).start()
            return c

        lax.fori_loop(0, tm, body, 0, unroll=8)

    @pl.when(i == 0)
    def _():
        fetch(dcur_ref, 0)

    @pl.when(i + 1 < n)
    def _():
        fetch(dnext_ref, 1 - slot)

    _rows_wait(y_hbm, buf.at[slot, 0], tm, sem.at[slot])
    _rows_wait(y_hbm, buf.at[slot, 1], tm, sem.at[slot])
    route = route_ref[...]
    lane = lax.broadcasted_iota(jnp.int32, route.shape, 1)
    w1 = jnp.sum(jnp.where(lane == 0, route, 0.0), axis=1, keepdims=True)
    w2 = jnp.sum(jnp.where(lane == 1, route, 0.0), axis=1, keepdims=True)
    o_ref[...] = x_ref[...] + (w1 * buf[slot, 0] + w2 * buf[slot, 1])


def moe_combine(x, ys, route, dest, tm=256):
    T, D = x.shape
    n = T // tm
    dest_tiles = dest.reshape(n, 1, 2 * tm)
    return pl.pallas_call(
        functools.partial(_combine_kernel, tm=tm),
        grid=(n,),
        in_specs=[pl.BlockSpec((1, 1, 2 * tm), lambda i: (i, 0, 0), memory_space=pltpu.SMEM),
                  pl.BlockSpec((1, 1, 2 * tm), lambda i: (jnp.minimum(i + 1, n - 1), 0, 0), memory_space=pltpu.SMEM),
                  pl.BlockSpec((tm, D), lambda i: (i, 0)),
                  pl.BlockSpec((tm, LANES), lambda i: (i, 0)),
                  pl.BlockSpec(memory_space=pl.ANY)],
        out_specs=pl.BlockSpec((tm, D), lambda i: (i, 0)),
        out_shape=jax.ShapeDtypeStruct((T, D), F32),
        scratch_shapes=[pltpu.VMEM((2, 2, tm, D), F32), pltpu.SemaphoreType.DMA((2,))],
        compiler_params=_cparams(("arbitrary",)),
        name="moe_combine",
    )(dest_tiles, dest_tiles, x, route, ys)


def hierarchical_moe(x, gain, w_grp, b_grp, w_exp, b_exp, w_gate_up, w_down, layer):
    T, D = x.shape
    h, route, cnt = moe_route(x, gain, w_grp, b_grp, w_exp, b_exp)
    expert = route[:, 2:4].astype(jnp.int32)
    rank = route[:, 4:6].astype(jnp.int32)
    counts = cnt[0, 0:N_EXPERTS].astype(jnp.int32)
    padded = ((counts + MOE_TILE - 1) // MOE_TILE) * MOE_TILE
    ends = jnp.cumsum(padded)
    offs = ends - padded
    dest = offs[expert] + rank
    n_rows = 2 * T + N_EXPERTS * MOE_TILE
    n_tiles = n_rows // MOE_TILE
    n_used = (ends[-1] // MOE_TILE).astype(jnp.int32)
    tile_start = jnp.arange(n_tiles, dtype=jnp.int32) * MOE_TILE
    tile_expert = jnp.sum((ends[None, :] <= tile_start[:, None]).astype(jnp.int32), axis=1)
    last_expert = tile_expert[jnp.maximum(n_used - 1, 0)]
    tile_expert = jnp.where(tile_start < ends[-1], tile_expert, last_expert)
    row_token = jnp.zeros((n_rows,), jnp.int32).at[dest.reshape(-1)].set(
        jnp.repeat(jnp.arange(T, dtype=jnp.int32), 2))
    ys = moe_experts(h, row_token, tile_expert, n_used.reshape(1), w_gate_up, w_down, layer)
    return moe_combine(x, ys, route, dest)


def _shift_cast_kernel(a_ref, b_ref, c_ref, o_ref, *, shift):
    blocks = [a_ref[0], b_ref[0], c_ref[0]]
    if shift:
        lane = lax.broadcasted_iota(jnp.int32, blocks[0].shape, 1)
        keep = lane < LANES - shift
        rolled = [pltpu.roll(b, LANES - shift, axis=1) for b in blocks]
        blocks = [jnp.where(keep, rolled[0], rolled[1]), jnp.where(keep, rolled[1], rolled[2])]
    o_ref[:, 0:LANES] = blocks[0].astype(o_ref.dtype)
    o_ref[:, LANES:2 * LANES] = blocks[1].astype(o_ref.dtype)


def shift_cast_columns(w, layer, col0, n_cols):
    _, D, n_in = w.shape
    b0, shift = divmod(col0, LANES)
    last = (n_in - 1) // LANES

    def spec(k):
        return pl.BlockSpec((1, D, LANES), lambda j, k=k: (layer, 0, jnp.minimum(b0 + 2 * j + k, last)))

    return pl.pallas_call(
        functools.partial(_shift_cast_kernel, shift=shift),
        grid=(n_cols // (2 * LANES),),
        in_specs=[spec(0), spec(1), spec(2)],
        out_specs=pl.BlockSpec((D, 2 * LANES), lambda j: (0, j)),
        out_shape=jax.ShapeDtypeStruct((D, n_cols), BF16),
        compiler_params=_cparams(("parallel",)),
        name="shift_cast_columns",
    )(w, w, w)


def kernel(x, norm_mix, w_in, cmp_pe, cmp_w1, cmp_w2, q_norm, k_norm, hg_lb_logits, hg_norm, gla_w_a2, gla_b_a,
           gla_norm, w_branch, w_out, norm_ffn, w_grp, b_grp, w_exp, b_exp, w_gate_up, w_down):
    B, T, D = x.shape
    assert B == 1 and D == D_MODEL
    xt = x[0]
    p_lb = jax.nn.softmax(hg_lb_logits.astype(F32), axis=0)
    lower_bounds = jnp.cumsum(p_lb, axis=0) - p_lb[0]
    for l in range(DEPTH):
        w_n = shift_cast_columns(w_in, l, *W_NSA)
        w_r = shift_cast_columns(w_in, l, *W_REC)
        w_m = shift_cast_columns(w_in, l, *W_MERGE)
        w_t = jnp.zeros((D, LANES), BF16)
        w_t = w_t.at[:, 0:W_GATE[1]].set(w_in[l, :, W_GATE[0]:W_GATE[0] + W_GATE[1]].astype(BF16))
        w_t = w_t.at[:, TAIL_GLA_A0:TAIL_GLA_A0 + W_GLA_A[1]].set(
            w_in[l, :, W_GLA_A[0]:W_GLA_A[0] + W_GLA_A[1]].astype(BF16))
        h = rmsnorm_bf16(xt, norm_mix[l])
        proj_n = matmul_bf16(h, w_n, tm=2048, tn=256, name="in_proj_nsa")
        proj_r = matmul_bf16(h, w_r, tm=2048, tn=512, name="in_proj_rec")
        proj_m = matmul_bf16(h, w_m, tm=2048, tn=512, name="in_proj_merge")
        proj_t = matmul_bf16(h, w_t, tm=2048, tn=LANES, name="in_proj_gates")
        o_cmp, o_slc, o_win = nsa_mixer(proj_n, proj_t, cmp_pe[l], cmp_w1[l], cmp_w2[l], q_norm[l], k_norm[l])
        o_b = hgrn2_mixer(proj_r, lower_bounds[l], hg_norm[l])
        o_c = gla_mixer(proj_r, proj_t, gla_w_a2[l], gla_b_a[l], gla_norm[l])
        merged = merge_branches(proj_m, o_cmp, o_slc, o_win, o_b, o_c, w_branch[l].astype(BF16))
        xt = matmul_bf16(merged, w_out[l].astype(BF16), res=xt, tm=1024, tn=512, name="out_proj")
        xt = hierarchical_moe(xt, norm_ffn[l], w_grp[l], b_grp[l], w_exp[l], b_exp[l], w_gate_up, w_down, l)
    return xt[None]
```

```python
import functools
import math

import numpy as np
import jax
import jax.numpy as jnp
from jax import lax
from jax.experimental import pallas as pl
from jax.experimental.pallas import tpu as pltpu

F32 = jnp.float32
BF16 = jnp.bfloat16
HIGHEST = lax.Precision.HIGHEST

D_MODEL = 2048
DEPTH = 2
BRANCH_WIDTH = D_MODEL // 4
NSA_HEADS = 8
NSA_KV_HEADS = 2
NSA_REP = NSA_HEADS // NSA_KV_HEADS
NSA_HD = 64
CMP_LEN = 32
CMP_STRIDE = 16
SLC_LEN = 64
SLC_TOP = 16
WINDOW = 512
HG_HEADS = 4
GLA_HEADS = 4
GLA_DK = 64
GLA_RANK = 16
GLA_TAU = 16.0
CHUNK = 64
SUB = 16
N_GROUPS = 4
EXPERTS_PER_GROUP = 8
N_EXPERTS = N_GROUPS * EXPERTS_PER_GROUP
D_FF_EXPERT = D_MODEL // 4
EPS = 1e-6
NEG = -1e30
FORCE = 1e4
TINY = 1e-30

LANES = 128
SUBLANES = 8
VMEM_LIMIT = 56 * 1024 * 1024

W_NSA = (0, 1280)
W_REC = (1304, 3584)
W_MERGE = (4904, 6144)
W_GATE = (1280, 24)
W_GLA_A = (4888, 16)
C_NSA_KV = 512
C_GLA_QK = 2048
C_GLA_VR = 2560
TAIL_GLA_A0 = W_GATE[1]

NT = (((1,), (1,)), ((), ()))
TN = (((0,), (0,)), ((), ()))


def _cparams(sem):
    return pltpu.CompilerParams(dimension_semantics=sem, vmem_limit_bytes=VMEM_LIMIT)


def _sigmoid(x):
    return 1.0 / (1.0 + jnp.exp(-x))


def _norm_kernel(x_ref, g_ref, o_ref):
    x = x_ref[...]
    ms = jnp.mean(x * x, axis=-1, keepdims=True)
    o_ref[...] = (x * lax.rsqrt(ms + EPS) * g_ref[...]).astype(o_ref.dtype)


def rmsnorm_bf16(x, gain, tm=512):
    T, D = x.shape
    return pl.pallas_call(
        _norm_kernel,
        grid=(T // tm,),
        in_specs=[pl.BlockSpec((tm, D), lambda i: (i, 0)), pl.BlockSpec((1, D), lambda i: (0, 0))],
        out_specs=pl.BlockSpec((tm, D), lambda i: (i, 0)),
        out_shape=jax.ShapeDtypeStruct((T, D), BF16),
        compiler_params=_cparams(("parallel",)),
        name="rmsnorm_bf16",
    )(x, gain.reshape(1, D))


def _mm_kernel(a_ref, b_ref, o_ref):
    o_ref[...] = jnp.dot(a_ref[...], b_ref[...], preferred_element_type=F32)


def _mm_res_kernel(a_ref, b_ref, r_ref, o_ref):
    o_ref[...] = r_ref[...] + jnp.dot(a_ref[...], b_ref[...], preferred_element_type=F32)


def matmul_bf16(a, b, res=None, tm=1024, tn=512, name="matmul_bf16"):
    T, K = a.shape
    N = b.shape[1]
    in_specs = [pl.BlockSpec((tm, K), lambda i, j: (i, 0)), pl.BlockSpec((K, tn), lambda i, j: (0, j))]
    args = [a, b]
    kern = _mm_kernel
    if res is not None:
        in_specs.append(pl.BlockSpec((tm, tn), lambda i, j: (i, j)))
        args.append(res)
        kern = _mm_res_kernel
    return pl.pallas_call(
        kern,
        grid=(T // tm, N // tn),
        in_specs=in_specs,
        out_specs=pl.BlockSpec((tm, tn), lambda i, j: (i, j)),
        out_shape=jax.ShapeDtypeStruct((T, N), F32),
        compiler_params=_cparams(("parallel", "arbitrary")),
        name=name,
    )(*args)


def _half_rmsnorm(x, gain2, lo):
    x2 = x * x
    s_lo = jnp.sum(jnp.where(lo, x2, 0.0), axis=1, keepdims=True)
    s_hi = jnp.sum(jnp.where(lo, 0.0, x2), axis=1, keepdims=True)
    ms = jnp.where(lo, s_lo, s_hi) * (1.0 / NSA_HD)
    return x * lax.rsqrt(ms + EPS) * gain2


def _nsa_prep_kernel(q_ref, kvc_ref, kvs_ref, kvw_ref, tail_ref, qg_ref, kg_ref,
                     qn_ref, kaug_ref, vs_ref, kw_ref, vw_ref, gate_ref, *, tm):
    i = pl.program_id(0)
    lane = lax.broadcasted_iota(jnp.int32, (tm, LANES), 1)
    lo = lane < NSA_HD
    qg = qg_ref[...]
    for c in range(4):
        x = q_ref[:, c * LANES:(c + 1) * LANES]
        qn_ref[:, c * LANES:(c + 1) * LANES] = (_half_rmsnorm(x, qg, lo) * (NSA_HD ** -0.5)).astype(BF16)

    def dup(kn):
        rolled = pltpu.roll(kn, NSA_HD, axis=1)
        return jnp.where(lo, kn, rolled), jnp.where(lo, rolled, kn)

    vt_zero = jnp.zeros((NSA_HD, tm), BF16)

    def store_vt(v_pair, ref):
        vt = jnp.transpose(v_pair)
        for g in range(NSA_KV_HEADS):
            head = vt[g * NSA_HD:(g + 1) * NSA_HD, :].astype(BF16)
            ref[g, 0:NSA_HD, :] = head
            ref[g, NSA_HD:LANES, :] = vt_zero
            ref[g, LANES:LANES + NSA_HD, :] = vt_zero
            ref[g, LANES + NSA_HD:2 * LANES, :] = head

    row = i * tm + lax.broadcasted_iota(jnp.int32, (tm, LANES), 0)
    onehot = jnp.where(row // SLC_LEN == lane, 1.0, 0.0).astype(BF16)
    k_extra = jnp.where(lane == NSA_HD, (row % SLC_LEN).astype(F32), 0.0)
    ks = dup(_half_rmsnorm(kvs_ref[:, 0:LANES], kg_ref[1:2, :], lo))
    store_vt(kvs_ref[:, LANES:2 * LANES], vs_ref)
    kw = dup(_half_rmsnorm(kvw_ref[:, 0:LANES], kg_ref[2:3, :], lo))
    store_vt(kvw_ref[:, LANES:2 * LANES], vw_ref)
    for g in range(NSA_KV_HEADS):
        kaug_ref[g, :, 0:LANES] = onehot
        kaug_ref[g, :, LANES:2 * LANES] = jnp.where(lo, ks[g], k_extra).astype(BF16)
        kw_ref[g] = kw[g].astype(BF16)
    gate_ref[...] = _sigmoid(tail_ref[...])


def nsa_prep(proj_n, proj_t, q_gain, k_gain, tm=512):
    T = proj_n.shape[0]
    qg2 = jnp.tile(q_gain.reshape(1, NSA_HD), (1, 2))
    kg2 = jnp.tile(k_gain.reshape(3, NSA_HD), (1, 2))
    G = NSA_KV_HEADS
    out_shape = (
        jax.ShapeDtypeStruct((T, 512), BF16),
        jax.ShapeDtypeStruct((G, T, 256), BF16),
        jax.ShapeDtypeStruct((G, 256, T), BF16),
        jax.ShapeDtypeStruct((G, T, 128), BF16),
        jax.ShapeDtypeStruct((G, 256, T), BF16),
        jax.ShapeDtypeStruct((T, 128), F32),
    )
    return pl.pallas_call(
        functools.partial(_nsa_prep_kernel, tm=tm),
        grid=(T // tm,),
        in_specs=[
            pl.BlockSpec((tm, 512), lambda i: (i, 0)),
            pl.BlockSpec((tm, 256), lambda i: (i, C_NSA_KV // 256)),
            pl.BlockSpec((tm, 256), lambda i: (i, C_NSA_KV // 256 + 1)),
            pl.BlockSpec((tm, 256), lambda i: (i, C_NSA_KV // 256 + 2)),
            pl.BlockSpec((tm, 128), lambda i: (i, 0)),
            pl.BlockSpec((1, 128), lambda i: (0, 0)),
            pl.BlockSpec((3, 128), lambda i: (0, 0)),
        ],
        out_specs=(
            pl.BlockSpec((tm, 512), lambda i: (i, 0)),
            pl.BlockSpec((G, tm, 256), lambda i: (0, i, 0)),
            pl.BlockSpec((G, 256, tm), lambda i: (0, 0, i)),
            pl.BlockSpec((G, tm, 128), lambda i: (0, i, 0)),
            pl.BlockSpec((G, 256, tm), lambda i: (0, 0, i)),
            pl.BlockSpec((tm, 128), lambda i: (i, 0)),
        ),
        out_shape=out_shape,
        compiler_params=_cparams(("parallel",)),
        name="nsa_prep",
    )(proj_n, proj_n, proj_n, proj_n, proj_t, qg2, kg2)


def _gelu_tanh(x):
    c = math.sqrt(2.0 / math.pi)
    return 0.5 * x * (1.0 + jnp.tanh(c * (x + 0.044715 * (x * x * x))))


def _compress_kernel(a_ref, pe_ref, w1_ref, w2_ref, kg_ref, o_ref, *, nb):
    kind = pl.program_id(0)
    a_lo = a_ref[0, 0, 0:nb, :]
    a_hi = a_ref[0, 0, 1:nb + 1, :]
    blocks = jnp.concatenate([a_lo, a_hi], axis=1) + pe_ref[0]
    h1 = jnp.dot(blocks.astype(BF16), w1_ref[0].astype(BF16), preferred_element_type=F32)
    y = jnp.dot(_gelu_tanh(h1).astype(BF16), w2_ref[0].astype(BF16), preferred_element_type=F32)
    ms = jnp.mean(y * y, axis=-1, keepdims=True)
    yn = y * lax.rsqrt(ms + EPS) * kg_ref[...]
    o_ref[0, 0] = jnp.where(kind == 0, yn, y)


def nsa_compress(kv_cmp, cmp_pe, cmp_w1, cmp_w2, k_gain0):
    T = kv_cmp.shape[0]
    nb = T // CMP_STRIDE
    G = NSA_KV_HEADS
    a = kv_cmp.reshape(T, 2, G, NSA_HD).transpose(1, 2, 0, 3).reshape(2, G, nb, CMP_STRIDE * NSA_HD)
    a = jnp.pad(a, ((0, 0), (0, 0), (0, 8), (0, 0)))
    pe = cmp_pe.reshape(2, 1, CMP_LEN * NSA_HD)
    return pl.pallas_call(
        functools.partial(_compress_kernel, nb=nb),
        grid=(2, G),
        in_specs=[
            pl.BlockSpec((1, 1, nb + 8, CMP_STRIDE * NSA_HD), lambda k, g: (k, g, 0, 0)),
            pl.BlockSpec((1, 1, CMP_LEN * NSA_HD), lambda k, g: (k, 0, 0)),
            pl.BlockSpec((1, CMP_LEN * NSA_HD, NSA_HD), lambda k, g: (k, 0, 0)),
            pl.BlockSpec((1, NSA_HD, NSA_HD), lambda k, g: (k, 0, 0)),
            pl.BlockSpec((1, NSA_HD), lambda k, g: (0, 0)),
        ],
        out_specs=pl.BlockSpec((1, 1, nb, NSA_HD), lambda k, g: (k, g, 0, 0)),
        out_shape=jax.ShapeDtypeStruct((2, G, nb, NSA_HD), F32),
        compiler_params=_cparams(("arbitrary", "arbitrary")),
        name="nsa_compress",
    )(a, pe, cmp_w1, cmp_w2, k_gain0.reshape(1, NSA_HD))


def _slope(g, r):
    return jnp.where(g == 0, 2.0 ** -(r + 1), 2.0 ** -(NSA_REP + r + 1)).astype(F32)


def _gate_pair(gates, lane, lo, col_even):
    ge = jnp.sum(jnp.where(lane == col_even, gates, 0.0), axis=1, keepdims=True)
    go = jnp.sum(jnp.where(lane == col_even + 1, gates, 0.0), axis=1, keepdims=True)
    return jnp.where(lo, ge, go)


def _cmp_sel_kernel(q_ref, kc_ref, vc_ref, gate_ref, mt_ref, o_ref, sel_ref, *, tq, nb, ns_pad):
    g = pl.program_id(0)
    qi = pl.program_id(1)
    t0 = qi * tq
    lane = lax.broadcasted_iota(jnp.int32, (tq, LANES), 1)
    lo = lane < NSA_HD
    n_idx = lax.broadcasted_iota(jnp.int32, (nb, tq), 0)
    t_idx = t0 + lax.broadcasted_iota(jnp.int32, (nb, tq), 1)
    dist = (t_idx - (n_idx * CMP_STRIDE + (CMP_LEN - 1))).astype(F32)
    vis = dist >= 0.0
    kc = kc_ref[0]
    imp = jnp.zeros((nb, tq), F32)
    gates = gate_ref[...]
    for a in range(2):
        acc = jnp.zeros((tq, LANES), F32)
        for half in range(2):
            r = 2 * a + half
            qh = q_ref[:, a * LANES:(a + 1) * LANES]
            qh = jnp.where(lo if half == 0 else jnp.logical_not(lo), qh, jnp.zeros_like(qh))
            s = lax.dot_general(kc, qh, NT, preferred_element_type=F32)
            s = jnp.where(vis, s - _slope(g, r) * dist, NEG)
            mx = jnp.max(s, axis=0, keepdims=True)
            e = jnp.where(vis, jnp.exp(s - mx), 0.0)
            den = jnp.sum(e, axis=0, keepdims=True)
            p = e * jnp.where(den > 0.0, 1.0 / den, 0.0)
            imp = imp + p
            v = vc_ref[0, :, half * LANES:(half + 1) * LANES]
            acc = acc + lax.dot_general(p.astype(BF16), v, TN, preferred_element_type=F32)
        gp = _gate_pair(gates, lane, lo, g * NSA_REP + 2 * a)
        o_ref[:, a * LANES:(a + 1) * LANES] = acc * gp

    score = jnp.dot(mt_ref[...], imp, precision=HIGHEST, preferred_element_type=F32)
    blk = lax.broadcasted_iota(jnp.int32, (ns_pad, tq), 0)
    cur = (t0 + lax.broadcasted_iota(jnp.int32, (ns_pad, tq), 1)) // SLC_LEN
    forced = (blk == 0) | (blk == cur) | (blk == cur - 1)
    score = jnp.where(forced, FORCE, score)
    score = jnp.where(blk <= cur, score, -1.0)
    sel = jnp.zeros((ns_pad, tq), F32)
    for _ in range(SLC_TOP):
        mx = jnp.max(score, axis=0, keepdims=True)
        idx = jnp.min(jnp.where(score == mx, blk, ns_pad), axis=0, keepdims=True)
        hit = blk == idx
        sel = jnp.where(hit, 1.0, sel)
        score = jnp.where(hit, -jnp.inf, score)
    bias_t = jnp.where((sel > 0.0) & (blk <= cur), 0.0, NEG)
    sel_ref[0] = jnp.transpose(bias_t).astype(BF16)


def _score_matrix(nb, ns_pad):
    ratio, span = SLC_LEN // CMP_STRIDE, CMP_LEN // CMP_STRIDE
    n_cmp = nb - 1
    m = np.zeros((ns_pad, nb), np.float32)
    for s in range(nb // ratio):
        for mm in range(ratio):
            for nn in range(span):
                c = ratio * s + mm - nn
                if 0 <= c < n_cmp:
                    m[s, c] += 1.0
    return jnp.asarray(m)


def nsa_cmp_select(qn, kc2, vc_lohi, gates, tq=256):
    T = qn.shape[0]
    nb = T // CMP_STRIDE
    ns_pad = LANES
    G = NSA_KV_HEADS
    return pl.pallas_call(
        functools.partial(_cmp_sel_kernel, tq=tq, nb=nb, ns_pad=ns_pad),
        grid=(G, T // tq),
        in_specs=[
            pl.BlockSpec((tq, 256), lambda g, i: (i, g)),
            pl.BlockSpec((1, nb, 128), lambda g, i: (g, 0, 0)),
            pl.BlockSpec((1, nb, 256), lambda g, i: (g, 0, 0)),
            pl.BlockSpec((tq, 128), lambda g, i: (i, 0)),
            pl.BlockSpec((ns_pad, nb), lambda g, i: (0, 0)),
        ],
        out_specs=(
            pl.BlockSpec((tq, 256), lambda g, i: (i, g)),
            pl.BlockSpec((1, tq, ns_pad), lambda g, i: (g, i, 0)),
        ),
        out_shape=(jax.ShapeDtypeStruct((T, 512), F32), jax.ShapeDtypeStruct((G, T, ns_pad), BF16)),
        compiler_params=_cparams(("parallel", "parallel")),
        name="nsa_cmp_select",
    )(qn, kc2, vc_lohi, gates, _score_matrix(nb, ns_pad))


def _sel_attn_kernel(qi_tab, kj_tab, first_tab, last_tab, live_tab, q_ref, sb_ref, k_ref, v_ref, gate_ref, o_ref,
                     qaug, m_scr, l_scr, acc_scr, *, tq, tk):
    g = pl.program_id(0)
    s = g * pl.num_programs(1) + pl.program_id(1)
    qi = qi_tab[s]
    kj = kj_tab[s]
    live = live_tab[s] == 1
    lane = lax.broadcasted_iota(jnp.int32, (tq, LANES), 1)
    lo = lane < NSA_HD

    @pl.when(first_tab[s] == 1)
    def _():
        sb = sb_ref[0].astype(F32)
        blk_rel = (lane - qi * (tq // SLC_LEN)).astype(F32)
        for r in range(NSA_REP):
            a, half = r // 2, r % 2
            slope = _slope(g, r)
            qh = q_ref[:, a * LANES:(a + 1) * LANES].astype(F32)
            if half:
                qh = pltpu.roll(qh, NSA_HD, axis=1)
            qh = jnp.where(lo, qh, jnp.where(lane == NSA_HD, slope, 0.0))
            qaug[r * tq:(r + 1) * tq, 0:LANES] = (sb + (slope * SLC_LEN) * blk_rel).astype(BF16)
            qaug[r * tq:(r + 1) * tq, LANES:2 * LANES] = qh.astype(BF16)
        m_scr[...] = jnp.full(m_scr.shape, NEG, F32)
        l_scr[...] = jnp.zeros(l_scr.shape, F32)
        acc_scr[...] = jnp.zeros(acc_scr.shape, F32)

    upper = lax.broadcasted_iota(jnp.int32, (LANES, tq), 0) < NSA_HD

    def step(masked):
        if masked:
            kpos = kj * tk + lax.broadcasted_iota(jnp.int32, (tk, tq), 0)
            qpos = qi * tq + lax.broadcasted_iota(jnp.int32, (tk, tq), 1)
            causal = kpos <= qpos
        k = k_ref[0]
        for a in range(2):
            pv = jnp.zeros((LANES, tq), F32)
            alphas = []
            for half in range(2):
                r = 2 * a + half
                sr = lax.dot_general(k, qaug[r * tq:(r + 1) * tq, :], NT, preferred_element_type=F32)
                if masked:
                    sr = jnp.where(causal, sr, NEG)
                m_prev = m_scr[r:r + 1, :]
                m_new = jnp.maximum(m_prev, jnp.max(sr, axis=0, keepdims=True))
                alpha = jnp.exp(m_prev - m_new)
                p = jnp.exp(sr - m_new)
                l_scr[r:r + 1, :] = alpha * l_scr[r:r + 1, :] + jnp.sum(p, axis=0, keepdims=True)
                m_scr[r:r + 1, :] = m_new
                vt = v_ref[0, half * LANES:(half + 1) * LANES, :]
                pv = pv + jnp.dot(vt, p.astype(BF16), preferred_element_type=F32)
                alphas.append(alpha)
            alpha_pair = jnp.where(upper, alphas[0], alphas[1])
            acc_scr[a * LANES:(a + 1) * LANES, :] = alpha_pair * acc_scr[a * LANES:(a + 1) * LANES, :] + pv

    on_diagonal = kj * tk + (tk - 1) > qi * tq

    @pl.when(live & on_diagonal)
    def _():
        step(True)

    @pl.when(live & jnp.logical_not(on_diagonal))
    def _():
        step(False)

    @pl.when(last_tab[s] == 1)
    def _():
        gates = gate_ref[...]
        for a in range(2):
            l_pair = jnp.where(upper, l_scr[2 * a:2 * a + 1, :], l_scr[2 * a + 1:2 * a + 2, :])
            gp = _gate_pair(gates, lane, lo, NSA_HEADS + g * NSA_REP + 2 * a)
            o_t = acc_scr[a * LANES:(a + 1) * LANES, :] / l_pair
            o_ref[:, a * LANES:(a + 1) * LANES] = jnp.transpose(o_t) * gp


def nsa_selected_attention(qn, selbias, kaug, vs_lohi, gates, tq=256, tk=512):
    T = qn.shape[0]
    G = NSA_KV_HEADS
    qi_l, kj_l, first_l, last_l = [], [], [], []
    for qi in range(T // tq):
        nk = (qi * tq + tq - 1) // tk + 1
        for kj in range(nk):
            qi_l.append(qi)
            kj_l.append(kj)
            first_l.append(1 if kj == 0 else 0)
            last_l.append(1 if kj == nk - 1 else 0)
    nsteps = len(qi_l)
    qi_c, kj_c, first_c, last_c = [np.asarray(t, np.int32) for t in (qi_l, kj_l, first_l, last_l)]
    nq, nkb = T // tq, tk // SLC_LEN
    chosen = (selbias.astype(F32) > 0.5 * NEG).reshape(G, nq, tq, LANES // nkb, nkb)
    tile_any = jnp.any(chosen, axis=(2, 4))
    active = tile_any[:, qi_c, kj_c] | (first_c == 1)[None, :] | (last_c == 1)[None, :]
    csum = jnp.cumsum(active.astype(jnp.int32), axis=1)
    n_active = csum[:, -1]
    k_idx = jnp.arange(nsteps, dtype=jnp.int32)
    src = jnp.sum((csum[:, None, :] <= k_idx[None, :, None]).astype(jnp.int32), axis=2)
    last_src = jnp.sum((csum < n_active[:, None]).astype(jnp.int32), axis=1)
    live = k_idx[None, :] < n_active[:, None]
    src = jnp.where(live, src, last_src[:, None])

    def table(values, mask_dead):
        t = jnp.asarray(values)[src]
        if mask_dead:
            t = jnp.where(live, t, 0)
        return t.reshape(-1).astype(jnp.int32)

    tabs = [table(qi_c, False), table(kj_c, False), table(first_c, True), table(last_c, True),
            live.reshape(-1).astype(jnp.int32)]

    def at(tab, g, s):
        return tab[g * nsteps + s]

    grid_spec = pltpu.PrefetchScalarGridSpec(
        num_scalar_prefetch=5,
        grid=(G, nsteps),
        in_specs=[
            pl.BlockSpec((tq, 256), lambda g, s, qt, kt, ft, lt, at_: (at(qt, g, s), g)),
            pl.BlockSpec((1, tq, 128), lambda g, s, qt, kt, ft, lt, at_: (g, at(qt, g, s), 0)),
            pl.BlockSpec((1, tk, 256), lambda g, s, qt, kt, ft, lt, at_: (g, at(kt, g, s), 0)),
            pl.BlockSpec((1, 256, tk), lambda g, s, qt, kt, ft, lt, at_: (g, 0, at(kt, g, s))),
            pl.BlockSpec((tq, 128), lambda g, s, qt, kt, ft, lt, at_: (at(qt, g, s), 0)),
        ],
        out_specs=pl.BlockSpec((tq, 256), lambda g, s, qt, kt, ft, lt, at_: (at(qt, g, s), g)),
        scratch_shapes=[
            pltpu.VMEM((NSA_REP * tq, 256), BF16),
            pltpu.VMEM((NSA_REP, tq), F32),
            pltpu.VMEM((NSA_REP, tq), F32),
            pltpu.VMEM((2 * LANES, tq), F32),
        ],
    )
    return pl.pallas_call(
        functools.partial(_sel_attn_kernel, tq=tq, tk=tk),
        grid_spec=grid_spec,
        out_shape=jax.ShapeDtypeStruct((T, 512), F32),
        compiler_params=_cparams(("parallel", "arbitrary")),
        name="nsa_selected_attention",
    )(*tabs, qn, selbias, kaug, vs_lohi, gates)


def _win_attn_kernel(q_ref, k0_ref, k1_ref, k2_ref, v0_ref, v1_ref, v2_ref, gate_ref, o_ref, *, tq):
    g = pl.program_id(0)
    qi = pl.program_id(1)
    lane = lax.broadcasted_iota(jnp.int32, (tq, LANES), 1)
    lo = lane < NSA_HD
    upper = lax.broadcasted_iota(jnp.int32, (LANES, tq), 0) < NSA_HD
    row = lax.broadcasted_iota(jnp.int32, (tq, tq), 0)
    qpos = qi * tq + lax.broadcasted_iota(jnp.int32, (tq, tq), 1)
    k_refs = (k0_ref, k1_ref, k2_ref)
    v_refs = (v0_ref, v1_ref, v2_ref)
    dists, masks = [], []
    for d in range(3):
        kpos = (qi - 2 + d) * tq + row
        dd = qpos - kpos
        dists.append(dd.astype(F32))
        masks.append((dd >= 0) & (dd < WINDOW) & (kpos >= 0))
    gates = gate_ref[...]
    for a in range(2):
        pv = jnp.zeros((LANES, tq), F32)
        ls = []
        for half in range(2):
            r = 2 * a + half
            qh = q_ref[:, a * LANES:(a + 1) * LANES]
            qh = jnp.where(lo if half == 0 else jnp.logical_not(lo), qh, jnp.zeros_like(qh))
            ss = []
            for d in range(3):
                sd = lax.dot_general(k_refs[d][0], qh, NT, preferred_element_type=F32)
                ss.append(jnp.where(masks[d], sd - _slope(g, r) * dists[d], NEG))
            mx = jnp.maximum(jnp.maximum(jnp.max(ss[0], axis=0, keepdims=True), jnp.max(ss[1], axis=0, keepdims=True)),
                             jnp.max(ss[2], axis=0, keepdims=True))
            l = jnp.zeros((1, tq), F32)
            for d in range(3):
                p = jnp.exp(ss[d] - mx)
                l = l + jnp.sum(p, axis=0, keepdims=True)
                vt = v_refs[d][0, half * LANES:(half + 1) * LANES, :]
                pv = pv + jnp.dot(vt, p.astype(BF16), preferred_element_type=F32)
            ls.append(l)
        l_pair = jnp.where(upper, ls[0], ls[1])
        gp = _gate_pair(gates, lane, lo, 2 * NSA_HEADS + g * NSA_REP + 2 * a)
        o_ref[:, a * LANES:(a + 1) * LANES] = jnp.transpose(pv / l_pair) * gp


def nsa_window_attention(qn, kw2, vw_lohi, gates, tq=256):
    T = qn.shape[0]
    G = NSA_KV_HEADS
    assert WINDOW == 2 * tq

    def kspec(d):
        return pl.BlockSpec((1, tq, 128), lambda g, i, d=d: (g, jnp.maximum(i - 2 + d, 0), 0))

    def vspec(d):
        return pl.BlockSpec((1, 256, tq), lambda g, i, d=d: (g, 0, jnp.maximum(i - 2 + d, 0)))

    return pl.pallas_call(
        functools.partial(_win_attn_kernel, tq=tq),
        grid=(G, T // tq),
        in_specs=[pl.BlockSpec((tq, 256), lambda g, i: (i, g)),
                  kspec(0), kspec(1), kspec(2),
                  vspec(0), vspec(1), vspec(2),
                  pl.BlockSpec((tq, 128), lambda g, i: (i, 0))],
        out_specs=pl.BlockSpec((tq, 256), lambda g, i: (i, g)),
        out_shape=jax.ShapeDtypeStruct((T, 512), F32),
        compiler_params=_cparams(("parallel", "parallel")),
        name="nsa_window_attention",
    )(qn, kw2, kw2, kw2, vw_lohi, vw_lohi, vw_lohi, gates)


def nsa_mixer(proj_n, proj_t, cmp_pe, cmp_w1, cmp_w2, q_gain, k_gain, parts=False):
    qn, kaug, vs_lohi, kw2, vw_lohi, gates = nsa_prep(proj_n, proj_t, q_gain, k_gain)
    kvc = nsa_compress(proj_n[:, C_NSA_KV:C_NSA_KV + 256], cmp_pe, cmp_w1, cmp_w2, k_gain[0])
    kc, vc = kvc[0], kvc[1]
    kc2 = jnp.concatenate([kc, kc], axis=-1).astype(BF16)
    zero = jnp.zeros_like(vc)
    vc_lohi = jnp.concatenate([vc, zero, zero, vc], axis=-1).astype(BF16)
    o_cmp, selbias = nsa_cmp_select(qn, kc2, vc_lohi, gates)
    o_slc = nsa_selected_attention(qn, selbias, kaug, vs_lohi, gates)
    o_win = nsa_window_attention(qn, kw2, vw_lohi, gates)
    if parts:
        return o_cmp + o_slc + o_win, (o_cmp, o_slc, o_win)
    return o_cmp, o_slc, o_win


def _diag_selector():
    m = np.zeros((SUB * LANES, LANES), np.float32)
    for j in range(SUB):
        for rep in range(CHUNK // SUB):
            m[j * LANES:(j + 1) * LANES, rep * SUB + j] = 1.0
    return jnp.asarray(m, dtype=BF16)


LOG2E = 1.0 / math.log(2.0)
REC_HEADS = 2


def _recur_tile(q_scr, k_scr, v_scr, g_scr, o_scr, st_scr, sel_ref, tb):
    row = lax.broadcasted_iota(jnp.int32, (CHUNK, LANES), 0)
    lane = lax.broadcasted_iota(jnp.int32, (CHUNK, LANES), 1)
    sub_row = row % SUB
    blockdiag = (row // SUB) == (lane // SUB)
    r64 = lax.broadcasted_iota(jnp.int32, (CHUNK, CHUNK), 0)
    c64 = lax.broadcasted_iota(jnp.int32, (CHUNK, CHUNK), 1)
    ltri = jnp.where(r64 >= c64, 1.0, 0.0).astype(F32)
    c_sub = lax.broadcasted_iota(jnp.int32, (SUB, CHUNK), 1)
    nsub = CHUNK // SUB

    def head_chunk(h, r0):
        qc = q_scr[h, pl.ds(r0, CHUNK), :]
        kc = k_scr[h, pl.ds(r0, CHUNK), :]
        vc = v_scr[h, pl.ds(r0, CHUNK), :]
        gc = g_scr[h, pl.ds(r0, CHUNK), :]
        b = jnp.dot(ltri, gc, precision=HIGHEST, preferred_element_type=F32)
        bend = b[CHUNK - 1:CHUNK, :]
        st = st_scr[h]
        o = lax.dot_general((qc * jnp.exp2(b)).astype(BF16), st.astype(BF16), NT, preferred_element_type=F32)
        k4 = kc.reshape(nsub, SUB, LANES)
        b4 = b.reshape(nsub, SUB, LANES)
        pieces = []
        for j in range(SUB):
            k_rep = jnp.broadcast_to(k4[:, j:j + 1, :], (nsub, SUB, LANES)).reshape(CHUNK, LANES)
            b_rep = jnp.broadcast_to(b4[:, j:j + 1, :], (nsub, SUB, LANES)).reshape(CHUNK, LANES)
            e = qc * k_rep * jnp.exp2(jnp.where(sub_row >= j, b - b_rep, NEG))
            pieces.append(e.astype(BF16))
        a_diag = jnp.dot(jnp.concatenate(pieces, axis=1), sel_ref[...], preferred_element_type=F32)
        a_diag = jnp.where(blockdiag, a_diag, 0.0)[:, 0:CHUNK]
        rows = [jnp.zeros((SUB, CHUNK), F32)]
        for i_sub in range(1, nsub):
            ref_b = b[i_sub * SUB - 1:i_sub * SUB, :]
            qt = qc[i_sub * SUB:(i_sub + 1) * SUB, :] * jnp.exp2(b[i_sub * SUB:(i_sub + 1) * SUB, :] - ref_b)
            kt = kc * jnp.exp2(jnp.minimum(ref_b - b, 0.0))
            a_i = lax.dot_general(qt.astype(BF16), kt.astype(BF16), NT, preferred_element_type=F32)
            rows.append(jnp.where(c_sub < i_sub * SUB, a_i, 0.0))
        a = jnp.concatenate(rows, axis=0) + a_diag
        o = o + jnp.dot(a.astype(BF16), vc.astype(BF16), preferred_element_type=F32)
        o_scr[h, pl.ds(r0, CHUNK), :] = o
        kend = kc * jnp.exp2(bend - b)
        st_scr[h] = st * jnp.exp2(bend) + lax.dot_general(vc.astype(BF16), kend.astype(BF16), TN,
                                                          preferred_element_type=F32)

    def chunk(c, carry):
        r0 = pl.multiple_of(c * CHUNK, CHUNK)
        for h in range(REC_HEADS):
            head_chunk(h, r0)
        return carry

    lax.fori_loop(0, tb // CHUNK, chunk, 0)


def _finish_recur(o_scr, gain_ref, gate, o_ref):
    for h in range(REC_HEADS):
        o = o_scr[h]
        ms = jnp.mean(o * o, axis=-1, keepdims=True)
        y = o * lax.rsqrt(ms + EPS) * gain_ref[...] * gate[:, h * LANES:(h + 1) * LANES]
        o_ref[:, h * LANES:(h + 1) * LANES] = y.astype(o_ref.dtype)


def _hgrn2_kernel(q_ref, f_ref, i_ref, gg_ref, lb_ref, gain_ref, sel_ref, o_ref,
                  q_scr, k_scr, v_scr, g_scr, o_scr, st_scr, *, tb):
    @pl.when(pl.program_id(1) == 0)
    def _():
        st_scr[...] = jnp.zeros(st_scr.shape, F32)

    for h in range(REC_HEADS):
        cols = slice(h * LANES, (h + 1) * LANES)
        lb = lb_ref[:, cols]
        z = f_ref[:, cols]
        f = lb + (1.0 - lb) * _sigmoid(z)
        q_scr[h] = q_ref[:, cols]
        k_scr[h] = (1.0 - lb) * _sigmoid(-z)
        v_scr[h] = i_ref[:, cols]
        g_scr[h] = jnp.log(jnp.maximum(f, TINY)) * LOG2E
    _recur_tile(q_scr, k_scr, v_scr, g_scr, o_scr, st_scr, sel_ref, tb)
    _finish_recur(o_scr, gain_ref, _sigmoid(gg_ref[...]), o_ref)


def _recur_scratch(tb):
    return ([pltpu.VMEM((REC_HEADS, tb, LANES), F32) for _ in range(5)]
            + [pltpu.VMEM((REC_HEADS, LANES, LANES), F32)])


def hgrn2_mixer(proj, lower_bound, norm_gain, tb=512):
    T = proj.shape[0]
    wide = REC_HEADS * LANES
    per = HG_HEADS // REC_HEADS

    def col(k):
        return pl.BlockSpec((tb, wide), lambda hp, i, k=k: (i, per * k + hp))

    return pl.pallas_call(
        functools.partial(_hgrn2_kernel, tb=tb),
        grid=(per, T // tb),
        in_specs=[col(0), col(1), col(2), col(3),
                  pl.BlockSpec((1, wide), lambda hp, i: (0, hp)),
                  pl.BlockSpec((1, LANES), lambda hp, i: (0, 0)),
                  pl.BlockSpec((SUB * LANES, LANES), lambda hp, i: (0, 0))],
        out_specs=pl.BlockSpec((tb, wide), lambda hp, i: (i, hp)),
        out_shape=jax.ShapeDtypeStruct((T, BRANCH_WIDTH), BF16),
        scratch_shapes=_recur_scratch(tb),
        compiler_params=_cparams(("parallel", "arbitrary")),
        name="hgrn2_mixer",
    )(proj, proj, proj, proj, lower_bound.reshape(1, -1), norm_gain.reshape(1, -1), _diag_selector())


def _gla_kernel(q_ref, k_ref, v_ref, r_ref, tail_ref, wa_ref, ba_ref, gain_ref, sel_ref, o_ref,
                q_scr, k_scr, v_scr, g_scr, o_scr, st_scr, *, tb):
    @pl.when(pl.program_id(1) == 0)
    def _():
        st_scr[...] = jnp.zeros(st_scr.shape, F32)

    lane = lax.broadcasted_iota(jnp.int32, (tb, LANES), 1)
    a = jnp.dot(tail_ref[...], wa_ref[...], precision=HIGHEST, preferred_element_type=F32) + ba_ref[...]
    log_sig = -(jnp.maximum(-a, 0.0) + jnp.log1p(jnp.exp(-jnp.abs(a))))
    q = q_ref[...] * (GLA_DK ** -0.5)
    k = k_ref[...]
    g2 = log_sig * (LOG2E / GLA_TAU)
    for h in range(REC_HEADS):
        mine = (lane < GLA_DK) if h == 0 else (lane >= GLA_DK)
        q_scr[h] = jnp.where(mine, q, 0.0)
        k_scr[h] = jnp.where(mine, k, 0.0)
        v_scr[h] = v_ref[:, h * LANES:(h + 1) * LANES]
        g_scr[h] = jnp.where(mine, g2, 0.0)
    _recur_tile(q_scr, k_scr, v_scr, g_scr, o_scr, st_scr, sel_ref, tb)
    r = r_ref[...]
    _finish_recur(o_scr, gain_ref, r * _sigmoid(r), o_ref)


def gla_mixer(proj, proj_t, w_a2, b_a, norm_gain, tb=512):
    T = proj.shape[0]
    assert REC_HEADS == 2 and LANES == 2 * GLA_DK
    wide = REC_HEADS * LANES
    cqk = C_GLA_QK // LANES
    cvr = C_GLA_VR // wide
    per = GLA_HEADS // REC_HEADS
    wa = jnp.zeros((LANES, GLA_HEADS * GLA_DK), F32).at[TAIL_GLA_A0:TAIL_GLA_A0 + GLA_RANK].set(w_a2)
    return pl.pallas_call(
        functools.partial(_gla_kernel, tb=tb),
        grid=(per, T // tb),
        in_specs=[pl.BlockSpec((tb, LANES), lambda hp, i: (i, cqk + hp)),
                  pl.BlockSpec((tb, LANES), lambda hp, i: (i, cqk + per + hp)),
                  pl.BlockSpec((tb, wide), lambda hp, i: (i, cvr + hp)),
                  pl.BlockSpec((tb, wide), lambda hp, i: (i, cvr + per + hp)),
                  pl.BlockSpec((tb, LANES), lambda hp, i: (i, 0)),
                  pl.BlockSpec((LANES, LANES), lambda hp, i: (0, hp)),
                  pl.BlockSpec((1, LANES), lambda hp, i: (0, hp)),
                  pl.BlockSpec((1, LANES), lambda hp, i: (0, 0)),
                  pl.BlockSpec((SUB * LANES, LANES), lambda hp, i: (0, 0))],
        out_specs=pl.BlockSpec((tb, wide), lambda hp, i: (i, hp)),
        out_shape=jax.ShapeDtypeStruct((T, BRANCH_WIDTH), BF16),
        scratch_shapes=_recur_scratch(tb),
        compiler_params=_cparams(("parallel", "arbitrary")),
        name="gla_mixer",
    )(proj, proj, proj, proj, proj_t, wa, b_a.reshape(1, -1), norm_gain.reshape(1, -1), _diag_selector())


def _merge_kernel(oc_ref, os_ref, ow_ref, ob_ref, og_ref, wb_ref, g0_ref, g1_ref, g2_ref, o_ref):
    o_a = (oc_ref[...] + os_ref[...] + ow_ref[...]).astype(BF16)
    acc = _sigmoid(g0_ref[...]) * jnp.dot(o_a, wb_ref[0], preferred_element_type=F32)
    acc = acc + _sigmoid(g1_ref[...]) * jnp.dot(ob_ref[...], wb_ref[1], preferred_element_type=F32)
    acc = acc + _sigmoid(g2_ref[...]) * jnp.dot(og_ref[...], wb_ref[2], preferred_element_type=F32)
    o_ref[...] = acc.astype(o_ref.dtype)


def merge_branches(proj, o_cmp, o_slc, o_win, o_b, o_c, w_branch_bf16, tm=512, tn=512):
    T = proj.shape[0]
    W = BRANCH_WIDTH
    nj = D_MODEL // tn
    ospec = pl.BlockSpec((tm, W), lambda i, j: (i, 0))

    def gspec(n):
        return pl.BlockSpec((tm, tn), lambda i, j, n=n: (i, n * nj + j))

    return pl.pallas_call(
        _merge_kernel,
        grid=(T // tm, nj),
        in_specs=[ospec, ospec, ospec, ospec, ospec,
                  pl.BlockSpec((3, W, tn), lambda i, j: (0, 0, j)),
                  gspec(0), gspec(1), gspec(2)],
        out_specs=pl.BlockSpec((tm, tn), lambda i, j: (i, j)),
        out_shape=jax.ShapeDtypeStruct((T, D_MODEL), BF16),
        compiler_params=_cparams(("parallel", "arbitrary")),
        name="merge_branches",
    )(o_cmp, o_slc, o_win, o_b, o_c, w_branch_bf16, proj, proj, proj)


MOE_TILE = 256


def _route_kernel(x_ref, g_ref, wr_ref, br_ref, h_ref, route_ref, cnt_ref, carry, *, tm):
    i = pl.program_id(0)

    @pl.when(i == 0)
    def _():
        carry[...] = jnp.zeros(carry.shape, F32)

    x = x_ref[...]
    ms = jnp.mean(x * x, axis=-1, keepdims=True)
    h = x * lax.rsqrt(ms + EPS) * g_ref[...]
    h_ref[...] = h
    logits = jnp.dot(h, wr_ref[...], precision=HIGHEST, preferred_element_type=F32) + br_ref[...]
    lane = lax.broadcasted_iota(jnp.int32, (tm, LANES), 1)

    def masked_softmax(mask):
        l = jnp.where(mask, logits, NEG)
        e = jnp.where(mask, jnp.exp(l - jnp.max(l, axis=1, keepdims=True)), 0.0)
        return e / jnp.sum(e, axis=1, keepdims=True)

    def top1(prob, mask):
        p = jnp.max(jnp.where(mask, prob, -1.0), axis=1, keepdims=True)
        idx = jnp.min(jnp.where(mask & (prob == p), lane, LANES), axis=1, keepdims=True)
        return p, idx

    gmask = lane < N_GROUPS
    gw, gidx = top1(masked_softmax(gmask), gmask)
    emask = (lane >= N_GROUPS) & (lane < N_GROUPS + N_EXPERTS) & ((lane - N_GROUPS) // EXPERTS_PER_GROUP == gidx)
    eprob = masked_softmax(emask)
    p1, i1 = top1(eprob, emask)
    rest = emask & (lane != i1)
    p2, i2 = top1(eprob, rest)
    psum = p1 + p2
    w1 = gw * (p1 / psum)
    w2 = gw * (p2 / psum)
    e1 = i1 - N_GROUPS
    e2 = i2 - N_GROUPS

    onehot = jnp.where((lane == e1) | (lane == e2), 1.0, 0.0)
    r = lax.broadcasted_iota(jnp.int32, (tm, tm), 0)
    c = lax.broadcasted_iota(jnp.int32, (tm, tm), 1)
    strict = jnp.where(r > c, 1.0, 0.0).astype(BF16)
    before = jnp.dot(strict, onehot.astype(BF16), preferred_element_type=F32) + carry[0:1, :]
    rank1 = jnp.sum(jnp.where(lane == e1, before, 0.0), axis=1, keepdims=True)
    rank2 = jnp.sum(jnp.where(lane == e2, before, 0.0), axis=1, keepdims=True)
    total = carry[0:1, :] + jnp.sum(onehot, axis=0, keepdims=True)
    carry[...] = jnp.broadcast_to(total, carry.shape)
    cnt_ref[...] = jnp.broadcast_to(total, cnt_ref.shape)

    out = jnp.where(lane == 0, w1, 0.0)
    out = jnp.where(lane == 1, w2, out)
    out = jnp.where(lane == 2, e1.astype(F32), out)
    out = jnp.where(lane == 3, e2.astype(F32), out)
    out = jnp.where(lane == 4, rank1, out)
    out = jnp.where(lane == 5, rank2, out)
    route_ref[...] = out


def moe_route(x, gain, w_grp, b_grp, w_exp, b_exp, tm=512):
    T, D = x.shape
    n_pad = LANES - N_GROUPS - N_EXPERTS
    wr = jnp.concatenate([w_grp, w_exp, jnp.zeros((D, n_pad), F32)], axis=1)
    br = jnp.concatenate([b_grp, b_exp, jnp.zeros((n_pad,), F32)]).reshape(1, LANES)
    return pl.pallas_call(
        functools.partial(_route_kernel, tm=tm),
        grid=(T // tm,),
        in_specs=[pl.BlockSpec((tm, D), lambda i: (i, 0)),
                  pl.BlockSpec((1, D), lambda i: (0, 0)),
                  pl.BlockSpec((D, LANES), lambda i: (0, 0)),
                  pl.BlockSpec((1, LANES), lambda i: (0, 0))],
        out_specs=(pl.BlockSpec((tm, D), lambda i: (i, 0)),
                   pl.BlockSpec((tm, LANES), lambda i: (i, 0)),
                   pl.BlockSpec((8, LANES), lambda i: (0, 0))),
        out_shape=(jax.ShapeDtypeStruct((T, D), F32),
                   jax.ShapeDtypeStruct((T, LANES), F32),
                   jax.ShapeDtypeStruct((8, LANES), F32)),
        scratch_shapes=[pltpu.VMEM((8, LANES), F32)],
        compiler_params=_cparams(("arbitrary",)),
        name="moe_route",
    )(x, gain.reshape(1, D), wr, br)


def _row_dma(src, src_row, dst, dst_row, sem):
    return pltpu.make_async_copy(src.at[pl.ds(src_row, 1), :], dst.at[pl.ds(dst_row, 1), :], sem)


def _rows_wait(src, dst, n, sem):
    pltpu.make_async_copy(src.at[pl.ds(0, n), :], dst.at[pl.ds(0, n), :], sem).wait()


def _expert_kernel(tile_expert, n_used, rcur_ref, rnext_ref, h_hbm, wgu_ref, wd_ref, o_ref,
                   xbuf, wgu_bf, wd_bf, sem):
    i = pl.program_id(0)
    used = i < n_used[0]
    slot = i % 2

    def fetch(rref, s):
        def body(r, c):
            _row_dma(h_hbm, rref[0, 0, r], xbuf.at[s], r, sem.at[s]).start()
            return c

        lax.fori_loop(0, MOE_TILE, body, 0, unroll=8)

    @pl.when(i == 0)
    def _():
        fetch(rcur_ref, 0)

    @pl.when(i + 1 < n_used[0])
    def _():
        fetch(rnext_ref, 1 - slot)

    @pl.when(used)
    def _():
        prev = tile_expert[jnp.maximum(i - 1, 0)]

        @pl.when((i == 0) | (tile_expert[i] != prev))
        def _():
            wgu_bf[...] = wgu_ref[0, 0].astype(BF16)
            wd_bf[...] = wd_ref[0, 0].astype(BF16)

        _rows_wait(h_hbm, xbuf.at[slot], MOE_TILE, sem.at[slot])
        x = xbuf[slot].astype(BF16)
        gu = jnp.dot(x, wgu_bf[...], preferred_element_type=F32)
        gate = gu[:, 0:D_FF_EXPERT]
        up = gu[:, D_FF_EXPERT:2 * D_FF_EXPERT]
        act = gate * _sigmoid(gate) * up
        o_ref[...] = jnp.dot(act.astype(BF16), wd_bf[...], preferred_element_type=F32)

    @pl.when(jnp.logical_not(used))
    def _():
        o_ref[...] = jnp.zeros(o_ref.shape, F32)


def moe_experts(h, row_token, tile_expert, n_used, w_gate_up, w_down, layer):
    T, D = h.shape
    n_tiles = row_token.shape[0] // MOE_TILE
    rows = row_token.reshape(n_tiles, 1, MOE_TILE)
    grid_spec = pltpu.PrefetchScalarGridSpec(
        num_scalar_prefetch=2,
        grid=(n_tiles,),
        in_specs=[
            pl.BlockSpec((1, 1, MOE_TILE), lambda i, te, nu: (i, 0, 0), memory_space=pltpu.SMEM),
            pl.BlockSpec((1, 1, MOE_TILE), lambda i, te, nu: (jnp.minimum(i + 1, n_tiles - 1), 0, 0),
                         memory_space=pltpu.SMEM),
            pl.BlockSpec(memory_space=pl.ANY),
            pl.BlockSpec((1, 1, D, 2 * D_FF_EXPERT), lambda i, te, nu: (layer, te[i], 0, 0)),
            pl.BlockSpec((1, 1, D_FF_EXPERT, D), lambda i, te, nu: (layer, te[i], 0, 0)),
        ],
        out_specs=pl.BlockSpec((MOE_TILE, D), lambda i, te, nu: (i, 0)),
        scratch_shapes=[
            pltpu.VMEM((2, MOE_TILE, D), F32),
            pltpu.VMEM((D, 2 * D_FF_EXPERT), BF16),
            pltpu.VMEM((D_FF_EXPERT, D), BF16),
            pltpu.SemaphoreType.DMA((2,)),
        ],
    )
    return pl.pallas_call(
        _expert_kernel,
        grid_spec=grid_spec,
        out_shape=jax.ShapeDtypeStruct((n_tiles * MOE_TILE, D), F32),
        compiler_params=_cparams(("arbitrary",)),
        name="moe_experts",
    )(tile_expert, n_used, rows, rows, h, w_gate_up, w_down)


def _combine_kernel(dcur_ref, dnext_ref, x_ref, route_ref, y_hbm, o_ref, buf, sem, *, tm):
    i = pl.program_id(0)
    n = pl.num_programs(0)
    slot = i % 2

    def fetch(dref, s):
        def body(r, c):
            _row_dma(y_hbm, dref[0, 0, 2 * r], buf.at[s, 0], r, sem.at[s]).start()
            _row_dma(y_hbm, dref[0, 0, 2 * r + 1], buf.at[s, 1], r, sem.at[s]).start()
            return c

        lax.fori_loop(0, tm, body, 0, unroll=8)

    @pl.when(i == 0)
    def _():
        fetch(dcur_ref, 0)

    @pl.when(i + 1 < n)
    def _():
        fetch(dnext_ref, 1 - slot)

    _rows_wait(y_hbm, buf.at[slot, 0], tm, sem.at[slot])
    _rows_wait(y_hbm, buf.at[slot, 1], tm, sem.at[slot])
    route = route_ref[...]
    lane = lax.broadcasted_iota(jnp.int32, route.shape, 1)
    w1 = jnp.sum(jnp.where(lane == 0, route, 0.0), axis=1, keepdims=True)
    w2 = jnp.sum(jnp.where(lane == 1, route, 0.0), axis=1, keepdims=True)
    o_ref[...] = x_ref[...] + (w1 * buf[slot, 0] + w2 * buf[slot, 1])


def moe_combine(x, ys, route, dest, tm=256):
    T, D = x.shape
    n = T // tm
    dest_tiles = dest.reshape(n, 1, 2 * tm)
    return pl.pallas_call(
        functools.partial(_combine_kernel, tm=tm),
        grid=(n,),
        in_specs=[pl.BlockSpec((1, 1, 2 * tm), lambda i: (i, 0, 0), memory_space=pltpu.SMEM),
                  pl.BlockSpec((1, 1, 2 * tm), lambda i: (jnp.minimum(i + 1, n - 1), 0, 0), memory_space=pltpu.SMEM),
                  pl.BlockSpec((tm, D), lambda i: (i, 0)),
                  pl.BlockSpec((tm, LANES), lambda i: (i, 0)),
                  pl.BlockSpec(memory_space=pl.ANY)],
        out_specs=pl.BlockSpec((tm, D), lambda i: (i, 0)),
        out_shape=jax.ShapeDtypeStruct((T, D), F32),
        scratch_shapes=[pltpu.VMEM((2, 2, tm, D), F32), pltpu.SemaphoreType.DMA((2,))],
        compiler_params=_cparams(("arbitrary",)),
        name="moe_combine",
    )(dest_tiles, dest_tiles, x, route, ys)


def hierarchical_moe(x, gain, w_grp, b_grp, w_exp, b_exp, w_gate_up, w_down, layer):
    T, D = x.shape
    h, route, cnt = moe_route(x, gain, w_grp, b_grp, w_exp, b_exp)
    expert = route[:, 2:4].astype(jnp.int32)
    rank = route[:, 4:6].astype(jnp.int32)
    counts = cnt[0, 0:N_EXPERTS].astype(jnp.int32)
    padded = ((counts + MOE_TILE - 1) // MOE_TILE) * MOE_TILE
    ends = jnp.cumsum(padded)
    offs = ends - padded
    dest = offs[expert] + rank
    n_rows = 2 * T + N_EXPERTS * MOE_TILE
    n_tiles = n_rows // MOE_TILE
    n_used = (ends[-1] // MOE_TILE).astype(jnp.int32)
    tile_start = jnp.arange(n_tiles, dtype=jnp.int32) * MOE_TILE
    tile_expert = jnp.sum((ends[None, :] <= tile_start[:, None]).astype(jnp.int32), axis=1)
    last_expert = tile_expert[jnp.maximum(n_used - 1, 0)]
    tile_expert = jnp.where(tile_start < ends[-1], tile_expert, last_expert)
    row_token = jnp.zeros((n_rows,), jnp.int32).at[dest.reshape(-1)].set(
        jnp.repeat(jnp.arange(T, dtype=jnp.int32), 2))
    ys = moe_experts(h, row_token, tile_expert, n_used.reshape(1), w_gate_up, w_down, layer)
    return moe_combine(x, ys, route, dest)


def _shift_cast_kernel(a_ref, b_ref, c_ref, o_ref, *, shift):
    blocks = [a_ref[0], b_ref[0], c_ref[0]]
    if shift:
        lane = lax.broadcasted_iota(jnp.int32, blocks[0].shape, 1)
        keep = lane < LANES - shift
        rolled = [pltpu.roll(b, LANES - shift, axis=1) for b in blocks]
        blocks = [jnp.where(keep, rolled[0], rolled[1]), jnp.where(keep, rolled[1], rolled[2])]
    o_ref[:, 0:LANES] = blocks[0].astype(o_ref.dtype)
    o_ref[:, LANES:2 * LANES] = blocks[1].astype(o_ref.dtype)


def shift_cast_columns(w, layer, col0, n_cols):
    _, D, n_in = w.shape
    b0, shift = divmod(col0, LANES)
    last = (n_in - 1) // LANES

    def spec(k):
        return pl.BlockSpec((1, D, LANES), lambda j, k=k: (layer, 0, jnp.minimum(b0 + 2 * j + k, last)))

    return pl.pallas_call(
        functools.partial(_shift_cast_kernel, shift=shift),
        grid=(n_cols // (2 * LANES),),
        in_specs=[spec(0), spec(1), spec(2)],
        out_specs=pl.BlockSpec((D, 2 * LANES), lambda j: (0, j)),
        out_shape=jax.ShapeDtypeStruct((D, n_cols), BF16),
        compiler_params=_cparams(("parallel",)),
        name="shift_cast_columns",
    )(w, w, w)


def _gate_cols_kernel(a_ref, b_ref, o_ref):
    lane = lax.broadcasted_iota(jnp.int32, o_ref.shape, 1)
    n_gate, n_a = W_GATE[1], W_GLA_A[1]
    out = jnp.where(lane < n_gate, a_ref[0], jnp.where(lane < n_gate + n_a, b_ref[0], 0.0))
    o_ref[...] = out.astype(o_ref.dtype)


def gate_columns(w, layer):
    _, D, _ = w.shape
    assert W_GATE[0] % LANES == 0 and W_GLA_A[0] % LANES == W_GATE[1]
    return pl.pallas_call(
        _gate_cols_kernel,
        grid=(1,),
        in_specs=[pl.BlockSpec((1, D, LANES), lambda j: (layer, 0, W_GATE[0] // LANES)),
                  pl.BlockSpec((1, D, LANES), lambda j: (layer, 0, W_GLA_A[0] // LANES))],
        out_specs=pl.BlockSpec((D, LANES), lambda j: (0, 0)),
        out_shape=jax.ShapeDtypeStruct((D, LANES), BF16),
        compiler_params=_cparams(("arbitrary",)),
        name="gate_columns",
    )(w, w)


def kernel(x, norm_mix, w_in, cmp_pe, cmp_w1, cmp_w2, q_norm, k_norm, hg_lb_logits, hg_norm, gla_w_a2, gla_b_a,
           gla_norm, w_branch, w_out, norm_ffn, w_grp, b_grp, w_exp, b_exp, w_gate_up, w_down):
    B, T, D = x.shape
    assert B == 1 and D == D_MODEL
    xt = x[0]
    p_lb = jax.nn.softmax(hg_lb_logits.astype(F32), axis=0)
    lower_bounds = jnp.cumsum(p_lb, axis=0) - p_lb[0]
    for l in range(DEPTH):
        w_n = shift_cast_columns(w_in, l, *W_NSA)
        w_r = shift_cast_columns(w_in, l, *W_REC)
        w_m = shift_cast_columns(w_in, l, *W_MERGE)
        w_t = gate_columns(w_in, l)
        h = rmsnorm_bf16(xt, norm_mix[l])
        proj_n = matmul_bf16(h, w_n, tm=2048, tn=256, name="in_proj_nsa")
        proj_r = matmul_bf16(h, w_r, tm=2048, tn=512, name="in_proj_rec")
        proj_m = matmul_bf16(h, w_m, tm=2048, tn=512, name="in_proj_merge")
        proj_t = matmul_bf16(h, w_t, tm=2048, tn=LANES, name="in_proj_gates")
        o_cmp, o_slc, o_win = nsa_mixer(proj_n, proj_t, cmp_pe[l], cmp_w1[l], cmp_w2[l], q_norm[l], k_norm[l])
        o_b = hgrn2_mixer(proj_r, lower_bounds[l], hg_norm[l])
        o_c = gla_mixer(proj_r, proj_t, gla_w_a2[l], gla_b_a[l], gla_norm[l])
        merged = merge_branches(proj_m, o_cmp, o_slc, o_win, o_b, o_c, w_branch[l].astype(BF16))
        xt = matmul_bf16(merged, w_out[l].astype(BF16), res=xt, tm=1024, tn=512, name="out_proj")
        xt = hierarchical_moe(xt, norm_ffn[l], w_grp[l], b_grp[l], w_exp[l], b_exp[l], w_gate_up, w_down, l)
    return xt[None]
```

```python
import functools
import math

import numpy as np
import jax
import jax.numpy as jnp
from jax import lax
from jax.experimental import pallas as pl
from jax.experimental.pallas import tpu as pltpu

F32 = jnp.float32
BF16 = jnp.bfloat16
HIGHEST = lax.Precision.HIGHEST

D_MODEL = 2048
DEPTH = 2
BRANCH_WIDTH = D_MODEL // 4
NSA_HEADS = 8
NSA_KV_HEADS = 2
NSA_REP = NSA_HEADS // NSA_KV_HEADS
NSA_HD = 64
CMP_LEN = 32
CMP_STRIDE = 16
SLC_LEN = 64
SLC_TOP = 16
WINDOW = 512
HG_HEADS = 4
GLA_HEADS = 4
GLA_DK = 64
GLA_RANK = 16
GLA_TAU = 16.0
CHUNK = 64
SUB = 16
N_GROUPS = 4
EXPERTS_PER_GROUP = 8
N_EXPERTS = N_GROUPS * EXPERTS_PER_GROUP
D_FF_EXPERT = D_MODEL // 4
EPS = 1e-6
NEG = -1e30
FORCE = 1e4
TINY = 1e-30

LANES = 128
SUBLANES = 8
VMEM_LIMIT = 56 * 1024 * 1024

W_NSA = (0, 1536)
W_REC = (1304, 3584)
W_MERGE = (4904, 6144)
W_GLA_A_BLOCK = (4864, 128)
C_NSA_KV = 512
C_NSA_GATE = 1280
C_GLA_QK = 2048
C_GLA_VR = 2560
TAIL_GLA_A0 = 4888 - W_GLA_A_BLOCK[0]

NT = (((1,), (1,)), ((), ()))
TN = (((0,), (0,)), ((), ()))


def _cparams(sem):
    return pltpu.CompilerParams(dimension_semantics=sem, vmem_limit_bytes=VMEM_LIMIT)


def _sigmoid(x):
    return 1.0 / (1.0 + jnp.exp(-x))


def _norm_kernel(x_ref, g_ref, o_ref):
    x = x_ref[...]
    ms = jnp.mean(x * x, axis=-1, keepdims=True)
    o_ref[...] = (x * lax.rsqrt(ms + EPS) * g_ref[...]).astype(o_ref.dtype)


def rmsnorm_bf16(x, gain, tm=512):
    T, D = x.shape
    return pl.pallas_call(
        _norm_kernel,
        grid=(T // tm,),
        in_specs=[pl.BlockSpec((tm, D), lambda i: (i, 0)), pl.BlockSpec((1, D), lambda i: (0, 0))],
        out_specs=pl.BlockSpec((tm, D), lambda i: (i, 0)),
        out_shape=jax.ShapeDtypeStruct((T, D), BF16),
        compiler_params=_cparams(("parallel",)),
        name="rmsnorm_bf16",
    )(x, gain.reshape(1, D))


def _mm_kernel(a_ref, b_ref, o_ref):
    o_ref[...] = jnp.dot(a_ref[...], b_ref[...], preferred_element_type=F32)


def _mm_res_kernel(a_ref, b_ref, r_ref, o_ref):
    o_ref[...] = r_ref[...] + jnp.dot(a_ref[...], b_ref[...], preferred_element_type=F32)


def matmul_bf16(a, b, res=None, tm=1024, tn=512, name="matmul_bf16"):
    T, K = a.shape
    N = b.shape[1]
    in_specs = [pl.BlockSpec((tm, K), lambda i, j: (i, 0)), pl.BlockSpec((K, tn), lambda i, j: (0, j))]
    args = [a, b]
    kern = _mm_kernel
    if res is not None:
        in_specs.append(pl.BlockSpec((tm, tn), lambda i, j: (i, j)))
        args.append(res)
        kern = _mm_res_kernel
    return pl.pallas_call(
        kern,
        grid=(T // tm, N // tn),
        in_specs=in_specs,
        out_specs=pl.BlockSpec((tm, tn), lambda i, j: (i, j)),
        out_shape=jax.ShapeDtypeStruct((T, N), F32),
        compiler_params=_cparams(("parallel", "arbitrary")),
        name=name,
    )(*args)


def _half_rmsnorm(x, gain2, lo):
    x2 = x * x
    s_lo = jnp.sum(jnp.where(lo, x2, 0.0), axis=1, keepdims=True)
    s_hi = jnp.sum(jnp.where(lo, 0.0, x2), axis=1, keepdims=True)
    ms = jnp.where(lo, s_lo, s_hi) * (1.0 / NSA_HD)
    return x * lax.rsqrt(ms + EPS) * gain2


def _nsa_prep_kernel(q_ref, kvc_ref, kvs_ref, kvw_ref, tail_ref, qg_ref, kg_ref,
                     qn_ref, kaug_ref, vs_ref, kw_ref, vw_ref, gate_ref, *, tm):
    i = pl.program_id(0)
    lane = lax.broadcasted_iota(jnp.int32, (tm, LANES), 1)
    lo = lane < NSA_HD
    qg = qg_ref[...]
    for c in range(4):
        x = q_ref[:, c * LANES:(c + 1) * LANES]
        qn_ref[:, c * LANES:(c + 1) * LANES] = (_half_rmsnorm(x, qg, lo) * (NSA_HD ** -0.5)).astype(BF16)

    def dup(kn):
        rolled = pltpu.roll(kn, NSA_HD, axis=1)
        return jnp.where(lo, kn, rolled), jnp.where(lo, rolled, kn)

    vt_zero = jnp.zeros((NSA_HD, tm), BF16)

    def store_vt(v_pair, ref):
        vt = jnp.transpose(v_pair)
        for g in range(NSA_KV_HEADS):
            head = vt[g * NSA_HD:(g + 1) * NSA_HD, :].astype(BF16)
            ref[g, 0:NSA_HD, :] = head
            ref[g, NSA_HD:LANES, :] = vt_zero
            ref[g, LANES:LANES + NSA_HD, :] = vt_zero
            ref[g, LANES + NSA_HD:2 * LANES, :] = head

    row = i * tm + lax.broadcasted_iota(jnp.int32, (tm, LANES), 0)
    onehot = jnp.where(row // SLC_LEN == lane, 1.0, 0.0).astype(BF16)
    k_extra = jnp.where(lane == NSA_HD, (row % SLC_LEN).astype(F32), 0.0)
    ks = dup(_half_rmsnorm(kvs_ref[:, 0:LANES], kg_ref[1:2, :], lo))
    store_vt(kvs_ref[:, LANES:2 * LANES], vs_ref)
    kw = dup(_half_rmsnorm(kvw_ref[:, 0:LANES], kg_ref[2:3, :], lo))
    store_vt(kvw_ref[:, LANES:2 * LANES], vw_ref)
    for g in range(NSA_KV_HEADS):
        kaug_ref[g, :, 0:LANES] = onehot
        kaug_ref[g, :, LANES:2 * LANES] = jnp.where(lo, ks[g], k_extra).astype(BF16)
        kw_ref[g] = kw[g].astype(BF16)
    gate_ref[...] = _sigmoid(tail_ref[...])


def nsa_prep(proj_n, q_gain, k_gain, tm=512):
    T = proj_n.shape[0]
    qg2 = jnp.tile(q_gain.reshape(1, NSA_HD), (1, 2))
    kg2 = jnp.tile(k_gain.reshape(3, NSA_HD), (1, 2))
    G = NSA_KV_HEADS
    out_shape = (
        jax.ShapeDtypeStruct((T, 512), BF16),
        jax.ShapeDtypeStruct((G, T, 256), BF16),
        jax.ShapeDtypeStruct((G, 256, T), BF16),
        jax.ShapeDtypeStruct((G, T, 128), BF16),
        jax.ShapeDtypeStruct((G, 256, T), BF16),
        jax.ShapeDtypeStruct((T, 128), F32),
    )
    return pl.pallas_call(
        functools.partial(_nsa_prep_kernel, tm=tm),
        grid=(T // tm,),
        in_specs=[
            pl.BlockSpec((tm, 512), lambda i: (i, 0)),
            pl.BlockSpec((tm, 256), lambda i: (i, C_NSA_KV // 256)),
            pl.BlockSpec((tm, 256), lambda i: (i, C_NSA_KV // 256 + 1)),
            pl.BlockSpec((tm, 256), lambda i: (i, C_NSA_KV // 256 + 2)),
            pl.BlockSpec((tm, 128), lambda i: (i, C_NSA_GATE // 128)),
            pl.BlockSpec((1, 128), lambda i: (0, 0)),
            pl.BlockSpec((3, 128), lambda i: (0, 0)),
        ],
        out_specs=(
            pl.BlockSpec((tm, 512), lambda i: (i, 0)),
            pl.BlockSpec((G, tm, 256), lambda i: (0, i, 0)),
            pl.BlockSpec((G, 256, tm), lambda i: (0, 0, i)),
            pl.BlockSpec((G, tm, 128), lambda i: (0, i, 0)),
            pl.BlockSpec((G, 256, tm), lambda i: (0, 0, i)),
            pl.BlockSpec((tm, 128), lambda i: (i, 0)),
        ),
        out_shape=out_shape,
        compiler_params=_cparams(("parallel",)),
        name="nsa_prep",
    )(proj_n, proj_n, proj_n, proj_n, proj_n, qg2, kg2)


def _gelu_tanh(x):
    c = math.sqrt(2.0 / math.pi)
    return 0.5 * x * (1.0 + jnp.tanh(c * (x + 0.044715 * (x * x * x))))


def _compress_kernel(a_ref, pe_ref, w1_ref, w2_ref, kg_ref, o_ref, *, nb):
    kind = pl.program_id(0)
    a_lo = a_ref[0, 0, 0:nb, :]
    a_hi = a_ref[0, 0, 1:nb + 1, :]
    blocks = jnp.concatenate([a_lo, a_hi], axis=1) + pe_ref[0]
    h1 = jnp.dot(blocks.astype(BF16), w1_ref[0].astype(BF16), preferred_element_type=F32)
    y = jnp.dot(_gelu_tanh(h1).astype(BF16), w2_ref[0].astype(BF16), preferred_element_type=F32)
    ms = jnp.mean(y * y, axis=-1, keepdims=True)
    yn = y * lax.rsqrt(ms + EPS) * kg_ref[...]
    o_ref[0, 0] = jnp.where(kind == 0, yn, y)


def nsa_compress(kv_cmp, cmp_pe, cmp_w1, cmp_w2, k_gain0):
    T = kv_cmp.shape[0]
    nb = T // CMP_STRIDE
    G = NSA_KV_HEADS
    a = kv_cmp.reshape(T, 2, G, NSA_HD).transpose(1, 2, 0, 3).reshape(2, G, nb, CMP_STRIDE * NSA_HD)
    a = jnp.pad(a, ((0, 0), (0, 0), (0, 8), (0, 0)))
    pe = cmp_pe.reshape(2, 1, CMP_LEN * NSA_HD)
    return pl.pallas_call(
        functools.partial(_compress_kernel, nb=nb),
        grid=(2, G),
        in_specs=[
            pl.BlockSpec((1, 1, nb + 8, CMP_STRIDE * NSA_HD), lambda k, g: (k, g, 0, 0)),
            pl.BlockSpec((1, 1, CMP_LEN * NSA_HD), lambda k, g: (k, 0, 0)),
            pl.BlockSpec((1, CMP_LEN * NSA_HD, NSA_HD), lambda k, g: (k, 0, 0)),
            pl.BlockSpec((1, NSA_HD, NSA_HD), lambda k, g: (k, 0, 0)),
            pl.BlockSpec((1, NSA_HD), lambda k, g: (0, 0)),
        ],
        out_specs=pl.BlockSpec((1, 1, nb, NSA_HD), lambda k, g: (k, g, 0, 0)),
        out_shape=jax.ShapeDtypeStruct((2, G, nb, NSA_HD), F32),
        compiler_params=_cparams(("arbitrary", "arbitrary")),
        name="nsa_compress",
    )(a, pe, cmp_w1, cmp_w2, k_gain0.reshape(1, NSA_HD))


def _slope(g, r):
    return jnp.where(g == 0, 2.0 ** -(r + 1), 2.0 ** -(NSA_REP + r + 1)).astype(F32)


def _gate_pair(gates, lane, lo, col_even):
    ge = jnp.sum(jnp.where(lane == col_even, gates, 0.0), axis=1, keepdims=True)
    go = jnp.sum(jnp.where(lane == col_even + 1, gates, 0.0), axis=1, keepdims=True)
    return jnp.where(lo, ge, go)


def _cmp_sel_kernel(q_ref, kc_ref, vc_ref, gate_ref, mt_ref, o_ref, sel_ref, *, tq, nb, ns_pad):
    g = pl.program_id(0)
    qi = pl.program_id(1)
    t0 = qi * tq
    lane = lax.broadcasted_iota(jnp.int32, (tq, LANES), 1)
    lo = lane < NSA_HD
    n_idx = lax.broadcasted_iota(jnp.int32, (nb, tq), 0)
    t_idx = t0 + lax.broadcasted_iota(jnp.int32, (nb, tq), 1)
    dist = (t_idx - (n_idx * CMP_STRIDE + (CMP_LEN - 1))).astype(F32)
    vis = dist >= 0.0
    kc = kc_ref[0]
    imp = jnp.zeros((nb, tq), F32)
    gates = gate_ref[...]
    for a in range(2):
        acc = jnp.zeros((tq, LANES), F32)
        for half in range(2):
            r = 2 * a + half
            qh = q_ref[:, a * LANES:(a + 1) * LANES]
            qh = jnp.where(lo if half == 0 else jnp.logical_not(lo), qh, jnp.zeros_like(qh))
            s = lax.dot_general(kc, qh, NT, preferred_element_type=F32)
            s = jnp.where(vis, s - _slope(g, r) * dist, NEG)
            mx = jnp.max(s, axis=0, keepdims=True)
            e = jnp.where(vis, jnp.exp(s - mx), 0.0)
            den = jnp.sum(e, axis=0, keepdims=True)
            p = e * jnp.where(den > 0.0, 1.0 / den, 0.0)
            imp = imp + p
            v = vc_ref[0, :, half * LANES:(half + 1) * LANES]
            acc = acc + lax.dot_general(p.astype(BF16), v, TN, preferred_element_type=F32)
        gp = _gate_pair(gates, lane, lo, g * NSA_REP + 2 * a)
        o_ref[:, a * LANES:(a + 1) * LANES] = acc * gp

    score = jnp.dot(mt_ref[...], imp, precision=HIGHEST, preferred_element_type=F32)
    blk = lax.broadcasted_iota(jnp.int32, (ns_pad, tq), 0)
    cur = (t0 + lax.broadcasted_iota(jnp.int32, (ns_pad, tq), 1)) // SLC_LEN
    forced = (blk == 0) | (blk == cur) | (blk == cur - 1)
    score = jnp.where(forced, FORCE, score)
    score = jnp.where(blk <= cur, score, -1.0)
    sel = jnp.zeros((ns_pad, tq), F32)
    for _ in range(SLC_TOP):
        mx = jnp.max(score, axis=0, keepdims=True)
        idx = jnp.min(jnp.where(score == mx, blk, ns_pad), axis=0, keepdims=True)
        hit = blk == idx
        sel = jnp.where(hit, 1.0, sel)
        score = jnp.where(hit, -jnp.inf, score)
    bias_t = jnp.where((sel > 0.0) & (blk <= cur), 0.0, NEG)
    sel_ref[0] = jnp.transpose(bias_t).astype(BF16)


def _score_matrix(nb, ns_pad):
    ratio, span = SLC_LEN // CMP_STRIDE, CMP_LEN // CMP_STRIDE
    n_cmp = nb - 1
    m = np.zeros((ns_pad, nb), np.float32)
    for s in range(nb // ratio):
        for mm in range(ratio):
            for nn in range(span):
                c = ratio * s + mm - nn
                if 0 <= c < n_cmp:
                    m[s, c] += 1.0
    return jnp.asarray(m)


def nsa_cmp_select(qn, kc2, vc_lohi, gates, tq=256):
    T = qn.shape[0]
    nb = T // CMP_STRIDE
    ns_pad = LANES
    G = NSA_KV_HEADS
    return pl.pallas_call(
        functools.partial(_cmp_sel_kernel, tq=tq, nb=nb, ns_pad=ns_pad),
        grid=(G, T // tq),
        in_specs=[
            pl.BlockSpec((tq, 256), lambda g, i: (i, g)),
            pl.BlockSpec((1, nb, 128), lambda g, i: (g, 0, 0)),
            pl.BlockSpec((1, nb, 256), lambda g, i: (g, 0, 0)),
            pl.BlockSpec((tq, 128), lambda g, i: (i, 0)),
            pl.BlockSpec((ns_pad, nb), lambda g, i: (0, 0)),
        ],
        out_specs=(
            pl.BlockSpec((tq, 256), lambda g, i: (i, g)),
            pl.BlockSpec((1, tq, ns_pad), lambda g, i: (g, i, 0)),
        ),
        out_shape=(jax.ShapeDtypeStruct((T, 512), F32), jax.ShapeDtypeStruct((G, T, ns_pad), BF16)),
        compiler_params=_cparams(("parallel", "parallel")),
        name="nsa_cmp_select",
    )(qn, kc2, vc_lohi, gates, _score_matrix(nb, ns_pad))


def _sel_attn_kernel(qi_tab, kj_tab, first_tab, last_tab, live_tab, q_ref, sb_ref, k_ref, v_ref, gate_ref, o_ref,
                     qaug, m_scr, l_scr, acc_scr, *, tq, tk):
    g = pl.program_id(0)
    s = g * pl.num_programs(1) + pl.program_id(1)
    qi = qi_tab[s]
    kj = kj_tab[s]
    live = live_tab[s] == 1
    lane = lax.broadcasted_iota(jnp.int32, (tq, LANES), 1)
    lo = lane < NSA_HD

    @pl.when(first_tab[s] == 1)
    def _():
        sb = sb_ref[0].astype(F32)
        blk_rel = (lane - qi * (tq // SLC_LEN)).astype(F32)
        for r in range(NSA_REP):
            a, half = r // 2, r % 2
            slope = _slope(g, r)
            qh = q_ref[:, a * LANES:(a + 1) * LANES].astype(F32)
            if half:
                qh = pltpu.roll(qh, NSA_HD, axis=1)
            qh = jnp.where(lo, qh, jnp.where(lane == NSA_HD, slope, 0.0))
            qaug[r * tq:(r + 1) * tq, 0:LANES] = (sb + (slope * SLC_LEN) * blk_rel).astype(BF16)
            qaug[r * tq:(r + 1) * tq, LANES:2 * LANES] = qh.astype(BF16)
        m_scr[...] = jnp.full(m_scr.shape, NEG, F32)
        l_scr[...] = jnp.zeros(l_scr.shape, F32)
        acc_scr[...] = jnp.zeros(acc_scr.shape, F32)

    upper = lax.broadcasted_iota(jnp.int32, (LANES, tq), 0) < NSA_HD

    def step(masked):
        if masked:
            kpos = kj * tk + lax.broadcasted_iota(jnp.int32, (tk, tq), 0)
            qpos = qi * tq + lax.broadcasted_iota(jnp.int32, (tk, tq), 1)
            causal = kpos <= qpos
        k = k_ref[0]
        for a in range(2):
            pv = jnp.zeros((LANES, tq), F32)
            alphas = []
            for half in range(2):
                r = 2 * a + half
                sr = lax.dot_general(k, qaug[r * tq:(r + 1) * tq, :], NT, preferred_element_type=F32)
                if masked:
                    sr = jnp.where(causal, sr, NEG)
                m_prev = m_scr[r:r + 1, :]
                m_new = jnp.maximum(m_prev, jnp.max(sr, axis=0, keepdims=True))
                alpha = jnp.exp(m_prev - m_new)
                p = jnp.exp(sr - m_new)
                l_scr[r:r + 1, :] = alpha * l_scr[r:r + 1, :] + jnp.sum(p, axis=0, keepdims=True)
                m_scr[r:r + 1, :] = m_new
                vt = v_ref[0, half * LANES:(half + 1) * LANES, :]
                pv = pv + jnp.dot(vt, p.astype(BF16), preferred_element_type=F32)
                alphas.append(alpha)
            alpha_pair = jnp.where(upper, alphas[0], alphas[1])
            acc_scr[a * LANES:(a + 1) * LANES, :] = alpha_pair * acc_scr[a * LANES:(a + 1) * LANES, :] + pv

    on_diagonal = kj * tk + (tk - 1) > qi * tq

    @pl.when(live & on_diagonal)
    def _():
        step(True)

    @pl.when(live & jnp.logical_not(on_diagonal))
    def _():
        step(False)

    @pl.when(last_tab[s] == 1)
    def _():
        gates = gate_ref[...]
        for a in range(2):
            l_pair = jnp.where(upper, l_scr[2 * a:2 * a + 1, :], l_scr[2 * a + 1:2 * a + 2, :])
            gp = _gate_pair(gates, lane, lo, NSA_HEADS + g * NSA_REP + 2 * a)
            o_t = acc_scr[a * LANES:(a + 1) * LANES, :] / l_pair
            o_ref[:, a * LANES:(a + 1) * LANES] = jnp.transpose(o_t) * gp


def nsa_selected_attention(qn, selbias, kaug, vs_lohi, gates, tq=256, tk=512):
    T = qn.shape[0]
    G = NSA_KV_HEADS
    qi_l, kj_l, first_l, last_l = [], [], [], []
    for qi in range(T // tq):
        nk = (qi * tq + tq - 1) // tk + 1
        for kj in range(nk):
            qi_l.append(qi)
            kj_l.append(kj)
            first_l.append(1 if kj == 0 else 0)
            last_l.append(1 if kj == nk - 1 else 0)
    nsteps = len(qi_l)
    qi_c, kj_c, first_c, last_c = [np.asarray(t, np.int32) for t in (qi_l, kj_l, first_l, last_l)]
    nq, nkb = T // tq, tk // SLC_LEN
    chosen = (selbias.astype(F32) > 0.5 * NEG).reshape(G, nq, tq, LANES // nkb, nkb)
    tile_any = jnp.any(chosen, axis=(2, 4))
    active = tile_any[:, qi_c, kj_c] | (first_c == 1)[None, :] | (last_c == 1)[None, :]
    csum = jnp.cumsum(active.astype(jnp.int32), axis=1)
    n_active = csum[:, -1]
    k_idx = jnp.arange(nsteps, dtype=jnp.int32)
    src = jnp.sum((csum[:, None, :] <= k_idx[None, :, None]).astype(jnp.int32), axis=2)
    last_src = jnp.sum((csum < n_active[:, None]).astype(jnp.int32), axis=1)
    live = k_idx[None, :] < n_active[:, None]
    src = jnp.where(live, src, last_src[:, None])

    def table(values, mask_dead):
        t = jnp.asarray(values)[src]
        if mask_dead:
            t = jnp.where(live, t, 0)
        return t.reshape(-1).astype(jnp.int32)

    tabs = [table(qi_c, False), table(kj_c, False), table(first_c, True), table(last_c, True),
            live.reshape(-1).astype(jnp.int32)]

    def at(tab, g, s):
        return tab[g * nsteps + s]

    grid_spec = pltpu.PrefetchScalarGridSpec(
        num_scalar_prefetch=5,
        grid=(G, nsteps),
        in_specs=[
            pl.BlockSpec((tq, 256), lambda g, s, qt, kt, ft, lt, at_: (at(qt, g, s), g)),
            pl.BlockSpec((1, tq, 128), lambda g, s, qt, kt, ft, lt, at_: (g, at(qt, g, s), 0)),
            pl.BlockSpec((1, tk, 256), lambda g, s, qt, kt, ft, lt, at_: (g, at(kt, g, s), 0)),
            pl.BlockSpec((1, 256, tk), lambda g, s, qt, kt, ft, lt, at_: (g, 0, at(kt, g, s))),
            pl.BlockSpec((tq, 128), lambda g, s, qt, kt, ft, lt, at_: (at(qt, g, s), 0)),
        ],
        out_specs=pl.BlockSpec((tq, 256), lambda g, s, qt, kt, ft, lt, at_: (at(qt, g, s), g)),
        scratch_shapes=[
            pltpu.VMEM((NSA_REP * tq, 256), BF16),
            pltpu.VMEM((NSA_REP, tq), F32),
            pltpu.VMEM((NSA_REP, tq), F32),
            pltpu.VMEM((2 * LANES, tq), F32),
        ],
    )
    return pl.pallas_call(
        functools.partial(_sel_attn_kernel, tq=tq, tk=tk),
        grid_spec=grid_spec,
        out_shape=jax.ShapeDtypeStruct((T, 512), F32),
        compiler_params=_cparams(("parallel", "arbitrary")),
        name="nsa_selected_attention",
    )(*tabs, qn, selbias, kaug, vs_lohi, gates)


def _win_attn_kernel(q_ref, k0_ref, k1_ref, k2_ref, v0_ref, v1_ref, v2_ref, gate_ref, o_ref, *, tq):
    g = pl.program_id(0)
    qi = pl.program_id(1)
    lane = lax.broadcasted_iota(jnp.int32, (tq, LANES), 1)
    lo = lane < NSA_HD
    upper = lax.broadcasted_iota(jnp.int32, (LANES, tq), 0) < NSA_HD
    row = lax.broadcasted_iota(jnp.int32, (tq, tq), 0)
    qpos = qi * tq + lax.broadcasted_iota(jnp.int32, (tq, tq), 1)
    k_refs = (k0_ref, k1_ref, k2_ref)
    v_refs = (v0_ref, v1_ref, v2_ref)
    dists, masks = [], []
    for d in range(3):
        kpos = (qi - 2 + d) * tq + row
        dd = qpos - kpos
        dists.append(dd.astype(F32))
        masks.append((dd >= 0) & (dd < WINDOW) & (kpos >= 0))
    gates = gate_ref[...]
    for a in range(2):
        pv = jnp.zeros((LANES, tq), F32)
        ls = []
        for half in range(2):
            r = 2 * a + half
            qh = q_ref[:, a * LANES:(a + 1) * LANES]
            qh = jnp.where(lo if half == 0 else jnp.logical_not(lo), qh, jnp.zeros_like(qh))
            ss = []
            for d in range(3):
                sd = lax.dot_general(k_refs[d][0], qh, NT, preferred_element_type=F32)
                ss.append(jnp.where(masks[d], sd - _slope(g, r) * dists[d], NEG))
            mx = jnp.maximum(jnp.maximum(jnp.max(ss[0], axis=0, keepdims=True), jnp.max(ss[1], axis=0, keepdims=True)),
                             jnp.max(ss[2], axis=0, keepdims=True))
            l = jnp.zeros((1, tq), F32)
            for d in range(3):
                p = jnp.exp(ss[d] - mx)
                l = l + jnp.sum(p, axis=0, keepdims=True)
                vt = v_refs[d][0, half * LANES:(half + 1) * LANES, :]
                pv = pv + jnp.dot(vt, p.astype(BF16), preferred_element_type=F32)
            ls.append(l)
        l_pair = jnp.where(upper, ls[0], ls[1])
        gp = _gate_pair(gates, lane, lo, 2 * NSA_HEADS + g * NSA_REP + 2 * a)
        o_ref[:, a * LANES:(a + 1) * LANES] = jnp.transpose(pv / l_pair) * gp


def nsa_window_attention(qn, kw2, vw_lohi, gates, tq=256):
    T = qn.shape[0]
    G = NSA_KV_HEADS
    assert WINDOW == 2 * tq

    def kspec(d):
        return pl.BlockSpec((1, tq, 128), lambda g, i, d=d: (g, jnp.maximum(i - 2 + d, 0), 0))

    def vspec(d):
        return pl.BlockSpec((1, 256, tq), lambda g, i, d=d: (g, 0, jnp.maximum(i - 2 + d, 0)))

    return pl.pallas_call(
        functools.partial(_win_attn_kernel, tq=tq),
        grid=(G, T // tq),
        in_specs=[pl.BlockSpec((tq, 256), lambda g, i: (i, g)),
                  kspec(0), kspec(1), kspec(2),
                  vspec(0), vspec(1), vspec(2),
                  pl.BlockSpec((tq, 128), lambda g, i: (i, 0))],
        out_specs=pl.BlockSpec((tq, 256), lambda g, i: (i, g)),
        out_shape=jax.ShapeDtypeStruct((T, 512), F32),
        compiler_params=_cparams(("parallel", "parallel")),
        name="nsa_window_attention",
    )(qn, kw2, kw2, kw2, vw_lohi, vw_lohi, vw_lohi, gates)


def nsa_mixer(proj_n, cmp_pe, cmp_w1, cmp_w2, q_gain, k_gain, parts=False):
    qn, kaug, vs_lohi, kw2, vw_lohi, gates = nsa_prep(proj_n, q_gain, k_gain)
    kvc = nsa_compress(proj_n[:, C_NSA_KV:C_NSA_KV + 256], cmp_pe, cmp_w1, cmp_w2, k_gain[0])
    kc, vc = kvc[0], kvc[1]
    kc2 = jnp.concatenate([kc, kc], axis=-1).astype(BF16)
    zero = jnp.zeros_like(vc)
    vc_lohi = jnp.concatenate([vc, zero, zero, vc], axis=-1).astype(BF16)
    o_cmp, selbias = nsa_cmp_select(qn, kc2, vc_lohi, gates)
    o_slc = nsa_selected_attention(qn, selbias, kaug, vs_lohi, gates)
    o_win = nsa_window_attention(qn, kw2, vw_lohi, gates)
    if parts:
        return o_cmp + o_slc + o_win, (o_cmp, o_slc, o_win)
    return o_cmp, o_slc, o_win


def _diag_selector():
    m = np.zeros((SUB * LANES, LANES), np.float32)
    for j in range(SUB):
        for rep in range(CHUNK // SUB):
            m[j * LANES:(j + 1) * LANES, rep * SUB + j] = 1.0
    return jnp.asarray(m, dtype=BF16)


LOG2E = 1.0 / math.log(2.0)
REC_HEADS = 2


def _recur_tile(q_scr, k_scr, v_scr, g_scr, o_scr, st_scr, sel_ref, tb):
    row = lax.broadcasted_iota(jnp.int32, (CHUNK, LANES), 0)
    lane = lax.broadcasted_iota(jnp.int32, (CHUNK, LANES), 1)
    sub_row = row % SUB
    blockdiag = (row // SUB) == (lane // SUB)
    r64 = lax.broadcasted_iota(jnp.int32, (CHUNK, CHUNK), 0)
    c64 = lax.broadcasted_iota(jnp.int32, (CHUNK, CHUNK), 1)
    ltri = jnp.where(r64 >= c64, 1.0, 0.0).astype(F32)
    c_sub = lax.broadcasted_iota(jnp.int32, (SUB, CHUNK), 1)
    nsub = CHUNK // SUB
    half = SUB // 2
    low_row = half + lax.broadcasted_iota(jnp.int32, (nsub, half, LANES), 1)

    def head_chunk(h, r0):
        qc = q_scr[h, pl.ds(r0, CHUNK), :]
        kc = k_scr[h, pl.ds(r0, CHUNK), :]
        vc = v_scr[h, pl.ds(r0, CHUNK), :]
        gc = g_scr[h, pl.ds(r0, CHUNK), :]
        b = jnp.dot(ltri, gc, precision=HIGHEST, preferred_element_type=F32)
        bend = b[CHUNK - 1:CHUNK, :]
        st = st_scr[h]
        o = lax.dot_general((qc * jnp.exp2(b)).astype(BF16), st.astype(BF16), NT, preferred_element_type=F32)
        k4 = kc.reshape(nsub, SUB, LANES)
        b4 = b.reshape(nsub, SUB, LANES)
        pieces = []
        for j in range(half):
            k_rep = jnp.broadcast_to(k4[:, j:j + 1, :], (nsub, SUB, LANES)).reshape(CHUNK, LANES)
            b_rep = jnp.broadcast_to(b4[:, j:j + 1, :], (nsub, SUB, LANES)).reshape(CHUNK, LANES)
            e = qc * k_rep * jnp.exp2(jnp.where(sub_row >= j, b - b_rep, NEG))
            pieces.append(e.astype(BF16))
        q_low = qc.reshape(nsub, SUB, LANES)[:, half:, :]
        b_low = b4[:, half:, :]
        for j in range(half, SUB):
            k_rep = jnp.broadcast_to(k4[:, j:j + 1, :], (nsub, half, LANES))
            b_rep = jnp.broadcast_to(b4[:, j:j + 1, :], (nsub, half, LANES))
            e_low = q_low * k_rep * jnp.exp2(jnp.where(low_row >= j, b_low - b_rep, NEG))
            e = jnp.concatenate([jnp.zeros_like(e_low), e_low], axis=1).reshape(CHUNK, LANES)
            pieces.append(e.astype(BF16))
        a_diag = jnp.dot(jnp.concatenate(pieces, axis=1), sel_ref[...], preferred_element_type=F32)
        a_diag = jnp.where(blockdiag, a_diag, 0.0)[:, 0:CHUNK]
        rows = [jnp.zeros((SUB, CHUNK), F32)]
        for i_sub in range(1, nsub):
            ref_b = b[i_sub * SUB - 1:i_sub * SUB, :]
            qt = qc[i_sub * SUB:(i_sub + 1) * SUB, :] * jnp.exp2(b[i_sub * SUB:(i_sub + 1) * SUB, :] - ref_b)
            kt = kc * jnp.exp2(jnp.minimum(ref_b - b, 0.0))
            a_i = lax.dot_general(qt.astype(BF16), kt.astype(BF16), NT, preferred_element_type=F32)
            rows.append(jnp.where(c_sub < i_sub * SUB, a_i, 0.0))
        a = jnp.concatenate(rows, axis=0) + a_diag
        o = o + jnp.dot(a.astype(BF16), vc.astype(BF16), preferred_element_type=F32)
        o_scr[h, pl.ds(r0, CHUNK), :] = o
        kend = kc * jnp.exp2(bend - b)
        st_scr[h] = st * jnp.exp2(bend) + lax.dot_general(vc.astype(BF16), kend.astype(BF16), TN,
                                                          preferred_element_type=F32)

    def chunk(c, carry):
        r0 = pl.multiple_of(c * CHUNK, CHUNK)
        for h in range(q_scr.shape[0]):
            head_chunk(h, r0)
        return carry

    lax.fori_loop(0, tb // CHUNK, chunk, 0)


def _finish_recur(o_scr, gain_ref, gate, o_ref):
    for h in range(o_scr.shape[0]):
        o = o_scr[h]
        ms = jnp.mean(o * o, axis=-1, keepdims=True)
        y = o * lax.rsqrt(ms + EPS) * gain_ref[...] * gate[:, h * LANES:(h + 1) * LANES]
        o_ref[:, h * LANES:(h + 1) * LANES] = y.astype(o_ref.dtype)


def _hgrn2_kernel(q_ref, f_ref, i_ref, gg_ref, lb_ref, gain_ref, sel_ref, o_ref,
                  q_scr, k_scr, v_scr, g_scr, o_scr, st_scr, *, tb):
    @pl.when(pl.program_id(1) == 0)
    def _():
        st_scr[...] = jnp.zeros(st_scr.shape, F32)

    for h in range(q_scr.shape[0]):
        cols = slice(h * LANES, (h + 1) * LANES)
        lb = lb_ref[:, cols]
        z = f_ref[:, cols]
        sg = _sigmoid(z)
        f = lb + (1.0 - lb) * sg
        q_scr[h] = q_ref[:, cols]
        k_scr[h] = (1.0 - lb) * (1.0 - sg)
        v_scr[h] = i_ref[:, cols]
        g_scr[h] = jnp.log(jnp.maximum(f, TINY)) * LOG2E
    _recur_tile(q_scr, k_scr, v_scr, g_scr, o_scr, st_scr, sel_ref, tb)
    _finish_recur(o_scr, gain_ref, _sigmoid(gg_ref[...]), o_ref)


def _recur_scratch(tb, heads):
    return ([pltpu.VMEM((heads, tb, LANES), F32) for _ in range(5)]
            + [pltpu.VMEM((heads, LANES, LANES), F32)])


def hgrn2_mixer(proj, lower_bound, norm_gain, tb=512, heads=HG_HEADS):
    T = proj.shape[0]
    wide = heads * LANES
    per = HG_HEADS // heads

    def col(k):
        return pl.BlockSpec((tb, wide), lambda hp, i, k=k: (i, per * k + hp))

    return pl.pallas_call(
        functools.partial(_hgrn2_kernel, tb=tb),
        grid=(per, T // tb),
        in_specs=[col(0), col(1), col(2), col(3),
                  pl.BlockSpec((1, wide), lambda hp, i: (0, hp)),
                  pl.BlockSpec((1, LANES), lambda hp, i: (0, 0)),
                  pl.BlockSpec((SUB * LANES, LANES), lambda hp, i: (0, 0))],
        out_specs=pl.BlockSpec((tb, wide), lambda hp, i: (i, hp)),
        out_shape=jax.ShapeDtypeStruct((T, BRANCH_WIDTH), BF16),
        scratch_shapes=_recur_scratch(tb, heads),
        compiler_params=_cparams(("parallel", "arbitrary")),
        name="hgrn2_mixer",
    )(proj, proj, proj, proj, lower_bound.reshape(1, -1), norm_gain.reshape(1, -1), _diag_selector())


def _gla_kernel(q_ref, k_ref, v_ref, r_ref, tail_ref, wa_ref, ba_ref, gain_ref, sel_ref, o_ref,
                q_scr, k_scr, v_scr, g_scr, o_scr, st_scr, *, tb):
    @pl.when(pl.program_id(1) == 0)
    def _():
        st_scr[...] = jnp.zeros(st_scr.shape, F32)

    lane = lax.broadcasted_iota(jnp.int32, (tb, LANES), 1)
    a = jnp.dot(tail_ref[...], wa_ref[...], precision=HIGHEST, preferred_element_type=F32) + ba_ref[...]
    log_sig = -(jnp.maximum(-a, 0.0) + jnp.log1p(jnp.exp(-jnp.abs(a))))
    q = q_ref[...] * (GLA_DK ** -0.5)
    k = k_ref[...]
    g2 = log_sig * (LOG2E / GLA_TAU)
    for h in range(REC_HEADS):
        mine = (lane < GLA_DK) if h == 0 else (lane >= GLA_DK)
        q_scr[h] = jnp.where(mine, q, 0.0)
        k_scr[h] = jnp.where(mine, k, 0.0)
        v_scr[h] = v_ref[:, h * LANES:(h + 1) * LANES]
        g_scr[h] = jnp.where(mine, g2, 0.0)
    _recur_tile(q_scr, k_scr, v_scr, g_scr, o_scr, st_scr, sel_ref, tb)
    r = r_ref[...]
    _finish_recur(o_scr, gain_ref, r * _sigmoid(r), o_ref)


def gla_mixer(proj, proj_t, w_a2, b_a, norm_gain, tb=512):
    T = proj.shape[0]
    assert REC_HEADS == 2 and LANES == 2 * GLA_DK
    wide = REC_HEADS * LANES
    cqk = C_GLA_QK // LANES
    cvr = C_GLA_VR // wide
    per = GLA_HEADS // REC_HEADS
    wa = jnp.zeros((LANES, GLA_HEADS * GLA_DK), F32).at[TAIL_GLA_A0:TAIL_GLA_A0 + GLA_RANK].set(w_a2)
    return pl.pallas_call(
        functools.partial(_gla_kernel, tb=tb),
        grid=(per, T // tb),
        in_specs=[pl.BlockSpec((tb, LANES), lambda hp, i: (i, cqk + hp)),
                  pl.BlockSpec((tb, LANES), lambda hp, i: (i, cqk + per + hp)),
                  pl.BlockSpec((tb, wide), lambda hp, i: (i, cvr + hp)),
                  pl.BlockSpec((tb, wide), lambda hp, i: (i, cvr + per + hp)),
                  pl.BlockSpec((tb, LANES), lambda hp, i: (i, 0)),
                  pl.BlockSpec((LANES, LANES), lambda hp, i: (0, hp)),
                  pl.BlockSpec((1, LANES), lambda hp, i: (0, hp)),
                  pl.BlockSpec((1, LANES), lambda hp, i: (0, 0)),
                  pl.BlockSpec((SUB * LANES, LANES), lambda hp, i: (0, 0))],
        out_specs=pl.BlockSpec((tb, wide), lambda hp, i: (i, hp)),
        out_shape=jax.ShapeDtypeStruct((T, BRANCH_WIDTH), BF16),
        scratch_shapes=_recur_scratch(tb, REC_HEADS),
        compiler_params=_cparams(("parallel", "arbitrary")),
        name="gla_mixer",
    )(proj, proj, proj, proj, proj_t, wa, b_a.reshape(1, -1), norm_gain.reshape(1, -1), _diag_selector())


def _merge_kernel(oc_ref, os_ref, ow_ref, ob_ref, og_ref, wb_ref, g0_ref, g1_ref, g2_ref, o_ref):
    o_a = (oc_ref[...] + os_ref[...] + ow_ref[...]).astype(BF16)
    acc = _sigmoid(g0_ref[...]) * jnp.dot(o_a, wb_ref[0], preferred_element_type=F32)
    acc = acc + _sigmoid(g1_ref[...]) * jnp.dot(ob_ref[...], wb_ref[1], preferred_element_type=F32)
    acc = acc + _sigmoid(g2_ref[...]) * jnp.dot(og_ref[...], wb_ref[2], preferred_element_type=F32)
    o_ref[...] = acc.astype(o_ref.dtype)


def merge_branches(proj, o_cmp, o_slc, o_win, o_b, o_c, w_branch_bf16, tm=512, tn=512):
    T = proj.shape[0]
    W = BRANCH_WIDTH
    nj = D_MODEL // tn
    ospec = pl.BlockSpec((tm, W), lambda i, j: (i, 0))

    def gspec(n):
        return pl.BlockSpec((tm, tn), lambda i, j, n=n: (i, n * nj + j))

    return pl.pallas_call(
        _merge_kernel,
        grid=(T // tm, nj),
        in_specs=[ospec, ospec, ospec, ospec, ospec,
                  pl.BlockSpec((3, W, tn), lambda i, j: (0, 0, j)),
                  gspec(0), gspec(1), gspec(2)],
        out_specs=pl.BlockSpec((tm, tn), lambda i, j: (i, j)),
        out_shape=jax.ShapeDtypeStruct((T, D_MODEL), BF16),
        compiler_params=_cparams(("parallel", "arbitrary")),
        name="merge_branches",
    )(o_cmp, o_slc, o_win, o_b, o_c, w_branch_bf16, proj, proj, proj)


MOE_TILE = 256


def _route_kernel(x_ref, g_ref, wr_ref, br_ref, h_ref, route_ref, cnt_ref, carry, *, tm):
    i = pl.program_id(0)

    @pl.when(i == 0)
    def _():
        carry[...] = jnp.zeros(carry.shape, F32)

    x = x_ref[...]
    ms = jnp.mean(x * x, axis=-1, keepdims=True)
    h = x * lax.rsqrt(ms + EPS) * g_ref[...]
    h_ref[...] = h
    logits = jnp.dot(h, wr_ref[...], precision=HIGHEST, preferred_element_type=F32) + br_ref[...]
    lane = lax.broadcasted_iota(jnp.int32, (tm, LANES), 1)

    def masked_softmax(mask):
        l = jnp.where(mask, logits, NEG)
        e = jnp.where(mask, jnp.exp(l - jnp.max(l, axis=1, keepdims=True)), 0.0)
        return e / jnp.sum(e, axis=1, keepdims=True)

    def top1(prob, mask):
        p = jnp.max(jnp.where(mask, prob, -1.0), axis=1, keepdims=True)
        idx = jnp.min(jnp.where(mask & (prob == p), lane, LANES), axis=1, keepdims=True)
        return p, idx

    gmask = lane < N_GROUPS
    gw, gidx = top1(masked_softmax(gmask), gmask)
    emask = (lane >= N_GROUPS) & (lane < N_GROUPS + N_EXPERTS) & ((lane - N_GROUPS) // EXPERTS_PER_GROUP == gidx)
    eprob = masked_softmax(emask)
    p1, i1 = top1(eprob, emask)
    rest = emask & (lane != i1)
    p2, i2 = top1(eprob, rest)
    psum = p1 + p2
    w1 = gw * (p1 / psum)
    w2 = gw * (p2 / psum)
    e1 = i1 - N_GROUPS
    e2 = i2 - N_GROUPS

    onehot = jnp.where((lane == e1) | (lane == e2), 1.0, 0.0)
    r = lax.broadcasted_iota(jnp.int32, (tm, tm), 0)
    c = lax.broadcasted_iota(jnp.int32, (tm, tm), 1)
    strict = jnp.where(r > c, 1.0, 0.0).astype(BF16)
    before = jnp.dot(strict, onehot.astype(BF16), preferred_element_type=F32) + carry[0:1, :]
    rank1 = jnp.sum(jnp.where(lane == e1, before, 0.0), axis=1, keepdims=True)
    rank2 = jnp.sum(jnp.where(lane == e2, before, 0.0), axis=1, keepdims=True)
    total = carry[0:1, :] + jnp.sum(onehot, axis=0, keepdims=True)
    carry[...] = jnp.broadcast_to(total, carry.shape)
    cnt_ref[...] = jnp.broadcast_to(total, cnt_ref.shape)

    out = jnp.where(lane == 0, w1, 0.0)
    out = jnp.where(lane == 1, w2, out)
    out = jnp.where(lane == 2, e1.astype(F32), out)
    out = jnp.where(lane == 3, e2.astype(F32), out)
    out = jnp.where(lane == 4, rank1, out)
    out = jnp.where(lane == 5, rank2, out)
    route_ref[...] = out


def moe_route(x, gain, w_grp, b_grp, w_exp, b_exp, tm=512):
    T, D = x.shape
    n_pad = LANES - N_GROUPS - N_EXPERTS
    wr = jnp.concatenate([w_grp, w_exp, jnp.zeros((D, n_pad), F32)], axis=1)
    br = jnp.concatenate([b_grp, b_exp, jnp.zeros((n_pad,), F32)]).reshape(1, LANES)
    return pl.pallas_call(
        functools.partial(_route_kernel, tm=tm),
        grid=(T // tm,),
        in_specs=[pl.BlockSpec((tm, D), lambda i: (i, 0)),
                  pl.BlockSpec((1, D), lambda i: (0, 0)),
                  pl.BlockSpec((D, LANES), lambda i: (0, 0)),
                  pl.BlockSpec((1, LANES), lambda i: (0, 0))],
        out_specs=(pl.BlockSpec((tm, D), lambda i: (i, 0)),
                   pl.BlockSpec((tm, LANES), lambda i: (i, 0)),
                   pl.BlockSpec((8, LANES), lambda i: (0, 0))),
        out_shape=(jax.ShapeDtypeStruct((T, D), F32),
                   jax.ShapeDtypeStruct((T, LANES), F32),
                   jax.ShapeDtypeStruct((8, LANES), F32)),
        scratch_shapes=[pltpu.VMEM((8, LANES), F32)],
        compiler_params=_cparams(("arbitrary",)),
        name="moe_route",
    )(x, gain.reshape(1, D), wr, br)


def _row_dma(src, src_row, dst, dst_row, sem):
    return pltpu.make_async_copy(src.at[pl.ds(src_row, 1), :], dst.at[pl.ds(dst_row, 1), :], sem)


def _rows_wait(src, dst, n, sem):
    pltpu.make_async_copy(src.at[pl.ds(0, n), :], dst.at[pl.ds(0, n), :], sem).wait()


def _expert_kernel(tile_expert, n_used, rcur_ref, rnext_ref, h_hbm, wgu_ref, wd_ref, o_ref,
                   xbuf, wgu_bf, wd_bf, sem):
    i = pl.program_id(0)
    used = i < n_used[0]
    slot = i % 2

    def fetch(rref, s):
        def body(r, c):
            _row_dma(h_hbm, rref[0, 0, r], xbuf.at[s], r, sem.at[s]).start()
            return c

        lax.fori_loop(0, MOE_TILE, body, 0, unroll=8)

    @pl.when(i == 0)
    def _():
        fetch(rcur_ref, 0)

    @pl.when(i + 1 < n_used[0])
    def _():
        fetch(rnext_ref, 1 - slot)

    @pl.when(used)
    def _():
        prev = tile_expert[jnp.maximum(i - 1, 0)]

        @pl.when((i == 0) | (tile_expert[i] != prev))
        def _():
            wgu_bf[...] = wgu_ref[0, 0].astype(BF16)
            wd_bf[...] = wd_ref[0, 0].astype(BF16)

        _rows_wait(h_hbm, xbuf.at[slot], MOE_TILE, sem.at[slot])
        x = xbuf[slot].astype(BF16)
        gu = jnp.dot(x, wgu_bf[...], preferred_element_type=F32)
        gate = gu[:, 0:D_FF_EXPERT]
        up = gu[:, D_FF_EXPERT:2 * D_FF_EXPERT]
        act = gate * _sigmoid(gate) * up
        o_ref[...] = jnp.dot(act.astype(BF16), wd_bf[...], preferred_element_type=F32)

    @pl.when(jnp.logical_not(used))
    def _():
        o_ref[...] = jnp.zeros(o_ref.shape, F32)


def moe_experts(h, row_token, tile_expert, n_used, w_gate_up, w_down, layer):
    T, D = h.shape
    n_tiles = row_token.shape[0] // MOE_TILE
    rows = row_token.reshape(n_tiles, 1, MOE_TILE)
    grid_spec = pltpu.PrefetchScalarGridSpec(
        num_scalar_prefetch=2,
        grid=(n_tiles,),
        in_specs=[
            pl.BlockSpec((1, 1, MOE_TILE), lambda i, te, nu: (i, 0, 0), memory_space=pltpu.SMEM),
            pl.BlockSpec((1, 1, MOE_TILE), lambda i, te, nu: (jnp.minimum(i + 1, n_tiles - 1), 0, 0),
                         memory_space=pltpu.SMEM),
            pl.BlockSpec(memory_space=pl.ANY),
            pl.BlockSpec((1, 1, D, 2 * D_FF_EXPERT), lambda i, te, nu: (layer, te[i], 0, 0)),
            pl.BlockSpec((1, 1, D_FF_EXPERT, D), lambda i, te, nu: (layer, te[i], 0, 0)),
        ],
        out_specs=pl.BlockSpec((MOE_TILE, D), lambda i, te, nu: (i, 0)),
        scratch_shapes=[
            pltpu.VMEM((2, MOE_TILE, D), F32),
            pltpu.VMEM((D, 2 * D_FF_EXPERT), BF16),
            pltpu.VMEM((D_FF_EXPERT, D), BF16),
            pltpu.SemaphoreType.DMA((2,)),
        ],
    )
    return pl.pallas_call(
        _expert_kernel,
        grid_spec=grid_spec,
        out_shape=jax.ShapeDtypeStruct((n_tiles * MOE_TILE, D), F32),
        compiler_params=_cparams(("arbitrary",)),
        name="moe_experts",
    )(tile_expert, n_used, rows, rows, h, w_gate_up, w_down)


def _combine_kernel(dcur_ref, dnext_ref, x_ref, route_ref, y_hbm, o_ref, buf, sem, *, tm):
    i = pl.program_id(0)
    n = pl.num_programs(0)
    slot = i % 2

    def fetch(dref, s):
        def body(r, c):
            _row_dma(y_hbm, dref[0, 0, 2 * r], buf.at[s, 0], r, sem.at[s]).start()
            _row_dma(y_hbm, dref[0, 0, 2 * r + 1], buf.at[s, 1], r, sem.at[s]).start()
            return c

        lax.fori_loop(0, tm, body, 0, unroll=8)

    @pl.when(i == 0)
    def _():
        fetch(dcur_ref, 0)

    @pl.when(i + 1 < n)
    def _():
        fetch(dnext_ref, 1 - slot)

    _rows_wait(y_hbm, buf.at[slot, 0], tm, sem.at[slot])
    _rows_wait(y_hbm, buf.at[slot, 1], tm, sem.at[slot])
    route = route_ref[...]
    lane = lax.broadcasted_iota(jnp.int32, route.shape, 1)
    w1 = jnp.sum(jnp.where(lane == 0, route, 0.0), axis=1, keepdims=True)
    w2 = jnp.sum(jnp.where(lane == 1, route, 0.0), axis=1, keepdims=True)
    o_ref[...] = x_ref[...] + (w1 * buf[slot, 0] + w2 * buf[slot, 1])


def moe_combine(x, ys, route, dest, tm=256):
    T, D = x.shape
    n = T // tm
    dest_tiles = dest.reshape(n, 1, 2 * tm)
    return pl.pallas_call(
        functools.partial(_combine_kernel, tm=tm),
        grid=(n,),
        in_specs=[pl.BlockSpec((1, 1, 2 * tm), lambda i: (i, 0, 0), memory_space=pltpu.SMEM),
                  pl.BlockSpec((1, 1, 2 * tm), lambda i: (jnp.minimum(i + 1, n - 1), 0, 0), memory_space=pltpu.SMEM),
                  pl.BlockSpec((tm, D), lambda i: (i, 0)),
                  pl.BlockSpec((tm, LANES), lambda i: (i, 0)),
                  pl.BlockSpec(memory_space=pl.ANY)],
        out_specs=pl.BlockSpec((tm, D), lambda i: (i, 0)),
        out_shape=jax.ShapeDtypeStruct((T, D), F32),
        scratch_shapes=[pltpu.VMEM((2, 2, tm, D), F32), pltpu.SemaphoreType.DMA((2,))],
        compiler_params=_cparams(("arbitrary",)),
        name="moe_combine",
    )(dest_tiles, dest_tiles, x, route, ys)


def hierarchical_moe(x, gain, w_grp, b_grp, w_exp, b_exp, w_gate_up, w_down, layer):
    T, D = x.shape
    h, route, cnt = moe_route(x, gain, w_grp, b_grp, w_exp, b_exp)
    expert = route[:, 2:4].astype(jnp.int32)
    rank = route[:, 4:6].astype(jnp.int32)
    counts = cnt[0, 0:N_EXPERTS].astype(jnp.int32)
    padded = ((counts + MOE_TILE - 1) // MOE_TILE) * MOE_TILE
    ends = jnp.cumsum(padded)
    offs = ends - padded
    dest = offs[expert] + rank
    n_rows = 2 * T + N_EXPERTS * MOE_TILE
    n_tiles = n_rows // MOE_TILE
    n_used = (ends[-1] // MOE_TILE).astype(jnp.int32)
    tile_start = jnp.arange(n_tiles, dtype=jnp.int32) * MOE_TILE
    tile_expert = jnp.sum((ends[None, :] <= tile_start[:, None]).astype(jnp.int32), axis=1)
    last_expert = tile_expert[jnp.maximum(n_used - 1, 0)]
    tile_expert = jnp.where(tile_start < ends[-1], tile_expert, last_expert)
    row_token = jnp.zeros((n_rows,), jnp.int32).at[dest.reshape(-1)].set(
        jnp.repeat(jnp.arange(T, dtype=jnp.int32), 2))
    ys = moe_experts(h, row_token, tile_expert, n_used.reshape(1), w_gate_up, w_down, layer)
    return moe_combine(x, ys, route, dest)


def _in_proj_kernel(a_ref, wt_hbm, o_ref, wbuf, sem, *, layer, row0, tn):
    i, j = pl.program_id(0), pl.program_id(1)
    ni, nj = pl.num_programs(0), pl.num_programs(1)
    step = i * nj + j
    slot = step % 2

    def fetch(jj, s):
        start = pl.multiple_of(row0 + jj * tn, SUBLANES)
        return pltpu.make_async_copy(wt_hbm.at[layer, pl.ds(start, tn), :], wbuf.at[s], sem.at[s])

    @pl.when(step == 0)
    def _():
        fetch(0, 0).start()

    @pl.when(step + 1 < ni * nj)
    def _():
        fetch((j + 1) % nj, 1 - slot).start()

    fetch(j, slot).wait()
    w = wbuf[slot].astype(BF16)
    o_ref[...] = lax.dot_general(a_ref[...], w, NT, preferred_element_type=F32)


def in_proj(h, wt, layer, row0, n, tn, name, tm=2048):
    T, K = h.shape
    assert n % tn == 0 and row0 % SUBLANES == 0 and tn % SUBLANES == 0
    return pl.pallas_call(
        functools.partial(_in_proj_kernel, layer=layer, row0=row0, tn=tn),
        grid=(T // tm, n // tn),
        in_specs=[pl.BlockSpec((tm, K), lambda i, j: (i, 0)), pl.BlockSpec(memory_space=pl.ANY)],
        out_specs=pl.BlockSpec((tm, tn), lambda i, j: (i, j)),
        out_shape=jax.ShapeDtypeStruct((T, n), F32),
        scratch_shapes=[pltpu.VMEM((2, tn, K), F32), pltpu.SemaphoreType.DMA((2,))],
        compiler_params=_cparams(("arbitrary", "arbitrary")),
        name=name,
    )(h, wt)


def kernel(x, norm_mix, w_in, cmp_pe, cmp_w1, cmp_w2, q_norm, k_norm, hg_lb_logits, hg_norm, gla_w_a2, gla_b_a,
           gla_norm, w_branch, w_out, norm_ffn, w_grp, b_grp, w_exp, b_exp, w_gate_up, w_down):
    B, T, D = x.shape
    assert B == 1 and D == D_MODEL
    xt = x[0]
    p_lb = jax.nn.softmax(hg_lb_logits.astype(F32), axis=0)
    lower_bounds = jnp.cumsum(p_lb, axis=0) - p_lb[0]
    wt = jnp.swapaxes(w_in, 1, 2)
    for l in range(DEPTH):
        h = rmsnorm_bf16(xt, norm_mix[l])
        proj_n = in_proj(h, wt, l, W_NSA[0], W_NSA[1], 512, "in_proj_nsa")
        proj_r = in_proj(h, wt, l, W_REC[0], W_REC[1], 512, "in_proj_rec")
        proj_a = in_proj(h, wt, l, W_GLA_A_BLOCK[0], W_GLA_A_BLOCK[1], LANES, "in_proj_gla_a")
        proj_m = in_proj(h, wt, l, W_MERGE[0], W_MERGE[1], 512, "in_proj_merge")
        o_cmp, o_slc, o_win = nsa_mixer(proj_n, cmp_pe[l], cmp_w1[l], cmp_w2[l], q_norm[l], k_norm[l])
        o_b = hgrn2_mixer(proj_r, lower_bounds[l], hg_norm[l])
        o_c = gla_mixer(proj_r, proj_a, gla_w_a2[l], gla_b_a[l], gla_norm[l])
        merged = merge_branches(proj_m, o_cmp, o_slc, o_win, o_b, o_c, w_branch[l].astype(BF16))
        xt = matmul_bf16(merged, w_out[l].astype(BF16), res=xt, tm=1024, tn=512, name="out_proj")
        xt = hierarchical_moe(xt, norm_ffn[l], w_grp[l], b_grp[l], w_exp[l], b_exp[l], w_gate_up, w_down, l)
    return xt[None]
```

```python
import functools
import math

import numpy as np
import jax
import jax.numpy as jnp
from jax import lax
from jax.experimental import pallas as pl
from jax.experimental.pallas import tpu as pltpu

F32 = jnp.float32
BF16 = jnp.bfloat16
HIGHEST = lax.Precision.HIGHEST

D_MODEL = 2048
DEPTH = 2
BRANCH_WIDTH = D_MODEL // 4
NSA_HEADS = 8
NSA_KV_HEADS = 2
NSA_REP = NSA_HEADS // NSA_KV_HEADS
NSA_HD = 64
CMP_LEN = 32
CMP_STRIDE = 16
SLC_LEN = 64
SLC_TOP = 16
WINDOW = 512
HG_HEADS = 4
GLA_HEADS = 4
GLA_DK = 64
GLA_RANK = 16
GLA_TAU = 16.0
CHUNK = 64
SUB = 16
N_GROUPS = 4
EXPERTS_PER_GROUP = 8
N_EXPERTS = N_GROUPS * EXPERTS_PER_GROUP
D_FF_EXPERT = D_MODEL // 4
EPS = 1e-6
NEG = -1e30
FORCE = 1e4
TINY = 1e-30

LANES = 128
SUBLANES = 8
VMEM_LIMIT = 56 * 1024 * 1024

W_NSA = (0, 1536)
W_REC = (1304, 3584)
W_MERGE = (4904, 6144)
W_GLA_A_BLOCK = (4864, 128)
C_NSA_KV = 512
C_NSA_GATE = 1280
C_GLA_QK = 2048
C_GLA_VR = 2560
TAIL_GLA_A0 = 4888 - W_GLA_A_BLOCK[0]

NT = (((1,), (1,)), ((), ()))
TN = (((0,), (0,)), ((), ()))


def _cparams(sem):
    return pltpu.CompilerParams(dimension_semantics=sem, vmem_limit_bytes=VMEM_LIMIT)


def _sigmoid(x):
    return 1.0 / (1.0 + jnp.exp(-x))


def _norm_kernel(x_ref, g_ref, o_ref):
    x = x_ref[...]
    ms = jnp.mean(x * x, axis=-1, keepdims=True)
    o_ref[...] = (x * lax.rsqrt(ms + EPS) * g_ref[...]).astype(o_ref.dtype)


def rmsnorm_bf16(x, gain, tm=512):
    T, D = x.shape
    return pl.pallas_call(
        _norm_kernel,
        grid=(T // tm,),
        in_specs=[pl.BlockSpec((tm, D), lambda i: (i, 0)), pl.BlockSpec((1, D), lambda i: (0, 0))],
        out_specs=pl.BlockSpec((tm, D), lambda i: (i, 0)),
        out_shape=jax.ShapeDtypeStruct((T, D), BF16),
        compiler_params=_cparams(("parallel",)),
        name="rmsnorm_bf16",
    )(x, gain.reshape(1, D))


def _mm_kernel(a_ref, b_ref, o_ref):
    o_ref[...] = jnp.dot(a_ref[...], b_ref[...], preferred_element_type=F32)


def _mm_res_kernel(a_ref, b_ref, r_ref, o_ref):
    o_ref[...] = r_ref[...] + jnp.dot(a_ref[...], b_ref[...], preferred_element_type=F32)


def matmul_bf16(a, b, res=None, tm=1024, tn=512, name="matmul_bf16"):
    T, K = a.shape
    N = b.shape[1]
    in_specs = [pl.BlockSpec((tm, K), lambda i, j: (i, 0)), pl.BlockSpec((K, tn), lambda i, j: (0, j))]
    args = [a, b]
    kern = _mm_kernel
    if res is not None:
        in_specs.append(pl.BlockSpec((tm, tn), lambda i, j: (i, j)))
        args.append(res)
        kern = _mm_res_kernel
    return pl.pallas_call(
        kern,
        grid=(T // tm, N // tn),
        in_specs=in_specs,
        out_specs=pl.BlockSpec((tm, tn), lambda i, j: (i, j)),
        out_shape=jax.ShapeDtypeStruct((T, N), F32),
        compiler_params=_cparams(("parallel", "arbitrary")),
        name=name,
    )(*args)


def _half_rmsnorm(x, gain2, lo):
    x2 = x * x
    s_lo = jnp.sum(jnp.where(lo, x2, 0.0), axis=1, keepdims=True)
    s_hi = jnp.sum(jnp.where(lo, 0.0, x2), axis=1, keepdims=True)
    ms = jnp.where(lo, s_lo, s_hi) * (1.0 / NSA_HD)
    return x * lax.rsqrt(ms + EPS) * gain2


def _nsa_prep_kernel(q_ref, kvc_ref, kvs_ref, kvw_ref, tail_ref, qg_ref, kg_ref,
                     qn_ref, kaug_ref, vs_ref, kw_ref, vw_ref, gate_ref, *, tm):
    i = pl.program_id(0)
    lane = lax.broadcasted_iota(jnp.int32, (tm, LANES), 1)
    lo = lane < NSA_HD
    qg = qg_ref[...]
    for c in range(4):
        x = q_ref[:, c * LANES:(c + 1) * LANES]
        qn_ref[:, c * LANES:(c + 1) * LANES] = (_half_rmsnorm(x, qg, lo) * (NSA_HD ** -0.5)).astype(BF16)

    def dup(kn):
        rolled = pltpu.roll(kn, NSA_HD, axis=1)
        return jnp.where(lo, kn, rolled), jnp.where(lo, rolled, kn)

    vt_zero = jnp.zeros((NSA_HD, tm), BF16)

    def store_vt(v_pair, ref):
        vt = jnp.transpose(v_pair)
        for g in range(NSA_KV_HEADS):
            head = vt[g * NSA_HD:(g + 1) * NSA_HD, :].astype(BF16)
            ref[g, 0:NSA_HD, :] = head
            ref[g, NSA_HD:LANES, :] = vt_zero
            ref[g, LANES:LANES + NSA_HD, :] = vt_zero
            ref[g, LANES + NSA_HD:2 * LANES, :] = head

    row = i * tm + lax.broadcasted_iota(jnp.int32, (tm, LANES), 0)
    onehot = jnp.where(row // SLC_LEN == lane, 1.0, 0.0).astype(BF16)
    k_extra = jnp.where(lane == NSA_HD, (row % SLC_LEN).astype(F32), 0.0)
    ks = dup(_half_rmsnorm(kvs_ref[:, 0:LANES], kg_ref[1:2, :], lo))
    store_vt(kvs_ref[:, LANES:2 * LANES], vs_ref)
    kw = dup(_half_rmsnorm(kvw_ref[:, 0:LANES], kg_ref[2:3, :], lo))
    store_vt(kvw_ref[:, LANES:2 * LANES], vw_ref)
    for g in range(NSA_KV_HEADS):
        kaug_ref[g, :, 0:LANES] = onehot
        kaug_ref[g, :, LANES:2 * LANES] = jnp.where(lo, ks[g], k_extra).astype(BF16)
        kw_ref[g] = kw[g].astype(BF16)
    gate_ref[...] = _sigmoid(tail_ref[...])


def nsa_prep(proj_n, q_gain, k_gain, tm=512):
    T = proj_n.shape[0]
    qg2 = jnp.tile(q_gain.reshape(1, NSA_HD), (1, 2))
    kg2 = jnp.tile(k_gain.reshape(3, NSA_HD), (1, 2))
    G = NSA_KV_HEADS
    out_shape = (
        jax.ShapeDtypeStruct((T, 512), BF16),
        jax.ShapeDtypeStruct((G, T, 256), BF16),
        jax.ShapeDtypeStruct((G, 256, T), BF16),
        jax.ShapeDtypeStruct((G, T, 128), BF16),
        jax.ShapeDtypeStruct((G, 256, T), BF16),
        jax.ShapeDtypeStruct((T, 128), F32),
    )
    return pl.pallas_call(
        functools.partial(_nsa_prep_kernel, tm=tm),
        grid=(T // tm,),
        in_specs=[
            pl.BlockSpec((tm, 512), lambda i: (i, 0)),
            pl.BlockSpec((tm, 256), lambda i: (i, C_NSA_KV // 256)),
            pl.BlockSpec((tm, 256), lambda i: (i, C_NSA_KV // 256 + 1)),
            pl.BlockSpec((tm, 256), lambda i: (i, C_NSA_KV // 256 + 2)),
            pl.BlockSpec((tm, 128), lambda i: (i, C_NSA_GATE // 128)),
            pl.BlockSpec((1, 128), lambda i: (0, 0)),
            pl.BlockSpec((3, 128), lambda i: (0, 0)),
        ],
        out_specs=(
            pl.BlockSpec((tm, 512), lambda i: (i, 0)),
            pl.BlockSpec((G, tm, 256), lambda i: (0, i, 0)),
            pl.BlockSpec((G, 256, tm), lambda i: (0, 0, i)),
            pl.BlockSpec((G, tm, 128), lambda i: (0, i, 0)),
            pl.BlockSpec((G, 256, tm), lambda i: (0, 0, i)),
            pl.BlockSpec((tm, 128), lambda i: (i, 0)),
        ),
        out_shape=out_shape,
        compiler_params=_cparams(("parallel",)),
        name="nsa_prep",
    )(proj_n, proj_n, proj_n, proj_n, proj_n, qg2, kg2)


def _gelu_tanh(x):
    c = math.sqrt(2.0 / math.pi)
    return 0.5 * x * (1.0 + jnp.tanh(c * (x + 0.044715 * (x * x * x))))


def _compress_kernel(a_ref, pe_ref, w1_ref, w2_ref, kg_ref, o_ref, *, nb):
    kind = pl.program_id(0)
    a_lo = a_ref[0, 0, 0:nb, :]
    a_hi = a_ref[0, 0, 1:nb + 1, :]
    blocks = jnp.concatenate([a_lo, a_hi], axis=1) + pe_ref[0]
    h1 = jnp.dot(blocks.astype(BF16), w1_ref[0].astype(BF16), preferred_element_type=F32)
    y = jnp.dot(_gelu_tanh(h1).astype(BF16), w2_ref[0].astype(BF16), preferred_element_type=F32)
    ms = jnp.mean(y * y, axis=-1, keepdims=True)
    yn = y * lax.rsqrt(ms + EPS) * kg_ref[...]
    o_ref[0, 0] = jnp.where(kind == 0, yn, y)


def nsa_compress(kv_cmp, cmp_pe, cmp_w1, cmp_w2, k_gain0):
    T = kv_cmp.shape[0]
    nb = T // CMP_STRIDE
    G = NSA_KV_HEADS
    a = kv_cmp.reshape(T, 2, G, NSA_HD).transpose(1, 2, 0, 3).reshape(2, G, nb, CMP_STRIDE * NSA_HD)
    a = jnp.pad(a, ((0, 0), (0, 0), (0, 8), (0, 0)))
    pe = cmp_pe.reshape(2, 1, CMP_LEN * NSA_HD)
    return pl.pallas_call(
        functools.partial(_compress_kernel, nb=nb),
        grid=(2, G),
        in_specs=[
            pl.BlockSpec((1, 1, nb + 8, CMP_STRIDE * NSA_HD), lambda k, g: (k, g, 0, 0)),
            pl.BlockSpec((1, 1, CMP_LEN * NSA_HD), lambda k, g: (k, 0, 0)),
            pl.BlockSpec((1, CMP_LEN * NSA_HD, NSA_HD), lambda k, g: (k, 0, 0)),
            pl.BlockSpec((1, NSA_HD, NSA_HD), lambda k, g: (k, 0, 0)),
            pl.BlockSpec((1, NSA_HD), lambda k, g: (0, 0)),
        ],
        out_specs=pl.BlockSpec((1, 1, nb, NSA_HD), lambda k, g: (k, g, 0, 0)),
        out_shape=jax.ShapeDtypeStruct((2, G, nb, NSA_HD), F32),
        compiler_params=_cparams(("arbitrary", "arbitrary")),
        name="nsa_compress",
    )(a, pe, cmp_w1, cmp_w2, k_gain0.reshape(1, NSA_HD))


def _slope(g, r):
    return jnp.where(g == 0, 2.0 ** -(r + 1), 2.0 ** -(NSA_REP + r + 1)).astype(F32)


def _gate_pair(gates, lane, lo, col_even):
    ge = jnp.sum(jnp.where(lane == col_even, gates, 0.0), axis=1, keepdims=True)
    go = jnp.sum(jnp.where(lane == col_even + 1, gates, 0.0), axis=1, keepdims=True)
    return jnp.where(lo, ge, go)


def _cmp_sel_kernel(q_ref, kc_ref, vc_ref, gate_ref, mt_ref, o_ref, sel_ref, *, tq, nb, ns_pad):
    g = pl.program_id(0)
    qi = pl.program_id(1)
    t0 = qi * tq
    lane = lax.broadcasted_iota(jnp.int32, (tq, LANES), 1)
    lo = lane < NSA_HD
    n_idx = lax.broadcasted_iota(jnp.int32, (nb, tq), 0)
    t_idx = t0 + lax.broadcasted_iota(jnp.int32, (nb, tq), 1)
    dist = (t_idx - (n_idx * CMP_STRIDE + (CMP_LEN - 1))).astype(F32)
    vis = dist >= 0.0
    kc = kc_ref[0]
    imp = jnp.zeros((nb, tq), F32)
    gates = gate_ref[...]
    for a in range(2):
        acc = jnp.zeros((tq, LANES), F32)
        for half in range(2):
            r = 2 * a + half
            qh = q_ref[:, a * LANES:(a + 1) * LANES]
            qh = jnp.where(lo if half == 0 else jnp.logical_not(lo), qh, jnp.zeros_like(qh))
            s = lax.dot_general(kc, qh, NT, preferred_element_type=F32)
            s = jnp.where(vis, s - _slope(g, r) * dist, NEG)
            mx = jnp.max(s, axis=0, keepdims=True)
            e = jnp.where(vis, jnp.exp(s - mx), 0.0)
            den = jnp.sum(e, axis=0, keepdims=True)
            p = e * jnp.where(den > 0.0, 1.0 / den, 0.0)
            imp = imp + p
            v = vc_ref[0, :, half * LANES:(half + 1) * LANES]
            acc = acc + lax.dot_general(p.astype(BF16), v, TN, preferred_element_type=F32)
        gp = _gate_pair(gates, lane, lo, g * NSA_REP + 2 * a)
        o_ref[:, a * LANES:(a + 1) * LANES] = acc * gp

    score = jnp.dot(mt_ref[...], imp, precision=HIGHEST, preferred_element_type=F32)
    blk = lax.broadcasted_iota(jnp.int32, (ns_pad, tq), 0)
    cur = (t0 + lax.broadcasted_iota(jnp.int32, (ns_pad, tq), 1)) // SLC_LEN
    forced = (blk == 0) | (blk == cur) | (blk == cur - 1)
    score = jnp.where(forced, FORCE, score)
    score = jnp.where(blk <= cur, score, -1.0)
    sel = jnp.zeros((ns_pad, tq), F32)
    for _ in range(SLC_TOP):
        mx = jnp.max(score, axis=0, keepdims=True)
        idx = jnp.min(jnp.where(score == mx, blk, ns_pad), axis=0, keepdims=True)
        hit = blk == idx
        sel = jnp.where(hit, 1.0, sel)
        score = jnp.where(hit, -jnp.inf, score)
    bias_t = jnp.where((sel > 0.0) & (blk <= cur), 0.0, NEG)
    sel_ref[0] = jnp.transpose(bias_t).astype(BF16)


def _score_matrix(nb, ns_pad):
    ratio, span = SLC_LEN // CMP_STRIDE, CMP_LEN // CMP_STRIDE
    n_cmp = nb - 1
    m = np.zeros((ns_pad, nb), np.float32)
    for s in range(nb // ratio):
        for mm in range(ratio):
            for nn in range(span):
                c = ratio * s + mm - nn
                if 0 <= c < n_cmp:
                    m[s, c] += 1.0
    return jnp.asarray(m)


def nsa_cmp_select(qn, kc2, vc_lohi, gates, tq=256):
    T = qn.shape[0]
    nb = T // CMP_STRIDE
    ns_pad = LANES
    G = NSA_KV_HEADS
    return pl.pallas_call(
        functools.partial(_cmp_sel_kernel, tq=tq, nb=nb, ns_pad=ns_pad),
        grid=(G, T // tq),
        in_specs=[
            pl.BlockSpec((tq, 256), lambda g, i: (i, g)),
            pl.BlockSpec((1, nb, 128), lambda g, i: (g, 0, 0)),
            pl.BlockSpec((1, nb, 256), lambda g, i: (g, 0, 0)),
            pl.BlockSpec((tq, 128), lambda g, i: (i, 0)),
            pl.BlockSpec((ns_pad, nb), lambda g, i: (0, 0)),
        ],
        out_specs=(
            pl.BlockSpec((tq, 256), lambda g, i: (i, g)),
            pl.BlockSpec((1, tq, ns_pad), lambda g, i: (g, i, 0)),
        ),
        out_shape=(jax.ShapeDtypeStruct((T, 512), F32), jax.ShapeDtypeStruct((G, T, ns_pad), BF16)),
        compiler_params=_cparams(("parallel", "parallel")),
        name="nsa_cmp_select",
    )(qn, kc2, vc_lohi, gates, _score_matrix(nb, ns_pad))


def _sel_attn_kernel(qi_tab, kj_tab, first_tab, last_tab, live_tab, q_ref, sb_ref, k_ref, v_ref, gate_ref, o_ref,
                     qaug, m_scr, l_scr, acc_scr, *, tq, tk):
    g = pl.program_id(0)
    s = g * pl.num_programs(1) + pl.program_id(1)
    qi = qi_tab[s]
    kj = kj_tab[s]
    live = live_tab[s] == 1
    lane = lax.broadcasted_iota(jnp.int32, (tq, LANES), 1)
    lo = lane < NSA_HD

    @pl.when(first_tab[s] == 1)
    def _():
        sb = sb_ref[0].astype(F32)
        blk_rel = (lane - qi * (tq // SLC_LEN)).astype(F32)
        for r in range(NSA_REP):
            a, half = r // 2, r % 2
            slope = _slope(g, r)
            qh = q_ref[:, a * LANES:(a + 1) * LANES].astype(F32)
            if half:
                qh = pltpu.roll(qh, NSA_HD, axis=1)
            qh = jnp.where(lo, qh, jnp.where(lane == NSA_HD, slope, 0.0))
            qaug[r * tq:(r + 1) * tq, 0:LANES] = (sb + (slope * SLC_LEN) * blk_rel).astype(BF16)
            qaug[r * tq:(r + 1) * tq, LANES:2 * LANES] = qh.astype(BF16)
        m_scr[...] = jnp.full(m_scr.shape, NEG, F32)
        l_scr[...] = jnp.zeros(l_scr.shape, F32)
        acc_scr[...] = jnp.zeros(acc_scr.shape, F32)

    upper = lax.broadcasted_iota(jnp.int32, (LANES, tq), 0) < NSA_HD

    def step(masked):
        if masked:
            kpos = kj * tk + lax.broadcasted_iota(jnp.int32, (tk, tq), 0)
            qpos = qi * tq + lax.broadcasted_iota(jnp.int32, (tk, tq), 1)
            causal = kpos <= qpos
        k = k_ref[0]
        for a in range(2):
            pv = jnp.zeros((LANES, tq), F32)
            alphas = []
            for half in range(2):
                r = 2 * a + half
                sr = lax.dot_general(k, qaug[r * tq:(r + 1) * tq, :], NT, preferred_element_type=F32)
                if masked:
                    sr = jnp.where(causal, sr, NEG)
                m_prev = m_scr[r:r + 1, :]
                m_new = jnp.maximum(m_prev, jnp.max(sr, axis=0, keepdims=True))
                alpha = jnp.exp(m_prev - m_new)
                p = jnp.exp(sr - m_new)
                l_scr[r:r + 1, :] = alpha * l_scr[r:r + 1, :] + jnp.sum(p, axis=0, keepdims=True)
                m_scr[r:r + 1, :] = m_new
                vt = v_ref[0, half * LANES:(half + 1) * LANES, :]
                pv = pv + jnp.dot(vt, p.astype(BF16), preferred_element_type=F32)
                alphas.append(alpha)
            alpha_pair = jnp.where(upper, alphas[0], alphas[1])
            acc_scr[a * LANES:(a + 1) * LANES, :] = alpha_pair * acc_scr[a * LANES:(a + 1) * LANES, :] + pv

    on_diagonal = kj * tk + (tk - 1) > qi * tq

    @pl.when(live & on_diagonal)
    def _():
        step(True)

    @pl.when(live & jnp.logical_not(on_diagonal))
    def _():
        step(False)

    @pl.when(last_tab[s] == 1)
    def _():
        gates = gate_ref[...]
        for a in range(2):
            l_pair = jnp.where(upper, l_scr[2 * a:2 * a + 1, :], l_scr[2 * a + 1:2 * a + 2, :])
            gp = _gate_pair(gates, lane, lo, NSA_HEADS + g * NSA_REP + 2 * a)
            o_t = acc_scr[a * LANES:(a + 1) * LANES, :] / l_pair
            o_ref[:, a * LANES:(a + 1) * LANES] = jnp.transpose(o_t) * gp


def nsa_selected_attention(qn, selbias, kaug, vs_lohi, gates, tq=256, tk=512):
    T = qn.shape[0]
    G = NSA_KV_HEADS
    qi_l, kj_l, first_l, last_l = [], [], [], []
    for qi in range(T // tq):
        nk = (qi * tq + tq - 1) // tk + 1
        for kj in range(nk):
            qi_l.append(qi)
            kj_l.append(kj)
            first_l.append(1 if kj == 0 else 0)
            last_l.append(1 if kj == nk - 1 else 0)
    nsteps = len(qi_l)
    qi_c, kj_c, first_c, last_c = [np.asarray(t, np.int32) for t in (qi_l, kj_l, first_l, last_l)]
    nq, nkb = T // tq, tk // SLC_LEN
    chosen = (selbias.astype(F32) > 0.5 * NEG).reshape(G, nq, tq, LANES // nkb, nkb)
    tile_any = jnp.any(chosen, axis=(2, 4))
    active = tile_any[:, qi_c, kj_c] | (first_c == 1)[None, :] | (last_c == 1)[None, :]
    csum = jnp.cumsum(active.astype(jnp.int32), axis=1)
    n_active = csum[:, -1]
    k_idx = jnp.arange(nsteps, dtype=jnp.int32)
    src = jnp.sum((csum[:, None, :] <= k_idx[None, :, None]).astype(jnp.int32), axis=2)
    last_src = jnp.sum((csum < n_active[:, None]).astype(jnp.int32), axis=1)
    live = k_idx[None, :] < n_active[:, None]
    src = jnp.where(live, src, last_src[:, None])

    def table(values, mask_dead):
        t = jnp.asarray(values)[src]
        if mask_dead:
            t = jnp.where(live, t, 0)
        return t.reshape(-1).astype(jnp.int32)

    tabs = [table(qi_c, False), table(kj_c, False), table(first_c, True), table(last_c, True),
            live.reshape(-1).astype(jnp.int32)]

    def at(tab, g, s):
        return tab[g * nsteps + s]

    grid_spec = pltpu.PrefetchScalarGridSpec(
        num_scalar_prefetch=5,
        grid=(G, nsteps),
        in_specs=[
            pl.BlockSpec((tq, 256), lambda g, s, qt, kt, ft, lt, at_: (at(qt, g, s), g)),
            pl.BlockSpec((1, tq, 128), lambda g, s, qt, kt, ft, lt, at_: (g, at(qt, g, s), 0)),
            pl.BlockSpec((1, tk, 256), lambda g, s, qt, kt, ft, lt, at_: (g, at(kt, g, s), 0)),
            pl.BlockSpec((1, 256, tk), lambda g, s, qt, kt, ft, lt, at_: (g, 0, at(kt, g, s))),
            pl.BlockSpec((tq, 128), lambda g, s, qt, kt, ft, lt, at_: (at(qt, g, s), 0)),
        ],
        out_specs=pl.BlockSpec((tq, 256), lambda g, s, qt, kt, ft, lt, at_: (at(qt, g, s), g)),
        scratch_shapes=[
            pltpu.VMEM((NSA_REP * tq, 256), BF16),
            pltpu.VMEM((NSA_REP, tq), F32),
            pltpu.VMEM((NSA_REP, tq), F32),
            pltpu.VMEM((2 * LANES, tq), F32),
        ],
    )
    return pl.pallas_call(
        functools.partial(_sel_attn_kernel, tq=tq, tk=tk),
        grid_spec=grid_spec,
        out_shape=jax.ShapeDtypeStruct((T, 512), F32),
        compiler_params=_cparams(("parallel", "arbitrary")),
        name="nsa_selected_attention",
    )(*tabs, qn, selbias, kaug, vs_lohi, gates)


def _win_attn_kernel(q_ref, k0_ref, k1_ref, k2_ref, v0_ref, v1_ref, v2_ref, gate_ref, o_ref, *, tq):
    g = pl.program_id(0)
    qi = pl.program_id(1)
    lane = lax.broadcasted_iota(jnp.int32, (tq, LANES), 1)
    lo = lane < NSA_HD
    upper = lax.broadcasted_iota(jnp.int32, (LANES, tq), 0) < NSA_HD
    row = lax.broadcasted_iota(jnp.int32, (tq, tq), 0)
    qpos = qi * tq + lax.broadcasted_iota(jnp.int32, (tq, tq), 1)
    k_refs = (k0_ref, k1_ref, k2_ref)
    v_refs = (v0_ref, v1_ref, v2_ref)
    dists, masks = [], []
    for d in range(3):
        kpos = (qi - 2 + d) * tq + row
        dd = qpos - kpos
        dists.append(dd.astype(F32))
        masks.append((dd >= 0) & (dd < WINDOW) & (kpos >= 0))
    gates = gate_ref[...]
    for a in range(2):
        pv = jnp.zeros((LANES, tq), F32)
        ls = []
        for half in range(2):
            r = 2 * a + half
            qh = q_ref[:, a * LANES:(a + 1) * LANES]
            qh = jnp.where(lo if half == 0 else jnp.logical_not(lo), qh, jnp.zeros_like(qh))
            ss = []
            for d in range(3):
                sd = lax.dot_general(k_refs[d][0], qh, NT, preferred_element_type=F32)
                ss.append(jnp.where(masks[d], sd - _slope(g, r) * dists[d], NEG))
            mx = jnp.maximum(jnp.maximum(jnp.max(ss[0], axis=0, keepdims=True), jnp.max(ss[1], axis=0, keepdims=True)),
                             jnp.max(ss[2], axis=0, keepdims=True))
            l = jnp.zeros((1, tq), F32)
            for d in range(3):
                p = jnp.exp(ss[d] - mx)
                l = l + jnp.sum(p, axis=0, keepdims=True)
                vt = v_refs[d][0, half * LANES:(half + 1) * LANES, :]
                pv = pv + jnp.dot(vt, p.astype(BF16), preferred_element_type=F32)
            ls.append(l)
        l_pair = jnp.where(upper, ls[0], ls[1])
        gp = _gate_pair(gates, lane, lo, 2 * NSA_HEADS + g * NSA_REP + 2 * a)
        o_ref[:, a * LANES:(a + 1) * LANES] = jnp.transpose(pv / l_pair) * gp


def nsa_window_attention(qn, kw2, vw_lohi, gates, tq=256):
    T = qn.shape[0]
    G = NSA_KV_HEADS
    assert WINDOW == 2 * tq

    def kspec(d):
        return pl.BlockSpec((1, tq, 128), lambda g, i, d=d: (g, jnp.maximum(i - 2 + d, 0), 0))

    def vspec(d):
        return pl.BlockSpec((1, 256, tq), lambda g, i, d=d: (g, 0, jnp.maximum(i - 2 + d, 0)))

    return pl.pallas_call(
        functools.partial(_win_attn_kernel, tq=tq),
        grid=(G, T // tq),
        in_specs=[pl.BlockSpec((tq, 256), lambda g, i: (i, g)),
                  kspec(0), kspec(1), kspec(2),
                  vspec(0), vspec(1), vspec(2),
                  pl.BlockSpec((tq, 128), lambda g, i: (i, 0))],
        out_specs=pl.BlockSpec((tq, 256), lambda g, i: (i, g)),
        out_shape=jax.ShapeDtypeStruct((T, 512), F32),
        compiler_params=_cparams(("parallel", "parallel")),
        name="nsa_window_attention",
    )(qn, kw2, kw2, kw2, vw_lohi, vw_lohi, vw_lohi, gates)


def nsa_mixer(proj_n, cmp_pe, cmp_w1, cmp_w2, q_gain, k_gain, parts=False):
    qn, kaug, vs_lohi, kw2, vw_lohi, gates = nsa_prep(proj_n, q_gain, k_gain)
    kvc = nsa_compress(proj_n[:, C_NSA_KV:C_NSA_KV + 256], cmp_pe, cmp_w1, cmp_w2, k_gain[0])
    kc, vc = kvc[0], kvc[1]
    kc2 = jnp.concatenate([kc, kc], axis=-1).astype(BF16)
    zero = jnp.zeros_like(vc)
    vc_lohi = jnp.concatenate([vc, zero, zero, vc], axis=-1).astype(BF16)
    o_cmp, selbias = nsa_cmp_select(qn, kc2, vc_lohi, gates)
    o_slc = nsa_selected_attention(qn, selbias, kaug, vs_lohi, gates)
    o_win = nsa_window_attention(qn, kw2, vw_lohi, gates)
    if parts:
        return o_cmp + o_slc + o_win, (o_cmp, o_slc, o_win)
    return o_cmp, o_slc, o_win


def _diag_selector():
    m = np.zeros((SUB * LANES, LANES), np.float32)
    for j in range(SUB):
        for rep in range(CHUNK // SUB):
            m[j * LANES:(j + 1) * LANES, rep * SUB + j] = 1.0
    return jnp.asarray(m, dtype=BF16)


LOG2E = 1.0 / math.log(2.0)


def _recur_tile(q_scr, k_scr, v_scr, g_scr, o_scr, st_scr, sel_ref, tb):
    row = lax.broadcasted_iota(jnp.int32, (CHUNK, LANES), 0)
    lane = lax.broadcasted_iota(jnp.int32, (CHUNK, LANES), 1)
    sub_row = row % SUB
    blockdiag = (row // SUB) == (lane // SUB)
    r64 = lax.broadcasted_iota(jnp.int32, (CHUNK, CHUNK), 0)
    c64 = lax.broadcasted_iota(jnp.int32, (CHUNK, CHUNK), 1)
    ltri = jnp.where(r64 >= c64, 1.0, 0.0).astype(F32)
    c_sub = lax.broadcasted_iota(jnp.int32, (SUB, CHUNK), 1)
    nsub = CHUNK // SUB
    half = SUB // 2
    low_row = half + lax.broadcasted_iota(jnp.int32, (nsub, half, LANES), 1)

    def head_chunk(h, r0):
        qc = q_scr[h, pl.ds(r0, CHUNK), :]
        kc = k_scr[h, pl.ds(r0, CHUNK), :]
        vc = v_scr[h, pl.ds(r0, CHUNK), :]
        gc = g_scr[h, pl.ds(r0, CHUNK), :]
        b = jnp.dot(ltri, gc, precision=HIGHEST, preferred_element_type=F32)
        bend = b[CHUNK - 1:CHUNK, :]
        st = st_scr[h]
        o = lax.dot_general((qc * jnp.exp2(b)).astype(BF16), st.astype(BF16), NT, preferred_element_type=F32)
        k4 = kc.reshape(nsub, SUB, LANES)
        b4 = b.reshape(nsub, SUB, LANES)
        pieces = []
        for j in range(half):
            k_rep = jnp.broadcast_to(k4[:, j:j + 1, :], (nsub, SUB, LANES)).reshape(CHUNK, LANES)
            b_rep = jnp.broadcast_to(b4[:, j:j + 1, :], (nsub, SUB, LANES)).reshape(CHUNK, LANES)
            e = qc * k_rep * jnp.exp2(jnp.where(sub_row >= j, b - b_rep, NEG))
            pieces.append(e.astype(BF16))
        q_low = qc.reshape(nsub, SUB, LANES)[:, half:, :]
        b_low = b4[:, half:, :]
        for j in range(half, SUB):
            k_rep = jnp.broadcast_to(k4[:, j:j + 1, :], (nsub, half, LANES))
            b_rep = jnp.broadcast_to(b4[:, j:j + 1, :], (nsub, half, LANES))
            e_low = q_low * k_rep * jnp.exp2(jnp.where(low_row >= j, b_low - b_rep, NEG))
            e = jnp.concatenate([jnp.zeros_like(e_low), e_low], axis=1).reshape(CHUNK, LANES)
            pieces.append(e.astype(BF16))
        a_diag = jnp.dot(jnp.concatenate(pieces, axis=1), sel_ref[...], preferred_element_type=F32)
        a_diag = jnp.where(blockdiag, a_diag, 0.0)[:, 0:CHUNK]
        rows = [jnp.zeros((SUB, CHUNK), F32)]
        for i_sub in range(1, nsub):
            ref_b = b[i_sub * SUB - 1:i_sub * SUB, :]
            qt = qc[i_sub * SUB:(i_sub + 1) * SUB, :] * jnp.exp2(b[i_sub * SUB:(i_sub + 1) * SUB, :] - ref_b)
            kt = kc * jnp.exp2(jnp.minimum(ref_b - b, 0.0))
            a_i = lax.dot_general(qt.astype(BF16), kt.astype(BF16), NT, preferred_element_type=F32)
            rows.append(jnp.where(c_sub < i_sub * SUB, a_i, 0.0))
        a = jnp.concatenate(rows, axis=0) + a_diag
        o = o + jnp.dot(a.astype(BF16), vc.astype(BF16), preferred_element_type=F32)
        o_scr[h, pl.ds(r0, CHUNK), :] = o
        kend = kc * jnp.exp2(bend - b)
        st_scr[h] = st * jnp.exp2(bend) + lax.dot_general(vc.astype(BF16), kend.astype(BF16), TN,
                                                          preferred_element_type=F32)

    def chunk(c, carry):
        r0 = pl.multiple_of(c * CHUNK, CHUNK)
        for h in range(q_scr.shape[0]):
            head_chunk(h, r0)
        return carry

    lax.fori_loop(0, tb // CHUNK, chunk, 0)


def _finish_recur(o_scr, gain_ref, gate, o_ref):
    for h in range(o_scr.shape[0]):
        o = o_scr[h]
        ms = jnp.mean(o * o, axis=-1, keepdims=True)
        y = o * lax.rsqrt(ms + EPS) * gain_ref[...] * gate[:, h * LANES:(h + 1) * LANES]
        o_ref[:, h * LANES:(h + 1) * LANES] = y.astype(o_ref.dtype)


def _hgrn2_kernel(q_ref, f_ref, i_ref, gg_ref, lb_ref, gain_ref, sel_ref, o_ref,
                  q_scr, k_scr, v_scr, g_scr, o_scr, st_scr, *, tb):
    @pl.when(pl.program_id(1) == 0)
    def _():
        st_scr[...] = jnp.zeros(st_scr.shape, F32)

    for h in range(q_scr.shape[0]):
        cols = slice(h * LANES, (h + 1) * LANES)
        lb = lb_ref[:, cols]
        z = f_ref[:, cols]
        sg = _sigmoid(z)
        f = lb + (1.0 - lb) * sg
        q_scr[h] = q_ref[:, cols]
        k_scr[h] = (1.0 - lb) * (1.0 - sg)
        v_scr[h] = i_ref[:, cols]
        g_scr[h] = jnp.log(jnp.maximum(f, TINY)) * LOG2E
    _recur_tile(q_scr, k_scr, v_scr, g_scr, o_scr, st_scr, sel_ref, tb)
    _finish_recur(o_scr, gain_ref, _sigmoid(gg_ref[...]), o_ref)


def _recur_scratch(tb, heads):
    return ([pltpu.VMEM((heads, tb, LANES), F32) for _ in range(5)]
            + [pltpu.VMEM((heads, LANES, LANES), F32)])


def hgrn2_mixer(proj, lower_bound, norm_gain, tb=512, heads=HG_HEADS):
    T = proj.shape[0]
    wide = heads * LANES
    per = HG_HEADS // heads

    def col(k):
        return pl.BlockSpec((tb, wide), lambda hp, i, k=k: (i, per * k + hp))

    return pl.pallas_call(
        functools.partial(_hgrn2_kernel, tb=tb),
        grid=(per, T // tb),
        in_specs=[col(0), col(1), col(2), col(3),
                  pl.BlockSpec((1, wide), lambda hp, i: (0, hp)),
                  pl.BlockSpec((1, LANES), lambda hp, i: (0, 0)),
                  pl.BlockSpec((SUB * LANES, LANES), lambda hp, i: (0, 0))],
        out_specs=pl.BlockSpec((tb, wide), lambda hp, i: (i, hp)),
        out_shape=jax.ShapeDtypeStruct((T, BRANCH_WIDTH), BF16),
        scratch_shapes=_recur_scratch(tb, heads),
        compiler_params=_cparams(("parallel", "arbitrary")),
        name="hgrn2_mixer",
    )(proj, proj, proj, proj, lower_bound.reshape(1, -1), norm_gain.reshape(1, -1), _diag_selector())


def _gla_kernel(q_ref, k_ref, v_ref, r_ref, tail_ref, wa_ref, ba_ref, gain_ref, sel_ref, o_ref,
                q_scr, k_scr, v_scr, g_scr, o_scr, st_scr, *, tb):
    @pl.when(pl.program_id(1) == 0)
    def _():
        st_scr[...] = jnp.zeros(st_scr.shape, F32)

    lane = lax.broadcasted_iota(jnp.int32, (tb, LANES), 1)
    a = jnp.dot(tail_ref[...], wa_ref[...], precision=HIGHEST, preferred_element_type=F32) + ba_ref[...]
    log_sig = -(jnp.maximum(-a, 0.0) + jnp.log1p(jnp.exp(-jnp.abs(a))))
    g2 = log_sig * (LOG2E / GLA_TAU)
    for h in range(q_scr.shape[0]):
        pair = slice((h // 2) * LANES, (h // 2 + 1) * LANES)
        mine = (lane < GLA_DK) if h % 2 == 0 else (lane >= GLA_DK)
        q_scr[h] = jnp.where(mine, q_ref[:, pair] * (GLA_DK ** -0.5), 0.0)
        k_scr[h] = jnp.where(mine, k_ref[:, pair], 0.0)
        v_scr[h] = v_ref[:, h * LANES:(h + 1) * LANES]
        g_scr[h] = jnp.where(mine, g2[:, pair], 0.0)
    _recur_tile(q_scr, k_scr, v_scr, g_scr, o_scr, st_scr, sel_ref, tb)
    r = r_ref[...]
    _finish_recur(o_scr, gain_ref, r * _sigmoid(r), o_ref)


def gla_mixer(proj, proj_t, w_a2, b_a, norm_gain, tb=512, heads=GLA_HEADS):
    T = proj.shape[0]
    assert heads % 2 == 0 and LANES == 2 * GLA_DK
    wide = heads * LANES
    half = heads * GLA_DK
    cqk = C_GLA_QK // half
    cvr = C_GLA_VR // wide
    per = GLA_HEADS // heads
    wa = jnp.zeros((LANES, GLA_HEADS * GLA_DK), F32).at[TAIL_GLA_A0:TAIL_GLA_A0 + GLA_RANK].set(w_a2)
    return pl.pallas_call(
        functools.partial(_gla_kernel, tb=tb),
        grid=(per, T // tb),
        in_specs=[pl.BlockSpec((tb, half), lambda hp, i: (i, cqk + hp)),
                  pl.BlockSpec((tb, half), lambda hp, i: (i, cqk + per + hp)),
                  pl.BlockSpec((tb, wide), lambda hp, i: (i, cvr + hp)),
                  pl.BlockSpec((tb, wide), lambda hp, i: (i, cvr + per + hp)),
                  pl.BlockSpec((tb, LANES), lambda hp, i: (i, 0)),
                  pl.BlockSpec((LANES, half), lambda hp, i: (0, hp)),
                  pl.BlockSpec((1, half), lambda hp, i: (0, hp)),
                  pl.BlockSpec((1, LANES), lambda hp, i: (0, 0)),
                  pl.BlockSpec((SUB * LANES, LANES), lambda hp, i: (0, 0))],
        out_specs=pl.BlockSpec((tb, wide), lambda hp, i: (i, hp)),
        out_shape=jax.ShapeDtypeStruct((T, BRANCH_WIDTH), BF16),
        scratch_shapes=_recur_scratch(tb, heads),
        compiler_params=_cparams(("parallel", "arbitrary")),
        name="gla_mixer",
    )(proj, proj, proj, proj, proj_t, wa, b_a.reshape(1, -1), norm_gain.reshape(1, -1), _diag_selector())


def _merge_kernel(oc_ref, os_ref, ow_ref, ob_ref, og_ref, wb_ref, g0_ref, g1_ref, g2_ref, o_ref):
    o_a = (oc_ref[...] + os_ref[...] + ow_ref[...]).astype(BF16)
    acc = g0_ref[...].astype(F32) * jnp.dot(o_a, wb_ref[0], preferred_element_type=F32)
    acc = acc + g1_ref[...].astype(F32) * jnp.dot(ob_ref[...], wb_ref[1], preferred_element_type=F32)
    acc = acc + g2_ref[...].astype(F32) * jnp.dot(og_ref[...], wb_ref[2], preferred_element_type=F32)
    o_ref[...] = acc.astype(o_ref.dtype)


def merge_branches(proj, o_cmp, o_slc, o_win, o_b, o_c, w_branch_bf16, tm=512, tn=512):
    T = proj.shape[0]
    W = BRANCH_WIDTH
    nj = D_MODEL // tn
    ospec = pl.BlockSpec((tm, W), lambda i, j: (i, 0))

    def gspec(n):
        return pl.BlockSpec((tm, tn), lambda i, j, n=n: (i, n * nj + j))

    return pl.pallas_call(
        _merge_kernel,
        grid=(T // tm, nj),
        in_specs=[ospec, ospec, ospec, ospec, ospec,
                  pl.BlockSpec((3, W, tn), lambda i, j: (0, 0, j)),
                  gspec(0), gspec(1), gspec(2)],
        out_specs=pl.BlockSpec((tm, tn), lambda i, j: (i, j)),
        out_shape=jax.ShapeDtypeStruct((T, D_MODEL), BF16),
        compiler_params=_cparams(("parallel", "arbitrary")),
        name="merge_branches",
    )(o_cmp, o_slc, o_win, o_b, o_c, w_branch_bf16, proj, proj, proj)


MOE_TILE = 256


def _route_kernel(x_ref, g_ref, wr_ref, br_ref, h_ref, route_ref, cnt_ref, carry, *, tm):
    i = pl.program_id(0)

    @pl.when(i == 0)
    def _():
        carry[...] = jnp.zeros(carry.shape, F32)

    x = x_ref[...]
    ms = jnp.mean(x * x, axis=-1, keepdims=True)
    h = x * lax.rsqrt(ms + EPS) * g_ref[...]
    h_ref[...] = h
    logits = jnp.dot(h, wr_ref[...], precision=HIGHEST, preferred_element_type=F32) + br_ref[...]
    lane = lax.broadcasted_iota(jnp.int32, (tm, LANES), 1)

    def masked_softmax(mask):
        l = jnp.where(mask, logits, NEG)
        e = jnp.where(mask, jnp.exp(l - jnp.max(l, axis=1, keepdims=True)), 0.0)
        return e / jnp.sum(e, axis=1, keepdims=True)

    def top1(prob, mask):
        p = jnp.max(jnp.where(mask, prob, -1.0), axis=1, keepdims=True)
        idx = jnp.min(jnp.where(mask & (prob == p), lane, LANES), axis=1, keepdims=True)
        return p, idx

    gmask = lane < N_GROUPS
    gw, gidx = top1(masked_softmax(gmask), gmask)
    emask = (lane >= N_GROUPS) & (lane < N_GROUPS + N_EXPERTS) & ((lane - N_GROUPS) // EXPERTS_PER_GROUP == gidx)
    eprob = masked_softmax(emask)
    p1, i1 = top1(eprob, emask)
    rest = emask & (lane != i1)
    p2, i2 = top1(eprob, rest)
    psum = p1 + p2
    w1 = gw * (p1 / psum)
    w2 = gw * (p2 / psum)
    e1 = i1 - N_GROUPS
    e2 = i2 - N_GROUPS

    onehot = jnp.where((lane == e1) | (lane == e2), 1.0, 0.0)
    r = lax.broadcasted_iota(jnp.int32, (tm, tm), 0)
    c = lax.broadcasted_iota(jnp.int32, (tm, tm), 1)
    strict = jnp.where(r > c, 1.0, 0.0).astype(BF16)
    before = jnp.dot(strict, onehot.astype(BF16), preferred_element_type=F32) + carry[0:1, :]
    rank1 = jnp.sum(jnp.where(lane == e1, before, 0.0), axis=1, keepdims=True)
    rank2 = jnp.sum(jnp.where(lane == e2, before, 0.0), axis=1, keepdims=True)
    total = carry[0:1, :] + jnp.sum(onehot, axis=0, keepdims=True)
    carry[...] = jnp.broadcast_to(total, carry.shape)
    cnt_ref[...] = jnp.broadcast_to(total, cnt_ref.shape)

    out = jnp.where(lane == 0, w1, 0.0)
    out = jnp.where(lane == 1, w2, out)
    out = jnp.where(lane == 2, e1.astype(F32), out)
    out = jnp.where(lane == 3, e2.astype(F32), out)
    out = jnp.where(lane == 4, rank1, out)
    out = jnp.where(lane == 5, rank2, out)
    route_ref[...] = out


def moe_route(x, gain, w_grp, b_grp, w_exp, b_exp, tm=512):
    T, D = x.shape
    n_pad = LANES - N_GROUPS - N_EXPERTS
    wr = jnp.concatenate([w_grp, w_exp, jnp.zeros((D, n_pad), F32)], axis=1)
    br = jnp.concatenate([b_grp, b_exp, jnp.zeros((n_pad,), F32)]).reshape(1, LANES)
    return pl.pallas_call(
        functools.partial(_route_kernel, tm=tm),
        grid=(T // tm,),
        in_specs=[pl.BlockSpec((tm, D), lambda i: (i, 0)),
                  pl.BlockSpec((1, D), lambda i: (0, 0)),
                  pl.BlockSpec((D, LANES), lambda i: (0, 0)),
                  pl.BlockSpec((1, LANES), lambda i: (0, 0))],
        out_specs=(pl.BlockSpec((tm, D), lambda i: (i, 0)),
                   pl.BlockSpec((tm, LANES), lambda i: (i, 0)),
                   pl.BlockSpec((8, LANES), lambda i: (0, 0))),
        out_shape=(jax.ShapeDtypeStruct((T, D), F32),
                   jax.ShapeDtypeStruct((T, LANES), F32),
                   jax.ShapeDtypeStruct((8, LANES), F32)),
        scratch_shapes=[pltpu.VMEM((8, LANES), F32)],
        compiler_params=_cparams(("arbitrary",)),
        name="moe_route",
    )(x, gain.reshape(1, D), wr, br)


def _row_dma(src, src_row, dst, dst_row, sem):
    return pltpu.make_async_copy(src.at[pl.ds(src_row, 1), :], dst.at[pl.ds(dst_row, 1), :], sem)


def _rows_wait(src, dst, n, sem):
    pltpu.make_async_copy(src.at[pl.ds(0, n), :], dst.at[pl.ds(0, n), :], sem).wait()


def _expert_kernel(tile_expert, n_used, rcur_ref, rnext_ref, h_hbm, wgu_ref, wd_ref, o_ref,
                   xbuf, wgu_bf, wd_bf, sem):
    i = pl.program_id(0)
    used = i < n_used[0]
    slot = i % 2

    def fetch(rref, s):
        def body(r, c):
            _row_dma(h_hbm, rref[0, 0, r], xbuf.at[s], r, sem.at[s]).start()
            return c

        lax.fori_loop(0, MOE_TILE, body, 0, unroll=8)

    @pl.when(i == 0)
    def _():
        fetch(rcur_ref, 0)

    @pl.when(i + 1 < n_used[0])
    def _():
        fetch(rnext_ref, 1 - slot)

    @pl.when(used)
    def _():
        prev = tile_expert[jnp.maximum(i - 1, 0)]

        @pl.when((i == 0) | (tile_expert[i] != prev))
        def _():
            wgu_bf[...] = wgu_ref[0, 0].astype(BF16)
            wd_bf[...] = wd_ref[0, 0].astype(BF16)

        _rows_wait(h_hbm, xbuf.at[slot], MOE_TILE, sem.at[slot])
        x = xbuf[slot].astype(BF16)
        gu = jnp.dot(x, wgu_bf[...], preferred_element_type=F32)
        gate = gu[:, 0:D_FF_EXPERT]
        up = gu[:, D_FF_EXPERT:2 * D_FF_EXPERT]
        act = gate * _sigmoid(gate) * up
        o_ref[...] = jnp.dot(act.astype(BF16), wd_bf[...], preferred_element_type=F32)

    @pl.when(jnp.logical_not(used))
    def _():
        o_ref[...] = jnp.zeros(o_ref.shape, F32)


def moe_experts(h, row_token, tile_expert, n_used, w_gate_up, w_down, layer):
    T, D = h.shape
    n_tiles = row_token.shape[0] // MOE_TILE
    rows = row_token.reshape(n_tiles, 1, MOE_TILE)
    grid_spec = pltpu.PrefetchScalarGridSpec(
        num_scalar_prefetch=2,
        grid=(n_tiles,),
        in_specs=[
            pl.BlockSpec((1, 1, MOE_TILE), lambda i, te, nu: (i, 0, 0), memory_space=pltpu.SMEM),
            pl.BlockSpec((1, 1, MOE_TILE), lambda i, te, nu: (jnp.minimum(i + 1, n_tiles - 1), 0, 0),
                         memory_space=pltpu.SMEM),
            pl.BlockSpec(memory_space=pl.ANY),
            pl.BlockSpec((1, 1, D, 2 * D_FF_EXPERT), lambda i, te, nu: (layer, te[i], 0, 0)),
            pl.BlockSpec((1, 1, D_FF_EXPERT, D), lambda i, te, nu: (layer, te[i], 0, 0)),
        ],
        out_specs=pl.BlockSpec((MOE_TILE, D), lambda i, te, nu: (i, 0)),
        scratch_shapes=[
            pltpu.VMEM((2, MOE_TILE, D), F32),
            pltpu.VMEM((D, 2 * D_FF_EXPERT), BF16),
            pltpu.VMEM((D_FF_EXPERT, D), BF16),
            pltpu.SemaphoreType.DMA((2,)),
        ],
    )
    return pl.pallas_call(
        _expert_kernel,
        grid_spec=grid_spec,
        out_shape=jax.ShapeDtypeStruct((n_tiles * MOE_TILE, D), F32),
        compiler_params=_cparams(("arbitrary",)),
        name="moe_experts",
    )(tile_expert, n_used, rows, rows, h, w_gate_up, w_down)


def _combine_kernel(dcur_ref, dnext_ref, x_ref, route_ref, y_hbm, o_ref, buf, sem, *, tm):
    i = pl.program_id(0)
    n = pl.num_programs(0)
    slot = i % 2

    def fetch(dref, s):
        def body(r, c):
            _row_dma(y_hbm, dref[0, 0, r], buf.at[s, 0], r, sem.at[s]).start()
            _row_dma(y_hbm, dref[0, 0, tm + r], buf.at[s, 1], r, sem.at[s]).start()
            return c

        lax.fori_loop(0, tm, body, 0, unroll=8)

    @pl.when(i == 0)
    def _():
        fetch(dcur_ref, 0)

    @pl.when(i + 1 < n)
    def _():
        fetch(dnext_ref, 1 - slot)

    _rows_wait(y_hbm, buf.at[slot, 0], tm, sem.at[slot])
    _rows_wait(y_hbm, buf.at[slot, 1], tm, sem.at[slot])
    route = route_ref[...]
    lane = lax.broadcasted_iota(jnp.int32, route.shape, 1)
    w1 = jnp.sum(jnp.where(lane == 0, route, 0.0), axis=1, keepdims=True)
    w2 = jnp.sum(jnp.where(lane == 1, route, 0.0), axis=1, keepdims=True)
    o_ref[...] = x_ref[...] + (w1 * buf[slot, 0] + w2 * buf[slot, 1])


def moe_combine(x, ys, route, dest, tm=256):
    T, D = x.shape
    n = T // tm
    dest_tiles = dest.reshape(2, n, tm).transpose(1, 0, 2).reshape(n, 1, 2 * tm)
    return pl.pallas_call(
        functools.partial(_combine_kernel, tm=tm),
        grid=(n,),
        in_specs=[pl.BlockSpec((1, 1, 2 * tm), lambda i: (i, 0, 0), memory_space=pltpu.SMEM),
                  pl.BlockSpec((1, 1, 2 * tm), lambda i: (jnp.minimum(i + 1, n - 1), 0, 0), memory_space=pltpu.SMEM),
                  pl.BlockSpec((tm, D), lambda i: (i, 0)),
                  pl.BlockSpec((tm, LANES), lambda i: (i, 0)),
                  pl.BlockSpec(memory_space=pl.ANY)],
        out_specs=pl.BlockSpec((tm, D), lambda i: (i, 0)),
        out_shape=jax.ShapeDtypeStruct((T, D), F32),
        scratch_shapes=[pltpu.VMEM((2, 2, tm, D), F32), pltpu.SemaphoreType.DMA((2,))],
        compiler_params=_cparams(("arbitrary",)),
        name="moe_combine",
    )(dest_tiles, dest_tiles, x, route, ys)


def hierarchical_moe(x, gain, w_grp, b_grp, w_exp, b_exp, w_gate_up, w_down, layer):
    T, D = x.shape
    h, route, cnt = moe_route(x, gain, w_grp, b_grp, w_exp, b_exp)
    route_t = route[:, 0:8].T
    expert = route_t[2:4].astype(jnp.int32)
    rank = route_t[4:6].astype(jnp.int32)
    counts = cnt[0, 0:N_EXPERTS].astype(jnp.int32)
    padded = ((counts + MOE_TILE - 1) // MOE_TILE) * MOE_TILE
    ends = jnp.cumsum(padded)
    offs = ends - padded
    dest = offs[expert] + rank
    n_rows = 2 * T + N_EXPERTS * MOE_TILE
    n_tiles = n_rows // MOE_TILE
    n_used = (ends[-1] // MOE_TILE).astype(jnp.int32)
    tile_start = jnp.arange(n_tiles, dtype=jnp.int32) * MOE_TILE
    tile_expert = jnp.sum((ends[None, :] <= tile_start[:, None]).astype(jnp.int32), axis=1)
    last_expert = tile_expert[jnp.maximum(n_used - 1, 0)]
    tile_expert = jnp.where(tile_start < ends[-1], tile_expert, last_expert)
    row_token = jnp.zeros((n_rows,), jnp.int32).at[dest.reshape(-1)].set(
        jnp.tile(jnp.arange(T, dtype=jnp.int32), 2))
    ys = moe_experts(h, row_token, tile_expert, n_used.reshape(1), w_gate_up, w_down, layer)
    return moe_combine(x, ys, route, dest)


def _in_proj_kernel(a_ref, wt_hbm, o_ref, wbuf, sem, *, layer, row0, tn, sigmoid_out):
    i, j = pl.program_id(0), pl.program_id(1)
    ni, nj = pl.num_programs(0), pl.num_programs(1)
    step = i * nj + j
    slot = step % 2

    def fetch(jj, s):
        start = pl.multiple_of(row0 + jj * tn, SUBLANES)
        return pltpu.make_async_copy(wt_hbm.at[layer, pl.ds(start, tn), :], wbuf.at[s], sem.at[s])

    @pl.when(step == 0)
    def _():
        fetch(0, 0).start()

    @pl.when(step + 1 < ni * nj)
    def _():
        fetch((j + 1) % nj, 1 - slot).start()

    fetch(j, slot).wait()
    w = wbuf[slot].astype(BF16)
    acc = lax.dot_general(a_ref[...], w, NT, preferred_element_type=F32)
    if sigmoid_out:
        acc = _sigmoid(acc)
    o_ref[...] = acc.astype(o_ref.dtype)


def in_proj(h, wt, layer, row0, n, tn, name, tm=2048, sigmoid_out=False):
    T, K = h.shape
    assert n % tn == 0 and row0 % SUBLANES == 0 and tn % SUBLANES == 0
    return pl.pallas_call(
        functools.partial(_in_proj_kernel, layer=layer, row0=row0, tn=tn, sigmoid_out=sigmoid_out),
        grid=(T // tm, n // tn),
        in_specs=[pl.BlockSpec((tm, K), lambda i, j: (i, 0)), pl.BlockSpec(memory_space=pl.ANY)],
        out_specs=pl.BlockSpec((tm, tn), lambda i, j: (i, j)),
        out_shape=jax.ShapeDtypeStruct((T, n), BF16 if sigmoid_out else F32),
        scratch_shapes=[pltpu.VMEM((2, tn, K), F32), pltpu.SemaphoreType.DMA((2,))],
        compiler_params=_cparams(("arbitrary", "arbitrary")),
        name=name,
    )(h, wt)


def kernel(x, norm_mix, w_in, cmp_pe, cmp_w1, cmp_w2, q_norm, k_norm, hg_lb_logits, hg_norm, gla_w_a2, gla_b_a,
           gla_norm, w_branch, w_out, norm_ffn, w_grp, b_grp, w_exp, b_exp, w_gate_up, w_down):
    B, T, D = x.shape
    assert B == 1 and D == D_MODEL
    xt = x[0]
    p_lb = jax.nn.softmax(hg_lb_logits.astype(F32), axis=0)
    lower_bounds = jnp.cumsum(p_lb, axis=0) - p_lb[0]
    wt = jnp.swapaxes(w_in, 1, 2)
    for l in range(DEPTH):
        h = rmsnorm_bf16(xt, norm_mix[l])
        proj_n = in_proj(h, wt, l, W_NSA[0], W_NSA[1], 512, "in_proj_nsa")
        proj_r = in_proj(h, wt, l, W_REC[0], W_REC[1], 512, "in_proj_rec")
        proj_a = in_proj(h, wt, l, W_GLA_A_BLOCK[0], W_GLA_A_BLOCK[1], LANES, "in_proj_gla_a")
        proj_m = in_proj(h, wt, l, W_MERGE[0], W_MERGE[1], 512, "in_proj_merge", sigmoid_out=True)
        o_cmp, o_slc, o_win = nsa_mixer(proj_n, cmp_pe[l], cmp_w1[l], cmp_w2[l], q_norm[l], k_norm[l])
        o_b = hgrn2_mixer(proj_r, lower_bounds[l], hg_norm[l])
        o_c = gla_mixer(proj_r, proj_a, gla_w_a2[l], gla_b_a[l], gla_norm[l])
        merged = merge_branches(proj_m, o_cmp, o_slc, o_win, o_b, o_c, w_branch[l].astype(BF16))
        xt = matmul_bf16(merged, w_out[l].astype(BF16), res=xt, tm=1024, tn=512, name="out_proj")
        xt = hierarchical_moe(xt, norm_ffn[l], w_grp[l], b_grp[l], w_exp[l], b_exp[l], w_gate_up, w_down, l)
    return xt[None]
```

```python
import functools
import math

import numpy as np
import jax
import jax.numpy as jnp
from jax import lax
from jax.experimental import pallas as pl
from jax.experimental.pallas import tpu as pltpu

F32 = jnp.float32
BF16 = jnp.bfloat16
HIGHEST = lax.Precision.HIGHEST

D_MODEL = 2048
DEPTH = 2
BRANCH_WIDTH = D_MODEL // 4
NSA_HEADS = 8
NSA_KV_HEADS = 2
NSA_REP = NSA_HEADS // NSA_KV_HEADS
NSA_HD = 64
CMP_LEN = 32
CMP_STRIDE = 16
SLC_LEN = 64
SLC_TOP = 16
WINDOW = 512
HG_HEADS = 4
GLA_HEADS = 4
GLA_DK = 64
GLA_RANK = 16
GLA_TAU = 16.0
CHUNK = 64
SUB = 16
N_GROUPS = 4
EXPERTS_PER_GROUP = 8
N_EXPERTS = N_GROUPS * EXPERTS_PER_GROUP
D_FF_EXPERT = D_MODEL // 4
EPS = 1e-6
NEG = -1e30
FORCE = 1e4
TINY = 1e-30

LANES = 128
SUBLANES = 8
VMEM_LIMIT = 56 * 1024 * 1024

W_NSA = (0, 1536)
W_REC = (1304, 3584)
W_MERGE = (4904, 6144)
W_GLA_A_BLOCK = (4864, 128)
C_NSA_KV = 512
C_NSA_GATE = 1280
C_GLA_QK = 2048
C_GLA_VR = 2560
TAIL_GLA_A0 = 4888 - W_GLA_A_BLOCK[0]

NT = (((1,), (1,)), ((), ()))
TN = (((0,), (0,)), ((), ()))


def _cparams(sem):
    return pltpu.CompilerParams(dimension_semantics=sem, vmem_limit_bytes=VMEM_LIMIT)


def _sigmoid(x):
    return 1.0 / (1.0 + jnp.exp(-x))


def _norm_kernel(x_ref, g_ref, o_ref):
    x = x_ref[...]
    ms = jnp.mean(x * x, axis=-1, keepdims=True)
    o_ref[...] = (x * lax.rsqrt(ms + EPS) * g_ref[...]).astype(o_ref.dtype)


def rmsnorm_bf16(x, gain, tm=512):
    T, D = x.shape
    return pl.pallas_call(
        _norm_kernel,
        grid=(T // tm,),
        in_specs=[pl.BlockSpec((tm, D), lambda i: (i, 0)), pl.BlockSpec((1, D), lambda i: (0, 0))],
        out_specs=pl.BlockSpec((tm, D), lambda i: (i, 0)),
        out_shape=jax.ShapeDtypeStruct((T, D), BF16),
        compiler_params=_cparams(("parallel",)),
        name="rmsnorm_bf16",
    )(x, gain.reshape(1, D))


def _mm_kernel(a_ref, b_ref, o_ref):
    o_ref[...] = jnp.dot(a_ref[...], b_ref[...], preferred_element_type=F32)


def _mm_res_kernel(a_ref, b_ref, r_ref, o_ref):
    o_ref[...] = r_ref[...] + jnp.dot(a_ref[...], b_ref[...], preferred_element_type=F32)


def matmul_bf16(a, b, res=None, tm=1024, tn=512, name="matmul_bf16"):
    T, K = a.shape
    N = b.shape[1]
    in_specs = [pl.BlockSpec((tm, K), lambda i, j: (i, 0)), pl.BlockSpec((K, tn), lambda i, j: (0, j))]
    args = [a, b]
    kern = _mm_kernel
    if res is not None:
        in_specs.append(pl.BlockSpec((tm, tn), lambda i, j: (i, j)))
        args.append(res)
        kern = _mm_res_kernel
    return pl.pallas_call(
        kern,
        grid=(T // tm, N // tn),
        in_specs=in_specs,
        out_specs=pl.BlockSpec((tm, tn), lambda i, j: (i, j)),
        out_shape=jax.ShapeDtypeStruct((T, N), F32),
        compiler_params=_cparams(("parallel", "arbitrary")),
        name=name,
    )(*args)


def _half_rmsnorm(x, gain2, lo):
    x2 = x * x
    s_lo = jnp.sum(jnp.where(lo, x2, 0.0), axis=1, keepdims=True)
    s_hi = jnp.sum(jnp.where(lo, 0.0, x2), axis=1, keepdims=True)
    ms = jnp.where(lo, s_lo, s_hi) * (1.0 / NSA_HD)
    return x * lax.rsqrt(ms + EPS) * gain2


def _nsa_prep_kernel(q_ref, kvc_ref, kvs_ref, kvw_ref, tail_ref, qg_ref, kg_ref,
                     qn_ref, kaug_ref, vs_ref, kw_ref, vw_ref, gate_ref, *, tm):
    i = pl.program_id(0)
    lane = lax.broadcasted_iota(jnp.int32, (tm, LANES), 1)
    lo = lane < NSA_HD
    qg = qg_ref[...]
    for c in range(4):
        x = q_ref[:, c * LANES:(c + 1) * LANES]
        qn_ref[:, c * LANES:(c + 1) * LANES] = (_half_rmsnorm(x, qg, lo) * (NSA_HD ** -0.5)).astype(BF16)

    def dup(kn):
        rolled = pltpu.roll(kn, NSA_HD, axis=1)
        return jnp.where(lo, kn, rolled), jnp.where(lo, rolled, kn)

    vt_zero = jnp.zeros((NSA_HD, tm), BF16)

    def store_vt(v_pair, ref):
        vt = jnp.transpose(v_pair)
        for g in range(NSA_KV_HEADS):
            head = vt[g * NSA_HD:(g + 1) * NSA_HD, :].astype(BF16)
            ref[g, 0:NSA_HD, :] = head
            ref[g, NSA_HD:LANES, :] = vt_zero
            ref[g, LANES:LANES + NSA_HD, :] = vt_zero
            ref[g, LANES + NSA_HD:2 * LANES, :] = head

    row = i * tm + lax.broadcasted_iota(jnp.int32, (tm, LANES), 0)
    onehot = jnp.where(row // SLC_LEN == lane, 1.0, 0.0).astype(BF16)
    k_extra = jnp.where(lane == NSA_HD, (row % SLC_LEN).astype(F32), 0.0)
    ks = dup(_half_rmsnorm(kvs_ref[:, 0:LANES], kg_ref[1:2, :], lo))
    store_vt(kvs_ref[:, LANES:2 * LANES], vs_ref)
    kw = dup(_half_rmsnorm(kvw_ref[:, 0:LANES], kg_ref[2:3, :], lo))
    store_vt(kvw_ref[:, LANES:2 * LANES], vw_ref)
    for g in range(NSA_KV_HEADS):
        kaug_ref[g, :, 0:LANES] = onehot
        kaug_ref[g, :, LANES:2 * LANES] = jnp.where(lo, ks[g], k_extra).astype(BF16)
        kw_ref[g] = kw[g].astype(BF16)
    gate_ref[...] = _sigmoid(tail_ref[...])


def nsa_prep(proj_n, q_gain, k_gain, tm=512):
    T = proj_n.shape[0]
    qg2 = jnp.tile(q_gain.reshape(1, NSA_HD), (1, 2))
    kg2 = jnp.tile(k_gain.reshape(3, NSA_HD), (1, 2))
    G = NSA_KV_HEADS
    out_shape = (
        jax.ShapeDtypeStruct((T, 512), BF16),
        jax.ShapeDtypeStruct((G, T, 256), BF16),
        jax.ShapeDtypeStruct((G, 256, T), BF16),
        jax.ShapeDtypeStruct((G, T, 128), BF16),
        jax.ShapeDtypeStruct((G, 256, T), BF16),
        jax.ShapeDtypeStruct((T, 128), F32),
    )
    return pl.pallas_call(
        functools.partial(_nsa_prep_kernel, tm=tm),
        grid=(T // tm,),
        in_specs=[
            pl.BlockSpec((tm, 512), lambda i: (i, 0)),
            pl.BlockSpec((tm, 256), lambda i: (i, C_NSA_KV // 256)),
            pl.BlockSpec((tm, 256), lambda i: (i, C_NSA_KV // 256 + 1)),
            pl.BlockSpec((tm, 256), lambda i: (i, C_NSA_KV // 256 + 2)),
            pl.BlockSpec((tm, 128), lambda i: (i, C_NSA_GATE // 128)),
            pl.BlockSpec((1, 128), lambda i: (0, 0)),
            pl.BlockSpec((3, 128), lambda i: (0, 0)),
        ],
        out_specs=(
            pl.BlockSpec((tm, 512), lambda i: (i, 0)),
            pl.BlockSpec((G, tm, 256), lambda i: (0, i, 0)),
            pl.BlockSpec((G, 256, tm), lambda i: (0, 0, i)),
            pl.BlockSpec((G, tm, 128), lambda i: (0, i, 0)),
            pl.BlockSpec((G, 256, tm), lambda i: (0, 0, i)),
            pl.BlockSpec((tm, 128), lambda i: (i, 0)),
        ),
        out_shape=out_shape,
        compiler_params=_cparams(("parallel",)),
        name="nsa_prep",
    )(proj_n, proj_n, proj_n, proj_n, proj_n, qg2, kg2)


def _gelu_tanh(x):
    c = math.sqrt(2.0 / math.pi)
    return 0.5 * x * (1.0 + jnp.tanh(c * (x + 0.044715 * (x * x * x))))


def _compress_kernel(a_ref, pe_ref, w1_ref, w2_ref, kg_ref, o_ref, *, nb):
    kind = pl.program_id(0)
    a_lo = a_ref[0, 0, 0:nb, :]
    a_hi = a_ref[0, 0, 1:nb + 1, :]
    blocks = jnp.concatenate([a_lo, a_hi], axis=1) + pe_ref[0]
    h1 = jnp.dot(blocks.astype(BF16), w1_ref[0].astype(BF16), preferred_element_type=F32)
    y = jnp.dot(_gelu_tanh(h1).astype(BF16), w2_ref[0].astype(BF16), preferred_element_type=F32)
    ms = jnp.mean(y * y, axis=-1, keepdims=True)
    yn = y * lax.rsqrt(ms + EPS) * kg_ref[...]
    o_ref[0, 0] = jnp.where(kind == 0, yn, y)


def nsa_compress(kv_cmp, cmp_pe, cmp_w1, cmp_w2, k_gain0):
    T = kv_cmp.shape[0]
    nb = T // CMP_STRIDE
    G = NSA_KV_HEADS
    a = kv_cmp.reshape(T, 2, G, NSA_HD).transpose(1, 2, 0, 3).reshape(2, G, nb, CMP_STRIDE * NSA_HD)
    a = jnp.pad(a, ((0, 0), (0, 0), (0, 8), (0, 0)))
    pe = cmp_pe.reshape(2, 1, CMP_LEN * NSA_HD)
    return pl.pallas_call(
        functools.partial(_compress_kernel, nb=nb),
        grid=(2, G),
        in_specs=[
            pl.BlockSpec((1, 1, nb + 8, CMP_STRIDE * NSA_HD), lambda k, g: (k, g, 0, 0)),
            pl.BlockSpec((1, 1, CMP_LEN * NSA_HD), lambda k, g: (k, 0, 0)),
            pl.BlockSpec((1, CMP_LEN * NSA_HD, NSA_HD), lambda k, g: (k, 0, 0)),
            pl.BlockSpec((1, NSA_HD, NSA_HD), lambda k, g: (k, 0, 0)),
            pl.BlockSpec((1, NSA_HD), lambda k, g: (0, 0)),
        ],
        out_specs=pl.BlockSpec((1, 1, nb, NSA_HD), lambda k, g: (k, g, 0, 0)),
        out_shape=jax.ShapeDtypeStruct((2, G, nb, NSA_HD), F32),
        compiler_params=_cparams(("arbitrary", "arbitrary")),
        name="nsa_compress",
    )(a, pe, cmp_w1, cmp_w2, k_gain0.reshape(1, NSA_HD))


def _slope(g, r):
    return jnp.where(g == 0, 2.0 ** -(r + 1), 2.0 ** -(NSA_REP + r + 1)).astype(F32)


def _gate_pair(gates, lane, lo, col_even):
    ge = jnp.sum(jnp.where(lane == col_even, gates, 0.0), axis=1, keepdims=True)
    go = jnp.sum(jnp.where(lane == col_even + 1, gates, 0.0), axis=1, keepdims=True)
    return jnp.where(lo, ge, go)


def _cmp_sel_kernel(q_ref, kc_ref, vc_ref, gate_ref, mt_ref, o_ref, sel_ref, *, tq, nb, ns_pad):
    g = pl.program_id(0)
    qi = pl.program_id(1)
    t0 = qi * tq
    lane = lax.broadcasted_iota(jnp.int32, (tq, LANES), 1)
    lo = lane < NSA_HD
    n_idx = lax.broadcasted_iota(jnp.int32, (nb, tq), 0)
    t_idx = t0 + lax.broadcasted_iota(jnp.int32, (nb, tq), 1)
    dist = (t_idx - (n_idx * CMP_STRIDE + (CMP_LEN - 1))).astype(F32)
    vis = dist >= 0.0
    kc = kc_ref[0]
    imp = jnp.zeros((nb, tq), F32)
    gates = gate_ref[...]
    for a in range(2):
        acc = jnp.zeros((tq, LANES), F32)
        for half in range(2):
            r = 2 * a + half
            qh = q_ref[:, a * LANES:(a + 1) * LANES]
            qh = jnp.where(lo if half == 0 else jnp.logical_not(lo), qh, jnp.zeros_like(qh))
            s = lax.dot_general(kc, qh, NT, preferred_element_type=F32)
            s = jnp.where(vis, s - _slope(g, r) * dist, NEG)
            mx = jnp.max(s, axis=0, keepdims=True)
            e = jnp.where(vis, jnp.exp(s - mx), 0.0)
            den = jnp.sum(e, axis=0, keepdims=True)
            p = e * jnp.where(den > 0.0, 1.0 / den, 0.0)
            imp = imp + p
            v = vc_ref[0, :, half * LANES:(half + 1) * LANES]
            acc = acc + lax.dot_general(p.astype(BF16), v, TN, preferred_element_type=F32)
        gp = _gate_pair(gates, lane, lo, g * NSA_REP + 2 * a)
        o_ref[:, a * LANES:(a + 1) * LANES] = acc * gp

    score_all = jnp.dot(mt_ref[...], imp, precision=HIGHEST, preferred_element_type=F32)
    blk = lax.broadcasted_iota(jnp.int32, (ns_pad, LANES), 0)
    for c in range(tq // LANES):
        cur = (t0 + c * LANES + lax.broadcasted_iota(jnp.int32, (ns_pad, LANES), 1)) // SLC_LEN
        forced = (blk == 0) | (blk == cur) | (blk == cur - 1)
        score = jnp.where(forced, FORCE, score_all[:, c * LANES:(c + 1) * LANES])
        score = jnp.where(blk <= cur, score, -1.0)
        for _ in range(SLC_TOP):
            mx = jnp.max(score, axis=0, keepdims=True)
            idx = jnp.min(jnp.where(score == mx, blk, ns_pad), axis=0, keepdims=True)
            score = jnp.where(blk == idx, -jnp.inf, score)
        bias_t = jnp.where((score == -jnp.inf) & (blk <= cur), 0.0, NEG)
        sel_ref[0, c * LANES:(c + 1) * LANES, :] = jnp.transpose(bias_t).astype(BF16)


def _score_matrix(nb, ns_pad):
    ratio, span = SLC_LEN // CMP_STRIDE, CMP_LEN // CMP_STRIDE
    n_cmp = nb - 1
    m = np.zeros((ns_pad, nb), np.float32)
    for s in range(nb // ratio):
        for mm in range(ratio):
            for nn in range(span):
                c = ratio * s + mm - nn
                if 0 <= c < n_cmp:
                    m[s, c] += 1.0
    return jnp.asarray(m)


def nsa_cmp_select(qn, kc2, vc_lohi, gates, tq=256):
    T = qn.shape[0]
    nb = T // CMP_STRIDE
    ns_pad = LANES
    G = NSA_KV_HEADS
    return pl.pallas_call(
        functools.partial(_cmp_sel_kernel, tq=tq, nb=nb, ns_pad=ns_pad),
        grid=(G, T // tq),
        in_specs=[
            pl.BlockSpec((tq, 256), lambda g, i: (i, g)),
            pl.BlockSpec((1, nb, 128), lambda g, i: (g, 0, 0)),
            pl.BlockSpec((1, nb, 256), lambda g, i: (g, 0, 0)),
            pl.BlockSpec((tq, 128), lambda g, i: (i, 0)),
            pl.BlockSpec((ns_pad, nb), lambda g, i: (0, 0)),
        ],
        out_specs=(
            pl.BlockSpec((tq, 256), lambda g, i: (i, g)),
            pl.BlockSpec((1, tq, ns_pad), lambda g, i: (g, i, 0)),
        ),
        out_shape=(jax.ShapeDtypeStruct((T, 512), F32), jax.ShapeDtypeStruct((G, T, ns_pad), BF16)),
        compiler_params=_cparams(("parallel", "parallel")),
        name="nsa_cmp_select",
    )(qn, kc2, vc_lohi, gates, _score_matrix(nb, ns_pad))


def _sel_attn_kernel(tile_tab, cnt_tab, q_ref, sb_ref, gate_ref, k_hbm, v_hbm, o_ref,
                     qaug, kbuf, vbuf, sem, m_scr, l_scr, acc_scr, *, tq, tk, max_tiles):
    g = pl.program_id(0)
    qi = pl.program_id(1)
    row = g * pl.num_programs(1) + qi
    n_tiles = cnt_tab[row]
    base = row * max_tiles
    lane = lax.broadcasted_iota(jnp.int32, (tq, LANES), 1)
    lo = lane < NSA_HD

    def copies(j, slot):
        start = pl.multiple_of(tile_tab[base + j] * tk, tk)
        return (pltpu.make_async_copy(k_hbm.at[g, pl.ds(start, tk), :], kbuf.at[slot], sem.at[0, slot]),
                pltpu.make_async_copy(v_hbm.at[g, :, pl.ds(start, tk)], vbuf.at[slot], sem.at[1, slot]))

    def start_fetch(j, slot):
        for cp in copies(j, slot):
            cp.start()

    def wait_fetch(j, slot):
        for cp in copies(j, slot):
            cp.wait()

    start_fetch(0, 0)

    sb = sb_ref[0].astype(F32)
    blk_rel = (lane - qi * (tq // SLC_LEN)).astype(F32)
    for r in range(NSA_REP):
        a, half = r // 2, r % 2
        slope = _slope(g, r)
        qh = q_ref[:, a * LANES:(a + 1) * LANES].astype(F32)
        if half:
            qh = pltpu.roll(qh, NSA_HD, axis=1)
        qh = jnp.where(lo, qh, jnp.where(lane == NSA_HD, slope, 0.0))
        qaug[r * tq:(r + 1) * tq, 0:LANES] = (sb + (slope * SLC_LEN) * blk_rel).astype(BF16)
        qaug[r * tq:(r + 1) * tq, LANES:2 * LANES] = qh.astype(BF16)
    m_scr[...] = jnp.full(m_scr.shape, NEG, F32)
    l_scr[...] = jnp.zeros(l_scr.shape, F32)
    acc_scr[...] = jnp.zeros(acc_scr.shape, F32)

    upper = lax.broadcasted_iota(jnp.int32, (LANES, tq), 0) < NSA_HD
    kj_diag = (qi * tq + (tq - 1)) // tk

    def step(masked, slot):
        if masked:
            kpos = kj_diag * tk + lax.broadcasted_iota(jnp.int32, (tk, tq), 0)
            qpos = qi * tq + lax.broadcasted_iota(jnp.int32, (tk, tq), 1)
            causal = kpos <= qpos
        k = kbuf[slot]
        for a in range(2):
            pv = jnp.zeros((LANES, tq), F32)
            alphas = []
            for half in range(2):
                r = 2 * a + half
                sr = lax.dot_general(k, qaug[r * tq:(r + 1) * tq, :], NT, preferred_element_type=F32)
                if masked:
                    sr = jnp.where(causal, sr, NEG)
                m_prev = m_scr[r:r + 1, :]
                m_new = jnp.maximum(m_prev, jnp.max(sr, axis=0, keepdims=True))
                alpha = jnp.exp(m_prev - m_new)
                p = jnp.exp(sr - m_new)
                l_scr[r:r + 1, :] = alpha * l_scr[r:r + 1, :] + jnp.sum(p, axis=0, keepdims=True)
                m_scr[r:r + 1, :] = m_new
                vt = vbuf[slot, half * LANES:(half + 1) * LANES, :]
                pv = pv + jnp.dot(vt, p.astype(BF16), preferred_element_type=F32)
                alphas.append(alpha)
            alpha_pair = jnp.where(upper, alphas[0], alphas[1])
            acc_scr[a * LANES:(a + 1) * LANES, :] = alpha_pair * acc_scr[a * LANES:(a + 1) * LANES, :] + pv

    def off_diagonal(j, carry):
        slot = j % 2
        start_fetch(j + 1, 1 - slot)
        wait_fetch(j, slot)
        step(False, slot)
        return carry

    lax.fori_loop(0, n_tiles - 1, off_diagonal, 0)
    last = n_tiles - 1
    wait_fetch(last, last % 2)
    step(True, last % 2)

    gates = gate_ref[...]
    for a in range(2):
        l_pair = jnp.where(upper, l_scr[2 * a:2 * a + 1, :], l_scr[2 * a + 1:2 * a + 2, :])
        gp = _gate_pair(gates, lane, lo, NSA_HEADS + g * NSA_REP + 2 * a)
        o_t = acc_scr[a * LANES:(a + 1) * LANES, :] / l_pair
        o_ref[:, a * LANES:(a + 1) * LANES] = jnp.transpose(o_t) * gp


def nsa_selected_attention(qn, selbias, kaug, vs_lohi, gates, tq=256, tk=512):
    T = qn.shape[0]
    G = NSA_KV_HEADS
    nq, nkt, nkb = T // tq, T // tk, tk // SLC_LEN
    chosen = (selbias.astype(F32) > 0.5 * NEG).reshape(G, nq, tq, LANES // nkb, nkb)
    tile_any = jnp.any(chosen, axis=(2, 4))[:, :, 0:nkt]
    kj = jnp.arange(nkt, dtype=jnp.int32)[None, None, :]
    kj_diag = ((jnp.arange(nq, dtype=jnp.int32) * tq + (tq - 1)) // tk)[None, :, None]
    active = (tile_any & (kj < kj_diag)) | (kj == kj_diag)
    csum = jnp.cumsum(active.astype(jnp.int32), axis=2)
    counts = csum[:, :, -1]
    tiles = jnp.sum((csum[:, :, None, :] <= kj[..., None]).astype(jnp.int32), axis=3)
    tiles = jnp.minimum(tiles, nkt - 1)
    grid_spec = pltpu.PrefetchScalarGridSpec(
        num_scalar_prefetch=2,
        grid=(G, nq),
        in_specs=[
            pl.BlockSpec((tq, 256), lambda g, i, tt, ct: (i, g)),
            pl.BlockSpec((1, tq, 128), lambda g, i, tt, ct: (g, i, 0)),
            pl.BlockSpec((tq, 128), lambda g, i, tt, ct: (i, 0)),
            pl.BlockSpec(memory_space=pl.ANY),
            pl.BlockSpec(memory_space=pl.ANY),
        ],
        out_specs=pl.BlockSpec((tq, 256), lambda g, i, tt, ct: (i, g)),
        scratch_shapes=[
            pltpu.VMEM((NSA_REP * tq, 256), BF16),
            pltpu.VMEM((2, tk, 256), BF16),
            pltpu.VMEM((2, 256, tk), BF16),
            pltpu.SemaphoreType.DMA((2, 2)),
            pltpu.VMEM((NSA_REP, tq), F32),
            pltpu.VMEM((NSA_REP, tq), F32),
            pltpu.VMEM((2 * LANES, tq), F32),
        ],
    )
    return pl.pallas_call(
        functools.partial(_sel_attn_kernel, tq=tq, tk=tk, max_tiles=nkt),
        grid_spec=grid_spec,
        out_shape=jax.ShapeDtypeStruct((T, 512), F32),
        compiler_params=_cparams(("parallel", "parallel")),
        name="nsa_selected_attention",
    )(tiles.reshape(-1).astype(jnp.int32), counts.reshape(-1).astype(jnp.int32),
      qn, selbias, gates, kaug, vs_lohi)


def _win_attn_kernel(q_ref, k0_ref, k1_ref, k2_ref, v0_ref, v1_ref, v2_ref, gate_ref, o_ref, *, tq):
    g = pl.program_id(0)
    qi = pl.program_id(1)
    lane = lax.broadcasted_iota(jnp.int32, (tq, LANES), 1)
    lo = lane < NSA_HD
    upper = lax.broadcasted_iota(jnp.int32, (LANES, tq), 0) < NSA_HD
    row = lax.broadcasted_iota(jnp.int32, (tq, tq), 0)
    qpos = qi * tq + lax.broadcasted_iota(jnp.int32, (tq, tq), 1)
    k_refs = (k0_ref, k1_ref, k2_ref)
    v_refs = (v0_ref, v1_ref, v2_ref)
    dists, masks = [], []
    for d in range(3):
        kpos = (qi - 2 + d) * tq + row
        dd = qpos - kpos
        dists.append(dd.astype(F32))
        masks.append((dd >= 0) & (dd < WINDOW) & (kpos >= 0))
    gates = gate_ref[...]
    for a in range(2):
        pv = jnp.zeros((LANES, tq), F32)
        ls = []
        for half in range(2):
            r = 2 * a + half
            qh = q_ref[:, a * LANES:(a + 1) * LANES]
            qh = jnp.where(lo if half == 0 else jnp.logical_not(lo), qh, jnp.zeros_like(qh))
            ss = []
            for d in range(3):
                sd = lax.dot_general(k_refs[d][0], qh, NT, preferred_element_type=F32)
                ss.append(jnp.where(masks[d], sd - _slope(g, r) * dists[d], NEG))
            mx = jnp.maximum(jnp.maximum(jnp.max(ss[0], axis=0, keepdims=True), jnp.max(ss[1], axis=0, keepdims=True)),
                             jnp.max(ss[2], axis=0, keepdims=True))
            l = jnp.zeros((1, tq), F32)
            for d in range(3):
                p = jnp.exp(ss[d] - mx)
                l = l + jnp.sum(p, axis=0, keepdims=True)
                vt = v_refs[d][0, half * LANES:(half + 1) * LANES, :]
                pv = pv + jnp.dot(vt, p.astype(BF16), preferred_element_type=F32)
            ls.append(l)
        l_pair = jnp.where(upper, ls[0], ls[1])
        gp = _gate_pair(gates, lane, lo, 2 * NSA_HEADS + g * NSA_REP + 2 * a)
        o_ref[:, a * LANES:(a + 1) * LANES] = jnp.transpose(pv / l_pair) * gp


def nsa_window_attention(qn, kw2, vw_lohi, gates, tq=256):
    T = qn.shape[0]
    G = NSA_KV_HEADS
    assert WINDOW == 2 * tq

    def kspec(d):
        return pl.BlockSpec((1, tq, 128), lambda g, i, d=d: (g, jnp.maximum(i - 2 + d, 0), 0))

    def vspec(d):
        return pl.BlockSpec((1, 256, tq), lambda g, i, d=d: (g, 0, jnp.maximum(i - 2 + d, 0)))

    return pl.pallas_call(
        functools.partial(_win_attn_kernel, tq=tq),
        grid=(G, T // tq),
        in_specs=[pl.BlockSpec((tq, 256), lambda g, i: (i, g)),
                  kspec(0), kspec(1), kspec(2),
                  vspec(0), vspec(1), vspec(2),
                  pl.BlockSpec((tq, 128), lambda g, i: (i, 0))],
        out_specs=pl.BlockSpec((tq, 256), lambda g, i: (i, g)),
        out_shape=jax.ShapeDtypeStruct((T, 512), F32),
        compiler_params=_cparams(("parallel", "parallel")),
        name="nsa_window_attention",
    )(qn, kw2, kw2, kw2, vw_lohi, vw_lohi, vw_lohi, gates)


def nsa_mixer(proj_n, cmp_pe, cmp_w1, cmp_w2, q_gain, k_gain, parts=False):
    qn, kaug, vs_lohi, kw2, vw_lohi, gates = nsa_prep(proj_n, q_gain, k_gain)
    kvc = nsa_compress(proj_n[:, C_NSA_KV:C_NSA_KV + 256], cmp_pe, cmp_w1, cmp_w2, k_gain[0])
    kc, vc = kvc[0], kvc[1]
    kc2 = jnp.concatenate([kc, kc], axis=-1).astype(BF16)
    zero = jnp.zeros_like(vc)
    vc_lohi = jnp.concatenate([vc, zero, zero, vc], axis=-1).astype(BF16)
    o_cmp, selbias = nsa_cmp_select(qn, kc2, vc_lohi, gates)
    o_slc = nsa_selected_attention(qn, selbias, kaug, vs_lohi, gates)
    o_win = nsa_window_attention(qn, kw2, vw_lohi, gates)
    if parts:
        return o_cmp + o_slc + o_win, (o_cmp, o_slc, o_win)
    return o_cmp, o_slc, o_win


def _diag_selector():
    m = np.zeros((SUB * LANES, LANES), np.float32)
    for j in range(SUB):
        for rep in range(CHUNK // SUB):
            m[j * LANES:(j + 1) * LANES, rep * SUB + j] = 1.0
    return jnp.asarray(m, dtype=BF16)


LOG2E = 1.0 / math.log(2.0)


def _recur_tile(q_scr, k_scr, v_scr, g_scr, o_scr, st_scr, sel_ref, tb):
    row = lax.broadcasted_iota(jnp.int32, (CHUNK, LANES), 0)
    lane = lax.broadcasted_iota(jnp.int32, (CHUNK, LANES), 1)
    sub_row = row % SUB
    blockdiag = (row // SUB) == (lane // SUB)
    r64 = lax.broadcasted_iota(jnp.int32, (CHUNK, CHUNK), 0)
    c64 = lax.broadcasted_iota(jnp.int32, (CHUNK, CHUNK), 1)
    ltri = jnp.where(r64 >= c64, 1.0, 0.0).astype(F32)
    c_sub = lax.broadcasted_iota(jnp.int32, (SUB, CHUNK), 1)
    nsub = CHUNK // SUB
    half = SUB // 2
    low_row = half + lax.broadcasted_iota(jnp.int32, (nsub, half, LANES), 1)

    def head_chunk(h, r0):
        qc = q_scr[h, pl.ds(r0, CHUNK), :]
        kc = k_scr[h, pl.ds(r0, CHUNK), :]
        vc = v_scr[h, pl.ds(r0, CHUNK), :]
        gc = g_scr[h, pl.ds(r0, CHUNK), :]
        b = jnp.dot(ltri, gc, precision=HIGHEST, preferred_element_type=F32)
        bend = b[CHUNK - 1:CHUNK, :]
        st = st_scr[h]
        o = lax.dot_general((qc * jnp.exp2(b)).astype(BF16), st.astype(BF16), NT, preferred_element_type=F32)
        k4 = kc.reshape(nsub, SUB, LANES)
        b4 = b.reshape(nsub, SUB, LANES)
        pieces = []
        for j in range(half):
            k_rep = jnp.broadcast_to(k4[:, j:j + 1, :], (nsub, SUB, LANES)).reshape(CHUNK, LANES)
            b_rep = jnp.broadcast_to(b4[:, j:j + 1, :], (nsub, SUB, LANES)).reshape(CHUNK, LANES)
            e = qc * k_rep * jnp.exp2(jnp.where(sub_row >= j, b - b_rep, NEG))
            pieces.append(e.astype(BF16))
        q_low = qc.reshape(nsub, SUB, LANES)[:, half:, :]
        b_low = b4[:, half:, :]
        for j in range(half, SUB):
            k_rep = jnp.broadcast_to(k4[:, j:j + 1, :], (nsub, half, LANES))
            b_rep = jnp.broadcast_to(b4[:, j:j + 1, :], (nsub, half, LANES))
            e_low = q_low * k_rep * jnp.exp2(jnp.where(low_row >= j, b_low - b_rep, NEG))
            e = jnp.concatenate([jnp.zeros_like(e_low), e_low], axis=1).reshape(CHUNK, LANES)
            pieces.append(e.astype(BF16))
        a_diag = jnp.dot(jnp.concatenate(pieces, axis=1), sel_ref[...], preferred_element_type=F32)
        a_diag = jnp.where(blockdiag, a_diag, 0.0)[:, 0:CHUNK]
        rows = [jnp.zeros((SUB, CHUNK), F32)]
        for i_sub in range(1, nsub):
            ref_b = b[i_sub * SUB - 1:i_sub * SUB, :]
            qt = qc[i_sub * SUB:(i_sub + 1) * SUB, :] * jnp.exp2(b[i_sub * SUB:(i_sub + 1) * SUB, :] - ref_b)
            kt = kc * jnp.exp2(jnp.minimum(ref_b - b, 0.0))
            a_i = lax.dot_general(qt.astype(BF16), kt.astype(BF16), NT, preferred_element_type=F32)
            rows.append(jnp.where(c_sub < i_sub * SUB, a_i, 0.0))
        a = jnp.concatenate(rows, axis=0) + a_diag
        o = o + jnp.dot(a.astype(BF16), vc.astype(BF16), preferred_element_type=F32)
        o_scr[h, pl.ds(r0, CHUNK), :] = o
        kend = kc * jnp.exp2(bend - b)
        st_scr[h] = st * jnp.exp2(bend) + lax.dot_general(vc.astype(BF16), kend.astype(BF16), TN,
                                                          preferred_element_type=F32)

    def chunk(c, carry):
        r0 = pl.multiple_of(c * CHUNK, CHUNK)
        for h in range(q_scr.shape[0]):
            head_chunk(h, r0)
        return carry

    lax.fori_loop(0, tb // CHUNK, chunk, 0)


def _finish_recur(o_scr, gain_ref, gate, o_ref):
    for h in range(o_scr.shape[0]):
        o = o_scr[h]
        ms = jnp.mean(o * o, axis=-1, keepdims=True)
        y = o * lax.rsqrt(ms + EPS) * gain_ref[...] * gate[:, h * LANES:(h + 1) * LANES]
        o_ref[:, h * LANES:(h + 1) * LANES] = y.astype(o_ref.dtype)


def _hgrn2_kernel(q_ref, f_ref, i_ref, gg_ref, lb_ref, gain_ref, sel_ref, o_ref,
                  q_scr, k_scr, v_scr, g_scr, o_scr, st_scr, *, tb):
    @pl.when(pl.program_id(1) == 0)
    def _():
        st_scr[...] = jnp.zeros(st_scr.shape, F32)

    for h in range(q_scr.shape[0]):
        cols = slice(h * LANES, (h + 1) * LANES)
        lb = lb_ref[:, cols]
        z = f_ref[:, cols]
        sg = _sigmoid(z)
        f = lb + (1.0 - lb) * sg
        q_scr[h] = q_ref[:, cols]
        k_scr[h] = (1.0 - lb) * (1.0 - sg)
        v_scr[h] = i_ref[:, cols]
        g_scr[h] = jnp.log(jnp.maximum(f, TINY)) * LOG2E
    _recur_tile(q_scr, k_scr, v_scr, g_scr, o_scr, st_scr, sel_ref, tb)
    _finish_recur(o_scr, gain_ref, _sigmoid(gg_ref[...]), o_ref)


def _recur_scratch(tb, heads):
    return ([pltpu.VMEM((heads, tb, LANES), F32) for _ in range(5)]
            + [pltpu.VMEM((heads, LANES, LANES), F32)])


def hgrn2_mixer(proj, lower_bound, norm_gain, tb=512, heads=HG_HEADS):
    T = proj.shape[0]
    wide = heads * LANES
    per = HG_HEADS // heads

    def col(k):
        return pl.BlockSpec((tb, wide), lambda hp, i, k=k: (i, per * k + hp))

    return pl.pallas_call(
        functools.partial(_hgrn2_kernel, tb=tb),
        grid=(per, T // tb),
        in_specs=[col(0), col(1), col(2), col(3),
                  pl.BlockSpec((1, wide), lambda hp, i: (0, hp)),
                  pl.BlockSpec((1, LANES), lambda hp, i: (0, 0)),
                  pl.BlockSpec((SUB * LANES, LANES), lambda hp, i: (0, 0))],
        out_specs=pl.BlockSpec((tb, wide), lambda hp, i: (i, hp)),
        out_shape=jax.ShapeDtypeStruct((T, BRANCH_WIDTH), BF16),
        scratch_shapes=_recur_scratch(tb, heads),
        compiler_params=_cparams(("parallel", "arbitrary")),
        name="hgrn2_mixer",
    )(proj, proj, proj, proj, lower_bound.reshape(1, -1), norm_gain.reshape(1, -1), _diag_selector())


def _gla_kernel(q_ref, k_ref, v_ref, r_ref, tail_ref, wa_ref, ba_ref, gain_ref, sel_ref, o_ref,
                q_scr, k_scr, v_scr, g_scr, o_scr, st_scr, *, tb):
    @pl.when(pl.program_id(1) == 0)
    def _():
        st_scr[...] = jnp.zeros(st_scr.shape, F32)

    lane = lax.broadcasted_iota(jnp.int32, (tb, LANES), 1)
    a = jnp.dot(tail_ref[...], wa_ref[...], precision=HIGHEST, preferred_element_type=F32) + ba_ref[...]
    log_sig = -(jnp.maximum(-a, 0.0) + jnp.log1p(jnp.exp(-jnp.abs(a))))
    g2 = log_sig * (LOG2E / GLA_TAU)
    for h in range(q_scr.shape[0]):
        pair = slice((h // 2) * LANES, (h // 2 + 1) * LANES)
        mine = (lane < GLA_DK) if h % 2 == 0 else (lane >= GLA_DK)
        q_scr[h] = jnp.where(mine, q_ref[:, pair] * (GLA_DK ** -0.5), 0.0)
        k_scr[h] = jnp.where(mine, k_ref[:, pair], 0.0)
        v_scr[h] = v_ref[:, h * LANES:(h + 1) * LANES]
        g_scr[h] = jnp.where(mine, g2[:, pair], 0.0)
    _recur_tile(q_scr, k_scr, v_scr, g_scr, o_scr, st_scr, sel_ref, tb)
    r = r_ref[...]
    _finish_recur(o_scr, gain_ref, r * _sigmoid(r), o_ref)


def gla_mixer(proj, proj_t, w_a2, b_a, norm_gain, tb=512, heads=GLA_HEADS):
    T = proj.shape[0]
    assert heads % 2 == 0 and LANES == 2 * GLA_DK
    wide = heads * LANES
    half = heads * GLA_DK
    cqk = C_GLA_QK // half
    cvr = C_GLA_VR // wide
    per = GLA_HEADS // heads
    wa = jnp.zeros((LANES, GLA_HEADS * GLA_DK), F32).at[TAIL_GLA_A0:TAIL_GLA_A0 + GLA_RANK].set(w_a2)
    return pl.pallas_call(
        functools.partial(_gla_kernel, tb=tb),
        grid=(per, T // tb),
        in_specs=[pl.BlockSpec((tb, half), lambda hp, i: (i, cqk + hp)),
                  pl.BlockSpec((tb, half), lambda hp, i: (i, cqk + per + hp)),
                  pl.BlockSpec((tb, wide), lambda hp, i: (i, cvr + hp)),
                  pl.BlockSpec((tb, wide), lambda hp, i: (i, cvr + per + hp)),
                  pl.BlockSpec((tb, LANES), lambda hp, i: (i, 0)),
                  pl.BlockSpec((LANES, half), lambda hp, i: (0, hp)),
                  pl.BlockSpec((1, half), lambda hp, i: (0, hp)),
                  pl.BlockSpec((1, LANES), lambda hp, i: (0, 0)),
                  pl.BlockSpec((SUB * LANES, LANES), lambda hp, i: (0, 0))],
        out_specs=pl.BlockSpec((tb, wide), lambda hp, i: (i, hp)),
        out_shape=jax.ShapeDtypeStruct((T, BRANCH_WIDTH), BF16),
        scratch_shapes=_recur_scratch(tb, heads),
        compiler_params=_cparams(("parallel", "arbitrary")),
        name="gla_mixer",
    )(proj, proj, proj, proj, proj_t, wa, b_a.reshape(1, -1), norm_gain.reshape(1, -1), _diag_selector())


def _merge_kernel(oc_ref, os_ref, ow_ref, ob_ref, og_ref, wb_ref, g0_ref, g1_ref, g2_ref, o_ref):
    o_a = (oc_ref[...] + os_ref[...] + ow_ref[...]).astype(BF16)
    acc = _sigmoid(g0_ref[...]) * jnp.dot(o_a, wb_ref[0], preferred_element_type=F32)
    acc = acc + _sigmoid(g1_ref[...]) * jnp.dot(ob_ref[...], wb_ref[1], preferred_element_type=F32)
    acc = acc + _sigmoid(g2_ref[...]) * jnp.dot(og_ref[...], wb_ref[2], preferred_element_type=F32)
    o_ref[...] = acc.astype(o_ref.dtype)


def merge_branches(proj, o_cmp, o_slc, o_win, o_b, o_c, w_branch_bf16, tm=512, tn=512):
    T = proj.shape[0]
    W = BRANCH_WIDTH
    nj = D_MODEL // tn
    ospec = pl.BlockSpec((tm, W), lambda i, j: (i, 0))

    def gspec(n):
        return pl.BlockSpec((tm, tn), lambda i, j, n=n: (i, n * nj + j))

    return pl.pallas_call(
        _merge_kernel,
        grid=(T // tm, nj),
        in_specs=[ospec, ospec, ospec, ospec, ospec,
                  pl.BlockSpec((3, W, tn), lambda i, j: (0, 0, j)),
                  gspec(0), gspec(1), gspec(2)],
        out_specs=pl.BlockSpec((tm, tn), lambda i, j: (i, j)),
        out_shape=jax.ShapeDtypeStruct((T, D_MODEL), BF16),
        compiler_params=_cparams(("parallel", "arbitrary")),
        name="merge_branches",
    )(o_cmp, o_slc, o_win, o_b, o_c, w_branch_bf16, proj, proj, proj)


MOE_TILE = 256


def _route_kernel(x_ref, g_ref, wr_ref, br_ref, h_ref, route_ref, cnt_ref, carry, *, tm):
    i = pl.program_id(0)

    @pl.when(i == 0)
    def _():
        carry[...] = jnp.zeros(carry.shape, F32)

    x = x_ref[...]
    ms = jnp.mean(x * x, axis=-1, keepdims=True)
    h = x * lax.rsqrt(ms + EPS) * g_ref[...]
    h_ref[...] = h
    logits = jnp.dot(h, wr_ref[...], precision=HIGHEST, preferred_element_type=F32) + br_ref[...]
    lane = lax.broadcasted_iota(jnp.int32, (tm, LANES), 1)

    def masked_softmax(mask):
        l = jnp.where(mask, logits, NEG)
        e = jnp.where(mask, jnp.exp(l - jnp.max(l, axis=1, keepdims=True)), 0.0)
        return e / jnp.sum(e, axis=1, keepdims=True)

    def top1(prob, mask):
        p = jnp.max(jnp.where(mask, prob, -1.0), axis=1, keepdims=True)
        idx = jnp.min(jnp.where(mask & (prob == p), lane, LANES), axis=1, keepdims=True)
        return p, idx

    gmask = lane < N_GROUPS
    gw, gidx = top1(masked_softmax(gmask), gmask)
    emask = (lane >= N_GROUPS) & (lane < N_GROUPS + N_EXPERTS) & ((lane - N_GROUPS) // EXPERTS_PER_GROUP == gidx)
    eprob = masked_softmax(emask)
    p1, i1 = top1(eprob, emask)
    rest = emask & (lane != i1)
    p2, i2 = top1(eprob, rest)
    psum = p1 + p2
    w1 = gw * (p1 / psum)
    w2 = gw * (p2 / psum)
    e1 = i1 - N_GROUPS
    e2 = i2 - N_GROUPS

    onehot = jnp.where((lane == e1) | (lane == e2), 1.0, 0.0)
    r = lax.broadcasted_iota(jnp.int32, (tm, tm), 0)
    c = lax.broadcasted_iota(jnp.int32, (tm, tm), 1)
    strict = jnp.where(r > c, 1.0, 0.0).astype(BF16)
    before = jnp.dot(strict, onehot.astype(BF16), preferred_element_type=F32) + carry[0:1, :]
    rank1 = jnp.sum(jnp.where(lane == e1, before, 0.0), axis=1, keepdims=True)
    rank2 = jnp.sum(jnp.where(lane == e2, before, 0.0), axis=1, keepdims=True)
    total = carry[0:1, :] + jnp.sum(onehot, axis=0, keepdims=True)
    carry[...] = jnp.broadcast_to(total, carry.shape)
    cnt_ref[...] = jnp.broadcast_to(total, cnt_ref.shape)

    out = jnp.where(lane == 0, w1, 0.0)
    out = jnp.where(lane == 1, w2, out)
    out = jnp.where(lane == 2, e1.astype(F32), out)
    out = jnp.where(lane == 3, e2.astype(F32), out)
    out = jnp.where(lane == 4, rank1, out)
    out = jnp.where(lane == 5, rank2, out)
    route_ref[...] = out


def moe_route(x, gain, w_grp, b_grp, w_exp, b_exp, tm=512):
    T, D = x.shape
    n_pad = LANES - N_GROUPS - N_EXPERTS
    wr = jnp.concatenate([w_grp, w_exp, jnp.zeros((D, n_pad), F32)], axis=1)
    br = jnp.concatenate([b_grp, b_exp, jnp.zeros((n_pad,), F32)]).reshape(1, LANES)
    return pl.pallas_call(
        functools.partial(_route_kernel, tm=tm),
        grid=(T // tm,),
        in_specs=[pl.BlockSpec((tm, D), lambda i: (i, 0)),
                  pl.BlockSpec((1, D), lambda i: (0, 0)),
                  pl.BlockSpec((D, LANES), lambda i: (0, 0)),
                  pl.BlockSpec((1, LANES), lambda i: (0, 0))],
        out_specs=(pl.BlockSpec((tm, D), lambda i: (i, 0)),
                   pl.BlockSpec((tm, LANES), lambda i: (i, 0)),
                   pl.BlockSpec((8, LANES), lambda i: (0, 0))),
        out_shape=(jax.ShapeDtypeStruct((T, D), F32),
                   jax.ShapeDtypeStruct((T, LANES), F32),
                   jax.ShapeDtypeStruct((8, LANES), F32)),
        scratch_shapes=[pltpu.VMEM((8, LANES), F32)],
        compiler_params=_cparams(("arbitrary",)),
        name="moe_route",
    )(x, gain.reshape(1, D), wr, br)


def _row_dma(src, src_row, dst, dst_row, sem):
    return pltpu.make_async_copy(src.at[pl.ds(src_row, 1), :], dst.at[pl.ds(dst_row, 1), :], sem)


def _rows_wait(src, dst, n, sem):
    pltpu.make_async_copy(src.at[pl.ds(0, n), :], dst.at[pl.ds(0, n), :], sem).wait()


def _expert_kernel(tile_expert, n_used, rcur_ref, rnext_ref, h_hbm, wgu_ref, wd_ref, o_ref,
                   xbuf, wgu_bf, wd_bf, sem):
    i = pl.program_id(0)
    used = i < n_used[0]
    slot = i % 2

    def fetch(rref, s):
        def body(r, c):
            _row_dma(h_hbm, rref[0, 0, r], xbuf.at[s], r, sem.at[s]).start()
            return c

        lax.fori_loop(0, MOE_TILE, body, 0, unroll=8)

    @pl.when(i == 0)
    def _():
        fetch(rcur_ref, 0)

    @pl.when(i + 1 < n_used[0])
    def _():
        fetch(rnext_ref, 1 - slot)

    @pl.when(used)
    def _():
        prev = tile_expert[jnp.maximum(i - 1, 0)]

        @pl.when((i == 0) | (tile_expert[i] != prev))
        def _():
            wgu_bf[...] = wgu_ref[0, 0].astype(BF16)
            wd_bf[...] = wd_ref[0, 0].astype(BF16)

        _rows_wait(h_hbm, xbuf.at[slot], MOE_TILE, sem.at[slot])
        x = xbuf[slot].astype(BF16)
        gu = jnp.dot(x, wgu_bf[...], preferred_element_type=F32)
        gate = gu[:, 0:D_FF_EXPERT]
        up = gu[:, D_FF_EXPERT:2 * D_FF_EXPERT]
        act = gate * _sigmoid(gate) * up
        o_ref[...] = jnp.dot(act.astype(BF16), wd_bf[...], preferred_element_type=F32)

    @pl.when(jnp.logical_not(used))
    def _():
        o_ref[...] = jnp.zeros(o_ref.shape, F32)


def moe_experts(h, row_token, tile_expert, n_used, w_gate_up, w_down, layer):
    T, D = h.shape
    n_tiles = row_token.shape[0] // MOE_TILE
    rows = row_token.reshape(n_tiles, 1, MOE_TILE)
    grid_spec = pltpu.PrefetchScalarGridSpec(
        num_scalar_prefetch=2,
        grid=(n_tiles,),
        in_specs=[
            pl.BlockSpec((1, 1, MOE_TILE), lambda i, te, nu: (i, 0, 0), memory_space=pltpu.SMEM),
            pl.BlockSpec((1, 1, MOE_TILE), lambda i, te, nu: (jnp.minimum(i + 1, n_tiles - 1), 0, 0),
                         memory_space=pltpu.SMEM),
            pl.BlockSpec(memory_space=pl.ANY),
            pl.BlockSpec((1, 1, D, 2 * D_FF_EXPERT), lambda i, te, nu: (layer, te[i], 0, 0)),
            pl.BlockSpec((1, 1, D_FF_EXPERT, D), lambda i, te, nu: (layer, te[i], 0, 0)),
        ],
        out_specs=pl.BlockSpec((MOE_TILE, D), lambda i, te, nu: (i, 0)),
        scratch_shapes=[
            pltpu.VMEM((2, MOE_TILE, D), F32),
            pltpu.VMEM((D, 2 * D_FF_EXPERT), BF16),
            pltpu.VMEM((D_FF_EXPERT, D), BF16),
            pltpu.SemaphoreType.DMA((2,)),
        ],
    )
    return pl.pallas_call(
        _expert_kernel,
        grid_spec=grid_spec,
        out_shape=jax.ShapeDtypeStruct((n_tiles * MOE_TILE, D), F32),
        compiler_params=_cparams(("arbitrary",)),
        name="moe_experts",
    )(tile_expert, n_used, rows, rows, h, w_gate_up, w_down)


def _combine_kernel(dcur_ref, dnext_ref, x_ref, route_ref, y_hbm, o_ref, buf, sem, *, tm):
    i = pl.program_id(0)
    n = pl.num_programs(0)
    slot = i % 2

    def fetch(dref, s):
        def body(r, c):
            _row_dma(y_hbm, dref[0, 0, r], buf.at[s, 0], r, sem.at[s]).start()
            _row_dma(y_hbm, dref[0, 0, tm + r], buf.at[s, 1], r, sem.at[s]).start()
            return c

        lax.fori_loop(0, tm, body, 0, unroll=8)

    @pl.when(i == 0)
    def _():
        fetch(dcur_ref, 0)

    @pl.when(i + 1 < n)
    def _():
        fetch(dnext_ref, 1 - slot)

    _rows_wait(y_hbm, buf.at[slot, 0], tm, sem.at[slot])
    _rows_wait(y_hbm, buf.at[slot, 1], tm, sem.at[slot])
    route = route_ref[...]
    lane = lax.broadcasted_iota(jnp.int32, route.shape, 1)
    w1 = jnp.sum(jnp.where(lane == 0, route, 0.0), axis=1, keepdims=True)
    w2 = jnp.sum(jnp.where(lane == 1, route, 0.0), axis=1, keepdims=True)
    o_ref[...] = x_ref[...] + (w1 * buf[slot, 0] + w2 * buf[slot, 1])


def moe_combine(x, ys, route, dest, tm=256):
    T, D = x.shape
    n = T // tm
    dest_tiles = dest.reshape(2, n, tm).transpose(1, 0, 2).reshape(n, 1, 2 * tm)
    return pl.pallas_call(
        functools.partial(_combine_kernel, tm=tm),
        grid=(n,),
        in_specs=[pl.BlockSpec((1, 1, 2 * tm), lambda i: (i, 0, 0), memory_space=pltpu.SMEM),
                  pl.BlockSpec((1, 1, 2 * tm), lambda i: (jnp.minimum(i + 1, n - 1), 0, 0), memory_space=pltpu.SMEM),
                  pl.BlockSpec((tm, D), lambda i: (i, 0)),
                  pl.BlockSpec((tm, LANES), lambda i: (i, 0)),
                  pl.BlockSpec(memory_space=pl.ANY)],
        out_specs=pl.BlockSpec((tm, D), lambda i: (i, 0)),
        out_shape=jax.ShapeDtypeStruct((T, D), F32),
        scratch_shapes=[pltpu.VMEM((2, 2, tm, D), F32), pltpu.SemaphoreType.DMA((2,))],
        compiler_params=_cparams(("arbitrary",)),
        name="moe_combine",
    )(dest_tiles, dest_tiles, x, route, ys)


def hierarchical_moe(x, gain, w_grp, b_grp, w_exp, b_exp, w_gate_up, w_down, layer):
    T, D = x.shape
    h, route, cnt = moe_route(x, gain, w_grp, b_grp, w_exp, b_exp)
    route_t = route[:, 0:8].T
    expert = route_t[2:4].astype(jnp.int32)
    rank = route_t[4:6].astype(jnp.int32)
    counts = cnt[0, 0:N_EXPERTS].astype(jnp.int32)
    padded = ((counts + MOE_TILE - 1) // MOE_TILE) * MOE_TILE
    ends = jnp.cumsum(padded)
    offs = ends - padded
    e_ids = jnp.arange(N_EXPERTS, dtype=jnp.int32)[:, None, None]
    dest = rank + jnp.sum(jnp.where(expert[None] == e_ids, offs[:, None, None], 0), axis=0)
    n_rows = 2 * T + N_EXPERTS * MOE_TILE
    n_tiles = n_rows // MOE_TILE
    n_used = (ends[-1] // MOE_TILE).astype(jnp.int32)
    tile_start = jnp.arange(n_tiles, dtype=jnp.int32) * MOE_TILE
    tile_expert = jnp.sum((ends[None, :] <= tile_start[:, None]).astype(jnp.int32), axis=1)
    last_expert = tile_expert[jnp.maximum(n_used - 1, 0)]
    tile_expert = jnp.where(tile_start < ends[-1], tile_expert, last_expert)
    row_token = jnp.zeros((n_rows,), jnp.int32).at[dest.reshape(-1)].set(
        jnp.tile(jnp.arange(T, dtype=jnp.int32), 2))
    ys = moe_experts(h, row_token, tile_expert, n_used.reshape(1), w_gate_up, w_down, layer)
    return moe_combine(x, ys, route, dest)


def _in_proj_kernel(a_ref, wt_hbm, o_ref, wbuf, sem, *, layer, row0, tn, sigmoid_out):
    i, j = pl.program_id(0), pl.program_id(1)
    ni, nj = pl.num_programs(0), pl.num_programs(1)
    step = i * nj + j
    slot = step % 2

    def fetch(jj, s):
        start = pl.multiple_of(row0 + jj * tn, SUBLANES)
        return pltpu.make_async_copy(wt_hbm.at[layer, pl.ds(start, tn), :], wbuf.at[s], sem.at[s])

    @pl.when(step == 0)
    def _():
        fetch(0, 0).start()

    @pl.when(step + 1 < ni * nj)
    def _():
        fetch((j + 1) % nj, 1 - slot).start()

    fetch(j, slot).wait()
    w = wbuf[slot].astype(BF16)
    acc = lax.dot_general(a_ref[...], w, NT, preferred_element_type=F32)
    if sigmoid_out:
        acc = _sigmoid(acc)
    o_ref[...] = acc.astype(o_ref.dtype)


def in_proj(h, wt, layer, row0, n, tn, name, tm=2048, sigmoid_out=False):
    T, K = h.shape
    assert n % tn == 0 and row0 % SUBLANES == 0 and tn % SUBLANES == 0
    return pl.pallas_call(
        functools.partial(_in_proj_kernel, layer=layer, row0=row0, tn=tn, sigmoid_out=sigmoid_out),
        grid=(T // tm, n // tn),
        in_specs=[pl.BlockSpec((tm, K), lambda i, j: (i, 0)), pl.BlockSpec(memory_space=pl.ANY)],
        out_specs=pl.BlockSpec((tm, tn), lambda i, j: (i, j)),
        out_shape=jax.ShapeDtypeStruct((T, n), BF16 if sigmoid_out else F32),
        scratch_shapes=[pltpu.VMEM((2, tn, K), F32), pltpu.SemaphoreType.DMA((2,))],
        compiler_params=_cparams(("arbitrary", "arbitrary")),
        name=name,
    )(h, wt)


def kernel(x, norm_mix, w_in, cmp_pe, cmp_w1, cmp_w2, q_norm, k_norm, hg_lb_logits, hg_norm, gla_w_a2, gla_b_a,
           gla_norm, w_branch, w_out, norm_ffn, w_grp, b_grp, w_exp, b_exp, w_gate_up, w_down):
    B, T, D = x.shape
    assert B == 1 and D == D_MODEL
    xt = x[0]
    p_lb = jax.nn.softmax(hg_lb_logits.astype(F32), axis=0)
    lower_bounds = jnp.cumsum(p_lb, axis=0) - p_lb[0]
    wt = jnp.swapaxes(w_in, 1, 2)
    for l in range(DEPTH):
        h = rmsnorm_bf16(xt, norm_mix[l])
        proj_n = in_proj(h, wt, l, W_NSA[0], W_NSA[1], 512, "in_proj_nsa")
        proj_r = in_proj(h, wt, l, W_REC[0], W_REC[1], 512, "in_proj_rec")
        proj_a = in_proj(h, wt, l, W_GLA_A_BLOCK[0], W_GLA_A_BLOCK[1], LANES, "in_proj_gla_a")
        proj_m = in_proj(h, wt, l, W_MERGE[0], W_MERGE[1], 512, "in_proj_merge")
        o_cmp, o_slc, o_win = nsa_mixer(proj_n, cmp_pe[l], cmp_w1[l], cmp_w2[l], q_norm[l], k_norm[l])
        o_b = hgrn2_mixer(proj_r, lower_bounds[l], hg_norm[l])
        o_c = gla_mixer(proj_r, proj_a, gla_w_a2[l], gla_b_a[l], gla_norm[l])
        merged = merge_branches(proj_m, o_cmp, o_slc, o_win, o_b, o_c, w_branch[l].astype(BF16))
        xt = matmul_bf16(merged, w_out[l].astype(BF16), res=xt, tm=1024, tn=512, name="out_proj")
        xt = hierarchical_moe(xt, norm_ffn[l], w_grp[l], b_grp[l], w_exp[l], b_exp[l], w_gate_up, w_down, l)
    return xt[None]
```

```python
import functools
import math

import numpy as np
import jax
import jax.numpy as jnp
from jax import lax
from jax.experimental import pallas as pl
from jax.experimental.pallas import tpu as pltpu

F32 = jnp.float32
BF16 = jnp.bfloat16
HIGHEST = lax.Precision.HIGHEST

D_MODEL = 2048
DEPTH = 2
BRANCH_WIDTH = D_MODEL // 4
NSA_HEADS = 8
NSA_KV_HEADS = 2
NSA_REP = NSA_HEADS // NSA_KV_HEADS
NSA_HD = 64
CMP_LEN = 32
CMP_STRIDE = 16
SLC_LEN = 64
SLC_TOP = 16
WINDOW = 512
HG_HEADS = 4
GLA_HEADS = 4
GLA_DK = 64
GLA_RANK = 16
GLA_TAU = 16.0
CHUNK = 64
SUB = 16
N_GROUPS = 4
EXPERTS_PER_GROUP = 8
N_EXPERTS = N_GROUPS * EXPERTS_PER_GROUP
D_FF_EXPERT = D_MODEL // 4
EPS = 1e-6
NEG = -1e30
FORCE = 1e4
TINY = 1e-30

LANES = 128
SUBLANES = 8
VMEM_LIMIT = 56 * 1024 * 1024

W_NSA = (0, 1536)
W_REC = (1304, 3584)
W_MERGE = (4904, 6144)
W_GLA_A_BLOCK = (4864, 128)
C_NSA_KV = 512
C_NSA_GATE = 1280
C_GLA_QK = 2048
C_GLA_VR = 2560
TAIL_GLA_A0 = 4888 - W_GLA_A_BLOCK[0]

NT = (((1,), (1,)), ((), ()))
TN = (((0,), (0,)), ((), ()))


def _cparams(sem):
    return pltpu.CompilerParams(dimension_semantics=sem, vmem_limit_bytes=VMEM_LIMIT)


def _sigmoid(x):
    return 1.0 / (1.0 + jnp.exp(-x))


def _norm_kernel(x_ref, g_ref, o_ref):
    x = x_ref[...]
    ms = jnp.mean(x * x, axis=-1, keepdims=True)
    o_ref[...] = (x * lax.rsqrt(ms + EPS) * g_ref[...]).astype(o_ref.dtype)


def rmsnorm_bf16(x, gain, tm=512):
    T, D = x.shape
    return pl.pallas_call(
        _norm_kernel,
        grid=(T // tm,),
        in_specs=[pl.BlockSpec((tm, D), lambda i: (i, 0)), pl.BlockSpec((1, D), lambda i: (0, 0))],
        out_specs=pl.BlockSpec((tm, D), lambda i: (i, 0)),
        out_shape=jax.ShapeDtypeStruct((T, D), BF16),
        compiler_params=_cparams(("parallel",)),
        name="rmsnorm_bf16",
    )(x, gain.reshape(1, D))


def _mm_kernel(a_ref, b_ref, o_ref):
    o_ref[...] = jnp.dot(a_ref[...], b_ref[...], preferred_element_type=F32)


def _mm_res_kernel(a_ref, b_ref, r_ref, o_ref):
    o_ref[...] = r_ref[...] + jnp.dot(a_ref[...], b_ref[...], preferred_element_type=F32)


def matmul_bf16(a, b, res=None, tm=1024, tn=512, name="matmul_bf16"):
    T, K = a.shape
    N = b.shape[1]
    in_specs = [pl.BlockSpec((tm, K), lambda i, j: (i, 0)), pl.BlockSpec((K, tn), lambda i, j: (0, j))]
    args = [a, b]
    kern = _mm_kernel
    if res is not None:
        in_specs.append(pl.BlockSpec((tm, tn), lambda i, j: (i, j)))
        args.append(res)
        kern = _mm_res_kernel
    return pl.pallas_call(
        kern,
        grid=(T // tm, N // tn),
        in_specs=in_specs,
        out_specs=pl.BlockSpec((tm, tn), lambda i, j: (i, j)),
        out_shape=jax.ShapeDtypeStruct((T, N), F32),
        compiler_params=_cparams(("parallel", "arbitrary")),
        name=name,
    )(*args)


def _half_rmsnorm(x, gain2, lo):
    x2 = x * x
    s_lo = jnp.sum(jnp.where(lo, x2, 0.0), axis=1, keepdims=True)
    s_hi = jnp.sum(jnp.where(lo, 0.0, x2), axis=1, keepdims=True)
    ms = jnp.where(lo, s_lo, s_hi) * (1.0 / NSA_HD)
    return x * lax.rsqrt(ms + EPS) * gain2


def _nsa_prep_kernel(q_ref, kvc_ref, kvs_ref, kvw_ref, tail_ref, qg_ref, kg_ref,
                     qn_ref, kaug_ref, vs_ref, kw_ref, vw_ref, gate_ref, *, tm):
    i = pl.program_id(0)
    lane = lax.broadcasted_iota(jnp.int32, (tm, LANES), 1)
    lo = lane < NSA_HD
    qg = qg_ref[...]
    for c in range(4):
        x = q_ref[:, c * LANES:(c + 1) * LANES]
        qn_ref[:, c * LANES:(c + 1) * LANES] = (_half_rmsnorm(x, qg, lo) * (NSA_HD ** -0.5)).astype(BF16)

    def dup(kn):
        rolled = pltpu.roll(kn, NSA_HD, axis=1)
        return jnp.where(lo, kn, rolled), jnp.where(lo, rolled, kn)

    vt_zero = jnp.zeros((NSA_HD, tm), BF16)

    def store_vt(v_pair, ref):
        vt = jnp.transpose(v_pair)
        for g in range(NSA_KV_HEADS):
            head = vt[g * NSA_HD:(g + 1) * NSA_HD, :].astype(BF16)
            ref[g, 0:NSA_HD, :] = head
            ref[g, NSA_HD:LANES, :] = vt_zero
            ref[g, LANES:LANES + NSA_HD, :] = vt_zero
            ref[g, LANES + NSA_HD:2 * LANES, :] = head

    row = i * tm + lax.broadcasted_iota(jnp.int32, (tm, LANES), 0)
    onehot = jnp.where(row // SLC_LEN == lane, 1.0, 0.0).astype(BF16)
    k_extra = jnp.where(lane == NSA_HD, (row % SLC_LEN).astype(F32), 0.0)
    ks = dup(_half_rmsnorm(kvs_ref[:, 0:LANES], kg_ref[1:2, :], lo))
    store_vt(kvs_ref[:, LANES:2 * LANES], vs_ref)
    kw = dup(_half_rmsnorm(kvw_ref[:, 0:LANES], kg_ref[2:3, :], lo))
    store_vt(kvw_ref[:, LANES:2 * LANES], vw_ref)
    for g in range(NSA_KV_HEADS):
        kaug_ref[g, :, 0:LANES] = onehot
        kaug_ref[g, :, LANES:2 * LANES] = jnp.where(lo, ks[g], k_extra).astype(BF16)
        kw_ref[g] = kw[g].astype(BF16)
    gate_ref[...] = _sigmoid(tail_ref[...])


def nsa_prep(proj_n, q_gain, k_gain, tm=512):
    T = proj_n.shape[0]
    qg2 = jnp.tile(q_gain.reshape(1, NSA_HD), (1, 2))
    kg2 = jnp.tile(k_gain.reshape(3, NSA_HD), (1, 2))
    G = NSA_KV_HEADS
    out_shape = (
        jax.ShapeDtypeStruct((T, 512), BF16),
        jax.ShapeDtypeStruct((G, T, 256), BF16),
        jax.ShapeDtypeStruct((G, 256, T), BF16),
        jax.ShapeDtypeStruct((G, T, 128), BF16),
        jax.ShapeDtypeStruct((G, 256, T), BF16),
        jax.ShapeDtypeStruct((T, 128), F32),
    )
    return pl.pallas_call(
        functools.partial(_nsa_prep_kernel, tm=tm),
        grid=(T // tm,),
        in_specs=[
            pl.BlockSpec((tm, 512), lambda i: (i, 0)),
            pl.BlockSpec((tm, 256), lambda i: (i, C_NSA_KV // 256)),
            pl.BlockSpec((tm, 256), lambda i: (i, C_NSA_KV // 256 + 1)),
            pl.BlockSpec((tm, 256), lambda i: (i, C_NSA_KV // 256 + 2)),
            pl.BlockSpec((tm, 128), lambda i: (i, C_NSA_GATE // 128)),
            pl.BlockSpec((1, 128), lambda i: (0, 0)),
            pl.BlockSpec((3, 128), lambda i: (0, 0)),
        ],
        out_specs=(
            pl.BlockSpec((tm, 512), lambda i: (i, 0)),
            pl.BlockSpec((G, tm, 256), lambda i: (0, i, 0)),
            pl.BlockSpec((G, 256, tm), lambda i: (0, 0, i)),
            pl.BlockSpec((G, tm, 128), lambda i: (0, i, 0)),
            pl.BlockSpec((G, 256, tm), lambda i: (0, 0, i)),
            pl.BlockSpec((tm, 128), lambda i: (i, 0)),
        ),
        out_shape=out_shape,
        compiler_params=_cparams(("parallel",)),
        name="nsa_prep",
    )(proj_n, proj_n, proj_n, proj_n, proj_n, qg2, kg2)


def _gelu_tanh(x):
    c = math.sqrt(2.0 / math.pi)
    return 0.5 * x * (1.0 + jnp.tanh(c * (x + 0.044715 * (x * x * x))))


def _compress_kernel(a_ref, pe_ref, w1_ref, w2_ref, kg_ref, o_ref, *, nb):
    kind = pl.program_id(0)
    a_lo = a_ref[0, 0, 0:nb, :]
    a_hi = a_ref[0, 0, 1:nb + 1, :]
    blocks = jnp.concatenate([a_lo, a_hi], axis=1) + pe_ref[0]
    h1 = jnp.dot(blocks.astype(BF16), w1_ref[0].astype(BF16), preferred_element_type=F32)
    y = jnp.dot(_gelu_tanh(h1).astype(BF16), w2_ref[0].astype(BF16), preferred_element_type=F32)
    ms = jnp.mean(y * y, axis=-1, keepdims=True)
    yn = y * lax.rsqrt(ms + EPS) * kg_ref[...]
    o_ref[0, 0] = jnp.where(kind == 0, yn, y)


def nsa_compress(kv_cmp, cmp_pe, cmp_w1, cmp_w2, k_gain0):
    T = kv_cmp.shape[0]
    nb = T // CMP_STRIDE
    G = NSA_KV_HEADS
    a = kv_cmp.reshape(T, 2, G, NSA_HD).transpose(1, 2, 0, 3).reshape(2, G, nb, CMP_STRIDE * NSA_HD)
    a = jnp.pad(a, ((0, 0), (0, 0), (0, 8), (0, 0)))
    pe = cmp_pe.reshape(2, 1, CMP_LEN * NSA_HD)
    return pl.pallas_call(
        functools.partial(_compress_kernel, nb=nb),
        grid=(2, G),
        in_specs=[
            pl.BlockSpec((1, 1, nb + 8, CMP_STRIDE * NSA_HD), lambda k, g: (k, g, 0, 0)),
            pl.BlockSpec((1, 1, CMP_LEN * NSA_HD), lambda k, g: (k, 0, 0)),
            pl.BlockSpec((1, CMP_LEN * NSA_HD, NSA_HD), lambda k, g: (k, 0, 0)),
            pl.BlockSpec((1, NSA_HD, NSA_HD), lambda k, g: (k, 0, 0)),
            pl.BlockSpec((1, NSA_HD), lambda k, g: (0, 0)),
        ],
        out_specs=pl.BlockSpec((1, 1, nb, NSA_HD), lambda k, g: (k, g, 0, 0)),
        out_shape=jax.ShapeDtypeStruct((2, G, nb, NSA_HD), F32),
        compiler_params=_cparams(("arbitrary", "arbitrary")),
        name="nsa_compress",
    )(a, pe, cmp_w1, cmp_w2, k_gain0.reshape(1, NSA_HD))


def _slope(g, r):
    return jnp.where(g == 0, 2.0 ** -(r + 1), 2.0 ** -(NSA_REP + r + 1)).astype(F32)


def _gate_pair(gates, lane, lo, col_even):
    ge = jnp.sum(jnp.where(lane == col_even, gates, 0.0), axis=1, keepdims=True)
    go = jnp.sum(jnp.where(lane == col_even + 1, gates, 0.0), axis=1, keepdims=True)
    return jnp.where(lo, ge, go)


def _cmp_sel_kernel(q_ref, kc_ref, vc_ref, gate_ref, mt_ref, o_ref, sel_ref, *, tq, nb, ns_pad):
    g = pl.program_id(0)
    qi = pl.program_id(1)
    t0 = qi * tq
    lane = lax.broadcasted_iota(jnp.int32, (tq, LANES), 1)
    lo = lane < NSA_HD
    n_idx = lax.broadcasted_iota(jnp.int32, (nb, tq), 0)
    t_idx = t0 + lax.broadcasted_iota(jnp.int32, (nb, tq), 1)
    dist = (t_idx - (n_idx * CMP_STRIDE + (CMP_LEN - 1))).astype(F32)
    vis = dist >= 0.0
    kc = kc_ref[0]
    imp = jnp.zeros((nb, tq), F32)
    gates = gate_ref[...]
    for a in range(2):
        acc = jnp.zeros((tq, LANES), F32)
        for half in range(2):
            r = 2 * a + half
            qh = q_ref[:, a * LANES:(a + 1) * LANES]
            qh = jnp.where(lo if half == 0 else jnp.logical_not(lo), qh, jnp.zeros_like(qh))
            s = lax.dot_general(kc, qh, NT, preferred_element_type=F32)
            s = jnp.where(vis, s - _slope(g, r) * dist, NEG)
            mx = jnp.max(s, axis=0, keepdims=True)
            e = jnp.where(vis, jnp.exp(s - mx), 0.0)
            den = jnp.sum(e, axis=0, keepdims=True)
            p = e * jnp.where(den > 0.0, 1.0 / den, 0.0)
            imp = imp + p
            v = vc_ref[0, :, half * LANES:(half + 1) * LANES]
            acc = acc + lax.dot_general(p.astype(BF16), v, TN, preferred_element_type=F32)
        gp = _gate_pair(gates, lane, lo, g * NSA_REP + 2 * a)
        o_ref[:, a * LANES:(a + 1) * LANES] = acc * gp

    score_all = jnp.dot(mt_ref[...], imp, precision=HIGHEST, preferred_element_type=F32)
    blk = lax.broadcasted_iota(jnp.int32, (ns_pad, LANES), 0)
    for c in range(tq // LANES):
        cur = (t0 + c * LANES + lax.broadcasted_iota(jnp.int32, (ns_pad, LANES), 1)) // SLC_LEN
        forced = (blk == 0) | (blk == cur) | (blk == cur - 1)
        score = jnp.where(forced, FORCE, score_all[:, c * LANES:(c + 1) * LANES])
        score = jnp.where(blk <= cur, score, -1.0)
        for _ in range(SLC_TOP):
            mx = jnp.max(score, axis=0, keepdims=True)
            idx = jnp.min(jnp.where(score == mx, blk, ns_pad), axis=0, keepdims=True)
            score = jnp.where(blk == idx, -jnp.inf, score)
        bias_t = jnp.where((score == -jnp.inf) & (blk <= cur), 0.0, NEG)
        sel_ref[0, c * LANES:(c + 1) * LANES, :] = jnp.transpose(bias_t).astype(BF16)


def _score_matrix(nb, ns_pad):
    ratio, span = SLC_LEN // CMP_STRIDE, CMP_LEN // CMP_STRIDE
    n_cmp = nb - 1
    m = np.zeros((ns_pad, nb), np.float32)
    for s in range(nb // ratio):
        for mm in range(ratio):
            for nn in range(span):
                c = ratio * s + mm - nn
                if 0 <= c < n_cmp:
                    m[s, c] += 1.0
    return jnp.asarray(m)


def nsa_cmp_select(qn, kc2, vc_lohi, gates, tq=256):
    T = qn.shape[0]
    nb = T // CMP_STRIDE
    ns_pad = LANES
    G = NSA_KV_HEADS
    return pl.pallas_call(
        functools.partial(_cmp_sel_kernel, tq=tq, nb=nb, ns_pad=ns_pad),
        grid=(G, T // tq),
        in_specs=[
            pl.BlockSpec((tq, 256), lambda g, i: (i, g)),
            pl.BlockSpec((1, nb, 128), lambda g, i: (g, 0, 0)),
            pl.BlockSpec((1, nb, 256), lambda g, i: (g, 0, 0)),
            pl.BlockSpec((tq, 128), lambda g, i: (i, 0)),
            pl.BlockSpec((ns_pad, nb), lambda g, i: (0, 0)),
        ],
        out_specs=(
            pl.BlockSpec((tq, 256), lambda g, i: (i, g)),
            pl.BlockSpec((1, tq, ns_pad), lambda g, i: (g, i, 0)),
        ),
        out_shape=(jax.ShapeDtypeStruct((T, 512), F32), jax.ShapeDtypeStruct((G, T, ns_pad), BF16)),
        compiler_params=_cparams(("parallel", "parallel")),
        name="nsa_cmp_select",
    )(qn, kc2, vc_lohi, gates, _score_matrix(nb, ns_pad))


KV_SLOTS = 3


def _sel_attn_kernel(tile_tab, cnt_tab, q_ref, sb_ref, gate_ref, k_hbm, v_hbm, o_ref,
                     qaug, kbuf, vbuf, sem, m_scr, l_scr, acc_scr, *, tq, tk, max_tiles):
    g = pl.program_id(0)
    qi = pl.program_id(1)
    row = g * pl.num_programs(1) + qi
    n_tiles = cnt_tab[row]
    base = row * max_tiles
    lane = lax.broadcasted_iota(jnp.int32, (tq, LANES), 1)
    lo = lane < NSA_HD

    def copies(j, slot):
        start = pl.multiple_of(tile_tab[base + j] * tk, tk)
        return (pltpu.make_async_copy(k_hbm.at[g, pl.ds(start, tk), :], kbuf.at[slot], sem.at[0, slot]),
                pltpu.make_async_copy(v_hbm.at[g, :, pl.ds(start, tk)], vbuf.at[slot], sem.at[1, slot]))

    def start_fetch(j, slot):
        for cp in copies(j, slot):
            cp.start()

    def wait_fetch(j, slot):
        for cp in copies(j, slot):
            cp.wait()

    start_fetch(0, 0)

    @pl.when(n_tiles > 1)
    def _():
        start_fetch(1, 1)

    sb = sb_ref[0].astype(F32)
    blk_rel = (lane - qi * (tq // SLC_LEN)).astype(F32)
    for r in range(NSA_REP):
        a, half = r // 2, r % 2
        slope = _slope(g, r)
        qh = q_ref[:, a * LANES:(a + 1) * LANES].astype(F32)
        if half:
            qh = pltpu.roll(qh, NSA_HD, axis=1)
        qh = jnp.where(lo, qh, jnp.where(lane == NSA_HD, slope, 0.0))
        qaug[r * tq:(r + 1) * tq, 0:LANES] = (sb + (slope * SLC_LEN) * blk_rel).astype(BF16)
        qaug[r * tq:(r + 1) * tq, LANES:2 * LANES] = qh.astype(BF16)
    m_scr[...] = jnp.full(m_scr.shape, NEG, F32)
    l_scr[...] = jnp.zeros(l_scr.shape, F32)
    acc_scr[...] = jnp.zeros(acc_scr.shape, F32)

    upper = lax.broadcasted_iota(jnp.int32, (LANES, tq), 0) < NSA_HD
    kj_diag = (qi * tq + (tq - 1)) // tk

    def step(masked, slot):
        if masked:
            kpos = kj_diag * tk + lax.broadcasted_iota(jnp.int32, (tk, tq), 0)
            qpos = qi * tq + lax.broadcasted_iota(jnp.int32, (tk, tq), 1)
            causal = kpos <= qpos
        k = kbuf[slot]
        for a in range(2):
            pv = jnp.zeros((LANES, tq), F32)
            alphas = []
            for half in range(2):
                r = 2 * a + half
                sr = lax.dot_general(k, qaug[r * tq:(r + 1) * tq, :], NT, preferred_element_type=F32)
                if masked:
                    sr = jnp.where(causal, sr, NEG)
                m_prev = m_scr[r:r + 1, :]
                m_new = jnp.maximum(m_prev, jnp.max(sr, axis=0, keepdims=True))
                alpha = jnp.exp(m_prev - m_new)
                p = jnp.exp(sr - m_new)
                l_scr[r:r + 1, :] = alpha * l_scr[r:r + 1, :] + jnp.sum(p, axis=0, keepdims=True)
                m_scr[r:r + 1, :] = m_new
                vt = vbuf[slot, half * LANES:(half + 1) * LANES, :]
                pv = pv + jnp.dot(vt, p.astype(BF16), preferred_element_type=F32)
                alphas.append(alpha)
            alpha_pair = jnp.where(upper, alphas[0], alphas[1])
            acc_scr[a * LANES:(a + 1) * LANES, :] = alpha_pair * acc_scr[a * LANES:(a + 1) * LANES, :] + pv

    def off_diagonal(j, carry):
        @pl.when(j + 2 < n_tiles)
        def _():
            start_fetch(j + 2, (j + 2) % KV_SLOTS)

        wait_fetch(j, j % KV_SLOTS)
        step(False, j % KV_SLOTS)
        return carry

    lax.fori_loop(0, n_tiles - 1, off_diagonal, 0)
    last = n_tiles - 1
    wait_fetch(last, last % KV_SLOTS)
    step(True, last % KV_SLOTS)

    gates = gate_ref[...]
    for a in range(2):
        l_pair = jnp.where(upper, l_scr[2 * a:2 * a + 1, :], l_scr[2 * a + 1:2 * a + 2, :])
        gp = _gate_pair(gates, lane, lo, NSA_HEADS + g * NSA_REP + 2 * a)
        o_t = acc_scr[a * LANES:(a + 1) * LANES, :] / l_pair
        o_ref[:, a * LANES:(a + 1) * LANES] = jnp.transpose(o_t) * gp


def nsa_selected_attention(qn, selbias, kaug, vs_lohi, gates, tq=256, tk=512):
    T = qn.shape[0]
    G = NSA_KV_HEADS
    nq, nkt, nkb = T // tq, T // tk, tk // SLC_LEN
    chosen = (selbias.astype(F32) > 0.5 * NEG).reshape(G, nq, tq, LANES // nkb, nkb)
    tile_any = jnp.any(chosen, axis=(2, 4))[:, :, 0:nkt]
    kj = jnp.arange(nkt, dtype=jnp.int32)[None, None, :]
    kj_diag = ((jnp.arange(nq, dtype=jnp.int32) * tq + (tq - 1)) // tk)[None, :, None]
    active = (tile_any & (kj < kj_diag)) | (kj == kj_diag)
    csum = jnp.cumsum(active.astype(jnp.int32), axis=2)
    counts = csum[:, :, -1]
    tiles = jnp.sum((csum[:, :, None, :] <= kj[..., None]).astype(jnp.int32), axis=3)
    tiles = jnp.minimum(tiles, nkt - 1)
    grid_spec = pltpu.PrefetchScalarGridSpec(
        num_scalar_prefetch=2,
        grid=(G, nq),
        in_specs=[
            pl.BlockSpec((tq, 256), lambda g, i, tt, ct: (i, g)),
            pl.BlockSpec((1, tq, 128), lambda g, i, tt, ct: (g, i, 0)),
            pl.BlockSpec((tq, 128), lambda g, i, tt, ct: (i, 0)),
            pl.BlockSpec(memory_space=pl.ANY),
            pl.BlockSpec(memory_space=pl.ANY),
        ],
        out_specs=pl.BlockSpec((tq, 256), lambda g, i, tt, ct: (i, g)),
        scratch_shapes=[
            pltpu.VMEM((NSA_REP * tq, 256), BF16),
            pltpu.VMEM((KV_SLOTS, tk, 256), BF16),
            pltpu.VMEM((KV_SLOTS, 256, tk), BF16),
            pltpu.SemaphoreType.DMA((2, KV_SLOTS)),
            pltpu.VMEM((NSA_REP, tq), F32),
            pltpu.VMEM((NSA_REP, tq), F32),
            pltpu.VMEM((2 * LANES, tq), F32),
        ],
    )
    return pl.pallas_call(
        functools.partial(_sel_attn_kernel, tq=tq, tk=tk, max_tiles=nkt),
        grid_spec=grid_spec,
        out_shape=jax.ShapeDtypeStruct((T, 512), F32),
        compiler_params=_cparams(("parallel", "parallel")),
        name="nsa_selected_attention",
    )(tiles.reshape(-1).astype(jnp.int32), counts.reshape(-1).astype(jnp.int32),
      qn, selbias, gates, kaug, vs_lohi)


def _win_attn_kernel(q_ref, k0_ref, k1_ref, k2_ref, v0_ref, v1_ref, v2_ref, gate_ref, o_ref, *, tq):
    g = pl.program_id(0)
    qi = pl.program_id(1)
    lane = lax.broadcasted_iota(jnp.int32, (tq, LANES), 1)
    lo = lane < NSA_HD
    upper = lax.broadcasted_iota(jnp.int32, (LANES, tq), 0) < NSA_HD
    row = lax.broadcasted_iota(jnp.int32, (tq, tq), 0)
    qpos = qi * tq + lax.broadcasted_iota(jnp.int32, (tq, tq), 1)
    k_refs = (k0_ref, k1_ref, k2_ref)
    v_refs = (v0_ref, v1_ref, v2_ref)
    dists, masks = [], []
    for d in range(3):
        kpos = (qi - 2 + d) * tq + row
        dd = qpos - kpos
        dists.append(dd.astype(F32))
        masks.append((dd >= 0) & (dd < WINDOW) & (kpos >= 0))
    gates = gate_ref[...]
    for a in range(2):
        pv = jnp.zeros((LANES, tq), F32)
        ls = []
        for half in range(2):
            r = 2 * a + half
            qh = q_ref[:, a * LANES:(a + 1) * LANES]
            qh = jnp.where(lo if half == 0 else jnp.logical_not(lo), qh, jnp.zeros_like(qh))
            ss = []
            for d in range(3):
                sd = lax.dot_general(k_refs[d][0], qh, NT, preferred_element_type=F32)
                ss.append(jnp.where(masks[d], sd - _slope(g, r) * dists[d], NEG))
            mx = jnp.maximum(jnp.maximum(jnp.max(ss[0], axis=0, keepdims=True), jnp.max(ss[1], axis=0, keepdims=True)),
                             jnp.max(ss[2], axis=0, keepdims=True))
            l = jnp.zeros((1, tq), F32)
            for d in range(3):
                p = jnp.exp(ss[d] - mx)
                l = l + jnp.sum(p, axis=0, keepdims=True)
                vt = v_refs[d][0, half * LANES:(half + 1) * LANES, :]
                pv = pv + jnp.dot(vt, p.astype(BF16), preferred_element_type=F32)
            ls.append(l)
        l_pair = jnp.where(upper, ls[0], ls[1])
        gp = _gate_pair(gates, lane, lo, 2 * NSA_HEADS + g * NSA_REP + 2 * a)
        o_ref[:, a * LANES:(a + 1) * LANES] = jnp.transpose(pv / l_pair) * gp


def nsa_window_attention(qn, kw2, vw_lohi, gates, tq=256):
    T = qn.shape[0]
    G = NSA_KV_HEADS
    assert WINDOW == 2 * tq

    def kspec(d):
        return pl.BlockSpec((1, tq, 128), lambda g, i, d=d: (g, jnp.maximum(i - 2 + d, 0), 0))

    def vspec(d):
        return pl.BlockSpec((1, 256, tq), lambda g, i, d=d: (g, 0, jnp.maximum(i - 2 + d, 0)))

    return pl.pallas_call(
        functools.partial(_win_attn_kernel, tq=tq),
        grid=(G, T // tq),
        in_specs=[pl.BlockSpec((tq, 256), lambda g, i: (i, g)),
                  kspec(0), kspec(1), kspec(2),
                  vspec(0), vspec(1), vspec(2),
                  pl.BlockSpec((tq, 128), lambda g, i: (i, 0))],
        out_specs=pl.BlockSpec((tq, 256), lambda g, i: (i, g)),
        out_shape=jax.ShapeDtypeStruct((T, 512), F32),
        compiler_params=_cparams(("parallel", "parallel")),
        name="nsa_window_attention",
    )(qn, kw2, kw2, kw2, vw_lohi, vw_lohi, vw_lohi, gates)


def nsa_mixer(proj_n, cmp_pe, cmp_w1, cmp_w2, q_gain, k_gain, parts=False):
    qn, kaug, vs_lohi, kw2, vw_lohi, gates = nsa_prep(proj_n, q_gain, k_gain)
    kvc = nsa_compress(proj_n[:, C_NSA_KV:C_NSA_KV + 256], cmp_pe, cmp_w1, cmp_w2, k_gain[0])
    kc, vc = kvc[0], kvc[1]
    kc2 = jnp.concatenate([kc, kc], axis=-1).astype(BF16)
    zero = jnp.zeros_like(vc)
    vc_lohi = jnp.concatenate([vc, zero, zero, vc], axis=-1).astype(BF16)
    o_cmp, selbias = nsa_cmp_select(qn, kc2, vc_lohi, gates)
    o_slc = nsa_selected_attention(qn, selbias, kaug, vs_lohi, gates)
    o_win = nsa_window_attention(qn, kw2, vw_lohi, gates)
    if parts:
        return o_cmp + o_slc + o_win, (o_cmp, o_slc, o_win)
    return o_cmp, o_slc, o_win


def _diag_selector():
    m = np.zeros((SUB * LANES, LANES), np.float32)
    for j in range(SUB):
        for rep in range(CHUNK // SUB):
            m[j * LANES:(j + 1) * LANES, rep * SUB + j] = 1.0
    return jnp.asarray(m, dtype=BF16)


LOG2E = 1.0 / math.log(2.0)


def _recur_tile(q_scr, k_scr, v_scr, g_scr, o_scr, st_scr, sel_ref, tb):
    row = lax.broadcasted_iota(jnp.int32, (CHUNK, LANES), 0)
    lane = lax.broadcasted_iota(jnp.int32, (CHUNK, LANES), 1)
    sub_row = row % SUB
    blockdiag = (row // SUB) == (lane // SUB)
    r64 = lax.broadcasted_iota(jnp.int32, (CHUNK, CHUNK), 0)
    c64 = lax.broadcasted_iota(jnp.int32, (CHUNK, CHUNK), 1)
    ltri = jnp.where(r64 >= c64, 1.0, 0.0).astype(F32)
    c_sub = lax.broadcasted_iota(jnp.int32, (SUB, CHUNK), 1)
    nsub = CHUNK // SUB
    half = SUB // 2
    low_row = half + lax.broadcasted_iota(jnp.int32, (nsub, half, LANES), 1)

    def head_chunk(h, r0):
        qc = q_scr[h, pl.ds(r0, CHUNK), :]
        kc = k_scr[h, pl.ds(r0, CHUNK), :]
        vc = v_scr[h, pl.ds(r0, CHUNK), :]
        gc = g_scr[h, pl.ds(r0, CHUNK), :]
        b = jnp.dot(ltri, gc, precision=HIGHEST, preferred_element_type=F32)
        bend = b[CHUNK - 1:CHUNK, :]
        st = st_scr[h]
        o = lax.dot_general((qc * jnp.exp2(b)).astype(BF16), st.astype(BF16), NT, preferred_element_type=F32)
        k4 = kc.reshape(nsub, SUB, LANES)
        b4 = b.reshape(nsub, SUB, LANES)
        pieces = []
        for j in range(half):
            k_rep = jnp.broadcast_to(k4[:, j:j + 1, :], (nsub, SUB, LANES)).reshape(CHUNK, LANES)
            b_rep = jnp.broadcast_to(b4[:, j:j + 1, :], (nsub, SUB, LANES)).reshape(CHUNK, LANES)
            e = qc * k_rep * jnp.exp2(jnp.where(sub_row >= j, b - b_rep, NEG))
            pieces.append(e.astype(BF16))
        q_low = qc.reshape(nsub, SUB, LANES)[:, half:, :]
        b_low = b4[:, half:, :]
        for j in range(half, SUB):
            k_rep = jnp.broadcast_to(k4[:, j:j + 1, :], (nsub, half, LANES))
            b_rep = jnp.broadcast_to(b4[:, j:j + 1, :], (nsub, half, LANES))
            e_low = q_low * k_rep * jnp.exp2(jnp.where(low_row >= j, b_low - b_rep, NEG))
            e = jnp.concatenate([jnp.zeros_like(e_low), e_low], axis=1).reshape(CHUNK, LANES)
            pieces.append(e.astype(BF16))
        a_diag = jnp.dot(jnp.concatenate(pieces, axis=1), sel_ref[...], preferred_element_type=F32)
        a_diag = jnp.where(blockdiag, a_diag, 0.0)[:, 0:CHUNK]
        rows = [jnp.zeros((SUB, CHUNK), F32)]
        for i_sub in range(1, nsub):
            ref_b = b[i_sub * SUB - 1:i_sub * SUB, :]
            qt = qc[i_sub * SUB:(i_sub + 1) * SUB, :] * jnp.exp2(b[i_sub * SUB:(i_sub + 1) * SUB, :] - ref_b)
            kt = kc * jnp.exp2(jnp.minimum(ref_b - b, 0.0))
            a_i = lax.dot_general(qt.astype(BF16), kt.astype(BF16), NT, preferred_element_type=F32)
            rows.append(jnp.where(c_sub < i_sub * SUB, a_i, 0.0))
        a = jnp.concatenate(rows, axis=0) + a_diag
        o = o + jnp.dot(a.astype(BF16), vc.astype(BF16), preferred_element_type=F32)
        o_scr[h, pl.ds(r0, CHUNK), :] = o
        kend = kc * jnp.exp2(bend - b)
        st_scr[h] = st * jnp.exp2(bend) + lax.dot_general(vc.astype(BF16), kend.astype(BF16), TN,
                                                          preferred_element_type=F32)

    def chunk(c, carry):
        r0 = pl.multiple_of(c * CHUNK, CHUNK)
        for h in range(q_scr.shape[0]):
            head_chunk(h, r0)
        return carry

    lax.fori_loop(0, tb // CHUNK, chunk, 0)


def _finish_recur(o_scr, gain_ref, gate, o_ref):
    for h in range(o_scr.shape[0]):
        o = o_scr[h]
        ms = jnp.mean(o * o, axis=-1, keepdims=True)
        y = o * lax.rsqrt(ms + EPS) * gain_ref[...] * gate[:, h * LANES:(h + 1) * LANES]
        o_ref[:, h * LANES:(h + 1) * LANES] = y.astype(o_ref.dtype)


def _hgrn2_kernel(q_ref, f_ref, i_ref, gg_ref, lb_ref, gain_ref, sel_ref, o_ref,
                  q_scr, k_scr, v_scr, g_scr, o_scr, st_scr, *, tb):
    @pl.when(pl.program_id(1) == 0)
    def _():
        st_scr[...] = jnp.zeros(st_scr.shape, F32)

    for h in range(q_scr.shape[0]):
        cols = slice(h * LANES, (h + 1) * LANES)
        lb = lb_ref[:, cols]
        z = f_ref[:, cols]
        sg = _sigmoid(z)
        f = lb + (1.0 - lb) * sg
        q_scr[h] = q_ref[:, cols]
        k_scr[h] = (1.0 - lb) * (1.0 - sg)
        v_scr[h] = i_ref[:, cols]
        g_scr[h] = jnp.log(jnp.maximum(f, TINY)) * LOG2E
    _recur_tile(q_scr, k_scr, v_scr, g_scr, o_scr, st_scr, sel_ref, tb)
    _finish_recur(o_scr, gain_ref, _sigmoid(gg_ref[...]), o_ref)


def _recur_scratch(tb, heads):
    return ([pltpu.VMEM((heads, tb, LANES), F32) for _ in range(5)]
            + [pltpu.VMEM((heads, LANES, LANES), F32)])


def hgrn2_mixer(proj, lower_bound, norm_gain, tb=512, heads=HG_HEADS):
    T = proj.shape[0]
    wide = heads * LANES
    per = HG_HEADS // heads

    def col(k):
        return pl.BlockSpec((tb, wide), lambda hp, i, k=k: (i, per * k + hp))

    return pl.pallas_call(
        functools.partial(_hgrn2_kernel, tb=tb),
        grid=(per, T // tb),
        in_specs=[col(0), col(1), col(2), col(3),
                  pl.BlockSpec((1, wide), lambda hp, i: (0, hp)),
                  pl.BlockSpec((1, LANES), lambda hp, i: (0, 0)),
                  pl.BlockSpec((SUB * LANES, LANES), lambda hp, i: (0, 0))],
        out_specs=pl.BlockSpec((tb, wide), lambda hp, i: (i, hp)),
        out_shape=jax.ShapeDtypeStruct((T, BRANCH_WIDTH), BF16),
        scratch_shapes=_recur_scratch(tb, heads),
        compiler_params=_cparams(("parallel", "arbitrary")),
        name="hgrn2_mixer",
    )(proj, proj, proj, proj, lower_bound.reshape(1, -1), norm_gain.reshape(1, -1), _diag_selector())


def _gla_kernel(q_ref, k_ref, v_ref, r_ref, tail_ref, wa_ref, ba_ref, gain_ref, sel_ref, o_ref,
                q_scr, k_scr, v_scr, g_scr, o_scr, st_scr, *, tb):
    @pl.when(pl.program_id(1) == 0)
    def _():
        st_scr[...] = jnp.zeros(st_scr.shape, F32)

    lane = lax.broadcasted_iota(jnp.int32, (tb, LANES), 1)
    a = jnp.dot(tail_ref[...], wa_ref[...], precision=HIGHEST, preferred_element_type=F32) + ba_ref[...]
    log_sig = -(jnp.maximum(-a, 0.0) + jnp.log1p(jnp.exp(-jnp.abs(a))))
    g2 = log_sig * (LOG2E / GLA_TAU)
    for h in range(q_scr.shape[0]):
        pair = slice((h // 2) * LANES, (h // 2 + 1) * LANES)
        mine = (lane < GLA_DK) if h % 2 == 0 else (lane >= GLA_DK)
        q_scr[h] = jnp.where(mine, q_ref[:, pair] * (GLA_DK ** -0.5), 0.0)
        k_scr[h] = jnp.where(mine, k_ref[:, pair], 0.0)
        v_scr[h] = v_ref[:, h * LANES:(h + 1) * LANES]
        g_scr[h] = jnp.where(mine, g2[:, pair], 0.0)
    _recur_tile(q_scr, k_scr, v_scr, g_scr, o_scr, st_scr, sel_ref, tb)
    r = r_ref[...]
    _finish_recur(o_scr, gain_ref, r * _sigmoid(r), o_ref)


def gla_mixer(proj, proj_t, w_a2, b_a, norm_gain, tb=512, heads=GLA_HEADS):
    T = proj.shape[0]
    assert heads % 2 == 0 and LANES == 2 * GLA_DK
    wide = heads * LANES
    half = heads * GLA_DK
    cqk = C_GLA_QK // half
    cvr = C_GLA_VR // wide
    per = GLA_HEADS // heads
    wa = jnp.zeros((LANES, GLA_HEADS * GLA_DK), F32).at[TAIL_GLA_A0:TAIL_GLA_A0 + GLA_RANK].set(w_a2)
    return pl.pallas_call(
        functools.partial(_gla_kernel, tb=tb),
        grid=(per, T // tb),
        in_specs=[pl.BlockSpec((tb, half), lambda hp, i: (i, cqk + hp)),
                  pl.BlockSpec((tb, half), lambda hp, i: (i, cqk + per + hp)),
                  pl.BlockSpec((tb, wide), lambda hp, i: (i, cvr + hp)),
                  pl.BlockSpec((tb, wide), lambda hp, i: (i, cvr + per + hp)),
                  pl.BlockSpec((tb, LANES), lambda hp, i: (i, 0)),
                  pl.BlockSpec((LANES, half), lambda hp, i: (0, hp)),
                  pl.BlockSpec((1, half), lambda hp, i: (0, hp)),
                  pl.BlockSpec((1, LANES), lambda hp, i: (0, 0)),
                  pl.BlockSpec((SUB * LANES, LANES), lambda hp, i: (0, 0))],
        out_specs=pl.BlockSpec((tb, wide), lambda hp, i: (i, hp)),
        out_shape=jax.ShapeDtypeStruct((T, BRANCH_WIDTH), BF16),
        scratch_shapes=_recur_scratch(tb, heads),
        compiler_params=_cparams(("parallel", "arbitrary")),
        name="gla_mixer",
    )(proj, proj, proj, proj, proj_t, wa, b_a.reshape(1, -1), norm_gain.reshape(1, -1), _diag_selector())


def _merge_kernel(oc_ref, os_ref, ow_ref, ob_ref, og_ref, wb_ref, g0_ref, g1_ref, g2_ref, o_ref):
    o_a = (oc_ref[...] + os_ref[...] + ow_ref[...]).astype(BF16)
    acc = _sigmoid(g0_ref[...]) * jnp.dot(o_a, wb_ref[0], preferred_element_type=F32)
    acc = acc + _sigmoid(g1_ref[...]) * jnp.dot(ob_ref[...], wb_ref[1], preferred_element_type=F32)
    acc = acc + _sigmoid(g2_ref[...]) * jnp.dot(og_ref[...], wb_ref[2], preferred_element_type=F32)
    o_ref[...] = acc.astype(o_ref.dtype)


def merge_branches(proj, o_cmp, o_slc, o_win, o_b, o_c, w_branch_bf16, tm=512, tn=512):
    T = proj.shape[0]
    W = BRANCH_WIDTH
    nj = D_MODEL // tn
    ospec = pl.BlockSpec((tm, W), lambda i, j: (i, 0))

    def gspec(n):
        return pl.BlockSpec((tm, tn), lambda i, j, n=n: (i, n * nj + j))

    return pl.pallas_call(
        _merge_kernel,
        grid=(T // tm, nj),
        in_specs=[ospec, ospec, ospec, ospec, ospec,
                  pl.BlockSpec((3, W, tn), lambda i, j: (0, 0, j)),
                  gspec(0), gspec(1), gspec(2)],
        out_specs=pl.BlockSpec((tm, tn), lambda i, j: (i, j)),
        out_shape=jax.ShapeDtypeStruct((T, D_MODEL), BF16),
        compiler_params=_cparams(("parallel", "arbitrary")),
        name="merge_branches",
    )(o_cmp, o_slc, o_win, o_b, o_c, w_branch_bf16, proj, proj, proj)


MOE_TILE = 256


def _route_kernel(x_ref, g_ref, wr_ref, br_ref, h_ref, route_ref, cnt_ref, carry, *, tm):
    i = pl.program_id(0)

    @pl.when(i == 0)
    def _():
        carry[...] = jnp.zeros(carry.shape, F32)

    x = x_ref[...]
    ms = jnp.mean(x * x, axis=-1, keepdims=True)
    h = x * lax.rsqrt(ms + EPS) * g_ref[...]
    h_ref[...] = h
    logits = jnp.dot(h, wr_ref[...], precision=HIGHEST, preferred_element_type=F32) + br_ref[...]
    lane = lax.broadcasted_iota(jnp.int32, (tm, LANES), 1)

    def masked_softmax(mask):
        l = jnp.where(mask, logits, NEG)
        e = jnp.where(mask, jnp.exp(l - jnp.max(l, axis=1, keepdims=True)), 0.0)
        return e / jnp.sum(e, axis=1, keepdims=True)

    def top1(prob, mask):
        p = jnp.max(jnp.where(mask, prob, -1.0), axis=1, keepdims=True)
        idx = jnp.min(jnp.where(mask & (prob == p), lane, LANES), axis=1, keepdims=True)
        return p, idx

    gmask = lane < N_GROUPS
    gw, gidx = top1(masked_softmax(gmask), gmask)
    emask = (lane >= N_GROUPS) & (lane < N_GROUPS + N_EXPERTS) & ((lane - N_GROUPS) // EXPERTS_PER_GROUP == gidx)
    eprob = masked_softmax(emask)
    p1, i1 = top1(eprob, emask)
    rest = emask & (lane != i1)
    p2, i2 = top1(eprob, rest)
    psum = p1 + p2
    w1 = gw * (p1 / psum)
    w2 = gw * (p2 / psum)
    e1 = i1 - N_GROUPS
    e2 = i2 - N_GROUPS

    onehot = jnp.where((lane == e1) | (lane == e2), 1.0, 0.0)
    r = lax.broadcasted_iota(jnp.int32, (tm, tm), 0)
    c = lax.broadcasted_iota(jnp.int32, (tm, tm), 1)
    strict = jnp.where(r > c, 1.0, 0.0).astype(BF16)
    before = jnp.dot(strict, onehot.astype(BF16), preferred_element_type=F32) + carry[0:1, :]
    rank1 = jnp.sum(jnp.where(lane == e1, before, 0.0), axis=1, keepdims=True)
    rank2 = jnp.sum(jnp.where(lane == e2, before, 0.0), axis=1, keepdims=True)
    total = carry[0:1, :] + jnp.sum(onehot, axis=0, keepdims=True)
    carry[...] = jnp.broadcast_to(total, carry.shape)
    cnt_ref[...] = jnp.broadcast_to(total, cnt_ref.shape)

    out = jnp.where(lane == 0, w1, 0.0)
    out = jnp.where(lane == 1, w2, out)
    out = jnp.where(lane == 2, e1.astype(F32), out)
    out = jnp.where(lane == 3, e2.astype(F32), out)
    out = jnp.where(lane == 4, rank1, out)
    out = jnp.where(lane == 5, rank2, out)
    route_ref[...] = out


def moe_route(x, gain, w_grp, b_grp, w_exp, b_exp, tm=512):
    T, D = x.shape
    n_pad = LANES - N_GROUPS - N_EXPERTS
    wr = jnp.concatenate([w_grp, w_exp, jnp.zeros((D, n_pad), F32)], axis=1)
    br = jnp.concatenate([b_grp, b_exp, jnp.zeros((n_pad,), F32)]).reshape(1, LANES)
    return pl.pallas_call(
        functools.partial(_route_kernel, tm=tm),
        grid=(T // tm,),
        in_specs=[pl.BlockSpec((tm, D), lambda i: (i, 0)),
                  pl.BlockSpec((1, D), lambda i: (0, 0)),
                  pl.BlockSpec((D, LANES), lambda i: (0, 0)),
                  pl.BlockSpec((1, LANES), lambda i: (0, 0))],
        out_specs=(pl.BlockSpec((tm, D), lambda i: (i, 0)),
                   pl.BlockSpec((tm, LANES), lambda i: (i, 0)),
                   pl.BlockSpec((8, LANES), lambda i: (0, 0))),
        out_shape=(jax.ShapeDtypeStruct((T, D), F32),
                   jax.ShapeDtypeStruct((T, LANES), F32),
                   jax.ShapeDtypeStruct((8, LANES), F32)),
        scratch_shapes=[pltpu.VMEM((8, LANES), F32)],
        compiler_params=_cparams(("arbitrary",)),
        name="moe_route",
    )(x, gain.reshape(1, D), wr, br)


def _row_dma(src, src_row, dst, dst_row, sem):
    return pltpu.make_async_copy(src.at[pl.ds(src_row, 1), :], dst.at[pl.ds(dst_row, 1), :], sem)


def _rows_wait(src, dst, n, sem):
    pltpu.make_async_copy(src.at[pl.ds(0, n), :], dst.at[pl.ds(0, n), :], sem).wait()


def _expert_kernel(tile_expert, n_used, rcur_ref, rnext_ref, h_hbm, wgu_ref, wd_ref, o_ref,
                   xbuf, wgu_bf, wd_bf, sem):
    i = pl.program_id(0)
    used = i < n_used[0]
    slot = i % 2

    @pl.when(i == 0)
    def _():
        def body(r, c):
            _row_dma(h_hbm, rcur_ref[0, 0, r], xbuf.at[0], r, sem.at[0]).start()
            return c

        lax.fori_loop(0, MOE_TILE, body, 0, unroll=8)

    @pl.when(used)
    def _():
        prev = tile_expert[jnp.maximum(i - 1, 0)]

        @pl.when((i == 0) | (tile_expert[i] != prev))
        def _():
            wgu_bf[...] = wgu_ref[0, 0].astype(BF16)
            wd_bf[...] = wd_ref[0, 0].astype(BF16)

        for r in range(MOE_TILE):
            _row_dma(h_hbm, rnext_ref[0, 0, r], xbuf.at[1 - slot], r, sem.at[1 - slot]).start()
        _rows_wait(h_hbm, xbuf.at[slot], MOE_TILE, sem.at[slot])
        x = xbuf[slot].astype(BF16)
        gu = jnp.dot(x, wgu_bf[...], preferred_element_type=F32)
        gate = gu[:, 0:D_FF_EXPERT]
        up = gu[:, D_FF_EXPERT:2 * D_FF_EXPERT]
        act = gate * _sigmoid(gate) * up
        o_ref[...] = jnp.dot(act.astype(BF16), wd_bf[...], preferred_element_type=F32)

    @pl.when(i == n_used[0])
    def _():
        _rows_wait(h_hbm, xbuf.at[slot], MOE_TILE, sem.at[slot])

    @pl.when(jnp.logical_not(used))
    def _():
        o_ref[...] = jnp.zeros(o_ref.shape, F32)


def moe_experts(h, row_token, tile_expert, n_used, w_gate_up, w_down, layer):
    T, D = h.shape
    n_tiles = row_token.shape[0] // MOE_TILE
    rows = row_token.reshape(n_tiles, 1, MOE_TILE)
    grid_spec = pltpu.PrefetchScalarGridSpec(
        num_scalar_prefetch=2,
        grid=(n_tiles,),
        in_specs=[
            pl.BlockSpec((1, 1, MOE_TILE), lambda i, te, nu: (i, 0, 0), memory_space=pltpu.SMEM),
            pl.BlockSpec((1, 1, MOE_TILE), lambda i, te, nu: (jnp.minimum(i + 1, n_tiles - 1), 0, 0),
                         memory_space=pltpu.SMEM),
            pl.BlockSpec(memory_space=pl.ANY),
            pl.BlockSpec((1, 1, D, 2 * D_FF_EXPERT), lambda i, te, nu: (layer, te[i], 0, 0)),
            pl.BlockSpec((1, 1, D_FF_EXPERT, D), lambda i, te, nu: (layer, te[i], 0, 0)),
        ],
        out_specs=pl.BlockSpec((MOE_TILE, D), lambda i, te, nu: (i, 0)),
        scratch_shapes=[
            pltpu.VMEM((2, MOE_TILE, D), F32),
            pltpu.VMEM((D, 2 * D_FF_EXPERT), BF16),
            pltpu.VMEM((D_FF_EXPERT, D), BF16),
            pltpu.SemaphoreType.DMA((2,)),
        ],
    )
    return pl.pallas_call(
        _expert_kernel,
        grid_spec=grid_spec,
        out_shape=jax.ShapeDtypeStruct((n_tiles * MOE_TILE, D), F32),
        compiler_params=_cparams(("arbitrary",)),
        name="moe_experts",
    )(tile_expert, n_used, rows, rows, h, w_gate_up, w_down)


def _combine_kernel(dcur_ref, dnext_ref, x_ref, route_ref, y_hbm, o_ref, buf, sem, *, tm):
    i = pl.program_id(0)
    n = pl.num_programs(0)
    slot = i % 2

    def fetch(dref, s):
        def body(r, c):
            _row_dma(y_hbm, dref[0, 0, r], buf.at[s, 0], r, sem.at[s]).start()
            _row_dma(y_hbm, dref[0, 0, tm + r], buf.at[s, 1], r, sem.at[s]).start()
            return c

        lax.fori_loop(0, tm, body, 0, unroll=8)

    @pl.when(i == 0)
    def _():
        fetch(dcur_ref, 0)

    @pl.when(i + 1 < n)
    def _():
        fetch(dnext_ref, 1 - slot)

    _rows_wait(y_hbm, buf.at[slot, 0], tm, sem.at[slot])
    _rows_wait(y_hbm, buf.at[slot, 1], tm, sem.at[slot])
    route = route_ref[...]
    lane = lax.broadcasted_iota(jnp.int32, route.shape, 1)
    w1 = jnp.sum(jnp.where(lane == 0, route, 0.0), axis=1, keepdims=True)
    w2 = jnp.sum(jnp.where(lane == 1, route, 0.0), axis=1, keepdims=True)
    o_ref[...] = x_ref[...] + (w1 * buf[slot, 0] + w2 * buf[slot, 1])


def moe_combine(x, ys, route, dest, tm=256):
    T, D = x.shape
    n = T // tm
    dest_tiles = dest.reshape(2, n, tm).transpose(1, 0, 2).reshape(n, 1, 2 * tm)
    return pl.pallas_call(
        functools.partial(_combine_kernel, tm=tm),
        grid=(n,),
        in_specs=[pl.BlockSpec((1, 1, 2 * tm), lambda i: (i, 0, 0), memory_space=pltpu.SMEM),
                  pl.BlockSpec((1, 1, 2 * tm), lambda i: (jnp.minimum(i + 1, n - 1), 0, 0), memory_space=pltpu.SMEM),
                  pl.BlockSpec((tm, D), lambda i: (i, 0)),
                  pl.BlockSpec((tm, LANES), lambda i: (i, 0)),
                  pl.BlockSpec(memory_space=pl.ANY)],
        out_specs=pl.BlockSpec((tm, D), lambda i: (i, 0)),
        out_shape=jax.ShapeDtypeStruct((T, D), F32),
        scratch_shapes=[pltpu.VMEM((2, 2, tm, D), F32), pltpu.SemaphoreType.DMA((2,))],
        compiler_params=_cparams(("arbitrary",)),
        name="moe_combine",
    )(dest_tiles, dest_tiles, x, route, ys)


def hierarchical_moe(x, gain, w_grp, b_grp, w_exp, b_exp, w_gate_up, w_down, layer):
    T, D = x.shape
    h, route, cnt = moe_route(x, gain, w_grp, b_grp, w_exp, b_exp)
    route_t = route[:, 0:8].T
    expert = route_t[2:4].astype(jnp.int32)
    rank = route_t[4:6].astype(jnp.int32)
    counts = cnt[0, 0:N_EXPERTS].astype(jnp.int32)
    padded = ((counts + MOE_TILE - 1) // MOE_TILE) * MOE_TILE
    ends = jnp.cumsum(padded)
    offs = ends - padded
    e_ids = jnp.arange(N_EXPERTS, dtype=jnp.int32)[:, None, None]
    dest = rank + jnp.sum(jnp.where(expert[None] == e_ids, offs[:, None, None], 0), axis=0)
    n_rows = 2 * T + N_EXPERTS * MOE_TILE
    n_tiles = n_rows // MOE_TILE
    n_used = (ends[-1] // MOE_TILE).astype(jnp.int32)
    tile_start = jnp.arange(n_tiles, dtype=jnp.int32) * MOE_TILE
    tile_expert = jnp.sum((ends[None, :] <= tile_start[:, None]).astype(jnp.int32), axis=1)
    last_expert = tile_expert[jnp.maximum(n_used - 1, 0)]
    tile_expert = jnp.where(tile_start < ends[-1], tile_expert, last_expert)
    row_token = jnp.zeros((n_rows,), jnp.int32).at[dest.reshape(-1)].set(
        jnp.tile(jnp.arange(T, dtype=jnp.int32), 2))
    ys = moe_experts(h, row_token, tile_expert, n_used.reshape(1), w_gate_up, w_down, layer)
    return moe_combine(x, ys, route, dest)


def _in_proj_kernel(a_ref, wt_hbm, o_ref, wbuf, sem, *, layer, row0, tn, sigmoid_out):
    i, j = pl.program_id(0), pl.program_id(1)
    ni, nj = pl.num_programs(0), pl.num_programs(1)
    step = i * nj + j
    slot = step % 2

    def fetch(jj, s):
        start = pl.multiple_of(row0 + jj * tn, SUBLANES)
        return pltpu.make_async_copy(wt_hbm.at[layer, pl.ds(start, tn), :], wbuf.at[s], sem.at[s])

    @pl.when(step == 0)
    def _():
        fetch(0, 0).start()

    @pl.when(step + 1 < ni * nj)
    def _():
        fetch((j + 1) % nj, 1 - slot).start()

    fetch(j, slot).wait()
    w = wbuf[slot].astype(BF16)
    acc = lax.dot_general(a_ref[...], w, NT, preferred_element_type=F32)
    if sigmoid_out:
        acc = _sigmoid(acc)
    o_ref[...] = acc.astype(o_ref.dtype)


def in_proj(h, wt, layer, row0, n, tn, name, tm=2048, sigmoid_out=False):
    T, K = h.shape
    assert n % tn == 0 and row0 % SUBLANES == 0 and tn % SUBLANES == 0
    return pl.pallas_call(
        functools.partial(_in_proj_kernel, layer=layer, row0=row0, tn=tn, sigmoid_out=sigmoid_out),
        grid=(T // tm, n // tn),
        in_specs=[pl.BlockSpec((tm, K), lambda i, j: (i, 0)), pl.BlockSpec(memory_space=pl.ANY)],
        out_specs=pl.BlockSpec((tm, tn), lambda i, j: (i, j)),
        out_shape=jax.ShapeDtypeStruct((T, n), BF16 if sigmoid_out else F32),
        scratch_shapes=[pltpu.VMEM((2, tn, K), F32), pltpu.SemaphoreType.DMA((2,))],
        compiler_params=_cparams(("arbitrary", "arbitrary")),
        name=name,
    )(h, wt)


def kernel(x, norm_mix, w_in, cmp_pe, cmp_w1, cmp_w2, q_norm, k_norm, hg_lb_logits, hg_norm, gla_w_a2, gla_b_a,
           gla_norm, w_branch, w_out, norm_ffn, w_grp, b_grp, w_exp, b_exp, w_gate_up, w_down):
    B, T, D = x.shape
    assert B == 1 and D == D_MODEL
    xt = x[0]
    p_lb = jax.nn.softmax(hg_lb_logits.astype(F32), axis=0)
    lower_bounds = jnp.cumsum(p_lb, axis=0) - p_lb[0]
    wt = jnp.swapaxes(w_in, 1, 2)
    for l in range(DEPTH):
        h = rmsnorm_bf16(xt, norm_mix[l])
        proj_n = in_proj(h, wt, l, W_NSA[0], W_NSA[1], 512, "in_proj_nsa")
        proj_r = in_proj(h, wt, l, W_REC[0], W_REC[1], 512, "in_proj_rec")
        proj_a = in_proj(h, wt, l, W_GLA_A_BLOCK[0], W_GLA_A_BLOCK[1], LANES, "in_proj_gla_a")
        proj_m = in_proj(h, wt, l, W_MERGE[0], W_MERGE[1], 512, "in_proj_merge")
        o_cmp, o_slc, o_win = nsa_mixer(proj_n, cmp_pe[l], cmp_w1[l], cmp_w2[l], q_norm[l], k_norm[l])
        o_b = hgrn2_mixer(proj_r, lower_bounds[l], hg_norm[l])
        o_c = gla_mixer(proj_r, proj_a, gla_w_a2[l], gla_b_a[l], gla_norm[l])
        merged = merge_branches(proj_m, o_cmp, o_slc, o_win, o_b, o_c, w_branch[l].astype(BF16))
        xt = matmul_bf16(merged, w_out[l].astype(BF16), res=xt, tm=1024, tn=512, name="out_proj")
        xt = hierarchical_moe(xt, norm_ffn[l], w_grp[l], b_grp[l], w_exp[l], b_exp[l], w_gate_up, w_down, l)
    return xt[None]
```

```python
import functools
import math

import numpy as np
import jax
import jax.numpy as jnp
from jax import lax
from jax.experimental import pallas as pl
from jax.experimental.pallas import tpu as pltpu

F32 = jnp.float32
BF16 = jnp.bfloat16
HIGHEST = lax.Precision.HIGHEST

D_MODEL = 2048
DEPTH = 2
BRANCH_WIDTH = D_MODEL // 4
NSA_HEADS = 8
NSA_KV_HEADS = 2
NSA_REP = NSA_HEADS // NSA_KV_HEADS
NSA_HD = 64
CMP_LEN = 32
CMP_STRIDE = 16
SLC_LEN = 64
SLC_TOP = 16
WINDOW = 512
HG_HEADS = 4
GLA_HEADS = 4
GLA_DK = 64
GLA_RANK = 16
GLA_TAU = 16.0
CHUNK = 64
SUB = 16
N_GROUPS = 4
EXPERTS_PER_GROUP = 8
N_EXPERTS = N_GROUPS * EXPERTS_PER_GROUP
D_FF_EXPERT = D_MODEL // 4
EPS = 1e-6
NEG = -1e30
FORCE = 1e4
TINY = 1e-30

LANES = 128
SUBLANES = 8
VMEM_LIMIT = 56 * 1024 * 1024

W_NSA = (0, 1536)
W_REC = (1304, 3584)
W_MERGE = (4904, 6144)
W_GLA_A_BLOCK = (4864, 128)
C_NSA_KV = 512
C_NSA_GATE = 1280
C_GLA_QK = 2048
C_GLA_VR = 2560
TAIL_GLA_A0 = 4888 - W_GLA_A_BLOCK[0]

NT = (((1,), (1,)), ((), ()))
TN = (((0,), (0,)), ((), ()))


def _cparams(sem):
    return pltpu.CompilerParams(dimension_semantics=sem, vmem_limit_bytes=VMEM_LIMIT)


def _sigmoid(x):
    return 1.0 / (1.0 + jnp.exp(-x))


def _norm_kernel(x_ref, g_ref, o_ref):
    x = x_ref[...]
    ms = jnp.mean(x * x, axis=-1, keepdims=True)
    o_ref[...] = (x * lax.rsqrt(ms + EPS) * g_ref[...]).astype(o_ref.dtype)


def rmsnorm_bf16(x, gain, tm=512):
    T, D = x.shape
    return pl.pallas_call(
        _norm_kernel,
        grid=(T // tm,),
        in_specs=[pl.BlockSpec((tm, D), lambda i: (i, 0)), pl.BlockSpec((1, D), lambda i: (0, 0))],
        out_specs=pl.BlockSpec((tm, D), lambda i: (i, 0)),
        out_shape=jax.ShapeDtypeStruct((T, D), BF16),
        compiler_params=_cparams(("parallel",)),
        name="rmsnorm_bf16",
    )(x, gain.reshape(1, D))


def _mm_kernel(a_ref, b_ref, o_ref):
    o_ref[...] = jnp.dot(a_ref[...], b_ref[...], preferred_element_type=F32)


def _mm_res_kernel(a_ref, b_ref, r_ref, o_ref):
    o_ref[...] = r_ref[...] + jnp.dot(a_ref[...], b_ref[...], preferred_element_type=F32)


def matmul_bf16(a, b, res=None, tm=1024, tn=512, name="matmul_bf16"):
    T, K = a.shape
    N = b.shape[1]
    in_specs = [pl.BlockSpec((tm, K), lambda i, j: (i, 0)), pl.BlockSpec((K, tn), lambda i, j: (0, j))]
    args = [a, b]
    kern = _mm_kernel
    if res is not None:
        in_specs.append(pl.BlockSpec((tm, tn), lambda i, j: (i, j)))
        args.append(res)
        kern = _mm_res_kernel
    return pl.pallas_call(
        kern,
        grid=(T // tm, N // tn),
        in_specs=in_specs,
        out_specs=pl.BlockSpec((tm, tn), lambda i, j: (i, j)),
        out_shape=jax.ShapeDtypeStruct((T, N), F32),
        compiler_params=_cparams(("parallel", "arbitrary")),
        name=name,
    )(*args)


def _half_rmsnorm(x, gain2, lo):
    x2 = x * x
    s_lo = jnp.sum(jnp.where(lo, x2, 0.0), axis=1, keepdims=True)
    s_hi = jnp.sum(jnp.where(lo, 0.0, x2), axis=1, keepdims=True)
    ms = jnp.where(lo, s_lo, s_hi) * (1.0 / NSA_HD)
    return x * lax.rsqrt(ms + EPS) * gain2


def _nsa_prep_kernel(q_ref, kvc_ref, kvs_ref, kvw_ref, tail_ref, qg_ref, kg_ref,
                     qn_ref, kaug_ref, vs_ref, kw_ref, vw_ref, gate_ref, *, tm):
    i = pl.program_id(0)
    lane = lax.broadcasted_iota(jnp.int32, (tm, LANES), 1)
    lo = lane < NSA_HD
    qg = qg_ref[...]
    for c in range(4):
        x = q_ref[:, c * LANES:(c + 1) * LANES]
        qn_ref[:, c * LANES:(c + 1) * LANES] = (_half_rmsnorm(x, qg, lo) * (NSA_HD ** -0.5)).astype(BF16)

    def dup(kn):
        rolled = pltpu.roll(kn, NSA_HD, axis=1)
        return jnp.where(lo, kn, rolled), jnp.where(lo, rolled, kn)

    vt_zero = jnp.zeros((NSA_HD, tm), BF16)

    def store_vt(v_pair, ref):
        vt = jnp.transpose(v_pair)
        for g in range(NSA_KV_HEADS):
            head = vt[g * NSA_HD:(g + 1) * NSA_HD, :].astype(BF16)
            ref[g, 0:NSA_HD, :] = head
            ref[g, NSA_HD:LANES, :] = vt_zero
            ref[g, LANES:LANES + NSA_HD, :] = vt_zero
            ref[g, LANES + NSA_HD:2 * LANES, :] = head

    row = i * tm + lax.broadcasted_iota(jnp.int32, (tm, LANES), 0)
    onehot = jnp.where(row // SLC_LEN == lane, 1.0, 0.0).astype(BF16)
    k_extra = jnp.where(lane == NSA_HD, (row % SLC_LEN).astype(F32), 0.0)
    ks = dup(_half_rmsnorm(kvs_ref[:, 0:LANES], kg_ref[1:2, :], lo))
    store_vt(kvs_ref[:, LANES:2 * LANES], vs_ref)
    kw = dup(_half_rmsnorm(kvw_ref[:, 0:LANES], kg_ref[2:3, :], lo))
    store_vt(kvw_ref[:, LANES:2 * LANES], vw_ref)
    for g in range(NSA_KV_HEADS):
        kaug_ref[g, :, 0:LANES] = onehot
        kaug_ref[g, :, LANES:2 * LANES] = jnp.where(lo, ks[g], k_extra).astype(BF16)
        kw_ref[g] = kw[g].astype(BF16)
    gate_ref[...] = _sigmoid(tail_ref[...])


def nsa_prep(proj_n, q_gain, k_gain, tm=512):
    T = proj_n.shape[0]
    qg2 = jnp.tile(q_gain.reshape(1, NSA_HD), (1, 2))
    kg2 = jnp.tile(k_gain.reshape(3, NSA_HD), (1, 2))
    G = NSA_KV_HEADS
    out_shape = (
        jax.ShapeDtypeStruct((T, 512), BF16),
        jax.ShapeDtypeStruct((G, T, 256), BF16),
        jax.ShapeDtypeStruct((G, 256, T), BF16),
        jax.ShapeDtypeStruct((G, T, 128), BF16),
        jax.ShapeDtypeStruct((G, 256, T), BF16),
        jax.ShapeDtypeStruct((T, 128), F32),
    )
    return pl.pallas_call(
        functools.partial(_nsa_prep_kernel, tm=tm),
        grid=(T // tm,),
        in_specs=[
            pl.BlockSpec((tm, 512), lambda i: (i, 0)),
            pl.BlockSpec((tm, 256), lambda i: (i, C_NSA_KV // 256)),
            pl.BlockSpec((tm, 256), lambda i: (i, C_NSA_KV // 256 + 1)),
            pl.BlockSpec((tm, 256), lambda i: (i, C_NSA_KV // 256 + 2)),
            pl.BlockSpec((tm, 128), lambda i: (i, C_NSA_GATE // 128)),
            pl.BlockSpec((1, 128), lambda i: (0, 0)),
            pl.BlockSpec((3, 128), lambda i: (0, 0)),
        ],
        out_specs=(
            pl.BlockSpec((tm, 512), lambda i: (i, 0)),
            pl.BlockSpec((G, tm, 256), lambda i: (0, i, 0)),
            pl.BlockSpec((G, 256, tm), lambda i: (0, 0, i)),
            pl.BlockSpec((G, tm, 128), lambda i: (0, i, 0)),
            pl.BlockSpec((G, 256, tm), lambda i: (0, 0, i)),
            pl.BlockSpec((tm, 128), lambda i: (i, 0)),
        ),
        out_shape=out_shape,
        compiler_params=_cparams(("parallel",)),
        name="nsa_prep",
    )(proj_n, proj_n, proj_n, proj_n, proj_n, qg2, kg2)


def _gelu_tanh(x):
    c = math.sqrt(2.0 / math.pi)
    return 0.5 * x * (1.0 + jnp.tanh(c * (x + 0.044715 * (x * x * x))))


def _compress_kernel(a_ref, pe_ref, w1_ref, w2_ref, kg_ref, o_ref, *, nb):
    kind = pl.program_id(0)
    a_lo = a_ref[0, 0, 0:nb, :]
    a_hi = a_ref[0, 0, 1:nb + 1, :]
    blocks = jnp.concatenate([a_lo, a_hi], axis=1) + pe_ref[0]
    h1 = jnp.dot(blocks.astype(BF16), w1_ref[0].astype(BF16), preferred_element_type=F32)
    y = jnp.dot(_gelu_tanh(h1).astype(BF16), w2_ref[0].astype(BF16), preferred_element_type=F32)
    ms = jnp.mean(y * y, axis=-1, keepdims=True)
    yn = y * lax.rsqrt(ms + EPS) * kg_ref[...]
    o_ref[0, 0] = jnp.where(kind == 0, yn, y)


def nsa_compress(kv_cmp, cmp_pe, cmp_w1, cmp_w2, k_gain0):
    T = kv_cmp.shape[0]
    nb = T // CMP_STRIDE
    G = NSA_KV_HEADS
    a = kv_cmp.reshape(T, 2, G, NSA_HD).transpose(1, 2, 0, 3).reshape(2, G, nb, CMP_STRIDE * NSA_HD)
    a = jnp.pad(a, ((0, 0), (0, 0), (0, 8), (0, 0)))
    pe = cmp_pe.reshape(2, 1, CMP_LEN * NSA_HD)
    return pl.pallas_call(
        functools.partial(_compress_kernel, nb=nb),
        grid=(2, G),
        in_specs=[
            pl.BlockSpec((1, 1, nb + 8, CMP_STRIDE * NSA_HD), lambda k, g: (k, g, 0, 0)),
            pl.BlockSpec((1, 1, CMP_LEN * NSA_HD), lambda k, g: (k, 0, 0)),
            pl.BlockSpec((1, CMP_LEN * NSA_HD, NSA_HD), lambda k, g: (k, 0, 0)),
            pl.BlockSpec((1, NSA_HD, NSA_HD), lambda k, g: (k, 0, 0)),
            pl.BlockSpec((1, NSA_HD), lambda k, g: (0, 0)),
        ],
        out_specs=pl.BlockSpec((1, 1, nb, NSA_HD), lambda k, g: (k, g, 0, 0)),
        out_shape=jax.ShapeDtypeStruct((2, G, nb, NSA_HD), F32),
        compiler_params=_cparams(("arbitrary", "arbitrary")),
        name="nsa_compress",
    )(a, pe, cmp_w1, cmp_w2, k_gain0.reshape(1, NSA_HD))


def _slope(g, r):
    return jnp.where(g == 0, 2.0 ** -(r + 1), 2.0 ** -(NSA_REP + r + 1)).astype(F32)


def _gate_pair(gates, lane, lo, col_even):
    ge = jnp.sum(jnp.where(lane == col_even, gates, 0.0), axis=1, keepdims=True)
    go = jnp.sum(jnp.where(lane == col_even + 1, gates, 0.0), axis=1, keepdims=True)
    return jnp.where(lo, ge, go)


def _cmp_sel_kernel(q_ref, kc_ref, vc_ref, gate_ref, mt_ref, o_ref, sel_ref, *, tq, nb, ns_pad):
    g = pl.program_id(0)
    qi = pl.program_id(1)
    t0 = qi * tq
    lane = lax.broadcasted_iota(jnp.int32, (tq, LANES), 1)
    lo = lane < NSA_HD
    n_idx = lax.broadcasted_iota(jnp.int32, (nb, tq), 0)
    t_idx = t0 + lax.broadcasted_iota(jnp.int32, (nb, tq), 1)
    dist = (t_idx - (n_idx * CMP_STRIDE + (CMP_LEN - 1))).astype(F32)
    vis = dist >= 0.0
    kc = kc_ref[0]
    imp = jnp.zeros((nb, tq), F32)
    gates = gate_ref[...]
    for a in range(2):
        acc = jnp.zeros((tq, LANES), F32)
        for half in range(2):
            r = 2 * a + half
            qh = q_ref[:, a * LANES:(a + 1) * LANES]
            qh = jnp.where(lo if half == 0 else jnp.logical_not(lo), qh, jnp.zeros_like(qh))
            s = lax.dot_general(kc, qh, NT, preferred_element_type=F32)
            s = jnp.where(vis, s - _slope(g, r) * dist, NEG)
            mx = jnp.max(s, axis=0, keepdims=True)
            e = jnp.where(vis, jnp.exp(s - mx), 0.0)
            den = jnp.sum(e, axis=0, keepdims=True)
            p = e * jnp.where(den > 0.0, 1.0 / den, 0.0)
            imp = imp + p
            v = vc_ref[0, :, half * LANES:(half + 1) * LANES]
            acc = acc + lax.dot_general(p.astype(BF16), v, TN, preferred_element_type=F32)
        gp = _gate_pair(gates, lane, lo, g * NSA_REP + 2 * a)
        o_ref[:, a * LANES:(a + 1) * LANES] = acc * gp

    score_all = jnp.dot(mt_ref[...], imp, precision=HIGHEST, preferred_element_type=F32)
    blk = lax.broadcasted_iota(jnp.int32, (ns_pad, LANES), 0)
    for c in range(tq // LANES):
        cur = (t0 + c * LANES + lax.broadcasted_iota(jnp.int32, (ns_pad, LANES), 1)) // SLC_LEN
        forced = (blk == 0) | (blk == cur) | (blk == cur - 1)
        score = jnp.where(forced, FORCE, score_all[:, c * LANES:(c + 1) * LANES])
        score = jnp.where(blk <= cur, score, -1.0)
        for _ in range(SLC_TOP):
            mx = jnp.max(score, axis=0, keepdims=True)
            idx = jnp.min(jnp.where(score == mx, blk, ns_pad), axis=0, keepdims=True)
            score = jnp.where(blk == idx, -jnp.inf, score)
        bias_t = jnp.where((score == -jnp.inf) & (blk <= cur), 0.0, NEG)
        sel_ref[0, c * LANES:(c + 1) * LANES, :] = jnp.transpose(bias_t).astype(BF16)


def _score_matrix(nb, ns_pad):
    ratio, span = SLC_LEN // CMP_STRIDE, CMP_LEN // CMP_STRIDE
    n_cmp = nb - 1
    m = np.zeros((ns_pad, nb), np.float32)
    for s in range(nb // ratio):
        for mm in range(ratio):
            for nn in range(span):
                c = ratio * s + mm - nn
                if 0 <= c < n_cmp:
                    m[s, c] += 1.0
    return jnp.asarray(m)


def nsa_cmp_select(qn, kc2, vc_lohi, gates, tq=256):
    T = qn.shape[0]
    nb = T // CMP_STRIDE
    ns_pad = LANES
    G = NSA_KV_HEADS
    return pl.pallas_call(
        functools.partial(_cmp_sel_kernel, tq=tq, nb=nb, ns_pad=ns_pad),
        grid=(G, T // tq),
        in_specs=[
            pl.BlockSpec((tq, 256), lambda g, i: (i, g)),
            pl.BlockSpec((1, nb, 128), lambda g, i: (g, 0, 0)),
            pl.BlockSpec((1, nb, 256), lambda g, i: (g, 0, 0)),
            pl.BlockSpec((tq, 128), lambda g, i: (i, 0)),
            pl.BlockSpec((ns_pad, nb), lambda g, i: (0, 0)),
        ],
        out_specs=(
            pl.BlockSpec((tq, 256), lambda g, i: (i, g)),
            pl.BlockSpec((1, tq, ns_pad), lambda g, i: (g, i, 0)),
        ),
        out_shape=(jax.ShapeDtypeStruct((T, 512), F32), jax.ShapeDtypeStruct((G, T, ns_pad), BF16)),
        compiler_params=_cparams(("parallel", "parallel")),
        name="nsa_cmp_select",
    )(qn, kc2, vc_lohi, gates, _score_matrix(nb, ns_pad))


KV_SLOTS = 3


def _sel_attn_kernel(tile_tab, cnt_tab, q_ref, sb_ref, gate_ref, k_hbm, v_hbm, o_ref,
                     qaug, kbuf, vbuf, sem, m_scr, l_scr, acc_scr, *, tq, tk, max_tiles):
    g = pl.program_id(0)
    qi = pl.program_id(1)
    row = g * pl.num_programs(1) + qi
    n_tiles = cnt_tab[row]
    base = row * max_tiles
    lane = lax.broadcasted_iota(jnp.int32, (tq, LANES), 1)
    lo = lane < NSA_HD

    def copies(j, slot):
        start = pl.multiple_of(tile_tab[base + j] * tk, tk)
        return (pltpu.make_async_copy(k_hbm.at[g, pl.ds(start, tk), :], kbuf.at[slot], sem.at[0, slot]),
                pltpu.make_async_copy(v_hbm.at[g, :, pl.ds(start, tk)], vbuf.at[slot], sem.at[1, slot]))

    def start_fetch(j, slot):
        for cp in copies(j, slot):
            cp.start()

    def wait_fetch(j, slot):
        for cp in copies(j, slot):
            cp.wait()

    start_fetch(0, 0)

    @pl.when(n_tiles > 1)
    def _():
        start_fetch(1, 1)

    sb = sb_ref[0].astype(F32)
    blk_rel = (lane - qi * (tq // SLC_LEN)).astype(F32)
    for r in range(NSA_REP):
        a, half = r // 2, r % 2
        slope = _slope(g, r)
        qh = q_ref[:, a * LANES:(a + 1) * LANES].astype(F32)
        if half:
            qh = pltpu.roll(qh, NSA_HD, axis=1)
        qh = jnp.where(lo, qh, jnp.where(lane == NSA_HD, slope, 0.0))
        qaug[r * tq:(r + 1) * tq, 0:LANES] = (sb + (slope * SLC_LEN) * blk_rel).astype(BF16)
        qaug[r * tq:(r + 1) * tq, LANES:2 * LANES] = qh.astype(BF16)
    m_scr[...] = jnp.full(m_scr.shape, NEG, F32)
    l_scr[...] = jnp.zeros(l_scr.shape, F32)
    acc_scr[...] = jnp.zeros(acc_scr.shape, F32)

    upper = lax.broadcasted_iota(jnp.int32, (LANES, tq), 0) < NSA_HD
    kj_diag = (qi * tq + (tq - 1)) // tk

    def step(masked, slot):
        if masked:
            kpos = kj_diag * tk + lax.broadcasted_iota(jnp.int32, (tk, tq), 0)
            qpos = qi * tq + lax.broadcasted_iota(jnp.int32, (tk, tq), 1)
            causal = kpos <= qpos
        k = kbuf[slot]
        for a in range(2):
            pv = jnp.zeros((LANES, tq), F32)
            alphas = []
            for half in range(2):
                r = 2 * a + half
                sr = lax.dot_general(k, qaug[r * tq:(r + 1) * tq, :], NT, preferred_element_type=F32)
                if masked:
                    sr = jnp.where(causal, sr, NEG)
                m_prev = m_scr[r:r + 1, :]
                m_new = jnp.maximum(m_prev, jnp.max(sr, axis=0, keepdims=True))
                alpha = jnp.exp(m_prev - m_new)
                p = jnp.exp(sr - m_new)
                l_scr[r:r + 1, :] = alpha * l_scr[r:r + 1, :] + jnp.sum(p, axis=0, keepdims=True)
                m_scr[r:r + 1, :] = m_new
                vt = vbuf[slot, half * LANES:(half + 1) * LANES, :]
                pv = pv + jnp.dot(vt, p.astype(BF16), preferred_element_type=F32)
                alphas.append(alpha)
            alpha_pair = jnp.where(upper, alphas[0], alphas[1])
            acc_scr[a * LANES:(a + 1) * LANES, :] = alpha_pair * acc_scr[a * LANES:(a + 1) * LANES, :] + pv

    def off_diagonal(j, carry):
        @pl.when(j + 2 < n_tiles)
        def _():
            start_fetch(j + 2, (j + 2) % KV_SLOTS)

        wait_fetch(j, j % KV_SLOTS)
        step(False, j % KV_SLOTS)
        return carry

    lax.fori_loop(0, n_tiles - 1, off_diagonal, 0)
    last = n_tiles - 1
    wait_fetch(last, last % KV_SLOTS)
    step(True, last % KV_SLOTS)

    gates = gate_ref[...]
    for a in range(2):
        l_pair = jnp.where(upper, l_scr[2 * a:2 * a + 1, :], l_scr[2 * a + 1:2 * a + 2, :])
        gp = _gate_pair(gates, lane, lo, NSA_HEADS + g * NSA_REP + 2 * a)
        o_t = acc_scr[a * LANES:(a + 1) * LANES, :] / l_pair
        o_ref[:, a * LANES:(a + 1) * LANES] = jnp.transpose(o_t) * gp


def nsa_selected_attention(qn, selbias, kaug, vs_lohi, gates, tq=256, tk=512):
    T = qn.shape[0]
    G = NSA_KV_HEADS
    nq, nkt, nkb = T // tq, T // tk, tk // SLC_LEN
    chosen = (selbias.astype(F32) > 0.5 * NEG).reshape(G, nq, tq, LANES // nkb, nkb)
    tile_any = jnp.any(chosen, axis=(2, 4))[:, :, 0:nkt]
    kj = jnp.arange(nkt, dtype=jnp.int32)[None, None, :]
    kj_diag = ((jnp.arange(nq, dtype=jnp.int32) * tq + (tq - 1)) // tk)[None, :, None]
    active = (tile_any & (kj < kj_diag)) | (kj == kj_diag)
    csum = jnp.cumsum(active.astype(jnp.int32), axis=2)
    counts = csum[:, :, -1]
    tiles = jnp.sum((csum[:, :, None, :] <= kj[..., None]).astype(jnp.int32), axis=3)
    tiles = jnp.minimum(tiles, nkt - 1)
    grid_spec = pltpu.PrefetchScalarGridSpec(
        num_scalar_prefetch=2,
        grid=(G, nq),
        in_specs=[
            pl.BlockSpec((tq, 256), lambda g, i, tt, ct: (i, g)),
            pl.BlockSpec((1, tq, 128), lambda g, i, tt, ct: (g, i, 0)),
            pl.BlockSpec((tq, 128), lambda g, i, tt, ct: (i, 0)),
            pl.BlockSpec(memory_space=pl.ANY),
            pl.BlockSpec(memory_space=pl.ANY),
        ],
        out_specs=pl.BlockSpec((tq, 256), lambda g, i, tt, ct: (i, g)),
        scratch_shapes=[
            pltpu.VMEM((NSA_REP * tq, 256), BF16),
            pltpu.VMEM((KV_SLOTS, tk, 256), BF16),
            pltpu.VMEM((KV_SLOTS, 256, tk), BF16),
            pltpu.SemaphoreType.DMA((2, KV_SLOTS)),
            pltpu.VMEM((NSA_REP, tq), F32),
            pltpu.VMEM((NSA_REP, tq), F32),
            pltpu.VMEM((2 * LANES, tq), F32),
        ],
    )
    return pl.pallas_call(
        functools.partial(_sel_attn_kernel, tq=tq, tk=tk, max_tiles=nkt),
        grid_spec=grid_spec,
        out_shape=jax.ShapeDtypeStruct((T, 512), F32),
        compiler_params=_cparams(("parallel", "parallel")),
        name="nsa_selected_attention",
    )(tiles.reshape(-1).astype(jnp.int32), counts.reshape(-1).astype(jnp.int32),
      qn, selbias, gates, kaug, vs_lohi)


def _win_attn_kernel(q_ref, k0_ref, k1_ref, k2_ref, v0_ref, v1_ref, v2_ref, gate_ref, o_ref, *, tq):
    g = pl.program_id(0)
    qi = pl.program_id(1)
    lane = lax.broadcasted_iota(jnp.int32, (tq, LANES), 1)
    lo = lane < NSA_HD
    upper = lax.broadcasted_iota(jnp.int32, (LANES, tq), 0) < NSA_HD
    row = lax.broadcasted_iota(jnp.int32, (tq, tq), 0)
    qpos = qi * tq + lax.broadcasted_iota(jnp.int32, (tq, tq), 1)
    k_refs = (k0_ref, k1_ref, k2_ref)
    v_refs = (v0_ref, v1_ref, v2_ref)
    dists, masks = [], []
    for d in range(3):
        kpos = (qi - 2 + d) * tq + row
        dd = qpos - kpos
        dists.append(dd.astype(F32))
        masks.append((dd >= 0) & (dd < WINDOW) & (kpos >= 0))
    gates = gate_ref[...]
    for a in range(2):
        pv = jnp.zeros((LANES, tq), F32)
        ls = []
        for half in range(2):
            r = 2 * a + half
            qh = q_ref[:, a * LANES:(a + 1) * LANES]
            qh = jnp.where(lo if half == 0 else jnp.logical_not(lo), qh, jnp.zeros_like(qh))
            ss = []
            for d in range(3):
                sd = lax.dot_general(k_refs[d][0], qh, NT, preferred_element_type=F32)
                ss.append(jnp.where(masks[d], sd - _slope(g, r) * dists[d], NEG))
            mx = jnp.maximum(jnp.maximum(jnp.max(ss[0], axis=0, keepdims=True), jnp.max(ss[1], axis=0, keepdims=True)),
                             jnp.max(ss[2], axis=0, keepdims=True))
            l = jnp.zeros((1, tq), F32)
            for d in range(3):
                p = jnp.exp(ss[d] - mx)
                l = l + jnp.sum(p, axis=0, keepdims=True)
                vt = v_refs[d][0, half * LANES:(half + 1) * LANES, :]
                pv = pv + jnp.dot(vt, p.astype(BF16), preferred_element_type=F32)
            ls.append(l)
        l_pair = jnp.where(upper, ls[0], ls[1])
        gp = _gate_pair(gates, lane, lo, 2 * NSA_HEADS + g * NSA_REP + 2 * a)
        o_ref[:, a * LANES:(a + 1) * LANES] = jnp.transpose(pv / l_pair) * gp


def nsa_window_attention(qn, kw2, vw_lohi, gates, tq=256):
    T = qn.shape[0]
    G = NSA_KV_HEADS
    assert WINDOW == 2 * tq

    def kspec(d):
        return pl.BlockSpec((1, tq, 128), lambda g, i, d=d: (g, jnp.maximum(i - 2 + d, 0), 0))

    def vspec(d):
        return pl.BlockSpec((1, 256, tq), lambda g, i, d=d: (g, 0, jnp.maximum(i - 2 + d, 0)))

    return pl.pallas_call(
        functools.partial(_win_attn_kernel, tq=tq),
        grid=(G, T // tq),
        in_specs=[pl.BlockSpec((tq, 256), lambda g, i: (i, g)),
                  kspec(0), kspec(1), kspec(2),
                  vspec(0), vspec(1), vspec(2),
                  pl.BlockSpec((tq, 128), lambda g, i: (i, 0))],
        out_specs=pl.BlockSpec((tq, 256), lambda g, i: (i, g)),
        out_shape=jax.ShapeDtypeStruct((T, 512), F32),
        compiler_params=_cparams(("parallel", "parallel")),
        name="nsa_window_attention",
    )(qn, kw2, kw2, kw2, vw_lohi, vw_lohi, vw_lohi, gates)


def nsa_mixer(proj_n, cmp_pe, cmp_w1, cmp_w2, q_gain, k_gain, parts=False):
    qn, kaug, vs_lohi, kw2, vw_lohi, gates = nsa_prep(proj_n, q_gain, k_gain)
    kvc = nsa_compress(proj_n[:, C_NSA_KV:C_NSA_KV + 256], cmp_pe, cmp_w1, cmp_w2, k_gain[0])
    kc, vc = kvc[0], kvc[1]
    kc2 = jnp.concatenate([kc, kc], axis=-1).astype(BF16)
    zero = jnp.zeros_like(vc)
    vc_lohi = jnp.concatenate([vc, zero, zero, vc], axis=-1).astype(BF16)
    o_cmp, selbias = nsa_cmp_select(qn, kc2, vc_lohi, gates)
    o_slc = nsa_selected_attention(qn, selbias, kaug, vs_lohi, gates)
    o_win = nsa_window_attention(qn, kw2, vw_lohi, gates)
    if parts:
        return o_cmp + o_slc + o_win, (o_cmp, o_slc, o_win)
    return o_cmp, o_slc, o_win


def _diag_selector():
    m = np.zeros((SUB * LANES, LANES), np.float32)
    for j in range(SUB):
        for rep in range(CHUNK // SUB):
            m[j * LANES:(j + 1) * LANES, rep * SUB + j] = 1.0
    return jnp.asarray(m, dtype=BF16)


LOG2E = 1.0 / math.log(2.0)


def _recur_tile(q_scr, k_scr, v_scr, g_scr, o_scr, st_scr, sel_ref, tb):
    row = lax.broadcasted_iota(jnp.int32, (CHUNK, LANES), 0)
    lane = lax.broadcasted_iota(jnp.int32, (CHUNK, LANES), 1)
    sub_row = row % SUB
    blockdiag = (row // SUB) == (lane // SUB)
    r64 = lax.broadcasted_iota(jnp.int32, (CHUNK, CHUNK), 0)
    c64 = lax.broadcasted_iota(jnp.int32, (CHUNK, CHUNK), 1)
    ltri = jnp.where(r64 >= c64, 1.0, 0.0).astype(F32)
    c_sub = lax.broadcasted_iota(jnp.int32, (SUB, CHUNK), 1)
    nsub = CHUNK // SUB
    half = SUB // 2
    low_row = half + lax.broadcasted_iota(jnp.int32, (nsub, half, LANES), 1)

    def head_chunk(h, r0):
        qc = q_scr[h, pl.ds(r0, CHUNK), :]
        kc = k_scr[h, pl.ds(r0, CHUNK), :]
        vc = v_scr[h, pl.ds(r0, CHUNK), :]
        gc = g_scr[h, pl.ds(r0, CHUNK), :]
        b = jnp.dot(ltri, gc, precision=HIGHEST, preferred_element_type=F32)
        bend = b[CHUNK - 1:CHUNK, :]
        st = st_scr[h]
        o = lax.dot_general((qc * jnp.exp2(b)).astype(BF16), st.astype(BF16), NT, preferred_element_type=F32)
        k4 = kc.reshape(nsub, SUB, LANES)
        b4 = b.reshape(nsub, SUB, LANES)
        pieces = []
        for j in range(half):
            k_rep = jnp.broadcast_to(k4[:, j:j + 1, :], (nsub, SUB, LANES)).reshape(CHUNK, LANES)
            b_rep = jnp.broadcast_to(b4[:, j:j + 1, :], (nsub, SUB, LANES)).reshape(CHUNK, LANES)
            e = qc * k_rep * jnp.exp2(jnp.where(sub_row >= j, b - b_rep, NEG))
            pieces.append(e.astype(BF16))
        q_low = qc.reshape(nsub, SUB, LANES)[:, half:, :]
        b_low = b4[:, half:, :]
        for j in range(half, SUB):
            k_rep = jnp.broadcast_to(k4[:, j:j + 1, :], (nsub, half, LANES))
            b_rep = jnp.broadcast_to(b4[:, j:j + 1, :], (nsub, half, LANES))
            e_low = q_low * k_rep * jnp.exp2(jnp.where(low_row >= j, b_low - b_rep, NEG))
            e = jnp.concatenate([jnp.zeros_like(e_low), e_low], axis=1).reshape(CHUNK, LANES)
            pieces.append(e.astype(BF16))
        a_diag = jnp.dot(jnp.concatenate(pieces, axis=1), sel_ref[...], preferred_element_type=F32)
        a_diag = jnp.where(blockdiag, a_diag, 0.0)[:, 0:CHUNK]
        rows = [jnp.zeros((SUB, CHUNK), F32)]
        for i_sub in range(1, nsub):
            ref_b = b[i_sub * SUB - 1:i_sub * SUB, :]
            qt = qc[i_sub * SUB:(i_sub + 1) * SUB, :] * jnp.exp2(b[i_sub * SUB:(i_sub + 1) * SUB, :] - ref_b)
            kt = kc * jnp.exp2(jnp.minimum(ref_b - b, 0.0))
            a_i = lax.dot_general(qt.astype(BF16), kt.astype(BF16), NT, preferred_element_type=F32)
            rows.append(jnp.where(c_sub < i_sub * SUB, a_i, 0.0))
        a = jnp.concatenate(rows, axis=0) + a_diag
        o = o + jnp.dot(a.astype(BF16), vc.astype(BF16), preferred_element_type=F32)
        o_scr[h, pl.ds(r0, CHUNK), :] = o
        kend = kc * jnp.exp2(bend - b)
        st_scr[h] = st * jnp.exp2(bend) + lax.dot_general(vc.astype(BF16), kend.astype(BF16), TN,
                                                          preferred_element_type=F32)

    def chunk(c, carry):
        r0 = pl.multiple_of(c * CHUNK, CHUNK)
        for h in range(q_scr.shape[0]):
            head_chunk(h, r0)
        return carry

    lax.fori_loop(0, tb // CHUNK, chunk, 0)


def _finish_recur(o_scr, gain_ref, gate, o_ref):
    for h in range(o_scr.shape[0]):
        o = o_scr[h]
        ms = jnp.mean(o * o, axis=-1, keepdims=True)
        y = o * lax.rsqrt(ms + EPS) * gain_ref[...] * gate[:, h * LANES:(h + 1) * LANES]
        o_ref[:, h * LANES:(h + 1) * LANES] = y.astype(o_ref.dtype)


def _hgrn2_kernel(q_ref, f_ref, i_ref, gg_ref, lb_ref, gain_ref, sel_ref, o_ref,
                  q_scr, k_scr, v_scr, g_scr, o_scr, st_scr, *, tb):
    @pl.when(pl.program_id(1) == 0)
    def _():
        st_scr[...] = jnp.zeros(st_scr.shape, F32)

    for h in range(q_scr.shape[0]):
        cols = slice(h * LANES, (h + 1) * LANES)
        lb = lb_ref[:, cols]
        z = f_ref[:, cols]
        sg = _sigmoid(z)
        f = lb + (1.0 - lb) * sg
        q_scr[h] = q_ref[:, cols]
        k_scr[h] = (1.0 - lb) * (1.0 - sg)
        v_scr[h] = i_ref[:, cols]
        g_scr[h] = jnp.log(jnp.maximum(f, TINY)) * LOG2E
    _recur_tile(q_scr, k_scr, v_scr, g_scr, o_scr, st_scr, sel_ref, tb)
    _finish_recur(o_scr, gain_ref, _sigmoid(gg_ref[...]), o_ref)


def _recur_scratch(tb, heads):
    return ([pltpu.VMEM((heads, tb, LANES), F32) for _ in range(5)]
            + [pltpu.VMEM((heads, LANES, LANES), F32)])


def hgrn2_mixer(proj, lower_bound, norm_gain, tb=512, heads=HG_HEADS):
    T = proj.shape[0]
    wide = heads * LANES
    per = HG_HEADS // heads

    def col(k):
        return pl.BlockSpec((tb, wide), lambda hp, i, k=k: (i, per * k + hp))

    return pl.pallas_call(
        functools.partial(_hgrn2_kernel, tb=tb),
        grid=(per, T // tb),
        in_specs=[col(0), col(1), col(2), col(3),
                  pl.BlockSpec((1, wide), lambda hp, i: (0, hp)),
                  pl.BlockSpec((1, LANES), lambda hp, i: (0, 0)),
                  pl.BlockSpec((SUB * LANES, LANES), lambda hp, i: (0, 0))],
        out_specs=pl.BlockSpec((tb, wide), lambda hp, i: (i, hp)),
        out_shape=jax.ShapeDtypeStruct((T, BRANCH_WIDTH), BF16),
        scratch_shapes=_recur_scratch(tb, heads),
        compiler_params=_cparams(("parallel", "arbitrary")),
        name="hgrn2_mixer",
    )(proj, proj, proj, proj, lower_bound.reshape(1, -1), norm_gain.reshape(1, -1), _diag_selector())


def _gla_kernel(q_ref, k_ref, v_ref, r_ref, tail_ref, wa_ref, ba_ref, gain_ref, sel_ref, o_ref,
                q_scr, k_scr, v_scr, g_scr, o_scr, st_scr, *, tb):
    @pl.when(pl.program_id(1) == 0)
    def _():
        st_scr[...] = jnp.zeros(st_scr.shape, F32)

    lane = lax.broadcasted_iota(jnp.int32, (tb, LANES), 1)
    a = jnp.dot(tail_ref[...], wa_ref[...], precision=HIGHEST, preferred_element_type=F32) + ba_ref[...]
    log_sig = -(jnp.maximum(-a, 0.0) + jnp.log1p(jnp.exp(-jnp.abs(a))))
    g2 = log_sig * (LOG2E / GLA_TAU)
    for h in range(q_scr.shape[0]):
        pair = slice((h // 2) * LANES, (h // 2 + 1) * LANES)
        mine = (lane < GLA_DK) if h % 2 == 0 else (lane >= GLA_DK)
        q_scr[h] = jnp.where(mine, q_ref[:, pair] * (GLA_DK ** -0.5), 0.0)
        k_scr[h] = jnp.where(mine, k_ref[:, pair], 0.0)
        v_scr[h] = v_ref[:, h * LANES:(h + 1) * LANES]
        g_scr[h] = jnp.where(mine, g2[:, pair], 0.0)
    _recur_tile(q_scr, k_scr, v_scr, g_scr, o_scr, st_scr, sel_ref, tb)
    r = r_ref[...]
    _finish_recur(o_scr, gain_ref, r * _sigmoid(r), o_ref)


def gla_mixer(proj, proj_t, w_a2, b_a, norm_gain, tb=512, heads=GLA_HEADS):
    T = proj.shape[0]
    assert heads % 2 == 0 and LANES == 2 * GLA_DK
    wide = heads * LANES
    half = heads * GLA_DK
    cqk = C_GLA_QK // half
    cvr = C_GLA_VR // wide
    per = GLA_HEADS // heads
    wa = jnp.zeros((LANES, GLA_HEADS * GLA_DK), F32).at[TAIL_GLA_A0:TAIL_GLA_A0 + GLA_RANK].set(w_a2)
    return pl.pallas_call(
        functools.partial(_gla_kernel, tb=tb),
        grid=(per, T // tb),
        in_specs=[pl.BlockSpec((tb, half), lambda hp, i: (i, cqk + hp)),
                  pl.BlockSpec((tb, half), lambda hp, i: (i, cqk + per + hp)),
                  pl.BlockSpec((tb, wide), lambda hp, i: (i, cvr + hp)),
                  pl.BlockSpec((tb, wide), lambda hp, i: (i, cvr + per + hp)),
                  pl.BlockSpec((tb, LANES), lambda hp, i: (i, 0)),
                  pl.BlockSpec((LANES, half), lambda hp, i: (0, hp)),
                  pl.BlockSpec((1, half), lambda hp, i: (0, hp)),
                  pl.BlockSpec((1, LANES), lambda hp, i: (0, 0)),
                  pl.BlockSpec((SUB * LANES, LANES), lambda hp, i: (0, 0))],
        out_specs=pl.BlockSpec((tb, wide), lambda hp, i: (i, hp)),
        out_shape=jax.ShapeDtypeStruct((T, BRANCH_WIDTH), BF16),
        scratch_shapes=_recur_scratch(tb, heads),
        compiler_params=_cparams(("parallel", "arbitrary")),
        name="gla_mixer",
    )(proj, proj, proj, proj, proj_t, wa, b_a.reshape(1, -1), norm_gain.reshape(1, -1), _diag_selector())


def _merge_kernel(oc_ref, os_ref, ow_ref, ob_ref, og_ref, wb_ref, g0_ref, g1_ref, g2_ref, o_ref):
    o_a = (oc_ref[...] + os_ref[...] + ow_ref[...]).astype(BF16)
    acc = _sigmoid(g0_ref[...]) * jnp.dot(o_a, wb_ref[0], preferred_element_type=F32)
    acc = acc + _sigmoid(g1_ref[...]) * jnp.dot(ob_ref[...], wb_ref[1], preferred_element_type=F32)
    acc = acc + _sigmoid(g2_ref[...]) * jnp.dot(og_ref[...], wb_ref[2], preferred_element_type=F32)
    o_ref[...] = acc.astype(o_ref.dtype)


def merge_branches(proj, o_cmp, o_slc, o_win, o_b, o_c, w_branch_bf16, tm=512, tn=512):
    T = proj.shape[0]
    W = BRANCH_WIDTH
    nj = D_MODEL // tn
    ospec = pl.BlockSpec((tm, W), lambda i, j: (i, 0))

    def gspec(n):
        return pl.BlockSpec((tm, tn), lambda i, j, n=n: (i, n * nj + j))

    return pl.pallas_call(
        _merge_kernel,
        grid=(T // tm, nj),
        in_specs=[ospec, ospec, ospec, ospec, ospec,
                  pl.BlockSpec((3, W, tn), lambda i, j: (0, 0, j)),
                  gspec(0), gspec(1), gspec(2)],
        out_specs=pl.BlockSpec((tm, tn), lambda i, j: (i, j)),
        out_shape=jax.ShapeDtypeStruct((T, D_MODEL), BF16),
        compiler_params=_cparams(("parallel", "arbitrary")),
        name="merge_branches",
    )(o_cmp, o_slc, o_win, o_b, o_c, w_branch_bf16, proj, proj, proj)


MOE_TILE = 256


def _route_kernel(x_ref, g_ref, wr_ref, br_ref, h_ref, route_ref, cnt_ref, carry, *, tm):
    i = pl.program_id(0)

    @pl.when(i == 0)
    def _():
        carry[...] = jnp.zeros(carry.shape, F32)

    x = x_ref[...]
    ms = jnp.mean(x * x, axis=-1, keepdims=True)
    h = x * lax.rsqrt(ms + EPS) * g_ref[...]
    h_ref[...] = h
    logits = jnp.dot(h, wr_ref[...], precision=HIGHEST, preferred_element_type=F32) + br_ref[...]
    lane = lax.broadcasted_iota(jnp.int32, (tm, LANES), 1)

    def masked_softmax(mask):
        l = jnp.where(mask, logits, NEG)
        e = jnp.where(mask, jnp.exp(l - jnp.max(l, axis=1, keepdims=True)), 0.0)
        return e / jnp.sum(e, axis=1, keepdims=True)

    def top1(prob, mask):
        p = jnp.max(jnp.where(mask, prob, -1.0), axis=1, keepdims=True)
        idx = jnp.min(jnp.where(mask & (prob == p), lane, LANES), axis=1, keepdims=True)
        return p, idx

    gmask = lane < N_GROUPS
    gw, gidx = top1(masked_softmax(gmask), gmask)
    emask = (lane >= N_GROUPS) & (lane < N_GROUPS + N_EXPERTS) & ((lane - N_GROUPS) // EXPERTS_PER_GROUP == gidx)
    eprob = masked_softmax(emask)
    p1, i1 = top1(eprob, emask)
    rest = emask & (lane != i1)
    p2, i2 = top1(eprob, rest)
    psum = p1 + p2
    w1 = gw * (p1 / psum)
    w2 = gw * (p2 / psum)
    e1 = i1 - N_GROUPS
    e2 = i2 - N_GROUPS

    onehot = jnp.where((lane == e1) | (lane == e2), 1.0, 0.0)
    r = lax.broadcasted_iota(jnp.int32, (tm, tm), 0)
    c = lax.broadcasted_iota(jnp.int32, (tm, tm), 1)
    strict = jnp.where(r > c, 1.0, 0.0).astype(BF16)
    before = jnp.dot(strict, onehot.astype(BF16), preferred_element_type=F32) + carry[0:1, :]
    rank1 = jnp.sum(jnp.where(lane == e1, before, 0.0), axis=1, keepdims=True)
    rank2 = jnp.sum(jnp.where(lane == e2, before, 0.0), axis=1, keepdims=True)
    total = carry[0:1, :] + jnp.sum(onehot, axis=0, keepdims=True)
    carry[...] = jnp.broadcast_to(total, carry.shape)
    cnt_ref[...] = jnp.broadcast_to(total, cnt_ref.shape)

    out = jnp.where(lane == 0, w1, 0.0)
    out = jnp.where(lane == 1, w2, out)
    out = jnp.where(lane == 2, e1.astype(F32), out)
    out = jnp.where(lane == 3, e2.astype(F32), out)
    out = jnp.where(lane == 4, rank1, out)
    out = jnp.where(lane == 5, rank2, out)
    route_ref[...] = out


def moe_route(x, gain, w_grp, b_grp, w_exp, b_exp, tm=512):
    T, D = x.shape
    n_pad = LANES - N_GROUPS - N_EXPERTS
    wr = jnp.concatenate([w_grp, w_exp, jnp.zeros((D, n_pad), F32)], axis=1)
    br = jnp.concatenate([b_grp, b_exp, jnp.zeros((n_pad,), F32)]).reshape(1, LANES)
    return pl.pallas_call(
        functools.partial(_route_kernel, tm=tm),
        grid=(T // tm,),
        in_specs=[pl.BlockSpec((tm, D), lambda i: (i, 0)),
                  pl.BlockSpec((1, D), lambda i: (0, 0)),
                  pl.BlockSpec((D, LANES), lambda i: (0, 0)),
                  pl.BlockSpec((1, LANES), lambda i: (0, 0))],
        out_specs=(pl.BlockSpec((tm, D), lambda i: (i, 0)),
                   pl.BlockSpec((tm, LANES), lambda i: (i, 0)),
                   pl.BlockSpec((8, LANES), lambda i: (0, 0))),
        out_shape=(jax.ShapeDtypeStruct((T, D), F32),
                   jax.ShapeDtypeStruct((T, LANES), F32),
                   jax.ShapeDtypeStruct((8, LANES), F32)),
        scratch_shapes=[pltpu.VMEM((8, LANES), F32)],
        compiler_params=_cparams(("arbitrary",)),
        name="moe_route",
    )(x, gain.reshape(1, D), wr, br)


def _row_dma(src, src_row, dst, dst_row, sem):
    return pltpu.make_async_copy(src.at[pl.ds(src_row, 1), :], dst.at[pl.ds(dst_row, 1), :], sem)


def _rows_wait(src, dst, n, sem):
    pltpu.make_async_copy(src.at[pl.ds(0, n), :], dst.at[pl.ds(0, n), :], sem).wait()


def _expert_kernel(tile_expert, tile_rows, n_used, rcur_ref, rnext_ref, h_hbm, wgu_ref, wd_ref, o_ref,
                   xbuf, wgu_bf, wd_bf, sem):
    i = pl.program_id(0)
    used = i < n_used[0]
    slot = i % 2

    def fetch(rref, s, count):
        def body(r, c):
            _row_dma(h_hbm, rref[0, 0, r], xbuf.at[s], r, sem.at[s]).start()
            return c

        lax.fori_loop(0, count, body, 0)

    def drain(s, count):
        for b in range(MOE_TILE.bit_length()):
            @pl.when((count & (1 << b)) != 0)
            def _():
                _rows_wait(h_hbm, xbuf.at[s], 1 << b, sem.at[s])

    @pl.when(i == 0)
    def _():
        xbuf[...] = jnp.zeros(xbuf.shape, F32)
        fetch(rcur_ref, 0, tile_rows[0])

    @pl.when(i + 1 < n_used[0])
    def _():
        fetch(rnext_ref, 1 - slot, tile_rows[i + 1])

    @pl.when(used)
    def _():
        prev = tile_expert[jnp.maximum(i - 1, 0)]

        @pl.when((i == 0) | (tile_expert[i] != prev))
        def _():
            wgu_bf[...] = wgu_ref[0, 0].astype(BF16)
            wd_bf[...] = wd_ref[0, 0].astype(BF16)

        drain(slot, tile_rows[i])
        x = xbuf[slot].astype(BF16)
        gu = jnp.dot(x, wgu_bf[...], preferred_element_type=F32)
        gate = gu[:, 0:D_FF_EXPERT]
        up = gu[:, D_FF_EXPERT:2 * D_FF_EXPERT]
        act = gate * _sigmoid(gate) * up
        o_ref[...] = jnp.dot(act.astype(BF16), wd_bf[...], preferred_element_type=F32)

    @pl.when(jnp.logical_not(used))
    def _():
        o_ref[...] = jnp.zeros(o_ref.shape, F32)


def moe_experts(h, row_token, tile_expert, tile_rows, n_used, w_gate_up, w_down, layer):
    T, D = h.shape
    n_tiles = row_token.shape[0] // MOE_TILE
    rows = row_token.reshape(n_tiles, 1, MOE_TILE)
    grid_spec = pltpu.PrefetchScalarGridSpec(
        num_scalar_prefetch=3,
        grid=(n_tiles,),
        in_specs=[
            pl.BlockSpec((1, 1, MOE_TILE), lambda i, te, tr, nu: (i, 0, 0), memory_space=pltpu.SMEM),
            pl.BlockSpec((1, 1, MOE_TILE), lambda i, te, tr, nu: (jnp.minimum(i + 1, n_tiles - 1), 0, 0),
                         memory_space=pltpu.SMEM),
            pl.BlockSpec(memory_space=pl.ANY),
            pl.BlockSpec((1, 1, D, 2 * D_FF_EXPERT), lambda i, te, tr, nu: (layer, te[i], 0, 0)),
            pl.BlockSpec((1, 1, D_FF_EXPERT, D), lambda i, te, tr, nu: (layer, te[i], 0, 0)),
        ],
        out_specs=pl.BlockSpec((MOE_TILE, D), lambda i, te, tr, nu: (i, 0)),
        scratch_shapes=[
            pltpu.VMEM((2, MOE_TILE, D), F32),
            pltpu.VMEM((D, 2 * D_FF_EXPERT), BF16),
            pltpu.VMEM((D_FF_EXPERT, D), BF16),
            pltpu.SemaphoreType.DMA((2,)),
        ],
    )
    return pl.pallas_call(
        _expert_kernel,
        grid_spec=grid_spec,
        out_shape=jax.ShapeDtypeStruct((n_tiles * MOE_TILE, D), F32),
        compiler_params=_cparams(("arbitrary",)),
        name="moe_experts",
    )(tile_expert, tile_rows, n_used, rows, rows, h, w_gate_up, w_down)


def _combine_kernel(dcur_ref, dnext_ref, x_ref, route_ref, y_hbm, o_ref, buf, sem, *, tm):
    i = pl.program_id(0)
    n = pl.num_programs(0)
    slot = i % 2

    def fetch(dref, s):
        def body(r, c):
            _row_dma(y_hbm, dref[0, 0, r], buf.at[s, 0], r, sem.at[s]).start()
            _row_dma(y_hbm, dref[0, 0, tm + r], buf.at[s, 1], r, sem.at[s]).start()
            return c

        lax.fori_loop(0, tm, body, 0, unroll=8)

    @pl.when(i == 0)
    def _():
        fetch(dcur_ref, 0)

    @pl.when(i + 1 < n)
    def _():
        fetch(dnext_ref, 1 - slot)

    _rows_wait(y_hbm, buf.at[slot, 0], tm, sem.at[slot])
    _rows_wait(y_hbm, buf.at[slot, 1], tm, sem.at[slot])
    route = route_ref[...]
    lane = lax.broadcasted_iota(jnp.int32, route.shape, 1)
    w1 = jnp.sum(jnp.where(lane == 0, route, 0.0), axis=1, keepdims=True)
    w2 = jnp.sum(jnp.where(lane == 1, route, 0.0), axis=1, keepdims=True)
    o_ref[...] = x_ref[...] + (w1 * buf[slot, 0] + w2 * buf[slot, 1])


def moe_combine(x, ys, route, dest, tm=256):
    T, D = x.shape
    n = T // tm
    dest_tiles = dest.reshape(2, n, tm).transpose(1, 0, 2).reshape(n, 1, 2 * tm)
    return pl.pallas_call(
        functools.partial(_combine_kernel, tm=tm),
        grid=(n,),
        in_specs=[pl.BlockSpec((1, 1, 2 * tm), lambda i: (i, 0, 0), memory_space=pltpu.SMEM),
                  pl.BlockSpec((1, 1, 2 * tm), lambda i: (jnp.minimum(i + 1, n - 1), 0, 0), memory_space=pltpu.SMEM),
                  pl.BlockSpec((tm, D), lambda i: (i, 0)),
                  pl.BlockSpec((tm, LANES), lambda i: (i, 0)),
                  pl.BlockSpec(memory_space=pl.ANY)],
        out_specs=pl.BlockSpec((tm, D), lambda i: (i, 0)),
        out_shape=jax.ShapeDtypeStruct((T, D), F32),
        scratch_shapes=[pltpu.VMEM((2, 2, tm, D), F32), pltpu.SemaphoreType.DMA((2,))],
        compiler_params=_cparams(("arbitrary",)),
        name="moe_combine",
    )(dest_tiles, dest_tiles, x, route, ys)


def hierarchical_moe(x, gain, w_grp, b_grp, w_exp, b_exp, w_gate_up, w_down, layer):
    T, D = x.shape
    h, route, cnt = moe_route(x, gain, w_grp, b_grp, w_exp, b_exp)
    route_t = route[:, 0:8].T
    expert = route_t[2:4].astype(jnp.int32)
    rank = route_t[4:6].astype(jnp.int32)
    counts = cnt[0, 0:N_EXPERTS].astype(jnp.int32)
    padded = ((counts + MOE_TILE - 1) // MOE_TILE) * MOE_TILE
    ends = jnp.cumsum(padded)
    offs = ends - padded
    e_ids = jnp.arange(N_EXPERTS, dtype=jnp.int32)[:, None, None]
    dest = rank + jnp.sum(jnp.where(expert[None] == e_ids, offs[:, None, None], 0), axis=0)
    n_rows = 2 * T + N_EXPERTS * MOE_TILE
    n_tiles = n_rows // MOE_TILE
    n_used = (ends[-1] // MOE_TILE).astype(jnp.int32)
    tile_start = jnp.arange(n_tiles, dtype=jnp.int32) * MOE_TILE
    tile_expert = jnp.sum((ends[None, :] <= tile_start[:, None]).astype(jnp.int32), axis=1)
    last_expert = tile_expert[jnp.maximum(n_used - 1, 0)]
    tile_expert = jnp.where(tile_start < ends[-1], tile_expert, last_expert)
    row_token = jnp.zeros((n_rows,), jnp.int32).at[dest.reshape(-1)].set(
        jnp.tile(jnp.arange(T, dtype=jnp.int32), 2))
    real_end = offs + counts
    tile_rows = jnp.sum(jnp.where(tile_expert[:, None] == jnp.arange(N_EXPERTS, dtype=jnp.int32)[None, :],
                                  real_end[None, :], 0), axis=1) - tile_start
    tile_rows = jnp.where(tile_start < ends[-1], jnp.clip(tile_rows, 0, MOE_TILE), 0).astype(jnp.int32)
    ys = moe_experts(h, row_token, tile_expert, tile_rows, n_used.reshape(1), w_gate_up, w_down, layer)
    return moe_combine(x, ys, route, dest)


def _in_proj_kernel(a_ref, wt_hbm, o_ref, wbuf, sem, *, layer, row0, tn, sigmoid_out):
    i, j = pl.program_id(0), pl.program_id(1)
    ni, nj = pl.num_programs(0), pl.num_programs(1)
    step = i * nj + j
    slot = step % 2

    def fetch(jj, s):
        start = pl.multiple_of(row0 + jj * tn, SUBLANES)
        return pltpu.make_async_copy(wt_hbm.at[layer, pl.ds(start, tn), :], wbuf.at[s], sem.at[s])

    @pl.when(step == 0)
    def _():
        fetch(0, 0).start()

    @pl.when(step + 1 < ni * nj)
    def _():
        fetch((j + 1) % nj, 1 - slot).start()

    fetch(j, slot).wait()
    w = wbuf[slot].astype(BF16)
    acc = lax.dot_general(a_ref[...], w, NT, preferred_element_type=F32)
    if sigmoid_out:
        acc = _sigmoid(acc)
    o_ref[...] = acc.astype(o_ref.dtype)


def in_proj(h, wt, layer, row0, n, tn, name, tm=2048, sigmoid_out=False):
    T, K = h.shape
    assert n % tn == 0 and row0 % SUBLANES == 0 and tn % SUBLANES == 0
    return pl.pallas_call(
        functools.partial(_in_proj_kernel, layer=layer, row0=row0, tn=tn, sigmoid_out=sigmoid_out),
        grid=(T // tm, n // tn),
        in_specs=[pl.BlockSpec((tm, K), lambda i, j: (i, 0)), pl.BlockSpec(memory_space=pl.ANY)],
        out_specs=pl.BlockSpec((tm, tn), lambda i, j: (i, j)),
        out_shape=jax.ShapeDtypeStruct((T, n), BF16 if sigmoid_out else F32),
        scratch_shapes=[pltpu.VMEM((2, tn, K), F32), pltpu.SemaphoreType.DMA((2,))],
        compiler_params=_cparams(("arbitrary", "arbitrary")),
        name=name,
    )(h, wt)


def kernel(x, norm_mix, w_in, cmp_pe, cmp_w1, cmp_w2, q_norm, k_norm, hg_lb_logits, hg_norm, gla_w_a2, gla_b_a,
           gla_norm, w_branch, w_out, norm_ffn, w_grp, b_grp, w_exp, b_exp, w_gate_up, w_down):
    B, T, D = x.shape
    assert B == 1 and D == D_MODEL
    xt = x[0]
    p_lb = jax.nn.softmax(hg_lb_logits.astype(F32), axis=0)
    lower_bounds = jnp.cumsum(p_lb, axis=0) - p_lb[0]
    wt = jnp.swapaxes(w_in, 1, 2)
    for l in range(DEPTH):
        h = rmsnorm_bf16(xt, norm_mix[l])
        proj_n = in_proj(h, wt, l, W_NSA[0], W_NSA[1], 512, "in_proj_nsa")
        proj_r = in_proj(h, wt, l, W_REC[0], W_REC[1], 512, "in_proj_rec")
        proj_a = in_proj(h, wt, l, W_GLA_A_BLOCK[0], W_GLA_A_BLOCK[1], LANES, "in_proj_gla_a")
        proj_m = in_proj(h, wt, l, W_MERGE[0], W_MERGE[1], 512, "in_proj_merge")
        o_cmp, o_slc, o_win = nsa_mixer(proj_n, cmp_pe[l], cmp_w1[l], cmp_w2[l], q_norm[l], k_norm[l])
        o_b = hgrn2_mixer(proj_r, lower_bounds[l], hg_norm[l])
        o_c = gla_mixer(proj_r, proj_a, gla_w_a2[l], gla_b_a[l], gla_norm[l])
        merged = merge_branches(proj_m, o_cmp, o_slc, o_win, o_b, o_c, w_branch[l].astype(BF16))
        xt = matmul_bf16(merged, w_out[l].astype(BF16), res=xt, tm=1024, tn=512, name="out_proj")
        xt = hierarchical_moe(xt, norm_ffn[l], w_grp[l], b_grp[l], w_exp[l], b_exp[l], w_gate_up, w_down, l)
    return xt[None]
```

```python
import functools
import math

import numpy as np
import jax
import jax.numpy as jnp
from jax import lax
from jax.experimental import pallas as pl
from jax.experimental.pallas import tpu as pltpu

F32 = jnp.float32
BF16 = jnp.bfloat16
HIGHEST = lax.Precision.HIGHEST

D_MODEL = 2048
DEPTH = 2
BRANCH_WIDTH = D_MODEL // 4
NSA_HEADS = 8
NSA_KV_HEADS = 2
NSA_REP = NSA_HEADS // NSA_KV_HEADS
NSA_HD = 64
CMP_LEN = 32
CMP_STRIDE = 16
SLC_LEN = 64
SLC_TOP = 16
WINDOW = 512
HG_HEADS = 4
GLA_HEADS = 4
GLA_DK = 64
GLA_RANK = 16
GLA_TAU = 16.0
CHUNK = 64
SUB = 16
N_GROUPS = 4
EXPERTS_PER_GROUP = 8
N_EXPERTS = N_GROUPS * EXPERTS_PER_GROUP
D_FF_EXPERT = D_MODEL // 4
EPS = 1e-6
NEG = -1e30
FORCE = 1e4
TINY = 1e-30

LANES = 128
SUBLANES = 8
VMEM_LIMIT = 56 * 1024 * 1024

W_NSA = (0, 1536)
W_REC = (1304, 3584)
W_MERGE = (4904, 6144)
W_GLA_A_BLOCK = (4864, 128)
C_NSA_KV = 512
C_NSA_GATE = 1280
C_GLA_QK = 2048
C_GLA_VR = 2560
TAIL_GLA_A0 = 4888 - W_GLA_A_BLOCK[0]

NT = (((1,), (1,)), ((), ()))
TN = (((0,), (0,)), ((), ()))


def _cparams(sem):
    return pltpu.CompilerParams(dimension_semantics=sem, vmem_limit_bytes=VMEM_LIMIT)


def _sigmoid(x):
    return 1.0 / (1.0 + jnp.exp(-x))


def _norm_kernel(x_ref, g_ref, o_ref):
    x = x_ref[...]
    ms = jnp.mean(x * x, axis=-1, keepdims=True)
    o_ref[...] = (x * lax.rsqrt(ms + EPS) * g_ref[...]).astype(o_ref.dtype)


def rmsnorm_bf16(x, gain, tm=512):
    T, D = x.shape
    return pl.pallas_call(
        _norm_kernel,
        grid=(T // tm,),
        in_specs=[pl.BlockSpec((tm, D), lambda i: (i, 0)), pl.BlockSpec((1, D), lambda i: (0, 0))],
        out_specs=pl.BlockSpec((tm, D), lambda i: (i, 0)),
        out_shape=jax.ShapeDtypeStruct((T, D), BF16),
        compiler_params=_cparams(("parallel",)),
        name="rmsnorm_bf16",
    )(x, gain.reshape(1, D))


def _mm_kernel(a_ref, b_ref, o_ref):
    o_ref[...] = jnp.dot(a_ref[...], b_ref[...], preferred_element_type=F32)


def _mm_res_kernel(a_ref, b_ref, r_ref, o_ref):
    o_ref[...] = r_ref[...] + jnp.dot(a_ref[...], b_ref[...], preferred_element_type=F32)


def matmul_bf16(a, b, res=None, tm=1024, tn=512, name="matmul_bf16"):
    T, K = a.shape
    N = b.shape[1]
    in_specs = [pl.BlockSpec((tm, K), lambda i, j: (i, 0)), pl.BlockSpec((K, tn), lambda i, j: (0, j))]
    args = [a, b]
    kern = _mm_kernel
    if res is not None:
        in_specs.append(pl.BlockSpec((tm, tn), lambda i, j: (i, j)))
        args.append(res)
        kern = _mm_res_kernel
    return pl.pallas_call(
        kern,
        grid=(T // tm, N // tn),
        in_specs=in_specs,
        out_specs=pl.BlockSpec((tm, tn), lambda i, j: (i, j)),
        out_shape=jax.ShapeDtypeStruct((T, N), F32),
        compiler_params=_cparams(("parallel", "arbitrary")),
        name=name,
    )(*args)


def _half_rmsnorm(x, gain2, lo):
    x2 = x * x
    s_lo = jnp.sum(jnp.where(lo, x2, 0.0), axis=1, keepdims=True)
    s_hi = jnp.sum(jnp.where(lo, 0.0, x2), axis=1, keepdims=True)
    ms = jnp.where(lo, s_lo, s_hi) * (1.0 / NSA_HD)
    return x * lax.rsqrt(ms + EPS) * gain2


def _nsa_prep_kernel(q_ref, kvc_ref, kvs_ref, kvw_ref, tail_ref, qg_ref, kg_ref,
                     qn_ref, kaug_ref, vs_ref, kw_ref, vw_ref, gate_ref, *, tm):
    i = pl.program_id(0)
    lane = lax.broadcasted_iota(jnp.int32, (tm, LANES), 1)
    lo = lane < NSA_HD
    qg = qg_ref[...]
    for c in range(4):
        x = q_ref[:, c * LANES:(c + 1) * LANES]
        qn_ref[:, c * LANES:(c + 1) * LANES] = (_half_rmsnorm(x, qg, lo) * (NSA_HD ** -0.5)).astype(BF16)

    def dup(kn):
        rolled = pltpu.roll(kn, NSA_HD, axis=1)
        return jnp.where(lo, kn, rolled), jnp.where(lo, rolled, kn)

    vt_zero = jnp.zeros((NSA_HD, tm), BF16)

    def store_vt(v_pair, ref):
        vt = jnp.transpose(v_pair)
        for g in range(NSA_KV_HEADS):
            head = vt[g * NSA_HD:(g + 1) * NSA_HD, :].astype(BF16)
            ref[g, 0:NSA_HD, :] = head
            ref[g, NSA_HD:LANES, :] = vt_zero
            ref[g, LANES:LANES + NSA_HD, :] = vt_zero
            ref[g, LANES + NSA_HD:2 * LANES, :] = head

    row = i * tm + lax.broadcasted_iota(jnp.int32, (tm, LANES), 0)
    onehot = jnp.where(row // SLC_LEN == lane, 1.0, 0.0).astype(BF16)
    k_extra = jnp.where(lane == NSA_HD, (row % SLC_LEN).astype(F32), 0.0)
    ks = dup(_half_rmsnorm(kvs_ref[:, 0:LANES], kg_ref[1:2, :], lo))
    store_vt(kvs_ref[:, LANES:2 * LANES], vs_ref)
    kw = dup(_half_rmsnorm(kvw_ref[:, 0:LANES], kg_ref[2:3, :], lo))
    store_vt(kvw_ref[:, LANES:2 * LANES], vw_ref)
    for g in range(NSA_KV_HEADS):
        kaug_ref[g, :, 0:LANES] = onehot
        kaug_ref[g, :, LANES:2 * LANES] = jnp.where(lo, ks[g], k_extra).astype(BF16)
        kw_ref[g] = kw[g].astype(BF16)
    gate_ref[...] = _sigmoid(tail_ref[...])


def nsa_prep(proj_n, q_gain, k_gain, tm=512):
    T = proj_n.shape[0]
    qg2 = jnp.tile(q_gain.reshape(1, NSA_HD), (1, 2))
    kg2 = jnp.tile(k_gain.reshape(3, NSA_HD), (1, 2))
    G = NSA_KV_HEADS
    out_shape = (
        jax.ShapeDtypeStruct((T, 512), BF16),
        jax.ShapeDtypeStruct((G, T, 256), BF16),
        jax.ShapeDtypeStruct((G, 256, T), BF16),
        jax.ShapeDtypeStruct((G, T, 128), BF16),
        jax.ShapeDtypeStruct((G, 256, T), BF16),
        jax.ShapeDtypeStruct((T, 128), F32),
    )
    return pl.pallas_call(
        functools.partial(_nsa_prep_kernel, tm=tm),
        grid=(T // tm,),
        in_specs=[
            pl.BlockSpec((tm, 512), lambda i: (i, 0)),
            pl.BlockSpec((tm, 256), lambda i: (i, C_NSA_KV // 256)),
            pl.BlockSpec((tm, 256), lambda i: (i, C_NSA_KV // 256 + 1)),
            pl.BlockSpec((tm, 256), lambda i: (i, C_NSA_KV // 256 + 2)),
            pl.BlockSpec((tm, 128), lambda i: (i, C_NSA_GATE // 128)),
            pl.BlockSpec((1, 128), lambda i: (0, 0)),
            pl.BlockSpec((3, 128), lambda i: (0, 0)),
        ],
        out_specs=(
            pl.BlockSpec((tm, 512), lambda i: (i, 0)),
            pl.BlockSpec((G, tm, 256), lambda i: (0, i, 0)),
            pl.BlockSpec((G, 256, tm), lambda i: (0, 0, i)),
            pl.BlockSpec((G, tm, 128), lambda i: (0, i, 0)),
            pl.BlockSpec((G, 256, tm), lambda i: (0, 0, i)),
            pl.BlockSpec((tm, 128), lambda i: (i, 0)),
        ),
        out_shape=out_shape,
        compiler_params=_cparams(("parallel",)),
        name="nsa_prep",
    )(proj_n, proj_n, proj_n, proj_n, proj_n, qg2, kg2)


def _gelu_tanh(x):
    c = math.sqrt(2.0 / math.pi)
    return 0.5 * x * (1.0 + jnp.tanh(c * (x + 0.044715 * (x * x * x))))


def _compress_kernel(a_ref, pe_ref, w1_ref, w2_ref, kg_ref, o_ref, *, nb):
    kind = pl.program_id(0)
    a_lo = a_ref[0, 0, 0:nb, :]
    a_hi = a_ref[0, 0, 1:nb + 1, :]
    blocks = jnp.concatenate([a_lo, a_hi], axis=1) + pe_ref[0]
    h1 = jnp.dot(blocks.astype(BF16), w1_ref[0].astype(BF16), preferred_element_type=F32)
    y = jnp.dot(_gelu_tanh(h1).astype(BF16), w2_ref[0].astype(BF16), preferred_element_type=F32)
    ms = jnp.mean(y * y, axis=-1, keepdims=True)
    yn = y * lax.rsqrt(ms + EPS) * kg_ref[...]
    o_ref[0, 0] = jnp.where(kind == 0, yn, y)


def nsa_compress(kv_cmp, cmp_pe, cmp_w1, cmp_w2, k_gain0):
    T = kv_cmp.shape[0]
    nb = T // CMP_STRIDE
    G = NSA_KV_HEADS
    a = kv_cmp.reshape(T, 2, G, NSA_HD).transpose(1, 2, 0, 3).reshape(2, G, nb, CMP_STRIDE * NSA_HD)
    a = jnp.pad(a, ((0, 0), (0, 0), (0, 8), (0, 0)))
    pe = cmp_pe.reshape(2, 1, CMP_LEN * NSA_HD)
    return pl.pallas_call(
        functools.partial(_compress_kernel, nb=nb),
        grid=(2, G),
        in_specs=[
            pl.BlockSpec((1, 1, nb + 8, CMP_STRIDE * NSA_HD), lambda k, g: (k, g, 0, 0)),
            pl.BlockSpec((1, 1, CMP_LEN * NSA_HD), lambda k, g: (k, 0, 0)),
            pl.BlockSpec((1, CMP_LEN * NSA_HD, NSA_HD), lambda k, g: (k, 0, 0)),
            pl.BlockSpec((1, NSA_HD, NSA_HD), lambda k, g: (k, 0, 0)),
            pl.BlockSpec((1, NSA_HD), lambda k, g: (0, 0)),
        ],
        out_specs=pl.BlockSpec((1, 1, nb, NSA_HD), lambda k, g: (k, g, 0, 0)),
        out_shape=jax.ShapeDtypeStruct((2, G, nb, NSA_HD), F32),
        compiler_params=_cparams(("arbitrary", "arbitrary")),
        name="nsa_compress",
    )(a, pe, cmp_w1, cmp_w2, k_gain0.reshape(1, NSA_HD))


def _slope(g, r):
    return jnp.where(g == 0, 2.0 ** -(r + 1), 2.0 ** -(NSA_REP + r + 1)).astype(F32)


def _gate_pair(gates, lane, lo, col_even):
    ge = jnp.sum(jnp.where(lane == col_even, gates, 0.0), axis=1, keepdims=True)
    go = jnp.sum(jnp.where(lane == col_even + 1, gates, 0.0), axis=1, keepdims=True)
    return jnp.where(lo, ge, go)


def _cmp_sel_kernel(q_ref, kc_ref, vc_ref, gate_ref, mt_ref, o_ref, sel_ref, *, tq, nb, ns_pad):
    g = pl.program_id(0)
    qi = pl.program_id(1)
    t0 = qi * tq
    lane = lax.broadcasted_iota(jnp.int32, (tq, LANES), 1)
    lo = lane < NSA_HD
    n_idx = lax.broadcasted_iota(jnp.int32, (nb, tq), 0)
    t_idx = t0 + lax.broadcasted_iota(jnp.int32, (nb, tq), 1)
    dist = (t_idx - (n_idx * CMP_STRIDE + (CMP_LEN - 1))).astype(F32)
    vis = dist >= 0.0
    kc = kc_ref[0]
    imp = jnp.zeros((nb, tq), F32)
    gates = gate_ref[...]
    for a in range(2):
        acc = jnp.zeros((tq, LANES), F32)
        for half in range(2):
            r = 2 * a + half
            qh = q_ref[:, a * LANES:(a + 1) * LANES]
            qh = jnp.where(lo if half == 0 else jnp.logical_not(lo), qh, jnp.zeros_like(qh))
            s = lax.dot_general(kc, qh, NT, preferred_element_type=F32)
            s = jnp.where(vis, s - _slope(g, r) * dist, NEG)
            mx = jnp.max(s, axis=0, keepdims=True)
            e = jnp.where(vis, jnp.exp(s - mx), 0.0)
            den = jnp.sum(e, axis=0, keepdims=True)
            p = e * jnp.where(den > 0.0, 1.0 / den, 0.0)
            imp = imp + p
            v = vc_ref[0, :, half * LANES:(half + 1) * LANES]
            acc = acc + lax.dot_general(p.astype(BF16), v, TN, preferred_element_type=F32)
        gp = _gate_pair(gates, lane, lo, g * NSA_REP + 2 * a)
        o_ref[:, a * LANES:(a + 1) * LANES] = acc * gp

    score_all = jnp.dot(mt_ref[...], imp, precision=HIGHEST, preferred_element_type=F32)
    blk = lax.broadcasted_iota(jnp.int32, (ns_pad, LANES), 0)
    for c in range(tq // LANES):
        cur = (t0 + c * LANES + lax.broadcasted_iota(jnp.int32, (ns_pad, LANES), 1)) // SLC_LEN
        forced = (blk == 0) | (blk == cur) | (blk == cur - 1)
        score = jnp.where(forced, FORCE, score_all[:, c * LANES:(c + 1) * LANES])
        score = jnp.where(blk <= cur, score, -1.0)
        for _ in range(SLC_TOP):
            mx = jnp.max(score, axis=0, keepdims=True)
            idx = jnp.min(jnp.where(score == mx, blk, ns_pad), axis=0, keepdims=True)
            score = jnp.where(blk == idx, -jnp.inf, score)
        bias_t = jnp.where((score == -jnp.inf) & (blk <= cur), 0.0, NEG)
        sel_ref[0, c * LANES:(c + 1) * LANES, :] = jnp.transpose(bias_t).astype(BF16)


def _score_matrix(nb, ns_pad):
    ratio, span = SLC_LEN // CMP_STRIDE, CMP_LEN // CMP_STRIDE
    n_cmp = nb - 1
    m = np.zeros((ns_pad, nb), np.float32)
    for s in range(nb // ratio):
        for mm in range(ratio):
            for nn in range(span):
                c = ratio * s + mm - nn
                if 0 <= c < n_cmp:
                    m[s, c] += 1.0
    return jnp.asarray(m)


def nsa_cmp_select(qn, kc2, vc_lohi, gates, tq=256):
    T = qn.shape[0]
    nb = T // CMP_STRIDE
    ns_pad = LANES
    G = NSA_KV_HEADS
    return pl.pallas_call(
        functools.partial(_cmp_sel_kernel, tq=tq, nb=nb, ns_pad=ns_pad),
        grid=(G, T // tq),
        in_specs=[
            pl.BlockSpec((tq, 256), lambda g, i: (i, g)),
            pl.BlockSpec((1, nb, 128), lambda g, i: (g, 0, 0)),
            pl.BlockSpec((1, nb, 256), lambda g, i: (g, 0, 0)),
            pl.BlockSpec((tq, 128), lambda g, i: (i, 0)),
            pl.BlockSpec((ns_pad, nb), lambda g, i: (0, 0)),
        ],
        out_specs=(
            pl.BlockSpec((tq, 256), lambda g, i: (i, g)),
            pl.BlockSpec((1, tq, ns_pad), lambda g, i: (g, i, 0)),
        ),
        out_shape=(jax.ShapeDtypeStruct((T, 512), F32), jax.ShapeDtypeStruct((G, T, ns_pad), BF16)),
        compiler_params=_cparams(("parallel", "parallel")),
        name="nsa_cmp_select",
    )(qn, kc2, vc_lohi, gates, _score_matrix(nb, ns_pad))


KV_SLOTS = 3


def _sel_attn_kernel(tile_tab, cnt_tab, q_ref, sb_ref, gate_ref, k_hbm, v_hbm, o_ref,
                     qaug, kbuf, vbuf, sem, m_scr, l_scr, acc_scr, *, tq, tk, max_tiles):
    g = pl.program_id(0)
    qi = pl.program_id(1)
    row = g * pl.num_programs(1) + qi
    n_tiles = cnt_tab[row]
    base = row * max_tiles
    lane = lax.broadcasted_iota(jnp.int32, (tq, LANES), 1)
    lo = lane < NSA_HD

    def copies(j, slot):
        start = pl.multiple_of(tile_tab[base + j] * tk, tk)
        return (pltpu.make_async_copy(k_hbm.at[g, pl.ds(start, tk), :], kbuf.at[slot], sem.at[0, slot]),
                pltpu.make_async_copy(v_hbm.at[g, :, pl.ds(start, tk)], vbuf.at[slot], sem.at[1, slot]))

    def start_fetch(j, slot):
        for cp in copies(j, slot):
            cp.start()

    def wait_fetch(j, slot):
        for cp in copies(j, slot):
            cp.wait()

    start_fetch(0, 0)

    @pl.when(n_tiles > 1)
    def _():
        start_fetch(1, 1)

    sb = sb_ref[0].astype(F32)
    blk_rel = (lane - qi * (tq // SLC_LEN)).astype(F32)
    for r in range(NSA_REP):
        a, half = r // 2, r % 2
        slope = _slope(g, r)
        qh = q_ref[:, a * LANES:(a + 1) * LANES].astype(F32)
        if half:
            qh = pltpu.roll(qh, NSA_HD, axis=1)
        qh = jnp.where(lo, qh, jnp.where(lane == NSA_HD, slope, 0.0))
        qaug[r * tq:(r + 1) * tq, 0:LANES] = (sb + (slope * SLC_LEN) * blk_rel).astype(BF16)
        qaug[r * tq:(r + 1) * tq, LANES:2 * LANES] = qh.astype(BF16)
    m_scr[...] = jnp.full(m_scr.shape, NEG, F32)
    l_scr[...] = jnp.zeros(l_scr.shape, F32)
    acc_scr[...] = jnp.zeros(acc_scr.shape, F32)

    upper = lax.broadcasted_iota(jnp.int32, (LANES, tq), 0) < NSA_HD
    kj_diag = (qi * tq + (tq - 1)) // tk

    def step(masked, slot):
        if masked:
            kpos = kj_diag * tk + lax.broadcasted_iota(jnp.int32, (tk, tq), 0)
            qpos = qi * tq + lax.broadcasted_iota(jnp.int32, (tk, tq), 1)
            causal = kpos <= qpos
        k = kbuf[slot]
        for a in range(2):
            pv = jnp.zeros((LANES, tq), F32)
            alphas = []
            for half in range(2):
                r = 2 * a + half
                sr = lax.dot_general(k, qaug[r * tq:(r + 1) * tq, :], NT, preferred_element_type=F32)
                if masked:
                    sr = jnp.where(causal, sr, NEG)
                m_prev = m_scr[r:r + 1, :]
                m_new = jnp.maximum(m_prev, jnp.max(sr, axis=0, keepdims=True))
                alpha = jnp.exp(m_prev - m_new)
                p = jnp.exp(sr - m_new)
                l_scr[r:r + 1, :] = alpha * l_scr[r:r + 1, :] + jnp.sum(p, axis=0, keepdims=True)
                m_scr[r:r + 1, :] = m_new
                vt = vbuf[slot, half * LANES:(half + 1) * LANES, :]
                pv = pv + jnp.dot(vt, p.astype(BF16), preferred_element_type=F32)
                alphas.append(alpha)
            alpha_pair = jnp.where(upper, alphas[0], alphas[1])
            acc_scr[a * LANES:(a + 1) * LANES, :] = alpha_pair * acc_scr[a * LANES:(a + 1) * LANES, :] + pv

    def off_diagonal(j, carry):
        @pl.when(j + 2 < n_tiles)
        def _():
            start_fetch(j + 2, (j + 2) % KV_SLOTS)

        wait_fetch(j, j % KV_SLOTS)
        step(False, j % KV_SLOTS)
        return carry

    lax.fori_loop(0, n_tiles - 1, off_diagonal, 0)
    last = n_tiles - 1
    wait_fetch(last, last % KV_SLOTS)
    step(True, last % KV_SLOTS)

    gates = gate_ref[...]
    for a in range(2):
        l_pair = jnp.where(upper, l_scr[2 * a:2 * a + 1, :], l_scr[2 * a + 1:2 * a + 2, :])
        gp = _gate_pair(gates, lane, lo, NSA_HEADS + g * NSA_REP + 2 * a)
        o_t = acc_scr[a * LANES:(a + 1) * LANES, :] / l_pair
        o_ref[:, a * LANES:(a + 1) * LANES] = jnp.transpose(o_t) * gp


def nsa_selected_attention(qn, selbias, kaug, vs_lohi, gates, tq=256, tk=512):
    T = qn.shape[0]
    G = NSA_KV_HEADS
    nq, nkt, nkb = T // tq, T // tk, tk // SLC_LEN
    chosen = (selbias.astype(F32) > 0.5 * NEG).reshape(G, nq, tq, LANES // nkb, nkb)
    tile_any = jnp.any(chosen, axis=(2, 4))[:, :, 0:nkt]
    kj = jnp.arange(nkt, dtype=jnp.int32)[None, None, :]
    kj_diag = ((jnp.arange(nq, dtype=jnp.int32) * tq + (tq - 1)) // tk)[None, :, None]
    active = (tile_any & (kj < kj_diag)) | (kj == kj_diag)
    csum = jnp.cumsum(active.astype(jnp.int32), axis=2)
    counts = csum[:, :, -1]
    tiles = jnp.sum((csum[:, :, None, :] <= kj[..., None]).astype(jnp.int32), axis=3)
    tiles = jnp.minimum(tiles, nkt - 1)
    grid_spec = pltpu.PrefetchScalarGridSpec(
        num_scalar_prefetch=2,
        grid=(G, nq),
        in_specs=[
            pl.BlockSpec((tq, 256), lambda g, i, tt, ct: (i, g)),
            pl.BlockSpec((1, tq, 128), lambda g, i, tt, ct: (g, i, 0)),
            pl.BlockSpec((tq, 128), lambda g, i, tt, ct: (i, 0)),
            pl.BlockSpec(memory_space=pl.ANY),
            pl.BlockSpec(memory_space=pl.ANY),
        ],
        out_specs=pl.BlockSpec((tq, 256), lambda g, i, tt, ct: (i, g)),
        scratch_shapes=[
            pltpu.VMEM((NSA_REP * tq, 256), BF16),
            pltpu.VMEM((KV_SLOTS, tk, 256), BF16),
            pltpu.VMEM((KV_SLOTS, 256, tk), BF16),
            pltpu.SemaphoreType.DMA((2, KV_SLOTS)),
            pltpu.VMEM((NSA_REP, tq), F32),
            pltpu.VMEM((NSA_REP, tq), F32),
            pltpu.VMEM((2 * LANES, tq), F32),
        ],
    )
    return pl.pallas_call(
        functools.partial(_sel_attn_kernel, tq=tq, tk=tk, max_tiles=nkt),
        grid_spec=grid_spec,
        out_shape=jax.ShapeDtypeStruct((T, 512), F32),
        compiler_params=_cparams(("parallel", "parallel")),
        name="nsa_selected_attention",
    )(tiles.reshape(-1).astype(jnp.int32), counts.reshape(-1).astype(jnp.int32),
      qn, selbias, gates, kaug, vs_lohi)


def _win_attn_kernel(q_ref, k0_ref, k1_ref, k2_ref, v0_ref, v1_ref, v2_ref, gate_ref, o_ref, *, tq):
    g = pl.program_id(0)
    qi = pl.program_id(1)
    lane = lax.broadcasted_iota(jnp.int32, (tq, LANES), 1)
    lo = lane < NSA_HD
    upper = lax.broadcasted_iota(jnp.int32, (LANES, tq), 0) < NSA_HD
    row = lax.broadcasted_iota(jnp.int32, (tq, tq), 0)
    qpos = qi * tq + lax.broadcasted_iota(jnp.int32, (tq, tq), 1)
    k_refs = (k0_ref, k1_ref, k2_ref)
    v_refs = (v0_ref, v1_ref, v2_ref)
    dists, masks = [], []
    for d in range(3):
        kpos = (qi - 2 + d) * tq + row
        dd = qpos - kpos
        dists.append(dd.astype(F32))
        masks.append((dd >= 0) & (dd < WINDOW) & (kpos >= 0))
    gates = gate_ref[...]
    for a in range(2):
        pv = jnp.zeros((LANES, tq), F32)
        ls = []
        for half in range(2):
            r = 2 * a + half
            qh = q_ref[:, a * LANES:(a + 1) * LANES]
            qh = jnp.where(lo if half == 0 else jnp.logical_not(lo), qh, jnp.zeros_like(qh))
            ss = []
            for d in range(3):
                sd = lax.dot_general(k_refs[d][0], qh, NT, preferred_element_type=F32)
                ss.append(jnp.where(masks[d], sd - _slope(g, r) * dists[d], NEG))
            mx = jnp.maximum(jnp.maximum(jnp.max(ss[0], axis=0, keepdims=True), jnp.max(ss[1], axis=0, keepdims=True)),
                             jnp.max(ss[2], axis=0, keepdims=True))
            l = jnp.zeros((1, tq), F32)
            for d in range(3):
                p = jnp.exp(ss[d] - mx)
                l = l + jnp.sum(p, axis=0, keepdims=True)
                vt = v_refs[d][0, half * LANES:(half + 1) * LANES, :]
                pv = pv + jnp.dot(vt, p.astype(BF16), preferred_element_type=F32)
            ls.append(l)
        l_pair = jnp.where(upper, ls[0], ls[1])
        gp = _gate_pair(gates, lane, lo, 2 * NSA_HEADS + g * NSA_REP + 2 * a)
        o_ref[:, a * LANES:(a + 1) * LANES] = jnp.transpose(pv / l_pair) * gp


def nsa_window_attention(qn, kw2, vw_lohi, gates, tq=256):
    T = qn.shape[0]
    G = NSA_KV_HEADS
    assert WINDOW == 2 * tq

    def kspec(d):
        return pl.BlockSpec((1, tq, 128), lambda g, i, d=d: (g, jnp.maximum(i - 2 + d, 0), 0))

    def vspec(d):
        return pl.BlockSpec((1, 256, tq), lambda g, i, d=d: (g, 0, jnp.maximum(i - 2 + d, 0)))

    return pl.pallas_call(
        functools.partial(_win_attn_kernel, tq=tq),
        grid=(G, T // tq),
        in_specs=[pl.BlockSpec((tq, 256), lambda g, i: (i, g)),
                  kspec(0), kspec(1), kspec(2),
                  vspec(0), vspec(1), vspec(2),
                  pl.BlockSpec((tq, 128), lambda g, i: (i, 0))],
        out_specs=pl.BlockSpec((tq, 256), lambda g, i: (i, g)),
        out_shape=jax.ShapeDtypeStruct((T, 512), F32),
        compiler_params=_cparams(("parallel", "parallel")),
        name="nsa_window_attention",
    )(qn, kw2, kw2, kw2, vw_lohi, vw_lohi, vw_lohi, gates)


def nsa_mixer(proj_n, cmp_pe, cmp_w1, cmp_w2, q_gain, k_gain, parts=False):
    qn, kaug, vs_lohi, kw2, vw_lohi, gates = nsa_prep(proj_n, q_gain, k_gain)
    kvc = nsa_compress(proj_n[:, C_NSA_KV:C_NSA_KV + 256], cmp_pe, cmp_w1, cmp_w2, k_gain[0])
    kc, vc = kvc[0], kvc[1]
    kc2 = jnp.concatenate([kc, kc], axis=-1).astype(BF16)
    zero = jnp.zeros_like(vc)
    vc_lohi = jnp.concatenate([vc, zero, zero, vc], axis=-1).astype(BF16)
    o_cmp, selbias = nsa_cmp_select(qn, kc2, vc_lohi, gates)
    o_slc = nsa_selected_attention(qn, selbias, kaug, vs_lohi, gates)
    o_win = nsa_window_attention(qn, kw2, vw_lohi, gates)
    if parts:
        return o_cmp + o_slc + o_win, (o_cmp, o_slc, o_win)
    return o_cmp, o_slc, o_win


def _diag_selector():
    m = np.zeros((SUB * LANES, LANES), np.float32)
    for j in range(SUB):
        for rep in range(CHUNK // SUB):
            m[j * LANES:(j + 1) * LANES, rep * SUB + j] = 1.0
    return jnp.asarray(m, dtype=BF16)


LOG2E = 1.0 / math.log(2.0)


def _recur_tile(q_scr, k_scr, v_scr, g_scr, o_scr, st_scr, sel_ref, tb):
    row = lax.broadcasted_iota(jnp.int32, (CHUNK, LANES), 0)
    lane = lax.broadcasted_iota(jnp.int32, (CHUNK, LANES), 1)
    sub_row = row % SUB
    blockdiag = (row // SUB) == (lane // SUB)
    r64 = lax.broadcasted_iota(jnp.int32, (CHUNK, CHUNK), 0)
    c64 = lax.broadcasted_iota(jnp.int32, (CHUNK, CHUNK), 1)
    ltri = jnp.where(r64 >= c64, 1.0, 0.0).astype(F32)
    c_sub = lax.broadcasted_iota(jnp.int32, (SUB, CHUNK), 1)
    nsub = CHUNK // SUB
    half = SUB // 2
    low_row = half + lax.broadcasted_iota(jnp.int32, (nsub, half, LANES), 1)

    def head_chunk(h, r0):
        qc = q_scr[h, pl.ds(r0, CHUNK), :]
        kc = k_scr[h, pl.ds(r0, CHUNK), :]
        vc = v_scr[h, pl.ds(r0, CHUNK), :]
        gc = g_scr[h, pl.ds(r0, CHUNK), :]
        b = jnp.dot(ltri, gc, precision=HIGHEST, preferred_element_type=F32)
        bend = b[CHUNK - 1:CHUNK, :]
        st = st_scr[h]
        o = lax.dot_general((qc * jnp.exp2(b)).astype(BF16), st.astype(BF16), NT, preferred_element_type=F32)
        k4 = kc.reshape(nsub, SUB, LANES)
        b4 = b.reshape(nsub, SUB, LANES)
        pieces = []
        for j in range(half):
            k_rep = jnp.broadcast_to(k4[:, j:j + 1, :], (nsub, SUB, LANES)).reshape(CHUNK, LANES)
            b_rep = jnp.broadcast_to(b4[:, j:j + 1, :], (nsub, SUB, LANES)).reshape(CHUNK, LANES)
            e = qc * k_rep * jnp.exp2(jnp.where(sub_row >= j, b - b_rep, NEG))
            pieces.append(e.astype(BF16))
        q_low = qc.reshape(nsub, SUB, LANES)[:, half:, :]
        b_low = b4[:, half:, :]
        for j in range(half, SUB):
            k_rep = jnp.broadcast_to(k4[:, j:j + 1, :], (nsub, half, LANES))
            b_rep = jnp.broadcast_to(b4[:, j:j + 1, :], (nsub, half, LANES))
            e_low = q_low * k_rep * jnp.exp2(jnp.where(low_row >= j, b_low - b_rep, NEG))
            e = jnp.concatenate([jnp.zeros_like(e_low), e_low], axis=1).reshape(CHUNK, LANES)
            pieces.append(e.astype(BF16))
        a_diag = jnp.dot(jnp.concatenate(pieces, axis=1), sel_ref[...], preferred_element_type=F32)
        a_diag = jnp.where(blockdiag, a_diag, 0.0)[:, 0:CHUNK]
        rows = [jnp.zeros((SUB, CHUNK), F32)]
        for i_sub in range(1, nsub):
            ref_b = b[i_sub * SUB - 1:i_sub * SUB, :]
            qt = qc[i_sub * SUB:(i_sub + 1) * SUB, :] * jnp.exp2(b[i_sub * SUB:(i_sub + 1) * SUB, :] - ref_b)
            kt = kc * jnp.exp2(jnp.minimum(ref_b - b, 0.0))
            a_i = lax.dot_general(qt.astype(BF16), kt.astype(BF16), NT, preferred_element_type=F32)
            rows.append(jnp.where(c_sub < i_sub * SUB, a_i, 0.0))
        a = jnp.concatenate(rows, axis=0) + a_diag
        o = o + jnp.dot(a.astype(BF16), vc.astype(BF16), preferred_element_type=F32)
        o_scr[h, pl.ds(r0, CHUNK), :] = o
        kend = kc * jnp.exp2(bend - b)
        st_scr[h] = st * jnp.exp2(bend) + lax.dot_general(vc.astype(BF16), kend.astype(BF16), TN,
                                                          preferred_element_type=F32)

    def chunk(c, carry):
        r0 = pl.multiple_of(c * CHUNK, CHUNK)
        for h in range(q_scr.shape[0]):
            head_chunk(h, r0)
        return carry

    lax.fori_loop(0, tb // CHUNK, chunk, 0)


def _finish_recur(o_scr, gain_ref, gate, o_ref):
    for h in range(o_scr.shape[0]):
        o = o_scr[h]
        ms = jnp.mean(o * o, axis=-1, keepdims=True)
        y = o * lax.rsqrt(ms + EPS) * gain_ref[...] * gate[:, h * LANES:(h + 1) * LANES]
        o_ref[:, h * LANES:(h + 1) * LANES] = y.astype(o_ref.dtype)


def _hgrn2_kernel(q_ref, f_ref, i_ref, gg_ref, lb_ref, gain_ref, sel_ref, o_ref,
                  q_scr, k_scr, v_scr, g_scr, o_scr, st_scr, *, tb):
    @pl.when(pl.program_id(1) == 0)
    def _():
        st_scr[...] = jnp.zeros(st_scr.shape, F32)

    for h in range(q_scr.shape[0]):
        cols = slice(h * LANES, (h + 1) * LANES)
        lb = lb_ref[:, cols]
        z = f_ref[:, cols]
        sg = _sigmoid(z)
        f = lb + (1.0 - lb) * sg
        q_scr[h] = q_ref[:, cols]
        k_scr[h] = (1.0 - lb) * (1.0 - sg)
        v_scr[h] = i_ref[:, cols]
        g_scr[h] = jnp.log(jnp.maximum(f, TINY)) * LOG2E
    _recur_tile(q_scr, k_scr, v_scr, g_scr, o_scr, st_scr, sel_ref, tb)
    _finish_recur(o_scr, gain_ref, _sigmoid(gg_ref[...]), o_ref)


def _recur_scratch(tb, heads):
    return ([pltpu.VMEM((heads, tb, LANES), F32) for _ in range(5)]
            + [pltpu.VMEM((heads, LANES, LANES), F32)])


def hgrn2_mixer(proj, lower_bound, norm_gain, tb=512, heads=HG_HEADS):
    T = proj.shape[0]
    wide = heads * LANES
    per = HG_HEADS // heads

    def col(k):
        return pl.BlockSpec((tb, wide), lambda hp, i, k=k: (i, per * k + hp))

    return pl.pallas_call(
        functools.partial(_hgrn2_kernel, tb=tb),
        grid=(per, T // tb),
        in_specs=[col(0), col(1), col(2), col(3),
                  pl.BlockSpec((1, wide), lambda hp, i: (0, hp)),
                  pl.BlockSpec((1, LANES), lambda hp, i: (0, 0)),
                  pl.BlockSpec((SUB * LANES, LANES), lambda hp, i: (0, 0))],
        out_specs=pl.BlockSpec((tb, wide), lambda hp, i: (i, hp)),
        out_shape=jax.ShapeDtypeStruct((T, BRANCH_WIDTH), BF16),
        scratch_shapes=_recur_scratch(tb, heads),
        compiler_params=_cparams(("parallel", "arbitrary")),
        name="hgrn2_mixer",
    )(proj, proj, proj, proj, lower_bound.reshape(1, -1), norm_gain.reshape(1, -1), _diag_selector())


def _gla_kernel(q_ref, k_ref, v_ref, r_ref, tail_ref, wa_ref, ba_ref, gain_ref, sel_ref, o_ref,
                q_scr, k_scr, v_scr, g_scr, o_scr, st_scr, *, tb):
    @pl.when(pl.program_id(1) == 0)
    def _():
        st_scr[...] = jnp.zeros(st_scr.shape, F32)

    lane = lax.broadcasted_iota(jnp.int32, (tb, LANES), 1)
    a = jnp.dot(tail_ref[...], wa_ref[...], precision=HIGHEST, preferred_element_type=F32) + ba_ref[...]
    log_sig = -(jnp.maximum(-a, 0.0) + jnp.log1p(jnp.exp(-jnp.abs(a))))
    g2 = log_sig * (LOG2E / GLA_TAU)
    for h in range(q_scr.shape[0]):
        pair = slice((h // 2) * LANES, (h // 2 + 1) * LANES)
        mine = (lane < GLA_DK) if h % 2 == 0 else (lane >= GLA_DK)
        q_scr[h] = jnp.where(mine, q_ref[:, pair] * (GLA_DK ** -0.5), 0.0)
        k_scr[h] = jnp.where(mine, k_ref[:, pair], 0.0)
        v_scr[h] = v_ref[:, h * LANES:(h + 1) * LANES]
        g_scr[h] = jnp.where(mine, g2[:, pair], 0.0)
    _recur_tile(q_scr, k_scr, v_scr, g_scr, o_scr, st_scr, sel_ref, tb)
    r = r_ref[...]
    _finish_recur(o_scr, gain_ref, r * _sigmoid(r), o_ref)


def gla_mixer(proj, proj_t, w_a2, b_a, norm_gain, tb=512, heads=GLA_HEADS):
    T = proj.shape[0]
    assert heads % 2 == 0 and LANES == 2 * GLA_DK
    wide = heads * LANES
    half = heads * GLA_DK
    cqk = C_GLA_QK // half
    cvr = C_GLA_VR // wide
    per = GLA_HEADS // heads
    wa = jnp.zeros((LANES, GLA_HEADS * GLA_DK), F32).at[TAIL_GLA_A0:TAIL_GLA_A0 + GLA_RANK].set(w_a2)
    return pl.pallas_call(
        functools.partial(_gla_kernel, tb=tb),
        grid=(per, T // tb),
        in_specs=[pl.BlockSpec((tb, half), lambda hp, i: (i, cqk + hp)),
                  pl.BlockSpec((tb, half), lambda hp, i: (i, cqk + per + hp)),
                  pl.BlockSpec((tb, wide), lambda hp, i: (i, cvr + hp)),
                  pl.BlockSpec((tb, wide), lambda hp, i: (i, cvr + per + hp)),
                  pl.BlockSpec((tb, LANES), lambda hp, i: (i, 0)),
                  pl.BlockSpec((LANES, half), lambda hp, i: (0, hp)),
                  pl.BlockSpec((1, half), lambda hp, i: (0, hp)),
                  pl.BlockSpec((1, LANES), lambda hp, i: (0, 0)),
                  pl.BlockSpec((SUB * LANES, LANES), lambda hp, i: (0, 0))],
        out_specs=pl.BlockSpec((tb, wide), lambda hp, i: (i, hp)),
        out_shape=jax.ShapeDtypeStruct((T, BRANCH_WIDTH), BF16),
        scratch_shapes=_recur_scratch(tb, heads),
        compiler_params=_cparams(("parallel", "arbitrary")),
        name="gla_mixer",
    )(proj, proj, proj, proj, proj_t, wa, b_a.reshape(1, -1), norm_gain.reshape(1, -1), _diag_selector())


def _merge_kernel(oc_ref, os_ref, ow_ref, ob_ref, og_ref, wb_ref, g0_ref, g1_ref, g2_ref, o_ref):
    o_a = (oc_ref[...] + os_ref[...] + ow_ref[...]).astype(BF16)
    acc = _sigmoid(g0_ref[...]) * jnp.dot(o_a, wb_ref[0], preferred_element_type=F32)
    acc = acc + _sigmoid(g1_ref[...]) * jnp.dot(ob_ref[...], wb_ref[1], preferred_element_type=F32)
    acc = acc + _sigmoid(g2_ref[...]) * jnp.dot(og_ref[...], wb_ref[2], preferred_element_type=F32)
    o_ref[...] = acc.astype(o_ref.dtype)


def merge_branches(proj, o_cmp, o_slc, o_win, o_b, o_c, w_branch_bf16, tm=512, tn=512):
    T = proj.shape[0]
    W = BRANCH_WIDTH
    nj = D_MODEL // tn
    ospec = pl.BlockSpec((tm, W), lambda i, j: (i, 0))

    def gspec(n):
        return pl.BlockSpec((tm, tn), lambda i, j, n=n: (i, n * nj + j))

    return pl.pallas_call(
        _merge_kernel,
        grid=(T // tm, nj),
        in_specs=[ospec, ospec, ospec, ospec, ospec,
                  pl.BlockSpec((3, W, tn), lambda i, j: (0, 0, j)),
                  gspec(0), gspec(1), gspec(2)],
        out_specs=pl.BlockSpec((tm, tn), lambda i, j: (i, j)),
        out_shape=jax.ShapeDtypeStruct((T, D_MODEL), BF16),
        compiler_params=_cparams(("parallel", "arbitrary")),
        name="merge_branches",
    )(o_cmp, o_slc, o_win, o_b, o_c, w_branch_bf16, proj, proj, proj)


MOE_TILE = 512


def _route_kernel(x_ref, g_ref, wr_ref, br_ref, h_ref, route_ref, cnt_ref, carry, *, tm):
    i = pl.program_id(0)

    @pl.when(i == 0)
    def _():
        carry[...] = jnp.zeros(carry.shape, F32)

    x = x_ref[...]
    ms = jnp.mean(x * x, axis=-1, keepdims=True)
    h = x * lax.rsqrt(ms + EPS) * g_ref[...]
    h_ref[...] = h
    logits = jnp.dot(h, wr_ref[...], precision=HIGHEST, preferred_element_type=F32) + br_ref[...]
    lane = lax.broadcasted_iota(jnp.int32, (tm, LANES), 1)

    def masked_softmax(mask):
        l = jnp.where(mask, logits, NEG)
        e = jnp.where(mask, jnp.exp(l - jnp.max(l, axis=1, keepdims=True)), 0.0)
        return e / jnp.sum(e, axis=1, keepdims=True)

    def top1(prob, mask):
        p = jnp.max(jnp.where(mask, prob, -1.0), axis=1, keepdims=True)
        idx = jnp.min(jnp.where(mask & (prob == p), lane, LANES), axis=1, keepdims=True)
        return p, idx

    gmask = lane < N_GROUPS
    gw, gidx = top1(masked_softmax(gmask), gmask)
    emask = (lane >= N_GROUPS) & (lane < N_GROUPS + N_EXPERTS) & ((lane - N_GROUPS) // EXPERTS_PER_GROUP == gidx)
    eprob = masked_softmax(emask)
    p1, i1 = top1(eprob, emask)
    rest = emask & (lane != i1)
    p2, i2 = top1(eprob, rest)
    psum = p1 + p2
    w1 = gw * (p1 / psum)
    w2 = gw * (p2 / psum)
    e1 = i1 - N_GROUPS
    e2 = i2 - N_GROUPS

    onehot = jnp.where((lane == e1) | (lane == e2), 1.0, 0.0)
    r = lax.broadcasted_iota(jnp.int32, (tm, tm), 0)
    c = lax.broadcasted_iota(jnp.int32, (tm, tm), 1)
    strict = jnp.where(r > c, 1.0, 0.0).astype(BF16)
    before = jnp.dot(strict, onehot.astype(BF16), preferred_element_type=F32) + carry[0:1, :]
    rank1 = jnp.sum(jnp.where(lane == e1, before, 0.0), axis=1, keepdims=True)
    rank2 = jnp.sum(jnp.where(lane == e2, before, 0.0), axis=1, keepdims=True)
    total = carry[0:1, :] + jnp.sum(onehot, axis=0, keepdims=True)
    carry[...] = jnp.broadcast_to(total, carry.shape)
    cnt_ref[...] = jnp.broadcast_to(total, cnt_ref.shape)

    out = jnp.where(lane == 0, w1, 0.0)
    out = jnp.where(lane == 1, w2, out)
    out = jnp.where(lane == 2, e1.astype(F32), out)
    out = jnp.where(lane == 3, e2.astype(F32), out)
    out = jnp.where(lane == 4, rank1, out)
    out = jnp.where(lane == 5, rank2, out)
    route_ref[...] = out


def moe_route(x, gain, w_grp, b_grp, w_exp, b_exp, tm=512):
    T, D = x.shape
    n_pad = LANES - N_GROUPS - N_EXPERTS
    wr = jnp.concatenate([w_grp, w_exp, jnp.zeros((D, n_pad), F32)], axis=1)
    br = jnp.concatenate([b_grp, b_exp, jnp.zeros((n_pad,), F32)]).reshape(1, LANES)
    return pl.pallas_call(
        functools.partial(_route_kernel, tm=tm),
        grid=(T // tm,),
        in_specs=[pl.BlockSpec((tm, D), lambda i: (i, 0)),
                  pl.BlockSpec((1, D), lambda i: (0, 0)),
                  pl.BlockSpec((D, LANES), lambda i: (0, 0)),
                  pl.BlockSpec((1, LANES), lambda i: (0, 0))],
        out_specs=(pl.BlockSpec((tm, D), lambda i: (i, 0)),
                   pl.BlockSpec((tm, LANES), lambda i: (i, 0)),
                   pl.BlockSpec((8, LANES), lambda i: (0, 0))),
        out_shape=(jax.ShapeDtypeStruct((T, D), F32),
                   jax.ShapeDtypeStruct((T, LANES), F32),
                   jax.ShapeDtypeStruct((8, LANES), F32)),
        scratch_shapes=[pltpu.VMEM((8, LANES), F32)],
        compiler_params=_cparams(("arbitrary",)),
        name="moe_route",
    )(x, gain.reshape(1, D), wr, br)


def _row_dma(src, src_row, dst, dst_row, sem):
    return pltpu.make_async_copy(src.at[pl.ds(src_row, 1), :], dst.at[pl.ds(dst_row, 1), :], sem)


def _rows_wait(src, dst, n, sem):
    pltpu.make_async_copy(src.at[pl.ds(0, n), :], dst.at[pl.ds(0, n), :], sem).wait()


def _expert_kernel(tile_expert, tile_rows, n_used, rcur_ref, rnext_ref, h_hbm, wgu_ref, wd_ref, o_ref,
                   xbuf, wgu_bf, wd_bf, sem):
    i = pl.program_id(0)
    used = i < n_used[0]
    slot = i % 2

    def fetch(rref, s, count):
        def body(r, c):
            _row_dma(h_hbm, rref[0, 0, r], xbuf.at[s], r, sem.at[s]).start()
            return c

        lax.fori_loop(0, count, body, 0)

    def drain(s, count):
        for b in range(MOE_TILE.bit_length()):
            @pl.when((count & (1 << b)) != 0)
            def _():
                _rows_wait(h_hbm, xbuf.at[s], 1 << b, sem.at[s])

    @pl.when(i == 0)
    def _():
        xbuf[...] = jnp.zeros(xbuf.shape, F32)
        fetch(rcur_ref, 0, tile_rows[0])

    @pl.when(i + 1 < n_used[0])
    def _():
        fetch(rnext_ref, 1 - slot, tile_rows[i + 1])

    @pl.when(used)
    def _():
        prev = tile_expert[jnp.maximum(i - 1, 0)]

        @pl.when((i == 0) | (tile_expert[i] != prev))
        def _():
            wgu_bf[...] = wgu_ref[0, 0].astype(BF16)
            wd_bf[...] = wd_ref[0, 0].astype(BF16)

        drain(slot, tile_rows[i])
        x = xbuf[slot].astype(BF16)
        gu = jnp.dot(x, wgu_bf[...], preferred_element_type=F32)
        gate = gu[:, 0:D_FF_EXPERT]
        up = gu[:, D_FF_EXPERT:2 * D_FF_EXPERT]
        act = gate * _sigmoid(gate) * up
        o_ref[...] = jnp.dot(act.astype(BF16), wd_bf[...], preferred_element_type=F32)

    @pl.when(jnp.logical_not(used))
    def _():
        o_ref[...] = jnp.zeros(o_ref.shape, F32)


def moe_experts(h, row_token, tile_expert, tile_rows, n_used, w_gate_up, w_down, layer):
    T, D = h.shape
    n_tiles = row_token.shape[0] // MOE_TILE
    rows = row_token.reshape(n_tiles, 1, MOE_TILE)
    grid_spec = pltpu.PrefetchScalarGridSpec(
        num_scalar_prefetch=3,
        grid=(n_tiles,),
        in_specs=[
            pl.BlockSpec((1, 1, MOE_TILE), lambda i, te, tr, nu: (i, 0, 0), memory_space=pltpu.SMEM),
            pl.BlockSpec((1, 1, MOE_TILE), lambda i, te, tr, nu: (jnp.minimum(i + 1, n_tiles - 1), 0, 0),
                         memory_space=pltpu.SMEM),
            pl.BlockSpec(memory_space=pl.ANY),
            pl.BlockSpec((1, 1, D, 2 * D_FF_EXPERT), lambda i, te, tr, nu: (layer, te[i], 0, 0)),
            pl.BlockSpec((1, 1, D_FF_EXPERT, D), lambda i, te, tr, nu: (layer, te[i], 0, 0)),
        ],
        out_specs=pl.BlockSpec((MOE_TILE, D), lambda i, te, tr, nu: (i, 0)),
        scratch_shapes=[
            pltpu.VMEM((2, MOE_TILE, D), F32),
            pltpu.VMEM((D, 2 * D_FF_EXPERT), BF16),
            pltpu.VMEM((D_FF_EXPERT, D), BF16),
            pltpu.SemaphoreType.DMA((2,)),
        ],
    )
    return pl.pallas_call(
        _expert_kernel,
        grid_spec=grid_spec,
        out_shape=jax.ShapeDtypeStruct((n_tiles * MOE_TILE, D), F32),
        compiler_params=_cparams(("arbitrary",)),
        name="moe_experts",
    )(tile_expert, tile_rows, n_used, rows, rows, h, w_gate_up, w_down)


def _combine_kernel(dcur_ref, dnext_ref, x_ref, route_ref, y_hbm, o_ref, buf, sem, *, tm):
    i = pl.program_id(0)
    n = pl.num_programs(0)
    slot = i % 2

    def fetch(dref, s):
        def body(r, c):
            _row_dma(y_hbm, dref[0, 0, r], buf.at[s, 0], r, sem.at[s]).start()
            _row_dma(y_hbm, dref[0, 0, tm + r], buf.at[s, 1], r, sem.at[s]).start()
            return c

        lax.fori_loop(0, tm, body, 0, unroll=8)

    @pl.when(i == 0)
    def _():
        fetch(dcur_ref, 0)

    @pl.when(i + 1 < n)
    def _():
        fetch(dnext_ref, 1 - slot)

    _rows_wait(y_hbm, buf.at[slot, 0], tm, sem.at[slot])
    _rows_wait(y_hbm, buf.at[slot, 1], tm, sem.at[slot])
    route = route_ref[...]
    lane = lax.broadcasted_iota(jnp.int32, route.shape, 1)
    w1 = jnp.sum(jnp.where(lane == 0, route, 0.0), axis=1, keepdims=True)
    w2 = jnp.sum(jnp.where(lane == 1, route, 0.0), axis=1, keepdims=True)
    o_ref[...] = x_ref[...] + (w1 * buf[slot, 0] + w2 * buf[slot, 1])


def moe_combine(x, ys, route, dest, tm=256):
    T, D = x.shape
    n = T // tm
    dest_tiles = dest.reshape(2, n, tm).transpose(1, 0, 2).reshape(n, 1, 2 * tm)
    return pl.pallas_call(
        functools.partial(_combine_kernel, tm=tm),
        grid=(n,),
        in_specs=[pl.BlockSpec((1, 1, 2 * tm), lambda i: (i, 0, 0), memory_space=pltpu.SMEM),
                  pl.BlockSpec((1, 1, 2 * tm), lambda i: (jnp.minimum(i + 1, n - 1), 0, 0), memory_space=pltpu.SMEM),
                  pl.BlockSpec((tm, D), lambda i: (i, 0)),
                  pl.BlockSpec((tm, LANES), lambda i: (i, 0)),
                  pl.BlockSpec(memory_space=pl.ANY)],
        out_specs=pl.BlockSpec((tm, D), lambda i: (i, 0)),
        out_shape=jax.ShapeDtypeStruct((T, D), F32),
        scratch_shapes=[pltpu.VMEM((2, 2, tm, D), F32), pltpu.SemaphoreType.DMA((2,))],
        compiler_params=_cparams(("arbitrary",)),
        name="moe_combine",
    )(dest_tiles, dest_tiles, x, route, ys)


def hierarchical_moe(x, gain, w_grp, b_grp, w_exp, b_exp, w_gate_up, w_down, layer):
    T, D = x.shape
    h, route, cnt = moe_route(x, gain, w_grp, b_grp, w_exp, b_exp)
    route_t = route[:, 0:8].T
    expert = route_t[2:4].astype(jnp.int32)
    rank = route_t[4:6].astype(jnp.int32)
    counts = cnt[0, 0:N_EXPERTS].astype(jnp.int32)
    padded = ((counts + MOE_TILE - 1) // MOE_TILE) * MOE_TILE
    ends = jnp.cumsum(padded)
    offs = ends - padded
    e_ids = jnp.arange(N_EXPERTS, dtype=jnp.int32)[:, None, None]
    dest = rank + jnp.sum(jnp.where(expert[None] == e_ids, offs[:, None, None], 0), axis=0)
    n_rows = 2 * T + N_EXPERTS * MOE_TILE
    n_tiles = n_rows // MOE_TILE
    n_used = (ends[-1] // MOE_TILE).astype(jnp.int32)
    tile_start = jnp.arange(n_tiles, dtype=jnp.int32) * MOE_TILE
    tile_expert = jnp.sum((ends[None, :] <= tile_start[:, None]).astype(jnp.int32), axis=1)
    last_expert = tile_expert[jnp.maximum(n_used - 1, 0)]
    tile_expert = jnp.where(tile_start < ends[-1], tile_expert, last_expert)
    row_token = jnp.zeros((n_rows,), jnp.int32).at[dest.reshape(-1)].set(
        jnp.tile(jnp.arange(T, dtype=jnp.int32), 2))
    real_end = offs + counts
    tile_rows = jnp.sum(jnp.where(tile_expert[:, None] == jnp.arange(N_EXPERTS, dtype=jnp.int32)[None, :],
                                  real_end[None, :], 0), axis=1) - tile_start
    tile_rows = jnp.where(tile_start < ends[-1], jnp.clip(tile_rows, 0, MOE_TILE), 0).astype(jnp.int32)
    ys = moe_experts(h, row_token, tile_expert, tile_rows, n_used.reshape(1), w_gate_up, w_down, layer)
    return moe_combine(x, ys, route, dest)


def _in_proj_kernel(a_ref, wt_hbm, o_ref, wbuf, sem, *, layer, row0, tn, sigmoid_out):
    i, j = pl.program_id(0), pl.program_id(1)
    ni, nj = pl.num_programs(0), pl.num_programs(1)
    step = i * nj + j
    slot = step % 2

    def fetch(jj, s):
        start = pl.multiple_of(row0 + jj * tn, SUBLANES)
        return pltpu.make_async_copy(wt_hbm.at[layer, pl.ds(start, tn), :], wbuf.at[s], sem.at[s])

    @pl.when(step == 0)
    def _():
        fetch(0, 0).start()

    @pl.when(step + 1 < ni * nj)
    def _():
        fetch((j + 1) % nj, 1 - slot).start()

    fetch(j, slot).wait()
    w = wbuf[slot].astype(BF16)
    acc = lax.dot_general(a_ref[...], w, NT, preferred_element_type=F32)
    if sigmoid_out:
        acc = _sigmoid(acc)
    o_ref[...] = acc.astype(o_ref.dtype)


def in_proj(h, wt, layer, row0, n, tn, name, tm=2048, sigmoid_out=False):
    T, K = h.shape
    assert n % tn == 0 and row0 % SUBLANES == 0 and tn % SUBLANES == 0
    return pl.pallas_call(
        functools.partial(_in_proj_kernel, layer=layer, row0=row0, tn=tn, sigmoid_out=sigmoid_out),
        grid=(T // tm, n // tn),
        in_specs=[pl.BlockSpec((tm, K), lambda i, j: (i, 0)), pl.BlockSpec(memory_space=pl.ANY)],
        out_specs=pl.BlockSpec((tm, tn), lambda i, j: (i, j)),
        out_shape=jax.ShapeDtypeStruct((T, n), BF16 if sigmoid_out else F32),
        scratch_shapes=[pltpu.VMEM((2, tn, K), F32), pltpu.SemaphoreType.DMA((2,))],
        compiler_params=_cparams(("arbitrary", "arbitrary")),
        name=name,
    )(h, wt)


def kernel(x, norm_mix, w_in, cmp_pe, cmp_w1, cmp_w2, q_norm, k_norm, hg_lb_logits, hg_norm, gla_w_a2, gla_b_a,
           gla_norm, w_branch, w_out, norm_ffn, w_grp, b_grp, w_exp, b_exp, w_gate_up, w_down):
    B, T, D = x.shape
    assert B == 1 and D == D_MODEL
    xt = x[0]
    p_lb = jax.nn.softmax(hg_lb_logits.astype(F32), axis=0)
    lower_bounds = jnp.cumsum(p_lb, axis=0) - p_lb[0]
    wt = jnp.swapaxes(w_in, 1, 2)
    for l in range(DEPTH):
        h = rmsnorm_bf16(xt, norm_mix[l])
        proj_n = in_proj(h, wt, l, W_NSA[0], W_NSA[1], 512, "in_proj_nsa")
        proj_r = in_proj(h, wt, l, W_REC[0], W_REC[1], 512, "in_proj_rec")
        proj_a = in_proj(h, wt, l, W_GLA_A_BLOCK[0], W_GLA_A_BLOCK[1], LANES, "in_proj_gla_a")
        proj_m = in_proj(h, wt, l, W_MERGE[0], W_MERGE[1], 512, "in_proj_merge")
        o_cmp, o_slc, o_win = nsa_mixer(proj_n, cmp_pe[l], cmp_w1[l], cmp_w2[l], q_norm[l], k_norm[l])
        o_b = hgrn2_mixer(proj_r, lower_bounds[l], hg_norm[l])
        o_c = gla_mixer(proj_r, proj_a, gla_w_a2[l], gla_b_a[l], gla_norm[l])
        merged = merge_branches(proj_m, o_cmp, o_slc, o_win, o_b, o_c, w_branch[l].astype(BF16))
        xt = matmul_bf16(merged, w_out[l].astype(BF16), res=xt, tm=1024, tn=512, name="out_proj")
        xt = hierarchical_moe(xt, norm_ffn[l], w_grp[l], b_grp[l], w_exp[l], b_exp[l], w_gate_up, w_down, l)
    return xt[None]
```

```python
import functools
import math

import numpy as np
import jax
import jax.numpy as jnp
from jax import lax
from jax.experimental import pallas as pl
from jax.experimental.pallas import tpu as pltpu

F32 = jnp.float32
BF16 = jnp.bfloat16
HIGHEST = lax.Precision.HIGHEST

D_MODEL = 2048
DEPTH = 2
BRANCH_WIDTH = D_MODEL // 4
NSA_HEADS = 8
NSA_KV_HEADS = 2
NSA_REP = NSA_HEADS // NSA_KV_HEADS
NSA_HD = 64
CMP_LEN = 32
CMP_STRIDE = 16
SLC_LEN = 64
SLC_TOP = 16
WINDOW = 512
HG_HEADS = 4
GLA_HEADS = 4
GLA_DK = 64
GLA_RANK = 16
GLA_TAU = 16.0
CHUNK = 64
SUB = 16
N_GROUPS = 4
EXPERTS_PER_GROUP = 8
N_EXPERTS = N_GROUPS * EXPERTS_PER_GROUP
D_FF_EXPERT = D_MODEL // 4
EPS = 1e-6
NEG = -1e30
FORCE = 1e4
TINY = 1e-30

LANES = 128
SUBLANES = 8
VMEM_LIMIT = 56 * 1024 * 1024

W_NSA = (0, 1536)
W_REC = (1304, 3584)
W_MERGE = (4904, 6144)
W_GLA_A_BLOCK = (4864, 128)
C_NSA_KV = 512
C_NSA_GATE = 1280
C_GLA_QK = 2048
C_GLA_VR = 2560
TAIL_GLA_A0 = 4888 - W_GLA_A_BLOCK[0]

NT = (((1,), (1,)), ((), ()))
TN = (((0,), (0,)), ((), ()))


def _cparams(sem):
    return pltpu.CompilerParams(dimension_semantics=sem, vmem_limit_bytes=VMEM_LIMIT)


def _sigmoid(x):
    return 1.0 / (1.0 + jnp.exp(-x))


def _norm_kernel(x_ref, g_ref, o_ref):
    x = x_ref[...]
    ms = jnp.mean(x * x, axis=-1, keepdims=True)
    o_ref[...] = (x * lax.rsqrt(ms + EPS) * g_ref[...]).astype(o_ref.dtype)


def rmsnorm_bf16(x, gain, tm=512):
    T, D = x.shape
    return pl.pallas_call(
        _norm_kernel,
        grid=(T // tm,),
        in_specs=[pl.BlockSpec((tm, D), lambda i: (i, 0)), pl.BlockSpec((1, D), lambda i: (0, 0))],
        out_specs=pl.BlockSpec((tm, D), lambda i: (i, 0)),
        out_shape=jax.ShapeDtypeStruct((T, D), BF16),
        compiler_params=_cparams(("parallel",)),
        name="rmsnorm_bf16",
    )(x, gain.reshape(1, D))


def _mm_kernel(a_ref, b_ref, o_ref):
    o_ref[...] = jnp.dot(a_ref[...], b_ref[...], preferred_element_type=F32)


def _mm_res_kernel(a_ref, b_ref, r_ref, o_ref):
    o_ref[...] = r_ref[...] + jnp.dot(a_ref[...], b_ref[...], preferred_element_type=F32)


def matmul_bf16(a, b, res=None, tm=1024, tn=512, name="matmul_bf16"):
    T, K = a.shape
    N = b.shape[1]
    in_specs = [pl.BlockSpec((tm, K), lambda i, j: (i, 0)), pl.BlockSpec((K, tn), lambda i, j: (0, j))]
    args = [a, b]
    kern = _mm_kernel
    if res is not None:
        in_specs.append(pl.BlockSpec((tm, tn), lambda i, j: (i, j)))
        args.append(res)
        kern = _mm_res_kernel
    return pl.pallas_call(
        kern,
        grid=(T // tm, N // tn),
        in_specs=in_specs,
        out_specs=pl.BlockSpec((tm, tn), lambda i, j: (i, j)),
        out_shape=jax.ShapeDtypeStruct((T, N), F32),
        compiler_params=_cparams(("parallel", "arbitrary")),
        name=name,
    )(*args)


def _half_rmsnorm(x, gain2, lo):
    x2 = x * x
    s_lo = jnp.sum(jnp.where(lo, x2, 0.0), axis=1, keepdims=True)
    s_hi = jnp.sum(jnp.where(lo, 0.0, x2), axis=1, keepdims=True)
    ms = jnp.where(lo, s_lo, s_hi) * (1.0 / NSA_HD)
    return x * lax.rsqrt(ms + EPS) * gain2


def _nsa_prep_kernel(q_ref, kvc_ref, kvs_ref, kvw_ref, tail_ref, qg_ref, kg_ref,
                     qn_ref, kaug_ref, vs_ref, kw_ref, vw_ref, gate_ref, *, tm):
    i = pl.program_id(0)
    lane = lax.broadcasted_iota(jnp.int32, (tm, LANES), 1)
    lo = lane < NSA_HD
    qg = qg_ref[...]
    for c in range(4):
        x = q_ref[:, c * LANES:(c + 1) * LANES]
        qn_ref[:, c * LANES:(c + 1) * LANES] = (_half_rmsnorm(x, qg, lo) * (NSA_HD ** -0.5)).astype(BF16)

    def dup(kn):
        rolled = pltpu.roll(kn, NSA_HD, axis=1)
        return jnp.where(lo, kn, rolled), jnp.where(lo, rolled, kn)

    vt_zero = jnp.zeros((NSA_HD, tm), BF16)

    def store_vt(v_pair, ref):
        vt = jnp.transpose(v_pair)
        for g in range(NSA_KV_HEADS):
            head = vt[g * NSA_HD:(g + 1) * NSA_HD, :].astype(BF16)
            ref[g, 0:NSA_HD, :] = head
            ref[g, NSA_HD:LANES, :] = vt_zero
            ref[g, LANES:LANES + NSA_HD, :] = vt_zero
            ref[g, LANES + NSA_HD:2 * LANES, :] = head

    row = i * tm + lax.broadcasted_iota(jnp.int32, (tm, LANES), 0)
    onehot = jnp.where(row // SLC_LEN == lane, 1.0, 0.0).astype(BF16)
    k_extra = jnp.where(lane == NSA_HD, (row % SLC_LEN).astype(F32), 0.0)
    ks = dup(_half_rmsnorm(kvs_ref[:, 0:LANES], kg_ref[1:2, :], lo))
    store_vt(kvs_ref[:, LANES:2 * LANES], vs_ref)
    kw = dup(_half_rmsnorm(kvw_ref[:, 0:LANES], kg_ref[2:3, :], lo))
    store_vt(kvw_ref[:, LANES:2 * LANES], vw_ref)
    for g in range(NSA_KV_HEADS):
        kaug_ref[g, :, 0:LANES] = onehot
        kaug_ref[g, :, LANES:2 * LANES] = jnp.where(lo, ks[g], k_extra).astype(BF16)
        kw_ref[g] = kw[g].astype(BF16)
    gate_ref[...] = _sigmoid(tail_ref[...])


def nsa_prep(proj_n, q_gain, k_gain, tm=512):
    T = proj_n.shape[0]
    qg2 = jnp.tile(q_gain.reshape(1, NSA_HD), (1, 2))
    kg2 = jnp.tile(k_gain.reshape(3, NSA_HD), (1, 2))
    G = NSA_KV_HEADS
    out_shape = (
        jax.ShapeDtypeStruct((T, 512), BF16),
        jax.ShapeDtypeStruct((G, T, 256), BF16),
        jax.ShapeDtypeStruct((G, 256, T), BF16),
        jax.ShapeDtypeStruct((G, T, 128), BF16),
        jax.ShapeDtypeStruct((G, 256, T), BF16),
        jax.ShapeDtypeStruct((T, 128), F32),
    )
    return pl.pallas_call(
        functools.partial(_nsa_prep_kernel, tm=tm),
        grid=(T // tm,),
        in_specs=[
            pl.BlockSpec((tm, 512), lambda i: (i, 0)),
            pl.BlockSpec((tm, 256), lambda i: (i, C_NSA_KV // 256)),
            pl.BlockSpec((tm, 256), lambda i: (i, C_NSA_KV // 256 + 1)),
            pl.BlockSpec((tm, 256), lambda i: (i, C_NSA_KV // 256 + 2)),
            pl.BlockSpec((tm, 128), lambda i: (i, C_NSA_GATE // 128)),
            pl.BlockSpec((1, 128), lambda i: (0, 0)),
            pl.BlockSpec((3, 128), lambda i: (0, 0)),
        ],
        out_specs=(
            pl.BlockSpec((tm, 512), lambda i: (i, 0)),
            pl.BlockSpec((G, tm, 256), lambda i: (0, i, 0)),
            pl.BlockSpec((G, 256, tm), lambda i: (0, 0, i)),
            pl.BlockSpec((G, tm, 128), lambda i: (0, i, 0)),
            pl.BlockSpec((G, 256, tm), lambda i: (0, 0, i)),
            pl.BlockSpec((tm, 128), lambda i: (i, 0)),
        ),
        out_shape=out_shape,
        compiler_params=_cparams(("parallel",)),
        name="nsa_prep",
    )(proj_n, proj_n, proj_n, proj_n, proj_n, qg2, kg2)


def _gelu_tanh(x):
    c = math.sqrt(2.0 / math.pi)
    return 0.5 * x * (1.0 + jnp.tanh(c * (x + 0.044715 * (x * x * x))))


def _compress_kernel(a_ref, pe_ref, w1_ref, w2_ref, kg_ref, o_ref, *, nb):
    kind = pl.program_id(0)
    a_lo = a_ref[0, 0, 0:nb, :]
    a_hi = a_ref[0, 0, 1:nb + 1, :]
    blocks = jnp.concatenate([a_lo, a_hi], axis=1) + pe_ref[0]
    h1 = jnp.dot(blocks.astype(BF16), w1_ref[0].astype(BF16), preferred_element_type=F32)
    y = jnp.dot(_gelu_tanh(h1).astype(BF16), w2_ref[0].astype(BF16), preferred_element_type=F32)
    ms = jnp.mean(y * y, axis=-1, keepdims=True)
    yn = y * lax.rsqrt(ms + EPS) * kg_ref[...]
    o_ref[0, 0] = jnp.where(kind == 0, yn, y)


def nsa_compress(kv_cmp, cmp_pe, cmp_w1, cmp_w2, k_gain0):
    T = kv_cmp.shape[0]
    nb = T // CMP_STRIDE
    G = NSA_KV_HEADS
    a = kv_cmp.reshape(T, 2, G, NSA_HD).transpose(1, 2, 0, 3).reshape(2, G, nb, CMP_STRIDE * NSA_HD)
    a = jnp.pad(a, ((0, 0), (0, 0), (0, 8), (0, 0)))
    pe = cmp_pe.reshape(2, 1, CMP_LEN * NSA_HD)
    return pl.pallas_call(
        functools.partial(_compress_kernel, nb=nb),
        grid=(2, G),
        in_specs=[
            pl.BlockSpec((1, 1, nb + 8, CMP_STRIDE * NSA_HD), lambda k, g: (k, g, 0, 0)),
            pl.BlockSpec((1, 1, CMP_LEN * NSA_HD), lambda k, g: (k, 0, 0)),
            pl.BlockSpec((1, CMP_LEN * NSA_HD, NSA_HD), lambda k, g: (k, 0, 0)),
            pl.BlockSpec((1, NSA_HD, NSA_HD), lambda k, g: (k, 0, 0)),
            pl.BlockSpec((1, NSA_HD), lambda k, g: (0, 0)),
        ],
        out_specs=pl.BlockSpec((1, 1, nb, NSA_HD), lambda k, g: (k, g, 0, 0)),
        out_shape=jax.ShapeDtypeStruct((2, G, nb, NSA_HD), F32),
        compiler_params=_cparams(("arbitrary", "arbitrary")),
        name="nsa_compress",
    )(a, pe, cmp_w1, cmp_w2, k_gain0.reshape(1, NSA_HD))


def _slope(g, r):
    return jnp.where(g == 0, 2.0 ** -(r + 1), 2.0 ** -(NSA_REP + r + 1)).astype(F32)


def _gate_pair(gates, lane, lo, col_even):
    ge = jnp.sum(jnp.where(lane == col_even, gates, 0.0), axis=1, keepdims=True)
    go = jnp.sum(jnp.where(lane == col_even + 1, gates, 0.0), axis=1, keepdims=True)
    return jnp.where(lo, ge, go)


def _cmp_sel_kernel(q_ref, kc_ref, vc_ref, gate_ref, mt_ref, o_ref, sel_ref, *, tq, nb, ns_pad):
    g = pl.program_id(0)
    qi = pl.program_id(1)
    t0 = qi * tq
    lane = lax.broadcasted_iota(jnp.int32, (tq, LANES), 1)
    lo = lane < NSA_HD
    n_idx = lax.broadcasted_iota(jnp.int32, (nb, tq), 0)
    t_idx = t0 + lax.broadcasted_iota(jnp.int32, (nb, tq), 1)
    dist = (t_idx - (n_idx * CMP_STRIDE + (CMP_LEN - 1))).astype(F32)
    vis = dist >= 0.0
    kc = kc_ref[0]
    imp = jnp.zeros((nb, tq), F32)
    gates = gate_ref[...]
    for a in range(2):
        acc = jnp.zeros((tq, LANES), F32)
        for half in range(2):
            r = 2 * a + half
            qh = q_ref[:, a * LANES:(a + 1) * LANES]
            qh = jnp.where(lo if half == 0 else jnp.logical_not(lo), qh, jnp.zeros_like(qh))
            s = lax.dot_general(kc, qh, NT, preferred_element_type=F32)
            s = jnp.where(vis, s - _slope(g, r) * dist, NEG)
            mx = jnp.max(s, axis=0, keepdims=True)
            e = jnp.where(vis, jnp.exp(s - mx), 0.0)
            den = jnp.sum(e, axis=0, keepdims=True)
            p = e * jnp.where(den > 0.0, 1.0 / den, 0.0)
            imp = imp + p
            v = vc_ref[0, :, half * LANES:(half + 1) * LANES]
            acc = acc + lax.dot_general(p.astype(BF16), v, TN, preferred_element_type=F32)
        gp = _gate_pair(gates, lane, lo, g * NSA_REP + 2 * a)
        o_ref[:, a * LANES:(a + 1) * LANES] = acc * gp

    score_all = jnp.dot(mt_ref[...], imp, precision=HIGHEST, preferred_element_type=F32)
    blk = lax.broadcasted_iota(jnp.int32, (ns_pad, LANES), 0)
    for c in range(tq // LANES):
        cur = (t0 + c * LANES + lax.broadcasted_iota(jnp.int32, (ns_pad, LANES), 1)) // SLC_LEN
        forced = (blk == 0) | (blk == cur) | (blk == cur - 1)
        score = jnp.where(forced, FORCE, score_all[:, c * LANES:(c + 1) * LANES])
        score = jnp.where(blk <= cur, score, -1.0)
        for _ in range(SLC_TOP):
            mx = jnp.max(score, axis=0, keepdims=True)
            idx = jnp.min(jnp.where(score == mx, blk, ns_pad), axis=0, keepdims=True)
            score = jnp.where(blk == idx, -jnp.inf, score)
        bias_t = jnp.where((score == -jnp.inf) & (blk <= cur), 0.0, NEG)
        sel_ref[0, c * LANES:(c + 1) * LANES, :] = jnp.transpose(bias_t).astype(BF16)


def _score_matrix(nb, ns_pad):
    ratio, span = SLC_LEN // CMP_STRIDE, CMP_LEN // CMP_STRIDE
    n_cmp = nb - 1
    m = np.zeros((ns_pad, nb), np.float32)
    for s in range(nb // ratio):
        for mm in range(ratio):
            for nn in range(span):
                c = ratio * s + mm - nn
                if 0 <= c < n_cmp:
                    m[s, c] += 1.0
    return jnp.asarray(m)


def nsa_cmp_select(qn, kc2, vc_lohi, gates, tq=256):
    T = qn.shape[0]
    nb = T // CMP_STRIDE
    ns_pad = LANES
    G = NSA_KV_HEADS
    return pl.pallas_call(
        functools.partial(_cmp_sel_kernel, tq=tq, nb=nb, ns_pad=ns_pad),
        grid=(G, T // tq),
        in_specs=[
            pl.BlockSpec((tq, 256), lambda g, i: (i, g)),
            pl.BlockSpec((1, nb, 128), lambda g, i: (g, 0, 0)),
            pl.BlockSpec((1, nb, 256), lambda g, i: (g, 0, 0)),
            pl.BlockSpec((tq, 128), lambda g, i: (i, 0)),
            pl.BlockSpec((ns_pad, nb), lambda g, i: (0, 0)),
        ],
        out_specs=(
            pl.BlockSpec((tq, 256), lambda g, i: (i, g)),
            pl.BlockSpec((1, tq, ns_pad), lambda g, i: (g, i, 0)),
        ),
        out_shape=(jax.ShapeDtypeStruct((T, 512), F32), jax.ShapeDtypeStruct((G, T, ns_pad), BF16)),
        compiler_params=_cparams(("parallel", "parallel")),
        name="nsa_cmp_select",
    )(qn, kc2, vc_lohi, gates, _score_matrix(nb, ns_pad))


KV_SLOTS = 3


def _sel_attn_kernel(tile_tab, cnt_tab, q_ref, sb_ref, gate_ref, k_hbm, v_hbm, o_ref,
                     qaug, kbuf, vbuf, sem, m_scr, l_scr, acc_scr, *, tq, tk, max_tiles):
    g = pl.program_id(0)
    qi = pl.program_id(1)
    row = g * pl.num_programs(1) + qi
    n_tiles = cnt_tab[row]
    base = row * max_tiles
    lane = lax.broadcasted_iota(jnp.int32, (tq, LANES), 1)
    lo = lane < NSA_HD

    def copies(j, slot):
        start = pl.multiple_of(tile_tab[base + j] * tk, tk)
        return (pltpu.make_async_copy(k_hbm.at[g, pl.ds(start, tk), :], kbuf.at[slot], sem.at[0, slot]),
                pltpu.make_async_copy(v_hbm.at[g, :, pl.ds(start, tk)], vbuf.at[slot], sem.at[1, slot]))

    def start_fetch(j, slot):
        for cp in copies(j, slot):
            cp.start()

    def wait_fetch(j, slot):
        for cp in copies(j, slot):
            cp.wait()

    start_fetch(0, 0)

    @pl.when(n_tiles > 1)
    def _():
        start_fetch(1, 1)

    sb = sb_ref[0].astype(F32)
    blk_rel = (lane - qi * (tq // SLC_LEN)).astype(F32)
    for r in range(NSA_REP):
        a, half = r // 2, r % 2
        slope = _slope(g, r)
        qh = q_ref[:, a * LANES:(a + 1) * LANES].astype(F32)
        if half:
            qh = pltpu.roll(qh, NSA_HD, axis=1)
        qh = jnp.where(lo, qh, jnp.where(lane == NSA_HD, slope, 0.0))
        qaug[r * tq:(r + 1) * tq, 0:LANES] = (sb + (slope * SLC_LEN) * blk_rel).astype(BF16)
        qaug[r * tq:(r + 1) * tq, LANES:2 * LANES] = qh.astype(BF16)
    m_scr[...] = jnp.full(m_scr.shape, NEG, F32)
    l_scr[...] = jnp.zeros(l_scr.shape, F32)
    acc_scr[...] = jnp.zeros(acc_scr.shape, F32)

    upper = lax.broadcasted_iota(jnp.int32, (LANES, tq), 0) < NSA_HD
    kj_diag = (qi * tq + (tq - 1)) // tk

    def step(masked, slot):
        if masked:
            kpos = kj_diag * tk + lax.broadcasted_iota(jnp.int32, (tk, tq), 0)
            qpos = qi * tq + lax.broadcasted_iota(jnp.int32, (tk, tq), 1)
            causal = kpos <= qpos
        k = kbuf[slot]
        for a in range(2):
            pv = jnp.zeros((LANES, tq), F32)
            alphas = []
            for half in range(2):
                r = 2 * a + half
                sr = lax.dot_general(k, qaug[r * tq:(r + 1) * tq, :], NT, preferred_element_type=F32)
                if masked:
                    sr = jnp.where(causal, sr, NEG)
                m_prev = m_scr[r:r + 1, :]
                m_new = jnp.maximum(m_prev, jnp.max(sr, axis=0, keepdims=True))
                alpha = jnp.exp(m_prev - m_new)
                p = jnp.exp(sr - m_new)
                l_scr[r:r + 1, :] = alpha * l_scr[r:r + 1, :] + jnp.sum(p, axis=0, keepdims=True)
                m_scr[r:r + 1, :] = m_new
                vt = vbuf[slot, half * LANES:(half + 1) * LANES, :]
                pv = pv + jnp.dot(vt, p.astype(BF16), preferred_element_type=F32)
                alphas.append(alpha)
            alpha_pair = jnp.where(upper, alphas[0], alphas[1])
            acc_scr[a * LANES:(a + 1) * LANES, :] = alpha_pair * acc_scr[a * LANES:(a + 1) * LANES, :] + pv

    def off_diagonal(j, carry):
        @pl.when(j + 2 < n_tiles)
        def _():
            start_fetch(j + 2, (j + 2) % KV_SLOTS)

        wait_fetch(j, j % KV_SLOTS)
        step(False, j % KV_SLOTS)
        return carry

    lax.fori_loop(0, n_tiles - 1, off_diagonal, 0)
    last = n_tiles - 1
    wait_fetch(last, last % KV_SLOTS)
    step(True, last % KV_SLOTS)

    gates = gate_ref[...]
    for a in range(2):
        l_pair = jnp.where(upper, l_scr[2 * a:2 * a + 1, :], l_scr[2 * a + 1:2 * a + 2, :])
        gp = _gate_pair(gates, lane, lo, NSA_HEADS + g * NSA_REP + 2 * a)
        o_t = acc_scr[a * LANES:(a + 1) * LANES, :] / l_pair
        o_ref[:, a * LANES:(a + 1) * LANES] = jnp.transpose(o_t) * gp


def nsa_selected_attention(qn, selbias, kaug, vs_lohi, gates, tq=256, tk=512):
    T = qn.shape[0]
    G = NSA_KV_HEADS
    nq, nkt, nkb = T // tq, T // tk, tk // SLC_LEN
    chosen = (selbias.astype(F32) > 0.5 * NEG).reshape(G, nq, tq, LANES // nkb, nkb)
    tile_any = jnp.any(chosen, axis=(2, 4))[:, :, 0:nkt]
    kj = jnp.arange(nkt, dtype=jnp.int32)[None, None, :]
    kj_diag = ((jnp.arange(nq, dtype=jnp.int32) * tq + (tq - 1)) // tk)[None, :, None]
    active = (tile_any & (kj < kj_diag)) | (kj == kj_diag)
    csum = jnp.cumsum(active.astype(jnp.int32), axis=2)
    counts = csum[:, :, -1]
    tiles = jnp.sum((csum[:, :, None, :] <= kj[..., None]).astype(jnp.int32), axis=3)
    tiles = jnp.minimum(tiles, nkt - 1)
    grid_spec = pltpu.PrefetchScalarGridSpec(
        num_scalar_prefetch=2,
        grid=(G, nq),
        in_specs=[
            pl.BlockSpec((tq, 256), lambda g, i, tt, ct: (i, g)),
            pl.BlockSpec((1, tq, 128), lambda g, i, tt, ct: (g, i, 0)),
            pl.BlockSpec((tq, 128), lambda g, i, tt, ct: (i, 0)),
            pl.BlockSpec(memory_space=pl.ANY),
            pl.BlockSpec(memory_space=pl.ANY),
        ],
        out_specs=pl.BlockSpec((tq, 256), lambda g, i, tt, ct: (i, g)),
        scratch_shapes=[
            pltpu.VMEM((NSA_REP * tq, 256), BF16),
            pltpu.VMEM((KV_SLOTS, tk, 256), BF16),
            pltpu.VMEM((KV_SLOTS, 256, tk), BF16),
            pltpu.SemaphoreType.DMA((2, KV_SLOTS)),
            pltpu.VMEM((NSA_REP, tq), F32),
            pltpu.VMEM((NSA_REP, tq), F32),
            pltpu.VMEM((2 * LANES, tq), F32),
        ],
    )
    return pl.pallas_call(
        functools.partial(_sel_attn_kernel, tq=tq, tk=tk, max_tiles=nkt),
        grid_spec=grid_spec,
        out_shape=jax.ShapeDtypeStruct((T, 512), F32),
        compiler_params=_cparams(("parallel", "parallel")),
        name="nsa_selected_attention",
    )(tiles.reshape(-1).astype(jnp.int32), counts.reshape(-1).astype(jnp.int32),
      qn, selbias, gates, kaug, vs_lohi)


def _win_attn_kernel(q_ref, k0_ref, k1_ref, k2_ref, v0_ref, v1_ref, v2_ref, gate_ref, o_ref, *, tq):
    g = pl.program_id(0)
    qi = pl.program_id(1)
    lane = lax.broadcasted_iota(jnp.int32, (tq, LANES), 1)
    lo = lane < NSA_HD
    upper = lax.broadcasted_iota(jnp.int32, (LANES, tq), 0) < NSA_HD
    row = lax.broadcasted_iota(jnp.int32, (tq, tq), 0)
    qpos = qi * tq + lax.broadcasted_iota(jnp.int32, (tq, tq), 1)
    k_refs = (k0_ref, k1_ref, k2_ref)
    v_refs = (v0_ref, v1_ref, v2_ref)
    dists, masks = [], []
    for d in range(3):
        kpos = (qi - 2 + d) * tq + row
        dd = qpos - kpos
        dists.append(dd.astype(F32))
        masks.append((dd >= 0) & (dd < WINDOW) & (kpos >= 0))
    gates = gate_ref[...]
    for a in range(2):
        pv = jnp.zeros((LANES, tq), F32)
        ls = []
        for half in range(2):
            r = 2 * a + half
            qh = q_ref[:, a * LANES:(a + 1) * LANES]
            qh = jnp.where(lo if half == 0 else jnp.logical_not(lo), qh, jnp.zeros_like(qh))
            ss = []
            for d in range(3):
                sd = lax.dot_general(k_refs[d][0], qh, NT, preferred_element_type=F32)
                ss.append(jnp.where(masks[d], sd - _slope(g, r) * dists[d], NEG))
            mx = jnp.maximum(jnp.maximum(jnp.max(ss[0], axis=0, keepdims=True), jnp.max(ss[1], axis=0, keepdims=True)),
                             jnp.max(ss[2], axis=0, keepdims=True))
            l = jnp.zeros((1, tq), F32)
            for d in range(3):
                p = jnp.exp(ss[d] - mx)
                l = l + jnp.sum(p, axis=0, keepdims=True)
                vt = v_refs[d][0, half * LANES:(half + 1) * LANES, :]
                pv = pv + jnp.dot(vt, p.astype(BF16), preferred_element_type=F32)
            ls.append(l)
        l_pair = jnp.where(upper, ls[0], ls[1])
        gp = _gate_pair(gates, lane, lo, 2 * NSA_HEADS + g * NSA_REP + 2 * a)
        o_ref[:, a * LANES:(a + 1) * LANES] = jnp.transpose(pv / l_pair) * gp


def nsa_window_attention(qn, kw2, vw_lohi, gates, tq=256):
    T = qn.shape[0]
    G = NSA_KV_HEADS
    assert WINDOW == 2 * tq

    def kspec(d):
        return pl.BlockSpec((1, tq, 128), lambda g, i, d=d: (g, jnp.maximum(i - 2 + d, 0), 0))

    def vspec(d):
        return pl.BlockSpec((1, 256, tq), lambda g, i, d=d: (g, 0, jnp.maximum(i - 2 + d, 0)))

    return pl.pallas_call(
        functools.partial(_win_attn_kernel, tq=tq),
        grid=(G, T // tq),
        in_specs=[pl.BlockSpec((tq, 256), lambda g, i: (i, g)),
                  kspec(0), kspec(1), kspec(2),
                  vspec(0), vspec(1), vspec(2),
                  pl.BlockSpec((tq, 128), lambda g, i: (i, 0))],
        out_specs=pl.BlockSpec((tq, 256), lambda g, i: (i, g)),
        out_shape=jax.ShapeDtypeStruct((T, 512), F32),
        compiler_params=_cparams(("parallel", "parallel")),
        name="nsa_window_attention",
    )(qn, kw2, kw2, kw2, vw_lohi, vw_lohi, vw_lohi, gates)


def nsa_mixer(proj_n, cmp_pe, cmp_w1, cmp_w2, q_gain, k_gain, parts=False):
    qn, kaug, vs_lohi, kw2, vw_lohi, gates = nsa_prep(proj_n, q_gain, k_gain)
    kvc = nsa_compress(proj_n[:, C_NSA_KV:C_NSA_KV + 256], cmp_pe, cmp_w1, cmp_w2, k_gain[0])
    kc, vc = kvc[0], kvc[1]
    kc2 = jnp.concatenate([kc, kc], axis=-1).astype(BF16)
    zero = jnp.zeros_like(vc)
    vc_lohi = jnp.concatenate([vc, zero, zero, vc], axis=-1).astype(BF16)
    o_cmp, selbias = nsa_cmp_select(qn, kc2, vc_lohi, gates)
    o_slc = nsa_selected_attention(qn, selbias, kaug, vs_lohi, gates)
    o_win = nsa_window_attention(qn, kw2, vw_lohi, gates)
    if parts:
        return o_cmp + o_slc + o_win, (o_cmp, o_slc, o_win)
    return o_cmp, o_slc, o_win


def _diag_selector():
    m = np.zeros((SUB * LANES, LANES), np.float32)
    for j in range(SUB):
        for rep in range(CHUNK // SUB):
            m[j * LANES:(j + 1) * LANES, rep * SUB + j] = 1.0
    return jnp.asarray(m, dtype=BF16)


LOG2E = 1.0 / math.log(2.0)


def _recur_tile(q_scr, k_scr, v_scr, g_scr, o_scr, st_scr, sel_ref, tb):
    row = lax.broadcasted_iota(jnp.int32, (CHUNK, LANES), 0)
    lane = lax.broadcasted_iota(jnp.int32, (CHUNK, LANES), 1)
    sub_row = row % SUB
    blockdiag = (row // SUB) == (lane // SUB)
    r64 = lax.broadcasted_iota(jnp.int32, (CHUNK, CHUNK), 0)
    c64 = lax.broadcasted_iota(jnp.int32, (CHUNK, CHUNK), 1)
    ltri = jnp.where(r64 >= c64, 1.0, 0.0).astype(F32)
    c_sub = lax.broadcasted_iota(jnp.int32, (SUB, CHUNK), 1)
    nsub = CHUNK // SUB
    half = SUB // 2
    low_row = half + lax.broadcasted_iota(jnp.int32, (nsub, half, LANES), 1)

    def head_chunk(h, r0):
        qc = q_scr[h, pl.ds(r0, CHUNK), :]
        kc = k_scr[h, pl.ds(r0, CHUNK), :]
        vc = v_scr[h, pl.ds(r0, CHUNK), :]
        gc = g_scr[h, pl.ds(r0, CHUNK), :]
        b = jnp.dot(ltri, gc, precision=HIGHEST, preferred_element_type=F32)
        bend = b[CHUNK - 1:CHUNK, :]
        st = st_scr[h]
        o = lax.dot_general((qc * jnp.exp2(b)).astype(BF16), st.astype(BF16), NT, preferred_element_type=F32)
        k4 = kc.reshape(nsub, SUB, LANES)
        b4 = b.reshape(nsub, SUB, LANES)
        pieces = []
        for j in range(half):
            k_rep = jnp.broadcast_to(k4[:, j:j + 1, :], (nsub, SUB, LANES)).reshape(CHUNK, LANES)
            b_rep = jnp.broadcast_to(b4[:, j:j + 1, :], (nsub, SUB, LANES)).reshape(CHUNK, LANES)
            e = qc * k_rep * jnp.exp2(jnp.where(sub_row >= j, b - b_rep, NEG))
            pieces.append(e.astype(BF16))
        q_low = qc.reshape(nsub, SUB, LANES)[:, half:, :]
        b_low = b4[:, half:, :]
        for j in range(half, SUB):
            k_rep = jnp.broadcast_to(k4[:, j:j + 1, :], (nsub, half, LANES))
            b_rep = jnp.broadcast_to(b4[:, j:j + 1, :], (nsub, half, LANES))
            e_low = q_low * k_rep * jnp.exp2(jnp.where(low_row >= j, b_low - b_rep, NEG))
            e = jnp.concatenate([jnp.zeros_like(e_low), e_low], axis=1).reshape(CHUNK, LANES)
            pieces.append(e.astype(BF16))
        a_diag = jnp.dot(jnp.concatenate(pieces, axis=1), sel_ref[...], preferred_element_type=F32)
        a_diag = jnp.where(blockdiag, a_diag, 0.0)[:, 0:CHUNK]
        rows = [jnp.zeros((SUB, CHUNK), F32)]
        for i_sub in range(1, nsub):
            ref_b = b[i_sub * SUB - 1:i_sub * SUB, :]
            qt = qc[i_sub * SUB:(i_sub + 1) * SUB, :] * jnp.exp2(b[i_sub * SUB:(i_sub + 1) * SUB, :] - ref_b)
            kt = kc * jnp.exp2(jnp.minimum(ref_b - b, 0.0))
            a_i = lax.dot_general(qt.astype(BF16), kt.astype(BF16), NT, preferred_element_type=F32)
            rows.append(jnp.where(c_sub < i_sub * SUB, a_i, 0.0))
        a = jnp.concatenate(rows, axis=0) + a_diag
        o = o + jnp.dot(a.astype(BF16), vc.astype(BF16), preferred_element_type=F32)
        o_scr[h, pl.ds(r0, CHUNK), :] = o
        kend = kc * jnp.exp2(bend - b)
        st_scr[h] = st * jnp.exp2(bend) + lax.dot_general(vc.astype(BF16), kend.astype(BF16), TN,
                                                          preferred_element_type=F32)

    def chunk(c, carry):
        r0 = pl.multiple_of(c * CHUNK, CHUNK)
        for h in range(q_scr.shape[0]):
            head_chunk(h, r0)
        return carry

    lax.fori_loop(0, tb // CHUNK, chunk, 0)


def _finish_recur(o_scr, gain_ref, gate, o_ref):
    for h in range(o_scr.shape[0]):
        o = o_scr[h]
        ms = jnp.mean(o * o, axis=-1, keepdims=True)
        y = o * lax.rsqrt(ms + EPS) * gain_ref[...] * gate[:, h * LANES:(h + 1) * LANES]
        o_ref[:, h * LANES:(h + 1) * LANES] = y.astype(o_ref.dtype)


def _hgrn2_kernel(q_ref, f_ref, i_ref, gg_ref, lb_ref, gain_ref, sel_ref, o_ref,
                  q_scr, k_scr, v_scr, g_scr, o_scr, st_scr, *, tb):
    @pl.when(pl.program_id(1) == 0)
    def _():
        st_scr[...] = jnp.zeros(st_scr.shape, F32)

    for h in range(q_scr.shape[0]):
        cols = slice(h * LANES, (h + 1) * LANES)
        lb = lb_ref[:, cols]
        z = f_ref[:, cols]
        sg = _sigmoid(z)
        f = lb + (1.0 - lb) * sg
        q_scr[h] = q_ref[:, cols]
        k_scr[h] = (1.0 - lb) * (1.0 - sg)
        v_scr[h] = i_ref[:, cols]
        g_scr[h] = jnp.log(jnp.maximum(f, TINY)) * LOG2E
    _recur_tile(q_scr, k_scr, v_scr, g_scr, o_scr, st_scr, sel_ref, tb)
    _finish_recur(o_scr, gain_ref, _sigmoid(gg_ref[...]), o_ref)


def _recur_scratch(tb, heads):
    return ([pltpu.VMEM((heads, tb, LANES), F32) for _ in range(5)]
            + [pltpu.VMEM((heads, LANES, LANES), F32)])


def hgrn2_mixer(proj, lower_bound, norm_gain, tb=512, heads=HG_HEADS):
    T = proj.shape[0]
    wide = heads * LANES
    per = HG_HEADS // heads

    def col(k):
        return pl.BlockSpec((tb, wide), lambda hp, i, k=k: (i, per * k + hp))

    return pl.pallas_call(
        functools.partial(_hgrn2_kernel, tb=tb),
        grid=(per, T // tb),
        in_specs=[col(0), col(1), col(2), col(3),
                  pl.BlockSpec((1, wide), lambda hp, i: (0, hp)),
                  pl.BlockSpec((1, LANES), lambda hp, i: (0, 0)),
                  pl.BlockSpec((SUB * LANES, LANES), lambda hp, i: (0, 0))],
        out_specs=pl.BlockSpec((tb, wide), lambda hp, i: (i, hp)),
        out_shape=jax.ShapeDtypeStruct((T, BRANCH_WIDTH), BF16),
        scratch_shapes=_recur_scratch(tb, heads),
        compiler_params=_cparams(("parallel", "arbitrary")),
        name="hgrn2_mixer",
    )(proj, proj, proj, proj, lower_bound.reshape(1, -1), norm_gain.reshape(1, -1), _diag_selector())


def _gla_kernel(q_ref, k_ref, v_ref, r_ref, tail_ref, wa_ref, ba_ref, gain_ref, sel_ref, o_ref,
                q_scr, k_scr, v_scr, g_scr, o_scr, st_scr, *, tb):
    @pl.when(pl.program_id(1) == 0)
    def _():
        st_scr[...] = jnp.zeros(st_scr.shape, F32)

    lane = lax.broadcasted_iota(jnp.int32, (tb, LANES), 1)
    a = jnp.dot(tail_ref[...], wa_ref[...], precision=HIGHEST, preferred_element_type=F32) + ba_ref[...]
    log_sig = -(jnp.maximum(-a, 0.0) + jnp.log1p(jnp.exp(-jnp.abs(a))))
    g2 = log_sig * (LOG2E / GLA_TAU)
    for h in range(q_scr.shape[0]):
        pair = slice((h // 2) * LANES, (h // 2 + 1) * LANES)
        mine = (lane < GLA_DK) if h % 2 == 0 else (lane >= GLA_DK)
        q_scr[h] = jnp.where(mine, q_ref[:, pair] * (GLA_DK ** -0.5), 0.0)
        k_scr[h] = jnp.where(mine, k_ref[:, pair], 0.0)
        v_scr[h] = v_ref[:, h * LANES:(h + 1) * LANES]
        g_scr[h] = jnp.where(mine, g2[:, pair], 0.0)
    _recur_tile(q_scr, k_scr, v_scr, g_scr, o_scr, st_scr, sel_ref, tb)
    r = r_ref[...]
    _finish_recur(o_scr, gain_ref, r * _sigmoid(r), o_ref)


def gla_mixer(proj, proj_t, w_a2, b_a, norm_gain, tb=512, heads=GLA_HEADS):
    T = proj.shape[0]
    assert heads % 2 == 0 and LANES == 2 * GLA_DK
    wide = heads * LANES
    half = heads * GLA_DK
    cqk = C_GLA_QK // half
    cvr = C_GLA_VR // wide
    per = GLA_HEADS // heads
    wa = jnp.zeros((LANES, GLA_HEADS * GLA_DK), F32).at[TAIL_GLA_A0:TAIL_GLA_A0 + GLA_RANK].set(w_a2)
    return pl.pallas_call(
        functools.partial(_gla_kernel, tb=tb),
        grid=(per, T // tb),
        in_specs=[pl.BlockSpec((tb, half), lambda hp, i: (i, cqk + hp)),
                  pl.BlockSpec((tb, half), lambda hp, i: (i, cqk + per + hp)),
                  pl.BlockSpec((tb, wide), lambda hp, i: (i, cvr + hp)),
                  pl.BlockSpec((tb, wide), lambda hp, i: (i, cvr + per + hp)),
                  pl.BlockSpec((tb, LANES), lambda hp, i: (i, 0)),
                  pl.BlockSpec((LANES, half), lambda hp, i: (0, hp)),
                  pl.BlockSpec((1, half), lambda hp, i: (0, hp)),
                  pl.BlockSpec((1, LANES), lambda hp, i: (0, 0)),
                  pl.BlockSpec((SUB * LANES, LANES), lambda hp, i: (0, 0))],
        out_specs=pl.BlockSpec((tb, wide), lambda hp, i: (i, hp)),
        out_shape=jax.ShapeDtypeStruct((T, BRANCH_WIDTH), BF16),
        scratch_shapes=_recur_scratch(tb, heads),
        compiler_params=_cparams(("parallel", "arbitrary")),
        name="gla_mixer",
    )(proj, proj, proj, proj, proj_t, wa, b_a.reshape(1, -1), norm_gain.reshape(1, -1), _diag_selector())


def _merge_kernel(oc_ref, os_ref, ow_ref, ob_ref, og_ref, wb_ref, g0_ref, g1_ref, g2_ref, o_ref):
    o_a = (oc_ref[...] + os_ref[...] + ow_ref[...]).astype(BF16)
    acc = _sigmoid(g0_ref[...]) * jnp.dot(o_a, wb_ref[0], preferred_element_type=F32)
    acc = acc + _sigmoid(g1_ref[...]) * jnp.dot(ob_ref[...], wb_ref[1], preferred_element_type=F32)
    acc = acc + _sigmoid(g2_ref[...]) * jnp.dot(og_ref[...], wb_ref[2], preferred_element_type=F32)
    o_ref[...] = acc.astype(o_ref.dtype)


def merge_branches(proj, o_cmp, o_slc, o_win, o_b, o_c, w_branch_bf16, tm=512, tn=512):
    T = proj.shape[0]
    W = BRANCH_WIDTH
    nj = D_MODEL // tn
    ospec = pl.BlockSpec((tm, W), lambda i, j: (i, 0))

    def gspec(n):
        return pl.BlockSpec((tm, tn), lambda i, j, n=n: (i, n * nj + j))

    return pl.pallas_call(
        _merge_kernel,
        grid=(T // tm, nj),
        in_specs=[ospec, ospec, ospec, ospec, ospec,
                  pl.BlockSpec((3, W, tn), lambda i, j: (0, 0, j)),
                  gspec(0), gspec(1), gspec(2)],
        out_specs=pl.BlockSpec((tm, tn), lambda i, j: (i, j)),
        out_shape=jax.ShapeDtypeStruct((T, D_MODEL), BF16),
        compiler_params=_cparams(("parallel", "arbitrary")),
        name="merge_branches",
    )(o_cmp, o_slc, o_win, o_b, o_c, w_branch_bf16, proj, proj, proj)


MOE_TILE = 512


def _route_kernel(x_ref, g_ref, wr_ref, br_ref, h_ref, route_ref, cnt_ref, carry, *, tm):
    i = pl.program_id(0)

    @pl.when(i == 0)
    def _():
        carry[...] = jnp.zeros(carry.shape, F32)

    x = x_ref[...]
    ms = jnp.mean(x * x, axis=-1, keepdims=True)
    h = x * lax.rsqrt(ms + EPS) * g_ref[...]
    h_ref[...] = h
    logits = jnp.dot(h, wr_ref[...], precision=HIGHEST, preferred_element_type=F32) + br_ref[...]
    lane = lax.broadcasted_iota(jnp.int32, (tm, LANES), 1)

    def masked_softmax(mask):
        l = jnp.where(mask, logits, NEG)
        e = jnp.where(mask, jnp.exp(l - jnp.max(l, axis=1, keepdims=True)), 0.0)
        return e / jnp.sum(e, axis=1, keepdims=True)

    def top1(prob, mask):
        p = jnp.max(jnp.where(mask, prob, -1.0), axis=1, keepdims=True)
        idx = jnp.min(jnp.where(mask & (prob == p), lane, LANES), axis=1, keepdims=True)
        return p, idx

    gmask = lane < N_GROUPS
    gw, gidx = top1(masked_softmax(gmask), gmask)
    emask = (lane >= N_GROUPS) & (lane < N_GROUPS + N_EXPERTS) & ((lane - N_GROUPS) // EXPERTS_PER_GROUP == gidx)
    eprob = masked_softmax(emask)
    p1, i1 = top1(eprob, emask)
    rest = emask & (lane != i1)
    p2, i2 = top1(eprob, rest)
    psum = p1 + p2
    w1 = gw * (p1 / psum)
    w2 = gw * (p2 / psum)
    e1 = i1 - N_GROUPS
    e2 = i2 - N_GROUPS

    onehot = jnp.where((lane == e1) | (lane == e2), 1.0, 0.0)
    r = lax.broadcasted_iota(jnp.int32, (tm, tm), 0)
    c = lax.broadcasted_iota(jnp.int32, (tm, tm), 1)
    strict = jnp.where(r > c, 1.0, 0.0).astype(BF16)
    before = jnp.dot(strict, onehot.astype(BF16), preferred_element_type=F32) + carry[0:1, :]
    rank1 = jnp.sum(jnp.where(lane == e1, before, 0.0), axis=1, keepdims=True)
    rank2 = jnp.sum(jnp.where(lane == e2, before, 0.0), axis=1, keepdims=True)
    total = carry[0:1, :] + jnp.sum(onehot, axis=0, keepdims=True)
    carry[...] = jnp.broadcast_to(total, carry.shape)
    cnt_ref[...] = jnp.broadcast_to(total, cnt_ref.shape)

    out = jnp.where(lane == 0, w1, 0.0)
    out = jnp.where(lane == 1, w2, out)
    out = jnp.where(lane == 2, e1.astype(F32), out)
    out = jnp.where(lane == 3, e2.astype(F32), out)
    out = jnp.where(lane == 4, rank1, out)
    out = jnp.where(lane == 5, rank2, out)
    route_ref[...] = out


def moe_route(x, gain, w_grp, b_grp, w_exp, b_exp, tm=512):
    T, D = x.shape
    n_pad = LANES - N_GROUPS - N_EXPERTS
    wr = jnp.concatenate([w_grp, w_exp, jnp.zeros((D, n_pad), F32)], axis=1)
    br = jnp.concatenate([b_grp, b_exp, jnp.zeros((n_pad,), F32)]).reshape(1, LANES)
    return pl.pallas_call(
        functools.partial(_route_kernel, tm=tm),
        grid=(T // tm,),
        in_specs=[pl.BlockSpec((tm, D), lambda i: (i, 0)),
                  pl.BlockSpec((1, D), lambda i: (0, 0)),
                  pl.BlockSpec((D, LANES), lambda i: (0, 0)),
                  pl.BlockSpec((1, LANES), lambda i: (0, 0))],
        out_specs=(pl.BlockSpec((tm, D), lambda i: (i, 0)),
                   pl.BlockSpec((tm, LANES), lambda i: (i, 0)),
                   pl.BlockSpec((8, LANES), lambda i: (0, 0))),
        out_shape=(jax.ShapeDtypeStruct((T, D), F32),
                   jax.ShapeDtypeStruct((T, LANES), F32),
                   jax.ShapeDtypeStruct((8, LANES), F32)),
        scratch_shapes=[pltpu.VMEM((8, LANES), F32)],
        compiler_params=_cparams(("arbitrary",)),
        name="moe_route",
    )(x, gain.reshape(1, D), wr, br)


def _row_dma(src, src_row, dst, dst_row, sem):
    return pltpu.make_async_copy(src.at[pl.ds(src_row, 1), :], dst.at[pl.ds(dst_row, 1), :], sem)


def _rows_wait(src, dst, n, sem):
    pltpu.make_async_copy(src.at[pl.ds(0, n), :], dst.at[pl.ds(0, n), :], sem).wait()


def _expert_kernel(tile_expert, tile_rows, n_used, rcur_ref, rnext_ref, h_hbm, wgu_ref, wd_ref, o_ref,
                   xbuf, wgu_bf, wd_bf, sem):
    i = pl.program_id(0)
    used = i < n_used[0]
    slot = i % 2

    def fetch(rref, s, count):
        def body(p, c):
            r = 2 * p
            _row_dma(h_hbm, rref[0, 0, r], xbuf.at[s], r, sem.at[s]).start(priority=0)
            _row_dma(h_hbm, rref[0, 0, r + 1], xbuf.at[s], r + 1, sem.at[s]).start(priority=1)
            return c

        lax.fori_loop(0, count // 2, body, 0)

        @pl.when(count % 2 == 1)
        def _():
            _row_dma(h_hbm, rref[0, 0, count - 1], xbuf.at[s], count - 1, sem.at[s]).start(priority=0)

    def drain(s, count):
        for b in range(MOE_TILE.bit_length()):
            @pl.when((count & (1 << b)) != 0)
            def _():
                _rows_wait(h_hbm, xbuf.at[s], 1 << b, sem.at[s])

    @pl.when(i == 0)
    def _():
        xbuf[...] = jnp.zeros(xbuf.shape, F32)
        fetch(rcur_ref, 0, tile_rows[0])

    @pl.when(i + 1 < n_used[0])
    def _():
        fetch(rnext_ref, 1 - slot, tile_rows[i + 1])

    @pl.when(used)
    def _():
        prev = tile_expert[jnp.maximum(i - 1, 0)]

        @pl.when((i == 0) | (tile_expert[i] != prev))
        def _():
            wgu_bf[...] = wgu_ref[0, 0].astype(BF16)
            wd_bf[...] = wd_ref[0, 0].astype(BF16)

        drain(slot, tile_rows[i])
        x = xbuf[slot].astype(BF16)
        gu = jnp.dot(x, wgu_bf[...], preferred_element_type=F32)
        gate = gu[:, 0:D_FF_EXPERT]
        up = gu[:, D_FF_EXPERT:2 * D_FF_EXPERT]
        act = gate * _sigmoid(gate) * up
        o_ref[...] = jnp.dot(act.astype(BF16), wd_bf[...], preferred_element_type=F32)

    @pl.when(jnp.logical_not(used))
    def _():
        o_ref[...] = jnp.zeros(o_ref.shape, F32)


def moe_experts(h, row_token, tile_expert, tile_rows, n_used, w_gate_up, w_down, layer):
    T, D = h.shape
    n_tiles = row_token.shape[0] // MOE_TILE
    rows = row_token.reshape(n_tiles, 1, MOE_TILE)
    grid_spec = pltpu.PrefetchScalarGridSpec(
        num_scalar_prefetch=3,
        grid=(n_tiles,),
        in_specs=[
            pl.BlockSpec((1, 1, MOE_TILE), lambda i, te, tr, nu: (i, 0, 0), memory_space=pltpu.SMEM),
            pl.BlockSpec((1, 1, MOE_TILE), lambda i, te, tr, nu: (jnp.minimum(i + 1, n_tiles - 1), 0, 0),
                         memory_space=pltpu.SMEM),
            pl.BlockSpec(memory_space=pl.ANY),
            pl.BlockSpec((1, 1, D, 2 * D_FF_EXPERT), lambda i, te, tr, nu: (layer, te[i], 0, 0)),
            pl.BlockSpec((1, 1, D_FF_EXPERT, D), lambda i, te, tr, nu: (layer, te[i], 0, 0)),
        ],
        out_specs=pl.BlockSpec((MOE_TILE, D), lambda i, te, tr, nu: (i, 0)),
        scratch_shapes=[
            pltpu.VMEM((2, MOE_TILE, D), F32),
            pltpu.VMEM((D, 2 * D_FF_EXPERT), BF16),
            pltpu.VMEM((D_FF_EXPERT, D), BF16),
            pltpu.SemaphoreType.DMA((2,)),
        ],
    )
    return pl.pallas_call(
        _expert_kernel,
        grid_spec=grid_spec,
        out_shape=jax.ShapeDtypeStruct((n_tiles * MOE_TILE, D), F32),
        compiler_params=_cparams(("arbitrary",)),
        name="moe_experts",
    )(tile_expert, tile_rows, n_used, rows, rows, h, w_gate_up, w_down)


def _combine_kernel(dcur_ref, dnext_ref, x_ref, route_ref, y_hbm, o_ref, buf, sem, *, tm):
    i = pl.program_id(0)
    n = pl.num_programs(0)
    slot = i % 2

    def fetch(dref, s):
        def body(r, c):
            _row_dma(y_hbm, dref[0, 0, r], buf.at[s, 0], r, sem.at[s]).start(priority=0)
            _row_dma(y_hbm, dref[0, 0, tm + r], buf.at[s, 1], r, sem.at[s]).start(priority=1)
            return c

        lax.fori_loop(0, tm, body, 0, unroll=8)

    @pl.when(i == 0)
    def _():
        fetch(dcur_ref, 0)

    @pl.when(i + 1 < n)
    def _():
        fetch(dnext_ref, 1 - slot)

    _rows_wait(y_hbm, buf.at[slot, 0], tm, sem.at[slot])
    _rows_wait(y_hbm, buf.at[slot, 1], tm, sem.at[slot])
    route = route_ref[...]
    lane = lax.broadcasted_iota(jnp.int32, route.shape, 1)
    w1 = jnp.sum(jnp.where(lane == 0, route, 0.0), axis=1, keepdims=True)
    w2 = jnp.sum(jnp.where(lane == 1, route, 0.0), axis=1, keepdims=True)
    o_ref[...] = x_ref[...] + (w1 * buf[slot, 0] + w2 * buf[slot, 1])


def moe_combine(x, ys, route, dest, tm=256):
    T, D = x.shape
    n = T // tm
    dest_tiles = dest.reshape(2, n, tm).transpose(1, 0, 2).reshape(n, 1, 2 * tm)
    return pl.pallas_call(
        functools.partial(_combine_kernel, tm=tm),
        grid=(n,),
        in_specs=[pl.BlockSpec((1, 1, 2 * tm), lambda i: (i, 0, 0), memory_space=pltpu.SMEM),
                  pl.BlockSpec((1, 1, 2 * tm), lambda i: (jnp.minimum(i + 1, n - 1), 0, 0), memory_space=pltpu.SMEM),
                  pl.BlockSpec((tm, D), lambda i: (i, 0)),
                  pl.BlockSpec((tm, LANES), lambda i: (i, 0)),
                  pl.BlockSpec(memory_space=pl.ANY)],
        out_specs=pl.BlockSpec((tm, D), lambda i: (i, 0)),
        out_shape=jax.ShapeDtypeStruct((T, D), F32),
        scratch_shapes=[pltpu.VMEM((2, 2, tm, D), F32), pltpu.SemaphoreType.DMA((2,))],
        compiler_params=_cparams(("arbitrary",)),
        name="moe_combine",
    )(dest_tiles, dest_tiles, x, route, ys)


def hierarchical_moe(x, gain, w_grp, b_grp, w_exp, b_exp, w_gate_up, w_down, layer):
    T, D = x.shape
    h, route, cnt = moe_route(x, gain, w_grp, b_grp, w_exp, b_exp)
    route_t = route[:, 0:8].T
    expert = route_t[2:4].astype(jnp.int32)
    rank = route_t[4:6].astype(jnp.int32)
    counts = cnt[0, 0:N_EXPERTS].astype(jnp.int32)
    padded = ((counts + MOE_TILE - 1) // MOE_TILE) * MOE_TILE
    ends = jnp.cumsum(padded)
    offs = ends - padded
    e_ids = jnp.arange(N_EXPERTS, dtype=jnp.int32)[:, None, None]
    dest = rank + jnp.sum(jnp.where(expert[None] == e_ids, offs[:, None, None], 0), axis=0)
    n_rows = 2 * T + N_EXPERTS * MOE_TILE
    n_tiles = n_rows // MOE_TILE
    n_used = (ends[-1] // MOE_TILE).astype(jnp.int32)
    tile_start = jnp.arange(n_tiles, dtype=jnp.int32) * MOE_TILE
    tile_expert = jnp.sum((ends[None, :] <= tile_start[:, None]).astype(jnp.int32), axis=1)
    last_expert = tile_expert[jnp.maximum(n_used - 1, 0)]
    tile_expert = jnp.where(tile_start < ends[-1], tile_expert, last_expert)
    row_token = jnp.zeros((n_rows,), jnp.int32).at[dest.reshape(-1)].set(
        jnp.tile(jnp.arange(T, dtype=jnp.int32), 2))
    real_end = offs + counts
    tile_rows = jnp.sum(jnp.where(tile_expert[:, None] == jnp.arange(N_EXPERTS, dtype=jnp.int32)[None, :],
                                  real_end[None, :], 0), axis=1) - tile_start
    tile_rows = jnp.where(tile_start < ends[-1], jnp.clip(tile_rows, 0, MOE_TILE), 0).astype(jnp.int32)
    ys = moe_experts(h, row_token, tile_expert, tile_rows, n_used.reshape(1), w_gate_up, w_down, layer)
    return moe_combine(x, ys, route, dest)


def _in_proj_kernel(a_ref, wt_hbm, o_ref, wbuf, sem, *, layer, row0, tn, sigmoid_out):
    i, j = pl.program_id(0), pl.program_id(1)
    ni, nj = pl.num_programs(0), pl.num_programs(1)
    step = i * nj + j
    slot = step % 2

    def fetch(jj, s):
        start = pl.multiple_of(row0 + jj * tn, SUBLANES)
        return pltpu.make_async_copy(wt_hbm.at[layer, pl.ds(start, tn), :], wbuf.at[s], sem.at[s])

    @pl.when(step == 0)
    def _():
        fetch(0, 0).start()

    @pl.when(step + 1 < ni * nj)
    def _():
        fetch((j + 1) % nj, 1 - slot).start()

    fetch(j, slot).wait()
    w = wbuf[slot].astype(BF16)
    acc = lax.dot_general(a_ref[...], w, NT, preferred_element_type=F32)
    if sigmoid_out:
        acc = _sigmoid(acc)
    o_ref[...] = acc.astype(o_ref.dtype)


def in_proj(h, wt, layer, row0, n, tn, name, tm=2048, sigmoid_out=False):
    T, K = h.shape
    assert n % tn == 0 and row0 % SUBLANES == 0 and tn % SUBLANES == 0
    return pl.pallas_call(
        functools.partial(_in_proj_kernel, layer=layer, row0=row0, tn=tn, sigmoid_out=sigmoid_out),
        grid=(T // tm, n // tn),
        in_specs=[pl.BlockSpec((tm, K), lambda i, j: (i, 0)), pl.BlockSpec(memory_space=pl.ANY)],
        out_specs=pl.BlockSpec((tm, tn), lambda i, j: (i, j)),
        out_shape=jax.ShapeDtypeStruct((T, n), BF16 if sigmoid_out else F32),
        scratch_shapes=[pltpu.VMEM((2, tn, K), F32), pltpu.SemaphoreType.DMA((2,))],
        compiler_params=_cparams(("arbitrary", "arbitrary")),
        name=name,
    )(h, wt)


def kernel(x, norm_mix, w_in, cmp_pe, cmp_w1, cmp_w2, q_norm, k_norm, hg_lb_logits, hg_norm, gla_w_a2, gla_b_a,
           gla_norm, w_branch, w_out, norm_ffn, w_grp, b_grp, w_exp, b_exp, w_gate_up, w_down):
    B, T, D = x.shape
    assert B == 1 and D == D_MODEL
    xt = x[0]
    p_lb = jax.nn.softmax(hg_lb_logits.astype(F32), axis=0)
    lower_bounds = jnp.cumsum(p_lb, axis=0) - p_lb[0]
    wt = jnp.swapaxes(w_in, 1, 2)
    for l in range(DEPTH):
        h = rmsnorm_bf16(xt, norm_mix[l])
        proj_n = in_proj(h, wt, l, W_NSA[0], W_NSA[1], 512, "in_proj_nsa")
        proj_r = in_proj(h, wt, l, W_REC[0], W_REC[1], 512, "in_proj_rec")
        proj_a = in_proj(h, wt, l, W_GLA_A_BLOCK[0], W_GLA_A_BLOCK[1], LANES, "in_proj_gla_a")
        proj_m = in_proj(h, wt, l, W_MERGE[0], W_MERGE[1], 512, "in_proj_merge")
        o_cmp, o_slc, o_win = nsa_mixer(proj_n, cmp_pe[l], cmp_w1[l], cmp_w2[l], q_norm[l], k_norm[l])
        o_b = hgrn2_mixer(proj_r, lower_bounds[l], hg_norm[l])
        o_c = gla_mixer(proj_r, proj_a, gla_w_a2[l], gla_b_a[l], gla_norm[l])
        merged = merge_branches(proj_m, o_cmp, o_slc, o_win, o_b, o_c, w_branch[l].astype(BF16))
        xt = matmul_bf16(merged, w_out[l].astype(BF16), res=xt, tm=1024, tn=512, name="out_proj")
        xt = hierarchical_moe(xt, norm_ffn[l], w_grp[l], b_grp[l], w_exp[l], b_exp[l], w_gate_up, w_down, l)
    return xt[None]
```
